```python
import math
import jax, jax.numpy as jnp
from jax import lax
import numpy as np

D_MODEL = 1024
BATCH = 8
SEQ = 2048
DEPTH = 4
DEC_BATCH = 128
DEC_SEQ = 4
PAST_LEN = 8192
PAGE_SIZE = 128

N_A_LAYERS = DEPTH // 2
N_B_LAYERS = DEPTH - N_A_LAYERS
D_RNN = D_MODEL
N_LRU_BLOCKS = 4
LRU_BLOCK = D_RNN // N_LRU_BLOCKS
CONV_WIDTH = 4
RG_C = 8.0
N_Q_HEADS = 16
N_KV_HEADS = 4
HEAD_DIM = 64
GROUP = N_Q_HEADS // N_KV_HEADS
WINDOW = 128
N_BUCKETS = 32
MAX_DISTANCE = WINDOW
N_GROUPS = 4
EXPERTS_PER_GROUP = 4
N_EXPERTS = N_GROUPS * EXPERTS_PER_GROUP
TOP_K_IN_GROUP = 2
D_EXPERT = 512
ALPHA = (2.0 * DEPTH) ** 0.25
BETA = (8.0 * DEPTH) ** -0.25
LN_EPS = 1e-5
NEG_INF = -1e30

kernel_name = "yoco_rglru_swa_sink_hier_moe_step"

F32 = jnp.float32


def layer_norm(x, g, b):
    xf = x.astype(F32)
    xc = xf - jnp.mean(xf, axis=-1, keepdims=True)
    var = jnp.mean(xc * xc, axis=-1, keepdims=True)
    return (xc * lax.rsqrt(var + LN_EPS) * g.astype(F32) + b.astype(F32)).astype(x.dtype)


def adaln(x, c, w, b):
    m = (c @ w + b)[:, None, :]
    shift, scale, gate = jnp.split(m, 3, axis=-1)
    return x * (1 + scale) + shift, 1 + gate


def _lin_rec_combine(left, right):
    a_l, b_l = left
    a_r, b_r = right
    return a_l * a_r, a_r * b_l + b_r


def rglru_mixer(u, h0, conv0, w_in, conv_w, conv_b, w_a, b_a, w_x, b_x, lam, w_out):
    B, T, _ = u.shape
    y, xr = jnp.split(u @ w_in, 2, axis=-1)
    y = jax.nn.gelu(y)
    x_ext = jnp.concatenate([conv0.astype(xr.dtype), xr], axis=1)
    xc = conv_b
    for k in range(CONV_WIDTH):
        xc = xc + x_ext[:, k:k + T] * conv_w[k]
    xb = xc.reshape(B, T, N_LRU_BLOCKS, LRU_BLOCK)
    r = jax.nn.sigmoid((jnp.einsum('btnd,nde->btne', xb, w_a).reshape(B, T, D_RNN) + b_a).astype(F32))
    i = jax.nn.sigmoid((jnp.einsum('btnd,nde->btne', xb, w_x).reshape(B, T, D_RNN) + b_x).astype(F32))
    log_a = -RG_C * r * jax.nn.softplus(-lam.astype(F32))
    a = jnp.exp(log_a)
    bt = jnp.sqrt(-jnp.expm1(2.0 * log_a)) * i * xc.astype(F32)
    bt = bt.at[:, 0].add(a[:, 0] * h0.astype(F32))
    _, h = lax.associative_scan(_lin_rec_combine, (a, bt), axis=1)
    out = (y * h.astype(y.dtype)) @ w_out
    return out, h[:, -1].astype(u.dtype), x_ext[:, T:]


def t5_bucket(d):
    max_exact = N_BUCKETS // 2
    d = jnp.maximum(d, 0)
    log_ratio = jnp.log(jnp.maximum(d, 1).astype(F32) / max_exact) / math.log(MAX_DISTANCE / max_exact)
    large = max_exact + (log_ratio * (N_BUCKETS - max_exact)).astype(jnp.int32)
    large = jnp.minimum(large, N_BUCKETS - 1)
    return jnp.where(d < max_exact, d, large)


def rel_bias_for(d, rel_bias):
    bias = rel_bias[t5_bucket(d)].astype(F32)
    nq, nk = d.shape
    return jnp.transpose(bias, (2, 0, 1)).reshape(N_KV_HEADS, GROUP, nq, nk)


def sink_softmax(scores, sinks):
    s = sinks.astype(F32)[:, :, None, None]
    m = jnp.maximum(jnp.max(scores, axis=-1, keepdims=True), s)
    p = jnp.exp(scores - m)
    return p / (jnp.sum(p, axis=-1, keepdims=True) + jnp.exp(s - m))


def swa_prompt(q, k, v, sinks, rel_bias):
    B, S = q.shape[:2]
    nb = S // WINDOW
    qb = q.reshape(B, nb, WINDOW, N_KV_HEADS, GROUP, HEAD_DIM)
    kb = k.reshape(B, nb, WINDOW, N_KV_HEADS, HEAD_DIM)
    vb = v.reshape(B, nb, WINDOW, N_KV_HEADS, HEAD_DIM)
    pad = ((0, 0), (1, 0), (0, 0), (0, 0), (0, 0))
    k_ext = jnp.concatenate([jnp.pad(kb, pad)[:, :-1], kb], axis=2)
    v_ext = jnp.concatenate([jnp.pad(vb, pad)[:, :-1], vb], axis=2)
    qi = jnp.arange(WINDOW)[:, None]
    kk = jnp.arange(2 * WINDOW)[None, :]
    d = qi + WINDOW - kk
    rel_ok = (d >= 0) & (d < WINDOW)
    key_ok = (jnp.arange(nb)[:, None] > 0) | (jnp.arange(2 * WINDOW)[None, :] >= WINDOW)
    mask = rel_ok[None] & key_ok[:, None, :]
    scores = jnp.einsum('bnqhgd,bnkhd->bnhgqk', qb, k_ext).astype(F32) * (HEAD_DIM ** -0.5)
    scores = scores + rel_bias_for(d, rel_bias)
    scores = jnp.where(mask[None, :, None, None], scores, NEG_INF)
    p = sink_softmax(scores, sinks).astype(v.dtype)
    o = jnp.einsum('bnhgqk,bnkhd->bnqhgd', p, v_ext)
    return o.reshape(B, S, N_Q_HEADS * HEAD_DIM)


def swa_sample(q, k_new, v_new, k_buf, v_buf, sinks, rel_bias):
    B, T = q.shape[:2]
    L = k_buf.shape[1]
    k_all = jnp.concatenate([k_buf.astype(k_new.dtype), k_new], axis=1)
    v_all = jnp.concatenate([v_buf.astype(v_new.dtype), v_new], axis=1)
    d = jnp.arange(T)[:, None] + L - jnp.arange(L + T)[None, :]
    mask = (d >= 0) & (d < WINDOW)
    qg = q.reshape(B, T, N_KV_HEADS, GROUP, HEAD_DIM)
    scores = jnp.einsum('bqhgd,bkhd->bhgqk', qg, k_all).astype(F32) * (HEAD_DIM ** -0.5)
    scores = scores + rel_bias_for(d, rel_bias)
    scores = jnp.where(mask, scores, NEG_INF)
    p = sink_softmax(scores, sinks).astype(v_all.dtype)
    o = jnp.einsum('bhgqk,bkhd->bqhgd', p, v_all)
    return o.reshape(B, T, N_Q_HEADS * HEAD_DIM)


def hier_moe(u, w_group, b_group, w_router, b_router, w1, w3, w2):
    B, T, D = u.shape
    t = u.reshape(B * T, D)
    n = t.shape[0]
    g_prob = jax.nn.softmax((t @ w_group + b_group).astype(F32), axis=-1)
    g_p, g_idx = lax.top_k(g_prob, 1)
    e_logits = (jnp.einsum('nd,dge->nge', t, w_router) + b_router).astype(F32)
    e_logits = e_logits[jnp.arange(n), g_idx[:, 0]]
    e_top, e_idx = lax.top_k(e_logits, TOP_K_IN_GROUP)
    w = jax.nn.softmax(e_top, axis=-1) * g_p
    eid = g_idx * EXPERTS_PER_GROUP + e_idx
    combine = jnp.einsum('nk,nke->ne', w, jax.nn.one_hot(eid, N_EXPERTS, dtype=F32)).astype(u.dtype)
    h = jax.nn.silu(jnp.einsum('nd,edf->nef', t, w1)) * jnp.einsum('nd,edf->nef', t, w3)
    y = jnp.einsum('nef,efd->nd', h * combine[:, :, None], w2)
    return y.reshape(B, T, D)


def trunk(x, c, rnn_h0, rnn_conv0, k_buf, v_buf,
          ada_w, ada_b, ln_g, ln_b,
          lru_w_in, lru_conv_w, lru_conv_b, lru_w_a, lru_b_a, lru_w_x, lru_b_x, lru_lambda, lru_w_out,
          kv_w, attn_w_q, attn_sinks, attn_w_o, rel_bias,
          moe_w_group, moe_b_group, moe_w_router, moe_b_router, moe_w1, moe_w3, moe_w2):
    B, T, _ = x.shape
    new_h, new_conv = [], []
    k_sh = v_sh = None
    for l in range(DEPTH):
        u, gate = adaln(x, c, ada_w[l, 0], ada_b[l, 0])
        if l < N_A_LAYERS:
            h0 = jnp.zeros((B, D_RNN), x.dtype) if rnn_h0 is None else rnn_h0[l]
            cv0 = jnp.zeros((B, CONV_WIDTH - 1, D_RNN), x.dtype) if rnn_conv0 is None else rnn_conv0[l]
            mix, h_last, conv_last = rglru_mixer(
                u, h0, cv0, lru_w_in[l], lru_conv_w[l], lru_conv_b[l], lru_w_a[l], lru_b_a[l],
                lru_w_x[l], lru_b_x[l], lru_lambda[l], lru_w_out[l])
            new_h.append(h_last)
            new_conv.append(conv_last)
        else:
            j = l - N_A_LAYERS
            q = (u @ attn_w_q[j]).reshape(B, T, N_Q_HEADS, HEAD_DIM)
            sinks = attn_sinks[j].reshape(N_KV_HEADS, GROUP)
            if k_buf is None:
                o = swa_prompt(q, k_sh, v_sh, sinks, rel_bias)
            else:
                o = swa_sample(q, k_sh, v_sh, k_buf, v_buf, sinks, rel_bias)
            mix = o @ attn_w_o[j]
        x = layer_norm(ALPHA * x + gate * mix, ln_g[l, 0], ln_b[l, 0])
        u, gate = adaln(x, c, ada_w[l, 1], ada_b[l, 1])
        ffn = hier_moe(u, moe_w_group[l], moe_b_group[l], moe_w_router[l], moe_b_router[l],
                       moe_w1[l], moe_w3[l], moe_w2[l])
        x = layer_norm(ALPHA * x + gate * ffn, ln_g[l, 1], ln_b[l, 1])
        if l == N_A_LAYERS - 1:
            k_sh, v_sh = jnp.split(x @ kv_w, 2, axis=-1)
            k_sh = k_sh.reshape(B, T, N_KV_HEADS, HEAD_DIM)
            v_sh = v_sh.reshape(B, T, N_KV_HEADS, HEAD_DIM)
    if k_buf is None:
        L = min(WINDOW, T)
        k_state, v_state = k_sh[:, T - L:], v_sh[:, T - L:]
    else:
        L = k_buf.shape[1]
        k_state = jnp.concatenate([k_buf.astype(k_sh.dtype), k_sh], axis=1)[:, -L:]
        v_state = jnp.concatenate([v_buf.astype(v_sh.dtype), v_sh], axis=1)[:, -L:]
    return x, jnp.stack(new_h), jnp.stack(new_conv), k_state, v_state


def setup_inputs(seed: int = 0) -> dict:
    key = jax.random.key(seed)
    keys = iter(jax.random.split(key, 40))

    def normal(shape, scale):
        return jax.random.normal(next(keys), shape, F32) * scale

    buf_len = min(WINDOW, PAST_LEN)
    u = jax.random.uniform(next(keys), (N_A_LAYERS, D_RNN), F32, 0.9, 0.999)
    a = u ** (1.0 / RG_C)
    lam = jnp.log(a) - jnp.log1p(-a)
    return {
        "x_prompt": normal((BATCH, SEQ, D_MODEL), 1.0),
        "x_sample": normal((DEC_BATCH, DEC_SEQ, D_MODEL), 1.0),
        "state_rnn_h": normal((N_A_LAYERS, DEC_BATCH, D_RNN), 0.5),
        "state_rnn_conv": normal((N_A_LAYERS, DEC_BATCH, CONV_WIDTH - 1, D_RNN), 1.0),
        "cache_win_k": normal((DEC_BATCH, buf_len, N_KV_HEADS, HEAD_DIM), 1.0),
        "cache_win_v": normal((DEC_BATCH, buf_len, N_KV_HEADS, HEAD_DIM), 1.0),
        "c_prompt": normal((BATCH, D_MODEL), 1.0),
        "c_sample": normal((DEC_BATCH, D_MODEL), 1.0),
        "ada_w": normal((DEPTH, 2, D_MODEL, 3 * D_MODEL), 0.1 * D_MODEL ** -0.5),
        "ada_b": normal((DEPTH, 2, 3 * D_MODEL), 0.01),
        "ln_g": 1.0 + normal((DEPTH, 2, D_MODEL), 0.02),
        "ln_b": normal((DEPTH, 2, D_MODEL), 0.02),
        "lru_w_in": normal((N_A_LAYERS, D_MODEL, 2 * D_RNN), D_MODEL ** -0.5),
        "lru_conv_w": normal((N_A_LAYERS, CONV_WIDTH, D_RNN), CONV_WIDTH ** -0.5),
        "lru_conv_b": normal((N_A_LAYERS, D_RNN), 0.01),
        "lru_w_a": normal((N_A_LAYERS, N_LRU_BLOCKS, LRU_BLOCK, LRU_BLOCK), LRU_BLOCK ** -0.5),
        "lru_b_a": normal((N_A_LAYERS, D_RNN), 0.01),
        "lru_w_x": normal((N_A_LAYERS, N_LRU_BLOCKS, LRU_BLOCK, LRU_BLOCK), LRU_BLOCK ** -0.5),
        "lru_b_x": normal((N_A_LAYERS, D_RNN), 0.01),
        "lru_lambda": lam,
        "lru_w_out": normal((N_A_LAYERS, D_RNN, D_MODEL), BETA * D_RNN ** -0.5),
        "kv_w": normal((D_MODEL, 2 * N_KV_HEADS * HEAD_DIM), D_MODEL ** -0.5),
        "attn_w_q": normal((N_B_LAYERS, D_MODEL, N_Q_HEADS * HEAD_DIM), D_MODEL ** -0.5),
        "attn_sinks": normal((N_B_LAYERS, N_Q_HEADS), 0.5),
        "attn_w_o": normal((N_B_LAYERS, N_Q_HEADS * HEAD_DIM, D_MODEL), BETA * (N_Q_HEADS * HEAD_DIM) ** -0.5),
        "rel_bias": normal((N_BUCKETS, N_Q_HEADS), 0.5),
        "moe_w_group": normal((DEPTH, D_MODEL, N_GROUPS), D_MODEL ** -0.5),
        "moe_b_group": normal((DEPTH, N_GROUPS), 0.01),
        "moe_w_router": normal((DEPTH, D_MODEL, N_GROUPS, EXPERTS_PER_GROUP), D_MODEL ** -0.5),
        "moe_b_router": normal((DEPTH, N_GROUPS, EXPERTS_PER_GROUP), 0.01),
        "moe_w1": normal((DEPTH, N_EXPERTS, D_MODEL, D_EXPERT), D_MODEL ** -0.5),
        "moe_w3": normal((DEPTH, N_EXPERTS, D_MODEL, D_EXPERT), D_MODEL ** -0.5),
        "moe_w2": normal((DEPTH, N_EXPERTS, D_EXPERT, D_MODEL), BETA * D_EXPERT ** -0.5),
    }


def reference(x_prompt, x_sample, state_rnn_h, state_rnn_conv, cache_win_k, cache_win_v,
              c_prompt, c_sample,
              ada_w, ada_b, ln_g, ln_b,
              lru_w_in, lru_conv_w, lru_conv_b, lru_w_a, lru_b_a, lru_w_x, lru_b_x, lru_lambda, lru_w_out,
              kv_w, attn_w_q, attn_sinks, attn_w_o, rel_bias,
              moe_w_group, moe_b_group, moe_w_router, moe_b_router, moe_w1, moe_w3, moe_w2):
    weights = (ada_w, ada_b, ln_g, ln_b,
               lru_w_in, lru_conv_w, lru_conv_b, lru_w_a, lru_b_a, lru_w_x, lru_b_x, lru_lambda, lru_w_out,
               kv_w, attn_w_q, attn_sinks, attn_w_o, rel_bias,
               moe_w_group, moe_b_group, moe_w_router, moe_b_router, moe_w1, moe_w3, moe_w2)
    y_prompt, h_p, conv_p, k_p, v_p = trunk(x_prompt, c_prompt, None, None, None, None, *weights)
    y_sample, h_s, conv_s, k_s, v_s = trunk(x_sample, c_sample, state_rnn_h, state_rnn_conv,
                                            cache_win_k, cache_win_v, *weights)
    return (y_prompt, y_sample, h_p, conv_p, k_p, v_p, h_s, conv_s, k_s, v_s)
```

```python
import functools
import math

import jax
import jax.numpy as jnp
from jax import lax
from jax.experimental import pallas as pl
from jax.experimental.pallas import tpu as pltpu

D = 1024
R = 1024
DEPTH = 4
N_A = 2
N_BLK = 4
BLK = R // N_BLK
CONV_W = 4
RG_C = 8.0
HQ = 16
HKV = 4
HD = 64
GRP = HQ // HKV
KVW = HKV * HD
WIN = 128
N_BUCKETS = 32
N_GROUPS = 4
E_PER_G = 4
N_EXP = 16
F = 512
ALPHA = (2.0 * DEPTH) ** 0.25
LN_EPS = 1e-5
NEG = -1e30
RLANES = 128
ROW_TILE = 8

F32 = jnp.float32
BF16 = jnp.bfloat16

VMEM_LIMIT = 56 * 1024 * 1024


def _cp(sem):
    return pltpu.CompilerParams(dimension_semantics=sem, vmem_limit_bytes=VMEM_LIMIT)


def _dot(a, b):
    return jnp.dot(a, b, preferred_element_type=F32)


def _layer_norm(z, g, b):
    mu = jnp.mean(z, axis=-1, keepdims=True)
    zc = z - mu
    var = jnp.mean(zc * zc, axis=-1, keepdims=True)
    return zc * lax.rsqrt(var + LN_EPS) * g + b


def _split_mod(mod_ref):
    return mod_ref[:, 0:D], mod_ref[:, D:2 * D], mod_ref[:, 2 * D:3 * D]


def _adaln_body(c_ref, w_ref, b_ref, o_ref):
    j = pl.program_id(1)
    acc = _dot(c_ref[...].astype(BF16), w_ref[...].astype(BF16)) + b_ref[...]
    o_ref[...] = acc + jnp.where(j > 0, 1.0, 0.0).astype(F32)


def _adaln(c_all, ada_w, ada_b):
    nc = c_all.shape[0]
    w = ada_w.reshape(2 * DEPTH, D, 3 * D)
    b = ada_b.reshape(2 * DEPTH, 1, 3 * D)
    return pl.pallas_call(
        _adaln_body,
        grid=(2 * DEPTH, 3),
        in_specs=[
            pl.BlockSpec((nc, D), lambda l, j: (0, 0)),
            pl.BlockSpec((None, D, D), lambda l, j: (l, 0, j)),
            pl.BlockSpec((None, 1, D), lambda l, j: (l, 0, j)),
        ],
        out_specs=pl.BlockSpec((None, nc, D), lambda l, j: (l, 0, j)),
        out_shape=jax.ShapeDtypeStruct((2 * DEPTH, nc, 3 * D), F32),
        compiler_params=_cp(("arbitrary", "arbitrary")),
        name="adaln",
    )(c_all, w, b)


def _lru_gates(xc, n, wa_ref, wx_ref, ba_ref, bx_ref, lam_ref):
    cols = slice(n * BLK, (n + 1) * BLK)
    xb = xc.astype(BF16)
    r = jax.nn.sigmoid(_dot(xb, wa_ref[n]) + ba_ref[:, cols])
    i = jax.nn.sigmoid(_dot(xb, wx_ref[n]) + bx_ref[:, cols])
    z = -lam_ref[:, cols]
    softplus = jnp.maximum(z, 0.0) + jnp.log1p(jnp.exp(-jnp.abs(z)))
    log_a = (-RG_C) * r * softplus
    a = jnp.exp(log_a)
    b = jnp.sqrt(1.0 - a * a) * i * xc
    return a, b


def _lru_prompt_body(x_ref, mod_ref, win_ref, cw_ref, cb_ref, wa_ref, wx_ref, ba_ref, bx_ref,
                     lam_ref, wout_ref, g_ref, b_ref,
                     xn_ref, hl_ref, cl_ref,
                     xbuf, abuf, hbuf, hc, *, tm):
    t = pl.program_id(1)

    @pl.when(t == 0)
    def _():
        xbuf[0:ROW_TILE, :] = jnp.zeros((ROW_TILE, R), F32)
        hc[...] = jnp.zeros((1, R), F32)

    x = x_ref[...]
    shift, sc1p, g1p = _split_mod(mod_ref)
    u = x * sc1p + shift
    yx = _dot(u.astype(BF16), win_ref[...])
    y = jax.nn.gelu(yx[:, :R])
    xbuf[ROW_TILE:ROW_TILE + tm, :] = yx[:, R:]

    rowmod = lax.broadcasted_iota(jnp.int32, (tm, BLK), 0) & (ROW_TILE - 1)
    for n in range(N_BLK):
        cols = slice(n * BLK, (n + 1) * BLK)
        xc = cb_ref[:, cols] + cw_ref[3:4, cols] * xbuf[ROW_TILE:ROW_TILE + tm, cols]
        for k in range(1, CONV_W):
            xc = xc + cw_ref[3 - k:4 - k, cols] * xbuf[ROW_TILE - k:ROW_TILE - k + tm, cols]
        a, b = _lru_gates(xc, n, wa_ref, wx_ref, ba_ref, bx_ref, lam_ref)
        s = 1
        while s < ROW_TILE:
            a_sh = pltpu.roll(a, s, 0)
            b_sh = pltpu.roll(b, s, 0)
            m = rowmod >= s
            b = jnp.where(m, a * b_sh, 0.0) + b
            a = jnp.where(m, a * a_sh, a)
            s *= 2
        abuf[:, cols] = a
        hbuf[:, cols] = b

    def group(g, h):
        r0 = pl.multiple_of(g * ROW_TILE, ROW_TILE)
        hg = abuf[pl.ds(r0, ROW_TILE), :] * h + hbuf[pl.ds(r0, ROW_TILE), :]
        hbuf[pl.ds(r0, ROW_TILE), :] = hg
        return hg[ROW_TILE - 1:ROW_TILE, :]

    h_end = lax.fori_loop(0, tm // ROW_TILE, group, hc[...])
    hc[...] = h_end
    hl_ref[...] = h_end
    tail = xbuf[tm + ROW_TILE - (CONV_W - 1):tm + ROW_TILE, :]
    cl_ref[...] = tail
    xbuf[ROW_TILE - (CONV_W - 1):ROW_TILE, :] = tail

    mix = _dot((y * hbuf[...]).astype(BF16), wout_ref[...])
    xn_ref[...] = _layer_norm(ALPHA * x + g1p * mix, g_ref[...], b_ref[...])


def _const_spec(shape):
    nd = len(shape)
    return pl.BlockSpec(shape, lambda *_: (0,) * nd)


def _lru_weight_specs():
    return [
        _const_spec((D, 2 * R)), _const_spec((CONV_W, R)), _const_spec((1, R)),
        _const_spec((N_BLK, BLK, BLK)), _const_spec((N_BLK, BLK, BLK)),
        _const_spec((1, R)), _const_spec((1, R)), _const_spec((1, R)),
        _const_spec((R, D)), _const_spec((1, D)), _const_spec((1, D)),
    ]


def _lru_prompt(x, mod, lw, ln_g, ln_b, *, nb, seq, tm):
    nt = seq // tm
    n = nb * seq
    return pl.pallas_call(
        functools.partial(_lru_prompt_body, tm=tm),
        grid=(nb, nt),
        in_specs=[
            pl.BlockSpec((tm, D), lambda b, t: (b * nt + t, 0)),
            pl.BlockSpec((None, 1, 3 * D), lambda b, t: (b, 0, 0)),
        ] + _lru_weight_specs(),
        out_specs=[
            pl.BlockSpec((tm, D), lambda b, t: (b * nt + t, 0)),
            pl.BlockSpec((None, 1, R), lambda b, t: (b, 0, 0)),
            pl.BlockSpec((None, CONV_W - 1, R), lambda b, t: (b, 0, 0)),
        ],
        out_shape=[
            jax.ShapeDtypeStruct((n, D), F32),
            jax.ShapeDtypeStruct((nb, 1, R), F32),
            jax.ShapeDtypeStruct((nb, CONV_W - 1, R), F32),
        ],
        scratch_shapes=[
            pltpu.VMEM((tm + ROW_TILE, R), F32),
            pltpu.VMEM((tm, R), F32),
            pltpu.VMEM((tm, R), F32),
            pltpu.VMEM((1, R), F32),
        ],
        compiler_params=_cp(("arbitrary", "arbitrary")),
        name="lru_prompt",
    )(x, mod, *lw, ln_g, ln_b)


def _lru_sample_body(x_ref, mod_ref, h0_ref, c0_ref, win_ref, cw_ref, cb_ref, wa_ref, wx_ref,
                     ba_ref, bx_ref, lam_ref, wout_ref, g_ref, b_ref,
                     xn_ref, hl_ref, cl_ref, hbuf, *, nb, steps):
    x = x_ref[...]
    shift, sc1p, g1p = _split_mod(mod_ref)
    u = x * sc1p + shift
    yx = _dot(u.astype(BF16), win_ref[...])
    y = jax.nn.gelu(yx[:, :R])
    xext = jnp.concatenate([c0_ref[...], yx[:, R:]], axis=0)
    cl_ref[...] = xext[steps * nb:, :]
    for n in range(N_BLK):
        cols = slice(n * BLK, (n + 1) * BLK)
        xc = cb_ref[:, cols] + cw_ref[0:1, cols] * xext[0:steps * nb, cols]
        for k in range(1, CONV_W):
            xc = xc + cw_ref[k:k + 1, cols] * xext[k * nb:(k + steps) * nb, cols]
        a, b = _lru_gates(xc, n, wa_ref, wx_ref, ba_ref, bx_ref, lam_ref)
        h = h0_ref[:, cols]
        for t in range(steps):
            h = a[t * nb:(t + 1) * nb, :] * h + b[t * nb:(t + 1) * nb, :]
            hbuf[t * nb:(t + 1) * nb, cols] = h
        hl_ref[:, cols] = h
    mix = _dot((y * hbuf[...]).astype(BF16), wout_ref[...])
    xn_ref[...] = _layer_norm(ALPHA * x + g1p * mix, g_ref[...], b_ref[...])


def _lru_sample(x, mod, h0, c0, lw, ln_g, ln_b, *, nb, steps):
    n = nb * steps
    return pl.pallas_call(
        functools.partial(_lru_sample_body, nb=nb, steps=steps),
        grid=(1,),
        in_specs=[
            _const_spec((n, D)), _const_spec((n, 3 * D)), _const_spec((nb, R)),
            _const_spec(((CONV_W - 1) * nb, R)),
        ] + _lru_weight_specs(),
        out_specs=[_const_spec((n, D)), _const_spec((nb, R)), _const_spec(((CONV_W - 1) * nb, R))],
        out_shape=[
            jax.ShapeDtypeStruct((n, D), F32),
            jax.ShapeDtypeStruct((nb, R), F32),
            jax.ShapeDtypeStruct(((CONV_W - 1) * nb, R), F32),
        ],
        scratch_shapes=[pltpu.VMEM((n, R), F32)],
        compiler_params=_cp(("arbitrary",)),
        name="lru_sample",
    )(x, mod, h0, c0, *lw, ln_g, ln_b)


def _route(logits):
    col = lax.broadcasted_iota(jnp.int32, logits.shape, 1).astype(F32)
    big = float(RLANES)
    is_g = col < N_GROUPS
    gl = jnp.where(is_g, logits, NEG)
    gmax = jnp.max(gl, axis=-1, keepdims=True)
    gidx = jnp.min(jnp.where(gl == gmax, col, big), axis=-1, keepdims=True)
    gsum = jnp.sum(jnp.where(is_g, jnp.exp(gl - gmax), 0.0), axis=-1, keepdims=True)
    g_p = 1.0 / gsum
    lo = N_GROUPS + gidx * E_PER_G
    in_grp = (col >= lo) & (col < lo + E_PER_G)
    el = jnp.where(in_grp, logits, NEG)
    t1 = jnp.max(el, axis=-1, keepdims=True)
    i1 = jnp.min(jnp.where(el == t1, col, big), axis=-1, keepdims=True)
    el2 = jnp.where(col == i1, NEG, el)
    t2 = jnp.max(el2, axis=-1, keepdims=True)
    i2 = jnp.min(jnp.where((el2 == t2) & in_grp & (col != i1), col, big), axis=-1, keepdims=True)
    e21 = jnp.exp(t2 - t1)
    w1 = g_p / (1.0 + e21)
    w2 = w1 * e21
    return jnp.where(col == i1, w1, 0.0) + jnp.where(col == i2, w2, 0.0)


def _moe_body(x_ref, mod_ref, wr_ref, br_ref, w1_ref, w3_ref, w2_ref, g_ref, b_ref,
              xn_ref, ub, comb, acc):
    e = pl.program_id(1)

    @pl.when(e == 0)
    def _():
        shift, sc1p, _ = _split_mod(mod_ref)
        u = (x_ref[...] * sc1p + shift).astype(BF16)
        ub[...] = u
        comb[...] = _route(_dot(u, wr_ref[...]) + br_ref[...])
        acc[...] = jnp.zeros(acc.shape, F32)

    u = ub[...]
    col = lax.broadcasted_iota(jnp.int32, comb.shape, 1)
    ce = jnp.sum(jnp.where(col == e + N_GROUPS, comb[...], 0.0), axis=-1, keepdims=True)
    h = jax.nn.silu(_dot(u, w1_ref[...])) * _dot(u, w3_ref[...]) * ce
    acc[...] += _dot(h.astype(BF16), w2_ref[...])

    @pl.when(e == N_EXP - 1)
    def _():
        g1p = mod_ref[:, 2 * D:3 * D]
        xn_ref[...] = _layer_norm(ALPHA * x_ref[...] + g1p * acc[...], g_ref[...], b_ref[...])


def _moe(x, mod, wr, br, w1, w3, w2, ln_g, ln_b, *, tm, tiles_per_mod):
    n = x.shape[0]
    mr = mod.shape[1]
    return pl.pallas_call(
        _moe_body,
        grid=(n // tm, N_EXP),
        in_specs=[
            pl.BlockSpec((tm, D), lambda i, e: (i, 0)),
            pl.BlockSpec((None, mr, 3 * D), lambda i, e: (i // tiles_per_mod, 0, 0)),
            pl.BlockSpec((D, RLANES), lambda i, e: (0, 0)),
            pl.BlockSpec((1, RLANES), lambda i, e: (0, 0)),
            pl.BlockSpec((None, D, F), lambda i, e: (e, 0, 0)),
            pl.BlockSpec((None, D, F), lambda i, e: (e, 0, 0)),
            pl.BlockSpec((None, F, D), lambda i, e: (e, 0, 0)),
            pl.BlockSpec((1, D), lambda i, e: (0, 0)),
            pl.BlockSpec((1, D), lambda i, e: (0, 0)),
        ],
        out_specs=pl.BlockSpec((tm, D), lambda i, e: (i, 0)),
        out_shape=jax.ShapeDtypeStruct((n, D), F32),
        scratch_shapes=[
            pltpu.VMEM((tm, D), BF16),
            pltpu.VMEM((tm, RLANES), F32),
            pltpu.VMEM((tm, D), F32),
        ],
        compiler_params=_cp(("arbitrary", "arbitrary")),
        name="moe",
    )(x, mod, wr, br, w1, w3, w2, ln_g, ln_b)


def _kv_body(x_ref, w_ref, k_ref, v_ref):
    kv = _dot(x_ref[...].astype(BF16), w_ref[...])
    k_ref[...] = kv[:, :KVW]
    v_ref[...] = kv[:, KVW:]


def _kv(x, w, *, tm):
    n = x.shape[0]
    return pl.pallas_call(
        _kv_body,
        grid=(n // tm,),
        in_specs=[pl.BlockSpec((tm, D), lambda i: (i, 0)), _const_spec((D, 2 * KVW))],
        out_specs=[pl.BlockSpec((tm, KVW), lambda i: (i, 0)), pl.BlockSpec((tm, KVW), lambda i: (i, 0))],
        out_shape=[jax.ShapeDtypeStruct((n, KVW), F32), jax.ShapeDtypeStruct((n, KVW), F32)],
        compiler_params=_cp(("arbitrary",)),
        name="kv_proj",
    )(x, w)


def _sink_attend(qh, kh, vh, bias, sink):
    s = lax.dot_general(qh, kh, (((1,), (1,)), ((), ())), preferred_element_type=F32) + bias
    m = jnp.maximum(jnp.max(s, axis=-1, keepdims=True), sink)
    p = jnp.exp(s - m)
    den = jnp.sum(p, axis=-1, keepdims=True) + jnp.exp(sink - m)
    return _dot(p.astype(BF16), vh) / den


def _attn_prompt_body(x_ref, mod_ref, wq_ref, wo_ref, kc_ref, kp_ref, vc_ref, vp_ref,
                      bias_ref, sink_ref, g_ref, b_ref, xn_ref, *, tq):
    t = pl.program_id(1)
    x = x_ref[...]
    shift, sc1p, g1p = _split_mod(mod_ref)
    u = (x * sc1p + shift).astype(BF16)
    q = (_dot(u, wq_ref[...]) * (HD ** -0.5)).astype(BF16)
    kc = kc_ref[...].astype(BF16)
    vc = vc_ref[...].astype(BF16)
    kp = kp_ref[...].astype(BF16)
    vp = vp_ref[...].astype(BF16)
    kcol = lax.broadcasted_iota(jnp.int32, (GRP * WIN, 2 * WIN), 1)
    first = jnp.where((kcol < WIN) & (t == 0), NEG, 0.0).astype(F32)
    rows_out = []
    for j in range(tq // WIN):
        rows = slice(j * WIN, (j + 1) * WIN)
        if j == 0:
            kk = jnp.concatenate([kp, kc[rows]], axis=0)
            vv = jnp.concatenate([vp, vc[rows]], axis=0)
        else:
            kk = kc[(j - 1) * WIN:(j + 1) * WIN]
            vv = vc[(j - 1) * WIN:(j + 1) * WIN]
        heads = [None] * HQ
        for hk in range(HKV):
            kcols = slice(hk * HD, (hk + 1) * HD)
            qh = jnp.concatenate(
                [q[rows, (hk * GRP + g) * HD:(hk * GRP + g + 1) * HD] for g in range(GRP)], axis=0)
            bias = bias_ref[hk]
            if j == 0:
                bias = bias + first
            o = _sink_attend(qh, kk[:, kcols], vv[:, kcols], bias, sink_ref[hk])
            for g in range(GRP):
                heads[hk * GRP + g] = o[g * WIN:(g + 1) * WIN]
        rows_out.append(jnp.concatenate(heads, axis=1))
    o_all = jnp.concatenate(rows_out, axis=0) if len(rows_out) > 1 else rows_out[0]
    mix = _dot(o_all.astype(BF16), wo_ref[...])
    xn_ref[...] = _layer_norm(ALPHA * x + g1p * mix, g_ref[...], b_ref[...])


def _attn_prompt(x, mod, wq, wo, k, v, bias, sink, ln_g, ln_b, *, nb, seq, tq):
    nt = seq // tq
    wpt = tq // WIN
    n = nb * seq

    def prev(b, t):
        return (jnp.maximum((b * nt + t) * wpt - 1, 0), 0)

    cur = lambda b, t: (b * nt + t, 0)
    return pl.pallas_call(
        functools.partial(_attn_prompt_body, tq=tq),
        grid=(nb, nt),
        in_specs=[
            pl.BlockSpec((tq, D), cur),
            pl.BlockSpec((None, 1, 3 * D), lambda b, t: (b, 0, 0)),
            _const_spec((D, HQ * HD)), _const_spec((HQ * HD, D)),
            pl.BlockSpec((tq, KVW), cur), pl.BlockSpec((WIN, KVW), prev),
            pl.BlockSpec((tq, KVW), cur), pl.BlockSpec((WIN, KVW), prev),
            _const_spec((HKV, GRP * WIN, 2 * WIN)), _const_spec((HKV, GRP * WIN, 1)),
            _const_spec((1, D)), _const_spec((1, D)),
        ],
        out_specs=pl.BlockSpec((tq, D), cur),
        out_shape=jax.ShapeDtypeStruct((n, D), F32),
        compiler_params=_cp(("arbitrary", "arbitrary")),
        name="attn_prompt",
    )(x, mod, wq, wo, k, k, v, v, bias, sink, ln_g, ln_b)


def _attn_sample_body(x_ref, mod_ref, wq_ref, wo_ref, kb_ref, vb_ref, kn_ref, vn_ref,
                      bias_ref, sink_ref, g_ref, b_ref, xn_ref, *, ns, steps):
    x = x_ref[...]
    shift, sc1p, g1p = _split_mod(mod_ref)
    u = (x * sc1p + shift).astype(BF16)
    q = (_dot(u, wq_ref[...]) * (HD ** -0.5)).astype(BF16)
    nbuf = kb_ref.shape[1]
    k_all = jnp.concatenate([kb_ref[...].reshape(ns * nbuf, KVW), kn_ref[...]], axis=0).astype(BF16)
    v_all = jnp.concatenate([vb_ref[...].reshape(ns * nbuf, KVW), vn_ref[...]], axis=0).astype(BF16)
    nr = ns * steps
    heads = [None] * HQ
    for hk in range(HKV):
        kcols = slice(hk * HD, (hk + 1) * HD)
        qh = jnp.concatenate(
            [q[:, (hk * GRP + g) * HD:(hk * GRP + g + 1) * HD] for g in range(GRP)], axis=0)
        o = _sink_attend(qh, k_all[:, kcols], v_all[:, kcols], bias_ref[hk], sink_ref[hk])
        for g in range(GRP):
            heads[hk * GRP + g] = o[g * nr:(g + 1) * nr]
    o_all = jnp.concatenate(heads, axis=1)
    mix = _dot(o_all.astype(BF16), wo_ref[...])
    xn_ref[...] = _layer_norm(ALPHA * x + g1p * mix, g_ref[...], b_ref[...])


def _attn_sample(x, mod, wq, wo, kbuf, vbuf, kn, vn, bias, sink, ln_g, ln_b, *, ns, steps):
    n = x.shape[0]
    nr = ns * steps
    nbuf = kbuf.shape[1]
    row = lambda i: (i, 0)
    return pl.pallas_call(
        functools.partial(_attn_sample_body, ns=ns, steps=steps),
        grid=(n // nr,),
        in_specs=[
            pl.BlockSpec((nr, D), row), pl.BlockSpec((nr, 3 * D), row),
            _const_spec((D, HQ * HD)), _const_spec((HQ * HD, D)),
            pl.BlockSpec((ns, nbuf, KVW), lambda i: (i, 0, 0)),
            pl.BlockSpec((ns, nbuf, KVW), lambda i: (i, 0, 0)),
            pl.BlockSpec((nr, KVW), row), pl.BlockSpec((nr, KVW), row),
            _const_spec(bias.shape), _const_spec(sink.shape),
            _const_spec((1, D)), _const_spec((1, D)),
        ],
        out_specs=pl.BlockSpec((nr, D), row),
        out_shape=jax.ShapeDtypeStruct((n, D), F32),
        compiler_params=_cp(("arbitrary",)),
        name="attn_sample",
    )(x, mod, wq, wo, kbuf, vbuf, kn, vn, bias, sink, ln_g, ln_b)


def _t5_bucket(d):
    max_exact = N_BUCKETS // 2
    d = jnp.maximum(d, 0)
    log_ratio = jnp.log(jnp.maximum(d, 1).astype(F32) / max_exact) / math.log(WIN / max_exact)
    large = max_exact + (log_ratio * (N_BUCKETS - max_exact)).astype(jnp.int32)
    large = jnp.minimum(large, N_BUCKETS - 1)
    return jnp.where(d < max_exact, d, large)


def _bias_table(d, ok, rel_bias):
    bias = rel_bias[_t5_bucket(d)].astype(F32)
    bias = jnp.where(ok[:, :, None], bias, NEG)
    nq, nk = d.shape
    return jnp.transpose(bias, (2, 0, 1)).reshape(HKV, GRP * nq, nk)


def _prompt_bias(rel_bias):
    d = jnp.arange(WIN)[:, None] + WIN - jnp.arange(2 * WIN)[None, :]
    return _bias_table(d, (d >= 0) & (d < WIN), rel_bias)


def _sample_bias(rel_bias, ns, steps, nbuf):
    qs = jnp.repeat(jnp.arange(ns), steps)
    qt = jnp.tile(jnp.arange(steps), ns)
    ks = jnp.concatenate([jnp.repeat(jnp.arange(ns), nbuf), jnp.repeat(jnp.arange(ns), steps)])
    kpos = jnp.concatenate([jnp.tile(jnp.arange(nbuf), ns), nbuf + jnp.tile(jnp.arange(steps), ns)])
    d = qt[:, None] + nbuf - kpos[None, :]
    ok = (d >= 0) & (d < WIN) & (qs[:, None] == ks[None, :])
    return _bias_table(d, ok, rel_bias)


def _sink_table(sinks, rows):
    return jnp.repeat(sinks.reshape(HKV, GRP), rows, axis=1)[:, :, None].astype(F32)


MOE_TM = 1024
LRU_TM = 256
ATTN_TQ = 256
SAMPLE_NS = 8


def kernel(x_prompt, x_sample, state_rnn_h, state_rnn_conv, cache_win_k, cache_win_v, c_prompt, c_sample, ada_w, ada_b, ln_g, ln_b, lru_w_in, lru_conv_w, lru_conv_b, lru_w_a, lru_b_a, lru_w_x, lru_b_x, lru_lambda, lru_w_out, kv_w, attn_w_q, attn_sinks, attn_w_o, rel_bias, moe_w_group, moe_b_group, moe_w_router, moe_b_router, moe_w1, moe_w3, moe_w2):
    nbp, seq, _ = x_prompt.shape
    nbs, steps, _ = x_sample.shape
    nbuf = cache_win_k.shape[1]
    npt = nbp * seq
    nst = nbs * steps

    mods = _adaln(jnp.concatenate([c_prompt, c_sample], axis=0), ada_w, ada_b)
    mod_p = mods[:, :nbp].reshape(DEPTH, 2, nbp, 1, 3 * D)
    mod_s = mods[:, nbp:].reshape(DEPTH, 2, nbs, 3 * D)

    def lru_weights(l):
        return (lru_w_in[l].astype(BF16), lru_conv_w[l], lru_conv_b[l][None], lru_w_a[l].astype(BF16),
                lru_w_x[l].astype(BF16), lru_b_a[l][None], lru_b_x[l][None], lru_lambda[l][None],
                lru_w_out[l].astype(BF16))

    def moe_weights(l):
        wr = jnp.concatenate([moe_w_group[l], moe_w_router[l].reshape(D, N_EXP)], axis=1)
        wr = jnp.pad(wr, ((0, 0), (0, RLANES - wr.shape[1]))).astype(BF16)
        br = jnp.concatenate([moe_b_group[l], moe_b_router[l].reshape(N_EXP)])
        br = jnp.pad(br, (0, RLANES - br.shape[0]))[None]
        return wr, br, moe_w1[l].astype(BF16), moe_w3[l].astype(BF16), moe_w2[l].astype(BF16)

    kvw = kv_w.astype(BF16)
    bias_p = _prompt_bias(rel_bias)
    bias_s = _sample_bias(rel_bias, SAMPLE_NS, steps, nbuf)

    xp = x_prompt.reshape(npt, D)
    xs = jnp.transpose(x_sample, (1, 0, 2)).reshape(nst, D)
    hp, cp_, hs, cs = [], [], [], []
    kp = vp = kn = vn = None
    for l in range(DEPTH):
        g0, b0 = ln_g[l, 0][None], ln_b[l, 0][None]
        g1, b1 = ln_g[l, 1][None], ln_b[l, 1][None]
        if l < N_A:
            lw = lru_weights(l)
            xp, h_l, c_l = _lru_prompt(xp, mod_p[l, 0], lw, g0, b0, nb=nbp, seq=seq, tm=LRU_TM)
            hp.append(h_l[:, 0])
            cp_.append(c_l)
            c0 = jnp.transpose(state_rnn_conv[l], (1, 0, 2)).reshape((CONV_W - 1) * nbs, R)
            xs, h_l, c_l = _lru_sample(xs, jnp.tile(mod_s[l, 0], (steps, 1)), state_rnn_h[l], c0,
                                       lw, g0, b0, nb=nbs, steps=steps)
            hs.append(h_l)
            cs.append(jnp.transpose(c_l.reshape(CONV_W - 1, nbs, R), (1, 0, 2)))
            mod_s1 = jnp.tile(mod_s[l, 1], (steps, 1))
        else:
            j = l - N_A
            wq, wo = attn_w_q[j].astype(BF16), attn_w_o[j].astype(BF16)
            xp = _attn_prompt(xp, mod_p[l, 0], wq, wo, kp, vp, bias_p, _sink_table(attn_sinks[j], WIN),
                              g0, b0, nb=nbp, seq=seq, tq=ATTN_TQ)
            xs = _attn_sample(xs, jnp.repeat(mod_s[l, 0], steps, axis=0), wq, wo,
                              cache_win_k.reshape(nbs, nbuf, KVW), cache_win_v.reshape(nbs, nbuf, KVW),
                              kn, vn, bias_s, _sink_table(attn_sinks[j], SAMPLE_NS * steps),
                              g0, b0, ns=SAMPLE_NS, steps=steps)
            mod_s1 = jnp.repeat(mod_s[l, 1], steps, axis=0)
        mw = moe_weights(l)
        xp = _moe(xp, mod_p[l, 1], *mw, g1, b1, tm=MOE_TM, tiles_per_mod=seq // MOE_TM)
        xs = _moe(xs, mod_s1[None], *mw, g1, b1, tm=nst, tiles_per_mod=1)
        if l == N_A - 1:
            xs = jnp.transpose(xs.reshape(steps, nbs, D), (1, 0, 2)).reshape(nst, D)
            kp, vp = _kv(xp, kvw, tm=1024)
            kn, vn = _kv(xs, kvw, tm=nst)

    lp = min(WIN, seq)
    k_p = kp.reshape(nbp, seq, HKV, HD)[:, seq - lp:]
    v_p = vp.reshape(nbp, seq, HKV, HD)[:, seq - lp:]
    k_s = jnp.concatenate([cache_win_k, kn.reshape(nbs, steps, HKV, HD)], axis=1)[:, -nbuf:]
    v_s = jnp.concatenate([cache_win_v, vn.reshape(nbs, steps, HKV, HD)], axis=1)[:, -nbuf:]
    return (xp.reshape(nbp, seq, D), xs.reshape(nbs, steps, D),
            jnp.stack(hp), jnp.stack(cp_), k_p, v_p,
            jnp.stack(hs), jnp.stack(cs), k_s, v_s)
```

```python
import functools
import math

import jax
import jax.numpy as jnp
from jax import lax
from jax.experimental import pallas as pl
from jax.experimental.pallas import tpu as pltpu

D = 1024
R = 1024
DEPTH = 4
N_A = 2
N_BLK = 4
BLK = R // N_BLK
CONV_W = 4
RG_C = 8.0
HQ = 16
HKV = 4
HD = 64
GRP = HQ // HKV
KVW = HKV * HD
WIN = 128
N_BUCKETS = 32
N_GROUPS = 4
E_PER_G = 4
N_EXP = 16
F = 512
ALPHA = (2.0 * DEPTH) ** 0.25
LN_EPS = 1e-5
NEG = -1e30
RLANES = 128
ROW_TILE = 8

F32 = jnp.float32
BF16 = jnp.bfloat16

VMEM_LIMIT = 56 * 1024 * 1024


def _cp(sem):
    return pltpu.CompilerParams(dimension_semantics=sem, vmem_limit_bytes=VMEM_LIMIT)


def _dot(a, b):
    return jnp.dot(a, b, preferred_element_type=F32)


def _layer_norm(z, g, b):
    mu = jnp.mean(z, axis=-1, keepdims=True)
    zc = z - mu
    var = jnp.mean(zc * zc, axis=-1, keepdims=True)
    return zc * lax.rsqrt(var + LN_EPS) * g + b


def _split_mod(mod_ref):
    return mod_ref[:, 0:D], mod_ref[:, D:2 * D], mod_ref[:, 2 * D:3 * D]


def _adaln_body(c_ref, w_ref, b_ref, o_ref):
    j = pl.program_id(1)
    acc = _dot(c_ref[...].astype(BF16), w_ref[...].astype(BF16)) + b_ref[...]
    o_ref[...] = acc + jnp.where(j > 0, 1.0, 0.0).astype(F32)


def _adaln(c_all, ada_w, ada_b):
    nc = c_all.shape[0]
    w = ada_w.reshape(2 * DEPTH, D, 3 * D)
    b = ada_b.reshape(2 * DEPTH, 1, 3 * D)
    return pl.pallas_call(
        _adaln_body,
        grid=(2 * DEPTH, 3),
        in_specs=[
            pl.BlockSpec((nc, D), lambda l, j: (0, 0)),
            pl.BlockSpec((None, D, D), lambda l, j: (l, 0, j)),
            pl.BlockSpec((None, 1, D), lambda l, j: (l, 0, j)),
        ],
        out_specs=pl.BlockSpec((None, nc, D), lambda l, j: (l, 0, j)),
        out_shape=jax.ShapeDtypeStruct((2 * DEPTH, nc, 3 * D), F32),
        compiler_params=_cp(("arbitrary", "arbitrary")),
        name="adaln",
    )(c_all, w, b)


def _lru_gates(xc, n, wa_ref, wx_ref, ba_ref, bx_ref, lam_ref):
    cols = slice(n * BLK, (n + 1) * BLK)
    xb = xc.astype(BF16)
    r = jax.nn.sigmoid(_dot(xb, wa_ref[n]) + ba_ref[:, cols])
    i = jax.nn.sigmoid(_dot(xb, wx_ref[n]) + bx_ref[:, cols])
    z = -lam_ref[:, cols]
    softplus = jnp.maximum(z, 0.0) + jnp.log1p(jnp.exp(-jnp.abs(z)))
    log_a = (-RG_C) * r * softplus
    a = jnp.exp(log_a)
    b = jnp.sqrt(1.0 - a * a) * i * xc
    return a, b


def _lru_prompt_body(x_ref, mod_ref, win_ref, cw_ref, cb_ref, wa_ref, wx_ref, ba_ref, bx_ref,
                     lam_ref, wout_ref, g_ref, b_ref,
                     xn_ref, hl_ref, cl_ref,
                     xbuf, abuf, hbuf, hc, *, tm):
    t = pl.program_id(1)

    @pl.when(t == 0)
    def _():
        xbuf[0:ROW_TILE, :] = jnp.zeros((ROW_TILE, R), F32)
        hc[...] = jnp.zeros((1, R), F32)

    x = x_ref[...]
    shift, sc1p, g1p = _split_mod(mod_ref)
    u = x * sc1p + shift
    yx = _dot(u.astype(BF16), win_ref[...])
    y = jax.nn.gelu(yx[:, :R])
    xbuf[ROW_TILE:ROW_TILE + tm, :] = yx[:, R:]

    rowmod = lax.broadcasted_iota(jnp.int32, (tm, BLK), 0) & (ROW_TILE - 1)
    for n in range(N_BLK):
        cols = slice(n * BLK, (n + 1) * BLK)
        xc = cb_ref[:, cols] + cw_ref[3:4, cols] * xbuf[ROW_TILE:ROW_TILE + tm, cols]
        for k in range(1, CONV_W):
            xc = xc + cw_ref[3 - k:4 - k, cols] * xbuf[ROW_TILE - k:ROW_TILE - k + tm, cols]
        a, b = _lru_gates(xc, n, wa_ref, wx_ref, ba_ref, bx_ref, lam_ref)
        s = 1
        while s < ROW_TILE:
            a_sh = pltpu.roll(a, s, 0)
            b_sh = pltpu.roll(b, s, 0)
            m = rowmod >= s
            b = jnp.where(m, a * b_sh, 0.0) + b
            a = jnp.where(m, a * a_sh, a)
            s *= 2
        abuf[:, cols] = a
        hbuf[:, cols] = b

    def group(g, h):
        r0 = pl.multiple_of(g * ROW_TILE, ROW_TILE)
        hg = abuf[pl.ds(r0, ROW_TILE), :] * h + hbuf[pl.ds(r0, ROW_TILE), :]
        hbuf[pl.ds(r0, ROW_TILE), :] = hg
        return hg[ROW_TILE - 1:ROW_TILE, :]

    h_end = lax.fori_loop(0, tm // ROW_TILE, group, hc[...])
    hc[...] = h_end
    hl_ref[...] = h_end
    tail = xbuf[tm + ROW_TILE - (CONV_W - 1):tm + ROW_TILE, :]
    cl_ref[...] = tail
    xbuf[ROW_TILE - (CONV_W - 1):ROW_TILE, :] = tail

    mix = _dot((y * hbuf[...]).astype(BF16), wout_ref[...])
    xn_ref[...] = _layer_norm(ALPHA * x + g1p * mix, g_ref[...], b_ref[...])


def _const_spec(shape):
    nd = len(shape)
    return pl.BlockSpec(shape, lambda *_: (0,) * nd)


def _lru_weight_specs():
    return [
        _const_spec((D, 2 * R)), _const_spec((CONV_W, R)), _const_spec((1, R)),
        _const_spec((N_BLK, BLK, BLK)), _const_spec((N_BLK, BLK, BLK)),
        _const_spec((1, R)), _const_spec((1, R)), _const_spec((1, R)),
        _const_spec((R, D)), _const_spec((1, D)), _const_spec((1, D)),
    ]


def _lru_prompt(x, mod, lw, ln_g, ln_b, *, nb, seq, tm):
    nt = seq // tm
    n = nb * seq
    return pl.pallas_call(
        functools.partial(_lru_prompt_body, tm=tm),
        grid=(nb, nt),
        in_specs=[
            pl.BlockSpec((tm, D), lambda b, t: (b * nt + t, 0)),
            pl.BlockSpec((None, 1, 3 * D), lambda b, t: (b, 0, 0)),
        ] + _lru_weight_specs(),
        out_specs=[
            pl.BlockSpec((tm, D), lambda b, t: (b * nt + t, 0)),
            pl.BlockSpec((None, 1, R), lambda b, t: (b, 0, 0)),
            pl.BlockSpec((None, CONV_W - 1, R), lambda b, t: (b, 0, 0)),
        ],
        out_shape=[
            jax.ShapeDtypeStruct((n, D), F32),
            jax.ShapeDtypeStruct((nb, 1, R), F32),
            jax.ShapeDtypeStruct((nb, CONV_W - 1, R), F32),
        ],
        scratch_shapes=[
            pltpu.VMEM((tm + ROW_TILE, R), F32),
            pltpu.VMEM((tm, R), F32),
            pltpu.VMEM((tm, R), F32),
            pltpu.VMEM((1, R), F32),
        ],
        compiler_params=_cp(("arbitrary", "arbitrary")),
        name="lru_prompt",
    )(x, mod, *lw, ln_g, ln_b)


def _lru_sample_body(x_ref, mod_ref, h0_ref, c0_ref, win_ref, cw_ref, cb_ref, wa_ref, wx_ref,
                     ba_ref, bx_ref, lam_ref, wout_ref, g_ref, b_ref,
                     xn_ref, hl_ref, cl_ref, hbuf, *, nb, steps):
    x = x_ref[...]
    shift, sc1p, g1p = _split_mod(mod_ref)
    u = x * sc1p + shift
    yx = _dot(u.astype(BF16), win_ref[...])
    y = jax.nn.gelu(yx[:, :R])
    xext = jnp.concatenate([c0_ref[...], yx[:, R:]], axis=0)
    cl_ref[...] = xext[steps * nb:, :]
    for n in range(N_BLK):
        cols = slice(n * BLK, (n + 1) * BLK)
        xc = cb_ref[:, cols] + cw_ref[0:1, cols] * xext[0:steps * nb, cols]
        for k in range(1, CONV_W):
            xc = xc + cw_ref[k:k + 1, cols] * xext[k * nb:(k + steps) * nb, cols]
        a, b = _lru_gates(xc, n, wa_ref, wx_ref, ba_ref, bx_ref, lam_ref)
        h = h0_ref[:, cols]
        for t in range(steps):
            h = a[t * nb:(t + 1) * nb, :] * h + b[t * nb:(t + 1) * nb, :]
            hbuf[t * nb:(t + 1) * nb, cols] = h
        hl_ref[:, cols] = h
    mix = _dot((y * hbuf[...]).astype(BF16), wout_ref[...])
    xn_ref[...] = _layer_norm(ALPHA * x + g1p * mix, g_ref[...], b_ref[...])


def _lru_sample(x, mod, h0, c0, lw, ln_g, ln_b, *, nb, steps):
    n = nb * steps
    return pl.pallas_call(
        functools.partial(_lru_sample_body, nb=nb, steps=steps),
        grid=(1,),
        in_specs=[
            _const_spec((n, D)), _const_spec((n, 3 * D)), _const_spec((nb, R)),
            _const_spec(((CONV_W - 1) * nb, R)),
        ] + _lru_weight_specs(),
        out_specs=[_const_spec((n, D)), _const_spec((nb, R)), _const_spec(((CONV_W - 1) * nb, R))],
        out_shape=[
            jax.ShapeDtypeStruct((n, D), F32),
            jax.ShapeDtypeStruct((nb, R), F32),
            jax.ShapeDtypeStruct(((CONV_W - 1) * nb, R), F32),
        ],
        scratch_shapes=[pltpu.VMEM((n, R), F32)],
        compiler_params=_cp(("arbitrary",)),
        name="lru_sample",
    )(x, mod, h0, c0, *lw, ln_g, ln_b)


def _route(logits):
    col = lax.broadcasted_iota(jnp.int32, logits.shape, 1).astype(F32)
    big = float(RLANES)
    is_g = col < N_GROUPS
    gl = jnp.where(is_g, logits, NEG)
    gmax = jnp.max(gl, axis=-1, keepdims=True)
    gidx = jnp.min(jnp.where(gl == gmax, col, big), axis=-1, keepdims=True)
    gsum = jnp.sum(jnp.where(is_g, jnp.exp(gl - gmax), 0.0), axis=-1, keepdims=True)
    g_p = 1.0 / gsum
    lo = N_GROUPS + gidx * E_PER_G
    in_grp = (col >= lo) & (col < lo + E_PER_G)
    el = jnp.where(in_grp, logits, NEG)
    t1 = jnp.max(el, axis=-1, keepdims=True)
    i1 = jnp.min(jnp.where(el == t1, col, big), axis=-1, keepdims=True)
    el2 = jnp.where(col == i1, NEG, el)
    t2 = jnp.max(el2, axis=-1, keepdims=True)
    i2 = jnp.min(jnp.where((el2 == t2) & in_grp & (col != i1), col, big), axis=-1, keepdims=True)
    e21 = jnp.exp(t2 - t1)
    w1 = g_p / (1.0 + e21)
    w2 = w1 * e21
    return jnp.where(col == i1, w1, 0.0) + jnp.where(col == i2, w2, 0.0)


def _moe_body(x_ref, mod_ref, wr_ref, br_ref, w1_ref, w3_ref, w2_ref, g_ref, b_ref,
              xn_ref, ub, comb, acc):
    e = pl.program_id(1)

    @pl.when(e == 0)
    def _():
        shift, sc1p, _ = _split_mod(mod_ref)
        u = (x_ref[...] * sc1p + shift).astype(BF16)
        ub[...] = u
        comb[...] = _route(_dot(u, wr_ref[...]) + br_ref[...])
        acc[...] = jnp.zeros(acc.shape, F32)

    u = ub[...]
    col = lax.broadcasted_iota(jnp.int32, comb.shape, 1)
    ce = jnp.sum(jnp.where(col == e + N_GROUPS, comb[...], 0.0), axis=-1, keepdims=True)
    h = jax.nn.silu(_dot(u, w1_ref[...])) * _dot(u, w3_ref[...]) * ce
    acc[...] += _dot(h.astype(BF16), w2_ref[...])

    @pl.when(e == N_EXP - 1)
    def _():
        g1p = mod_ref[:, 2 * D:3 * D]
        xn_ref[...] = _layer_norm(ALPHA * x_ref[...] + g1p * acc[...], g_ref[...], b_ref[...])


def _moe(x, mod, wr, br, w1, w3, w2, ln_g, ln_b, *, tm, tiles_per_mod):
    n = x.shape[0]
    mr = mod.shape[1]
    return pl.pallas_call(
        _moe_body,
        grid=(n // tm, N_EXP),
        in_specs=[
            pl.BlockSpec((tm, D), lambda i, e: (i, 0)),
            pl.BlockSpec((None, mr, 3 * D), lambda i, e: (i // tiles_per_mod, 0, 0)),
            pl.BlockSpec((D, RLANES), lambda i, e: (0, 0)),
            pl.BlockSpec((1, RLANES), lambda i, e: (0, 0)),
            pl.BlockSpec((None, D, F), lambda i, e: (e, 0, 0)),
            pl.BlockSpec((None, D, F), lambda i, e: (e, 0, 0)),
            pl.BlockSpec((None, F, D), lambda i, e: (e, 0, 0)),
            pl.BlockSpec((1, D), lambda i, e: (0, 0)),
            pl.BlockSpec((1, D), lambda i, e: (0, 0)),
        ],
        out_specs=pl.BlockSpec((tm, D), lambda i, e: (i, 0)),
        out_shape=jax.ShapeDtypeStruct((n, D), F32),
        scratch_shapes=[
            pltpu.VMEM((tm, D), BF16),
            pltpu.VMEM((tm, RLANES), F32),
            pltpu.VMEM((tm, D), F32),
        ],
        compiler_params=_cp(("arbitrary", "arbitrary")),
        name="moe",
    )(x, mod, wr, br, w1, w3, w2, ln_g, ln_b)


_NT = (((1,), (1,)), ((), ()))
_TN = (((0,), (0,)), ((), ()))


def _kv_body(x_ref, w_ref, wvt_ref, k_ref, v_ref, k16_ref, vt16_ref):
    xb = x_ref[...].astype(BF16)
    kv = _dot(xb, w_ref[...])
    k_ref[...] = kv[:, :KVW]
    v_ref[...] = kv[:, KVW:]
    k16_ref[...] = kv[:, :KVW].astype(BF16)
    vt16_ref[...] = lax.dot_general(wvt_ref[...], xb, _NT, preferred_element_type=F32).astype(BF16)


def _kv(x, w, wvt, *, tm):
    n = x.shape[0]
    row = lambda i: (i, 0)
    return pl.pallas_call(
        _kv_body,
        grid=(n // tm,),
        in_specs=[pl.BlockSpec((tm, D), row), _const_spec((D, 2 * KVW)), _const_spec((KVW, D))],
        out_specs=[pl.BlockSpec((tm, KVW), row), pl.BlockSpec((tm, KVW), row),
                   pl.BlockSpec((tm, KVW), row), pl.BlockSpec((KVW, tm), lambda i: (0, i))],
        out_shape=[jax.ShapeDtypeStruct((n, KVW), F32), jax.ShapeDtypeStruct((n, KVW), F32),
                   jax.ShapeDtypeStruct((n, KVW), BF16), jax.ShapeDtypeStruct((KVW, n), BF16)],
        compiler_params=_cp(("arbitrary",)),
        name="kv_proj",
    )(x, w, wvt)


def _sink_attend(qh, kh, vh, bias, sink):
    s = lax.dot_general(qh, kh, (((1,), (1,)), ((), ())), preferred_element_type=F32) + bias
    m = jnp.maximum(jnp.max(s, axis=-1, keepdims=True), sink)
    p = jnp.exp(s - m)
    den = jnp.sum(p, axis=-1, keepdims=True) + jnp.exp(sink - m)
    return _dot(p.astype(BF16), vh) / den


def _attn_prompt_body(x_ref, mod_ref, wqt_ref, wo_ref, kc_ref, kp_ref, vtc_ref, vtp_ref,
                      bias_ref, sink_ref, g_ref, b_ref, xn_ref, *, tq):
    t = pl.program_id(1)
    x = x_ref[...]
    shift, sc1p, g1p = _split_mod(mod_ref)
    u = (x * sc1p + shift).astype(BF16)
    qt = lax.dot_general(wqt_ref[...], u, _NT, preferred_element_type=F32)
    qt = (qt * (HD ** -0.5)).astype(BF16)
    kc = kc_ref[...]
    vtc = vtc_ref[...]
    krow = lax.broadcasted_iota(jnp.int32, (2 * WIN, GRP * WIN), 0)
    first = jnp.where((krow < WIN) & (t == 0), NEG, 0.0).astype(F32)
    blocks = []
    for j in range(tq // WIN):
        qcols = slice(j * WIN, (j + 1) * WIN)
        if j == 0:
            kk = jnp.concatenate([kp_ref[...], kc[0:WIN]], axis=0)
            vvt = jnp.concatenate([vtp_ref[...], vtc[:, 0:WIN]], axis=1)
        else:
            kk = kc[(j - 1) * WIN:(j + 1) * WIN]
            vvt = vtc[:, (j - 1) * WIN:(j + 1) * WIN]
        heads = []
        for hk in range(HKV):
            hrows = slice(hk * HD, (hk + 1) * HD)
            qht = jnp.concatenate(
                [qt[(hk * GRP + g) * HD:(hk * GRP + g + 1) * HD, qcols] for g in range(GRP)], axis=1)
            st = _dot(kk[:, hrows], qht) + bias_ref[hk]
            if j == 0:
                st = st + first
            sink = sink_ref[hk]
            m = jnp.maximum(jnp.max(st, axis=0, keepdims=True), sink)
            pt = jnp.exp(st - m)
            den = jnp.sum(pt, axis=0, keepdims=True) + jnp.exp(sink - m)
            ot = _dot(vvt[hrows, :], pt.astype(BF16)) * (1.0 / den)
            for g in range(GRP):
                heads.append(ot[:, g * WIN:(g + 1) * WIN])
        blocks.append(jnp.concatenate(heads, axis=0))
    oallt = jnp.concatenate(blocks, axis=1) if len(blocks) > 1 else blocks[0]
    mix = lax.dot_general(oallt.astype(BF16), wo_ref[...], _TN, preferred_element_type=F32)
    xn_ref[...] = _layer_norm(ALPHA * x + g1p * mix, g_ref[...], b_ref[...])


def _attn_prompt(x, mod, wqt, wo, k16, vt16, bias, sink, ln_g, ln_b, *, nb, seq, tq):
    nt = seq // tq
    wpt = tq // WIN
    n = nb * seq

    def prev(b, t):
        return jnp.maximum((b * nt + t) * wpt - 1, 0)

    cur = lambda b, t: (b * nt + t, 0)
    return pl.pallas_call(
        functools.partial(_attn_prompt_body, tq=tq),
        grid=(nb, nt),
        in_specs=[
            pl.BlockSpec((tq, D), cur),
            pl.BlockSpec((None, 1, 3 * D), lambda b, t: (b, 0, 0)),
            _const_spec((HQ * HD, D)), _const_spec((HQ * HD, D)),
            pl.BlockSpec((tq, KVW), cur), pl.BlockSpec((WIN, KVW), lambda b, t: (prev(b, t), 0)),
            pl.BlockSpec((KVW, tq), lambda b, t: (0, b * nt + t)),
            pl.BlockSpec((KVW, WIN), lambda b, t: (0, prev(b, t))),
            _const_spec((HKV, 2 * WIN, GRP * WIN)), _const_spec((HKV, 1, GRP * WIN)),
            _const_spec((1, D)), _const_spec((1, D)),
        ],
        out_specs=pl.BlockSpec((tq, D), cur),
        out_shape=jax.ShapeDtypeStruct((n, D), F32),
        compiler_params=_cp(("arbitrary", "arbitrary")),
        name="attn_prompt",
    )(x, mod, wqt, wo, k16, k16, vt16, vt16, bias, sink, ln_g, ln_b)


def _attn_sample_body(x_ref, mod_ref, wq_ref, wo_ref, kb_ref, vb_ref, kn_ref, vn_ref,
                      bias_ref, sink_ref, g_ref, b_ref, xn_ref, *, ns, steps):
    x = x_ref[...]
    shift, sc1p, g1p = _split_mod(mod_ref)
    u = (x * sc1p + shift).astype(BF16)
    q = (_dot(u, wq_ref[...]) * (HD ** -0.5)).astype(BF16)
    nbuf = kb_ref.shape[1]
    k_all = jnp.concatenate([kb_ref[...].reshape(ns * nbuf, KVW), kn_ref[...]], axis=0).astype(BF16)
    v_all = jnp.concatenate([vb_ref[...].reshape(ns * nbuf, KVW), vn_ref[...]], axis=0).astype(BF16)
    nr = ns * steps
    heads = [None] * HQ
    for hk in range(HKV):
        kcols = slice(hk * HD, (hk + 1) * HD)
        qh = jnp.concatenate(
            [q[:, (hk * GRP + g) * HD:(hk * GRP + g + 1) * HD] for g in range(GRP)], axis=0)
        o = _sink_attend(qh, k_all[:, kcols], v_all[:, kcols], bias_ref[hk], sink_ref[hk])
        for g in range(GRP):
            heads[hk * GRP + g] = o[g * nr:(g + 1) * nr]
    o_all = jnp.concatenate(heads, axis=1)
    mix = _dot(o_all.astype(BF16), wo_ref[...])
    xn_ref[...] = _layer_norm(ALPHA * x + g1p * mix, g_ref[...], b_ref[...])


def _attn_sample(x, mod, wq, wo, kbuf, vbuf, kn, vn, bias, sink, ln_g, ln_b, *, ns, steps):
    n = x.shape[0]
    nr = ns * steps
    nbuf = kbuf.shape[1]
    row = lambda i: (i, 0)
    return pl.pallas_call(
        functools.partial(_attn_sample_body, ns=ns, steps=steps),
        grid=(n // nr,),
        in_specs=[
            pl.BlockSpec((nr, D), row), pl.BlockSpec((nr, 3 * D), row),
            _const_spec((D, HQ * HD)), _const_spec((HQ * HD, D)),
            pl.BlockSpec((ns, nbuf, KVW), lambda i: (i, 0, 0)),
            pl.BlockSpec((ns, nbuf, KVW), lambda i: (i, 0, 0)),
            pl.BlockSpec((nr, KVW), row), pl.BlockSpec((nr, KVW), row),
            _const_spec(bias.shape), _const_spec(sink.shape),
            _const_spec((1, D)), _const_spec((1, D)),
        ],
        out_specs=pl.BlockSpec((nr, D), row),
        out_shape=jax.ShapeDtypeStruct((n, D), F32),
        compiler_params=_cp(("arbitrary",)),
        name="attn_sample",
    )(x, mod, wq, wo, kbuf, vbuf, kn, vn, bias, sink, ln_g, ln_b)


def _t5_bucket(d):
    max_exact = N_BUCKETS // 2
    d = jnp.maximum(d, 0)
    log_ratio = jnp.log(jnp.maximum(d, 1).astype(F32) / max_exact) / math.log(WIN / max_exact)
    large = max_exact + (log_ratio * (N_BUCKETS - max_exact)).astype(jnp.int32)
    large = jnp.minimum(large, N_BUCKETS - 1)
    return jnp.where(d < max_exact, d, large)


def _bias_table(d, ok, rel_bias):
    tab = jnp.concatenate([rel_bias[_t5_bucket(jnp.arange(WIN))].astype(F32),
                           jnp.full((1, HQ), NEG, F32)], axis=0)
    sel = jax.nn.one_hot(jnp.where(ok, d, WIN), WIN + 1, dtype=F32)
    return jnp.einsum('abd,dh->hab', sel, tab, precision=lax.Precision.HIGHEST)


def _prompt_bias(rel_bias):
    d = jnp.arange(WIN)[None, :] + WIN - jnp.arange(2 * WIN)[:, None]
    b = _bias_table(d, (d >= 0) & (d < WIN), rel_bias).reshape(HKV, GRP, 2 * WIN, WIN)
    return jnp.transpose(b, (0, 2, 1, 3)).reshape(HKV, 2 * WIN, GRP * WIN)


def _sample_bias(rel_bias, ns, steps, nbuf):
    qs = jnp.repeat(jnp.arange(ns), steps)
    qt = jnp.tile(jnp.arange(steps), ns)
    ks = jnp.concatenate([jnp.repeat(jnp.arange(ns), nbuf), jnp.repeat(jnp.arange(ns), steps)])
    kpos = jnp.concatenate([jnp.tile(jnp.arange(nbuf), ns), nbuf + jnp.tile(jnp.arange(steps), ns)])
    d = qt[:, None] + nbuf - kpos[None, :]
    ok = (d >= 0) & (d < WIN) & (qs[:, None] == ks[None, :])
    return _bias_table(d, ok, rel_bias).reshape(HKV, GRP * ns * steps, kpos.shape[0])


def _sink_table(sinks, rows):
    return jnp.repeat(sinks.reshape(HKV, GRP), rows, axis=1)[:, :, None].astype(F32)


MOE_TM = 1024
LRU_TM = 256
ATTN_TQ = 256
SAMPLE_NS = 8


def kernel(x_prompt, x_sample, state_rnn_h, state_rnn_conv, cache_win_k, cache_win_v, c_prompt, c_sample, ada_w, ada_b, ln_g, ln_b, lru_w_in, lru_conv_w, lru_conv_b, lru_w_a, lru_b_a, lru_w_x, lru_b_x, lru_lambda, lru_w_out, kv_w, attn_w_q, attn_sinks, attn_w_o, rel_bias, moe_w_group, moe_b_group, moe_w_router, moe_b_router, moe_w1, moe_w3, moe_w2):
    nbp, seq, _ = x_prompt.shape
    nbs, steps, _ = x_sample.shape
    nbuf = cache_win_k.shape[1]
    npt = nbp * seq
    nst = nbs * steps

    mods = _adaln(jnp.concatenate([c_prompt, c_sample], axis=0), ada_w, ada_b)
    mod_p = mods[:, :nbp].reshape(DEPTH, 2, nbp, 1, 3 * D)
    mod_s = mods[:, nbp:].reshape(DEPTH, 2, nbs, 3 * D)

    def lru_weights(l):
        return (lru_w_in[l].astype(BF16), lru_conv_w[l], lru_conv_b[l][None], lru_w_a[l].astype(BF16),
                lru_w_x[l].astype(BF16), lru_b_a[l][None], lru_b_x[l][None], lru_lambda[l][None],
                lru_w_out[l].astype(BF16))

    def moe_weights(l):
        wr = jnp.concatenate([moe_w_group[l], moe_w_router[l].reshape(D, N_EXP)], axis=1)
        wr = jnp.pad(wr, ((0, 0), (0, RLANES - wr.shape[1]))).astype(BF16)
        br = jnp.concatenate([moe_b_group[l], moe_b_router[l].reshape(N_EXP)])
        br = jnp.pad(br, (0, RLANES - br.shape[0]))[None]
        return wr, br, moe_w1[l].astype(BF16), moe_w3[l].astype(BF16), moe_w2[l].astype(BF16)

    kvw = kv_w.astype(BF16)
    kvwt = kvw[:, KVW:].T
    bias_p = _prompt_bias(rel_bias)
    bias_s = _sample_bias(rel_bias, SAMPLE_NS, steps, nbuf)

    xp = x_prompt.reshape(npt, D)
    xs = jnp.transpose(x_sample, (1, 0, 2)).reshape(nst, D)
    hp, cp_, hs, cs = [], [], [], []
    kp = vp = kn = vn = kp16 = vpt16 = None
    for l in range(DEPTH):
        g0, b0 = ln_g[l, 0][None], ln_b[l, 0][None]
        g1, b1 = ln_g[l, 1][None], ln_b[l, 1][None]
        if l < N_A:
            lw = lru_weights(l)
            xp, h_l, c_l = _lru_prompt(xp, mod_p[l, 0], lw, g0, b0, nb=nbp, seq=seq, tm=LRU_TM)
            hp.append(h_l[:, 0])
            cp_.append(c_l)
            c0 = jnp.transpose(state_rnn_conv[l], (1, 0, 2)).reshape((CONV_W - 1) * nbs, R)
            xs, h_l, c_l = _lru_sample(xs, jnp.tile(mod_s[l, 0], (steps, 1)), state_rnn_h[l], c0,
                                       lw, g0, b0, nb=nbs, steps=steps)
            hs.append(h_l)
            cs.append(jnp.transpose(c_l.reshape(CONV_W - 1, nbs, R), (1, 0, 2)))
            mod_s1 = jnp.tile(mod_s[l, 1], (steps, 1))
        else:
            j = l - N_A
            wq, wo = attn_w_q[j].astype(BF16), attn_w_o[j].astype(BF16)
            sink_p = jnp.repeat(attn_sinks[j].reshape(HKV, GRP), WIN, axis=1)[:, None, :]
            xp = _attn_prompt(xp, mod_p[l, 0], wq.T, wo, kp16, vpt16, bias_p, sink_p,
                              g0, b0, nb=nbp, seq=seq, tq=ATTN_TQ)
            xs = _attn_sample(xs, jnp.repeat(mod_s[l, 0], steps, axis=0), wq, wo,
                              cache_win_k.reshape(nbs, nbuf, KVW), cache_win_v.reshape(nbs, nbuf, KVW),
                              kn, vn, bias_s, _sink_table(attn_sinks[j], SAMPLE_NS * steps),
                              g0, b0, ns=SAMPLE_NS, steps=steps)
            mod_s1 = jnp.repeat(mod_s[l, 1], steps, axis=0)
        mw = moe_weights(l)
        xp = _moe(xp, mod_p[l, 1], *mw, g1, b1, tm=MOE_TM, tiles_per_mod=seq // MOE_TM)
        xs = _moe(xs, mod_s1[None], *mw, g1, b1, tm=nst, tiles_per_mod=1)
        if l == N_A - 1:
            xs = jnp.transpose(xs.reshape(steps, nbs, D), (1, 0, 2)).reshape(nst, D)
            kp, vp, kp16, vpt16 = _kv(xp, kvw, kvwt, tm=1024)
            kn, vn, _, _ = _kv(xs, kvw, kvwt, tm=nst)

    lp = min(WIN, seq)
    k_p = kp.reshape(nbp, seq, HKV, HD)[:, seq - lp:]
    v_p = vp.reshape(nbp, seq, HKV, HD)[:, seq - lp:]
    k_s = jnp.concatenate([cache_win_k, kn.reshape(nbs, steps, HKV, HD)], axis=1)[:, -nbuf:]
    v_s = jnp.concatenate([cache_win_v, vn.reshape(nbs, steps, HKV, HD)], axis=1)[:, -nbuf:]
    return (xp.reshape(nbp, seq, D), xs.reshape(nbs, steps, D),
            jnp.stack(hp), jnp.stack(cp_), k_p, v_p,
            jnp.stack(hs), jnp.stack(cs), k_s, v_s)
```

```python
import functools
import math

import jax
import jax.numpy as jnp
from jax import lax
from jax.experimental import pallas as pl
from jax.experimental.pallas import tpu as pltpu

D = 1024
R = 1024
DEPTH = 4
N_A = 2
N_BLK = 4
BLK = R // N_BLK
CONV_W = 4
RG_C = 8.0
HQ = 16
HKV = 4
HD = 64
GRP = HQ // HKV
KVW = HKV * HD
WIN = 128
N_BUCKETS = 32
N_GROUPS = 4
E_PER_G = 4
N_EXP = 16
F = 512
ALPHA = (2.0 * DEPTH) ** 0.25
LN_EPS = 1e-5
NEG = -1e30
RLANES = 128
ROW_TILE = 8

F32 = jnp.float32
BF16 = jnp.bfloat16

VMEM_LIMIT = 56 * 1024 * 1024


def _cp(sem):
    return pltpu.CompilerParams(dimension_semantics=sem, vmem_limit_bytes=VMEM_LIMIT)


def _dot(a, b):
    return jnp.dot(a, b, preferred_element_type=F32)


def _layer_norm(z, g, b):
    mu = jnp.mean(z, axis=-1, keepdims=True)
    zc = z - mu
    var = jnp.mean(zc * zc, axis=-1, keepdims=True)
    return zc * lax.rsqrt(var + LN_EPS) * g + b


def _split_mod(mod_ref):
    return mod_ref[:, 0:D], mod_ref[:, D:2 * D], mod_ref[:, 2 * D:3 * D]


def _adaln_body(c_ref, w_ref, b_ref, o_ref):
    j = pl.program_id(1)
    acc = _dot(c_ref[...].astype(BF16), w_ref[...].astype(BF16)) + b_ref[...]
    o_ref[...] = acc + jnp.where(j > 0, 1.0, 0.0).astype(F32)


def _adaln(c_all, ada_w, ada_b):
    nc = c_all.shape[0]
    w = ada_w.reshape(2 * DEPTH, D, 3 * D)
    b = ada_b.reshape(2 * DEPTH, 1, 3 * D)
    return pl.pallas_call(
        _adaln_body,
        grid=(2 * DEPTH, 3),
        in_specs=[
            pl.BlockSpec((nc, D), lambda l, j: (0, 0)),
            pl.BlockSpec((None, D, D), lambda l, j: (l, 0, j)),
            pl.BlockSpec((None, 1, D), lambda l, j: (l, 0, j)),
        ],
        out_specs=pl.BlockSpec((None, nc, D), lambda l, j: (l, 0, j)),
        out_shape=jax.ShapeDtypeStruct((2 * DEPTH, nc, 3 * D), F32),
        compiler_params=_cp(("arbitrary", "arbitrary")),
        name="adaln",
    )(c_all, w, b)


def _lru_gates(xc, n, wa_ref, wx_ref, ba_ref, bx_ref, lam_ref):
    cols = slice(n * BLK, (n + 1) * BLK)
    xb = xc.astype(BF16)
    r = jax.nn.sigmoid(_dot(xb, wa_ref[n]) + ba_ref[:, cols])
    i = jax.nn.sigmoid(_dot(xb, wx_ref[n]) + bx_ref[:, cols])
    z = -lam_ref[:, cols]
    softplus = jnp.maximum(z, 0.0) + jnp.log1p(jnp.exp(-jnp.abs(z)))
    log_a = (-RG_C) * r * softplus
    a = jnp.exp(log_a)
    b = jnp.sqrt(1.0 - a * a) * i * xc
    return a, b


def _lru_prompt_body(x_ref, mod_ref, win_ref, cw_ref, cb_ref, wa_ref, wx_ref, ba_ref, bx_ref,
                     lam_ref, wout_ref, g_ref, b_ref,
                     xn_ref, hl_ref, cl_ref,
                     xbuf, abuf, hbuf, hc, *, tm):
    t = pl.program_id(1)

    @pl.when(t == 0)
    def _():
        xbuf[0:ROW_TILE, :] = jnp.zeros((ROW_TILE, R), F32)
        hc[...] = jnp.zeros((1, R), F32)

    x = x_ref[...]
    shift, sc1p, g1p = _split_mod(mod_ref)
    u = x * sc1p + shift
    yx = _dot(u.astype(BF16), win_ref[...])
    y = jax.nn.gelu(yx[:, :R])
    xbuf[ROW_TILE:ROW_TILE + tm, :] = yx[:, R:]

    rowmod = lax.broadcasted_iota(jnp.int32, (tm, BLK), 0) & (ROW_TILE - 1)
    for n in range(N_BLK):
        cols = slice(n * BLK, (n + 1) * BLK)
        xc = cb_ref[:, cols] + cw_ref[3:4, cols] * xbuf[ROW_TILE:ROW_TILE + tm, cols]
        for k in range(1, CONV_W):
            xc = xc + cw_ref[3 - k:4 - k, cols] * xbuf[ROW_TILE - k:ROW_TILE - k + tm, cols]
        a, b = _lru_gates(xc, n, wa_ref, wx_ref, ba_ref, bx_ref, lam_ref)
        s = 1
        while s < ROW_TILE:
            a_sh = pltpu.roll(a, s, 0)
            b_sh = pltpu.roll(b, s, 0)
            m = rowmod >= s
            b = jnp.where(m, a * b_sh, 0.0) + b
            a = jnp.where(m, a * a_sh, a)
            s *= 2
        abuf[:, cols] = a
        hbuf[:, cols] = b

    def group(g, h):
        r0 = pl.multiple_of(g * ROW_TILE, ROW_TILE)
        hg = abuf[pl.ds(r0, ROW_TILE), :] * h + hbuf[pl.ds(r0, ROW_TILE), :]
        hbuf[pl.ds(r0, ROW_TILE), :] = hg
        return hg[ROW_TILE - 1:ROW_TILE, :]

    h_end = lax.fori_loop(0, tm // ROW_TILE, group, hc[...])
    hc[...] = h_end
    hl_ref[...] = h_end
    tail = xbuf[tm + ROW_TILE - (CONV_W - 1):tm + ROW_TILE, :]
    cl_ref[...] = tail
    xbuf[ROW_TILE - (CONV_W - 1):ROW_TILE, :] = tail

    mix = _dot((y * hbuf[...]).astype(BF16), wout_ref[...])
    xn_ref[...] = _layer_norm(ALPHA * x + g1p * mix, g_ref[...], b_ref[...])


def _const_spec(shape):
    nd = len(shape)
    return pl.BlockSpec(shape, lambda *_: (0,) * nd)


def _lru_weight_specs():
    return [
        _const_spec((D, 2 * R)), _const_spec((CONV_W, R)), _const_spec((1, R)),
        _const_spec((N_BLK, BLK, BLK)), _const_spec((N_BLK, BLK, BLK)),
        _const_spec((1, R)), _const_spec((1, R)), _const_spec((1, R)),
        _const_spec((R, D)), _const_spec((1, D)), _const_spec((1, D)),
    ]


def _lru_prompt(x, mod, lw, ln_g, ln_b, *, nb, seq, tm):
    nt = seq // tm
    n = nb * seq
    return pl.pallas_call(
        functools.partial(_lru_prompt_body, tm=tm),
        grid=(nb, nt),
        in_specs=[
            pl.BlockSpec((tm, D), lambda b, t: (b * nt + t, 0)),
            pl.BlockSpec((None, 1, 3 * D), lambda b, t: (b, 0, 0)),
        ] + _lru_weight_specs(),
        out_specs=[
            pl.BlockSpec((tm, D), lambda b, t: (b * nt + t, 0)),
            pl.BlockSpec((None, 1, R), lambda b, t: (b, 0, 0)),
            pl.BlockSpec((None, CONV_W - 1, R), lambda b, t: (b, 0, 0)),
        ],
        out_shape=[
            jax.ShapeDtypeStruct((n, D), F32),
            jax.ShapeDtypeStruct((nb, 1, R), F32),
            jax.ShapeDtypeStruct((nb, CONV_W - 1, R), F32),
        ],
        scratch_shapes=[
            pltpu.VMEM((tm + ROW_TILE, R), F32),
            pltpu.VMEM((tm, R), F32),
            pltpu.VMEM((tm, R), F32),
            pltpu.VMEM((1, R), F32),
        ],
        compiler_params=_cp(("arbitrary", "arbitrary")),
        name="lru_prompt",
    )(x, mod, *lw, ln_g, ln_b)


def _lru_sample_body(x_ref, mod_ref, h0_ref, c0_ref, win_ref, cw_ref, cb_ref, wa_ref, wx_ref,
                     ba_ref, bx_ref, lam_ref, wout_ref, g_ref, b_ref,
                     xn_ref, hl_ref, cl_ref, hbuf, *, nb, steps):
    x = x_ref[...]
    shift, sc1p, g1p = _split_mod(mod_ref)
    u = x * sc1p + shift
    yx = _dot(u.astype(BF16), win_ref[...])
    y = jax.nn.gelu(yx[:, :R])
    xext = jnp.concatenate([c0_ref[...], yx[:, R:]], axis=0)
    cl_ref[...] = xext[steps * nb:, :]
    for n in range(N_BLK):
        cols = slice(n * BLK, (n + 1) * BLK)
        xc = cb_ref[:, cols] + cw_ref[0:1, cols] * xext[0:steps * nb, cols]
        for k in range(1, CONV_W):
            xc = xc + cw_ref[k:k + 1, cols] * xext[k * nb:(k + steps) * nb, cols]
        a, b = _lru_gates(xc, n, wa_ref, wx_ref, ba_ref, bx_ref, lam_ref)
        h = h0_ref[:, cols]
        for t in range(steps):
            h = a[t * nb:(t + 1) * nb, :] * h + b[t * nb:(t + 1) * nb, :]
            hbuf[t * nb:(t + 1) * nb, cols] = h
        hl_ref[:, cols] = h
    mix = _dot((y * hbuf[...]).astype(BF16), wout_ref[...])
    xn_ref[...] = _layer_norm(ALPHA * x + g1p * mix, g_ref[...], b_ref[...])


def _lru_sample(x, mod, h0, c0, lw, ln_g, ln_b, *, nb, steps):
    n = nb * steps
    return pl.pallas_call(
        functools.partial(_lru_sample_body, nb=nb, steps=steps),
        grid=(1,),
        in_specs=[
            _const_spec((n, D)), _const_spec((n, 3 * D)), _const_spec((nb, R)),
            _const_spec(((CONV_W - 1) * nb, R)),
        ] + _lru_weight_specs(),
        out_specs=[_const_spec((n, D)), _const_spec((nb, R)), _const_spec(((CONV_W - 1) * nb, R))],
        out_shape=[
            jax.ShapeDtypeStruct((n, D), F32),
            jax.ShapeDtypeStruct((nb, R), F32),
            jax.ShapeDtypeStruct(((CONV_W - 1) * nb, R), F32),
        ],
        scratch_shapes=[pltpu.VMEM((n, R), F32)],
        compiler_params=_cp(("arbitrary",)),
        name="lru_sample",
    )(x, mod, h0, c0, *lw, ln_g, ln_b)


def _top2(logits):
    col = lax.broadcasted_iota(jnp.int32, logits.shape, 1).astype(F32)
    big = float(RLANES)
    is_g = col < N_GROUPS
    gl = jnp.where(is_g, logits, NEG)
    gmax = jnp.max(gl, axis=-1, keepdims=True)
    gidx = jnp.min(jnp.where(gl == gmax, col, big), axis=-1, keepdims=True)
    gsum = jnp.sum(jnp.where(is_g, jnp.exp(gl - gmax), 0.0), axis=-1, keepdims=True)
    g_p = 1.0 / gsum
    lo = N_GROUPS + gidx * E_PER_G
    in_grp = (col >= lo) & (col < lo + E_PER_G)
    el = jnp.where(in_grp, logits, NEG)
    t1 = jnp.max(el, axis=-1, keepdims=True)
    i1 = jnp.min(jnp.where(el == t1, col, big), axis=-1, keepdims=True)
    el2 = jnp.where(col == i1, NEG, el)
    t2 = jnp.max(el2, axis=-1, keepdims=True)
    i2 = jnp.min(jnp.where((el2 == t2) & in_grp & (col != i1), col, big), axis=-1, keepdims=True)
    e21 = jnp.exp(t2 - t1)
    w1 = g_p / (1.0 + e21)
    w2 = w1 * e21
    return col, gidx, i1, i2, w1, w2


def _route(logits):
    col, _, i1, i2, w1, w2 = _top2(logits)
    return jnp.where(col == i1, w1, 0.0) + jnp.where(col == i2, w2, 0.0)


PAIRS = E_PER_G * (E_PER_G - 1) // 2
NCLS = N_GROUPS * PAIRS
XW = D + RLANES
FFN_TM = 256
ROUTE_TM = 512
DISPATCH_TD = 2048
COMBINE_TC = 256
DMA_UNROLL = 8


def _route_sort_body(x_ref, mod_ref, wr_ref, br_ref, urow_ref, cnt_ref, carry):
    i = pl.program_id(0)

    @pl.when(i == 0)
    def _():
        carry[...] = jnp.zeros(carry.shape, F32)

    tm = x_ref.shape[0]
    shift, sc1p, _ = _split_mod(mod_ref)
    u = x_ref[...] * sc1p + shift
    col, gidx, i1, i2, w1, w2 = _top2(_dot(u.astype(BF16), wr_ref[...]) + br_ref[...])
    first = N_GROUPS + gidx * E_PER_G
    lo = jnp.minimum(i1, i2) - first
    hi = jnp.maximum(i1, i2) - first
    w_lo = jnp.where(i1 < i2, w1, w2)
    w_hi = jnp.where(i1 < i2, w2, w1)
    cls = gidx * PAIRS + lo * (2 * E_PER_G - 1 - lo) * 0.5 + (hi - lo - 1.0)
    onehot = (col == cls).astype(F32)
    r = lax.broadcasted_iota(jnp.int32, (tm, tm), 0)
    c = lax.broadcasted_iota(jnp.int32, (tm, tm), 1)
    earlier = _dot((c < r).astype(BF16), onehot.astype(BF16)) + carry[...]
    rank = jnp.sum(onehot * earlier, axis=-1, keepdims=True)
    carry[...] += jnp.sum(onehot, axis=0, keepdims=True)
    cnt_ref[...] = carry[...]
    urow_ref[:, 0:D] = u
    urow_ref[:, D:XW] = (jnp.where(col == 0, w_lo, 0.0) + jnp.where(col == 1, w_hi, 0.0)
                         + jnp.where(col == 2, cls, 0.0) + jnp.where(col == 3, rank, 0.0))


def _route_sort(x, mod, wr, br, *, tiles_per_mod):
    n = x.shape[0]
    mr = mod.shape[1]
    return pl.pallas_call(
        _route_sort_body,
        grid=(n // ROUTE_TM,),
        in_specs=[
            pl.BlockSpec((ROUTE_TM, D), lambda i: (i, 0)),
            pl.BlockSpec((None, mr, 3 * D), lambda i: (i // tiles_per_mod, 0, 0)),
            _const_spec((D, RLANES)), _const_spec((1, RLANES)),
        ],
        out_specs=[pl.BlockSpec((ROUTE_TM, XW), lambda i: (i, 0)), _const_spec((1, RLANES))],
        out_shape=[jax.ShapeDtypeStruct((n, XW), F32), jax.ShapeDtypeStruct((1, RLANES), F32)],
        scratch_shapes=[pltpu.VMEM((1, RLANES), F32)],
        compiler_params=_cp(("arbitrary",)),
        name="moe_route",
    )(x, mod, wr, br)


def _row_copy(src, src_row, dst, dst_row, sem):
    return pltpu.make_async_copy(src.at[pl.ds(src_row, 1)], dst.at[pl.ds(dst_row, 1)], sem)


def _dispatch_body(pos_ref, urow_hbm, xs_in, xs_hbm, sem):
    del xs_in
    base = pl.program_id(0) * DISPATCH_TD

    def issue(r, carry):
        _row_copy(urow_hbm, base + r, xs_hbm, pos_ref[base + r], sem).start()
        return carry

    lax.fori_loop(0, DISPATCH_TD, issue, 0, unroll=DMA_UNROLL)
    pltpu.make_async_copy(urow_hbm.at[pl.ds(0, DISPATCH_TD)], xs_hbm.at[pl.ds(0, DISPATCH_TD)], sem).wait()


def _dispatch(pos, urow, xs_zero):
    n = urow.shape[0]
    any_spec = pl.BlockSpec(memory_space=pl.ANY)
    return pl.pallas_call(
        _dispatch_body,
        grid_spec=pltpu.PrefetchScalarGridSpec(
            num_scalar_prefetch=1, grid=(n // DISPATCH_TD,),
            in_specs=[any_spec, any_spec], out_specs=any_spec,
            scratch_shapes=[pltpu.SemaphoreType.DMA(())]),
        out_shape=jax.ShapeDtypeStruct(xs_zero.shape, F32),
        input_output_aliases={2: 0},
        compiler_params=_cp(("arbitrary",)),
        name="moe_dispatch",
    )(pos, urow, xs_zero)


def _ffn_sorted_body(ta_ref, tb_ref, na_ref, xs_ref, w1a, w3a, w2a, w1b, w3b, w2b, ys_ref):
    del ta_ref, tb_ref
    active = pl.program_id(0) < na_ref[0]

    @pl.when(active)
    def _():
        x = xs_ref[:, 0:D].astype(BF16)
        ha = jax.nn.silu(_dot(x, w1a[...])) * _dot(x, w3a[...]) * xs_ref[:, D:D + 1]
        hb = jax.nn.silu(_dot(x, w1b[...])) * _dot(x, w3b[...]) * xs_ref[:, D + 1:D + 2]
        ys_ref[...] = _dot(ha.astype(BF16), w2a[...]) + _dot(hb.astype(BF16), w2b[...])

    @pl.when(jnp.logical_not(active))
    def _():
        ys_ref[...] = jnp.zeros(ys_ref.shape, F32)


def _ffn_sorted(ta, tb, na, xs, w1, w3, w2):
    nt = ta.shape[0]
    ea = lambda t, ta, tb, na: (ta[t], 0, 0)
    eb = lambda t, ta, tb, na: (tb[t], 0, 0)
    rows = lambda t, ta, tb, na: (t, 0)
    return pl.pallas_call(
        _ffn_sorted_body,
        grid_spec=pltpu.PrefetchScalarGridSpec(
            num_scalar_prefetch=3, grid=(nt,),
            in_specs=[
                pl.BlockSpec((FFN_TM, XW), rows),
                pl.BlockSpec((None, D, F), ea), pl.BlockSpec((None, D, F), ea), pl.BlockSpec((None, F, D), ea),
                pl.BlockSpec((None, D, F), eb), pl.BlockSpec((None, D, F), eb), pl.BlockSpec((None, F, D), eb),
            ],
            out_specs=pl.BlockSpec((FFN_TM, D), rows)),
        out_shape=jax.ShapeDtypeStruct((nt * FFN_TM, D), F32),
        compiler_params=_cp(("arbitrary",)),
        name="moe_ffn",
    )(ta, tb, na, xs, w1, w3, w2, w1, w3, w2)


def _combine_body(pos_ref, x_ref, mod_ref, ys_hbm, g_ref, b_ref, xn_ref, buf, sem):
    i = pl.program_id(0)
    nsteps = pl.num_programs(0)
    tc = x_ref.shape[0]

    def gather(step, slot):
        def issue(r, carry):
            _row_copy(ys_hbm, pos_ref[step * tc + r], buf.at[slot], r, sem.at[slot]).start()
            return carry
        lax.fori_loop(0, tc, issue, 0, unroll=DMA_UNROLL)

    @pl.when(i == 0)
    def _():
        gather(0, 0)

    @pl.when(i + 1 < nsteps)
    def _():
        gather(i + 1, (i + 1) % 2)

    slot = i % 2
    pltpu.make_async_copy(ys_hbm.at[pl.ds(0, tc)], buf.at[slot], sem.at[slot]).wait()
    g1p = mod_ref[:, 2 * D:3 * D]
    xn_ref[...] = _layer_norm(ALPHA * x_ref[...] + g1p * buf[slot], g_ref[...], b_ref[...])


def _combine(pos, x, mod, ys, ln_g, ln_b, *, tiles_per_mod):
    n = x.shape[0]
    mr = mod.shape[1]
    tc = COMBINE_TC
    tpm = tiles_per_mod * (ROUTE_TM // tc)
    return pl.pallas_call(
        _combine_body,
        grid_spec=pltpu.PrefetchScalarGridSpec(
            num_scalar_prefetch=1, grid=(n // tc,),
            in_specs=[
                pl.BlockSpec((tc, D), lambda i, pos: (i, 0)),
                pl.BlockSpec((None, mr, 3 * D), lambda i, pos: (i // tpm, 0, 0)),
                pl.BlockSpec(memory_space=pl.ANY),
                pl.BlockSpec((1, D), lambda i, pos: (0, 0)), pl.BlockSpec((1, D), lambda i, pos: (0, 0)),
            ],
            out_specs=pl.BlockSpec((tc, D), lambda i, pos: (i, 0)),
            scratch_shapes=[pltpu.VMEM((2, tc, D), F32), pltpu.SemaphoreType.DMA((2,))]),
        out_shape=jax.ShapeDtypeStruct((n, D), F32),
        compiler_params=_cp(("arbitrary",)),
        name="moe_combine",
    )(pos, x, mod, ys, ln_g, ln_b)


def _moe_sparse(x, mod, wr, br, w1, w3, w2, ln_g, ln_b, *, tiles_per_mod):
    n = x.shape[0]
    urow, cnt = _route_sort(x, mod, wr, br, tiles_per_mod=tiles_per_mod)
    cls = urow[:, D + 2].astype(jnp.int32)
    rank = urow[:, D + 3].astype(jnp.int32)
    ntile = (cnt[0, :NCLS].astype(jnp.int32) + FFN_TM - 1) // FFN_TM
    tile_end = jnp.cumsum(ntile)
    row_start = (tile_end - ntile) * FFN_TM
    pos = rank + jnp.sum(jax.nn.one_hot(cls, NCLS, dtype=jnp.int32) * row_start[None, :], axis=1)
    nt = n // FFN_TM + NCLS
    total = tile_end[NCLS - 1]
    tr = jnp.minimum(jnp.arange(nt, dtype=jnp.int32), total - 1)
    tcls = jnp.sum((tr[:, None] >= tile_end[None, :]).astype(jnp.int32), axis=1)
    grp, pair = tcls // PAIRS, tcls % PAIRS
    lo = (pair >= 3).astype(jnp.int32) + (pair >= 5).astype(jnp.int32)
    hi = pair + 1 - lo * (2 * E_PER_G - 3 - lo) // 2
    ta = grp * E_PER_G + lo
    tb = grp * E_PER_G + hi
    xs = _dispatch(pos, urow, jnp.zeros((nt * FFN_TM, XW), F32))
    ys = _ffn_sorted(ta, tb, total[None], xs, w1, w3, w2)
    return _combine(pos, x, mod, ys, ln_g, ln_b, tiles_per_mod=tiles_per_mod)


def _moe_body(x_ref, mod_ref, wr_ref, br_ref, w1_ref, w3_ref, w2_ref, g_ref, b_ref,
              xn_ref, ub, comb, acc):
    e = pl.program_id(1)

    @pl.when(e == 0)
    def _():
        shift, sc1p, _ = _split_mod(mod_ref)
        u = (x_ref[...] * sc1p + shift).astype(BF16)
        ub[...] = u
        comb[...] = _route(_dot(u, wr_ref[...]) + br_ref[...])
        acc[...] = jnp.zeros(acc.shape, F32)

    u = ub[...]
    col = lax.broadcasted_iota(jnp.int32, comb.shape, 1)
    ce = jnp.sum(jnp.where(col == e + N_GROUPS, comb[...], 0.0), axis=-1, keepdims=True)
    h = jax.nn.silu(_dot(u, w1_ref[...])) * _dot(u, w3_ref[...]) * ce
    acc[...] += _dot(h.astype(BF16), w2_ref[...])

    @pl.when(e == N_EXP - 1)
    def _():
        g1p = mod_ref[:, 2 * D:3 * D]
        xn_ref[...] = _layer_norm(ALPHA * x_ref[...] + g1p * acc[...], g_ref[...], b_ref[...])


def _moe(x, mod, wr, br, w1, w3, w2, ln_g, ln_b, *, tm, tiles_per_mod):
    n = x.shape[0]
    mr = mod.shape[1]
    return pl.pallas_call(
        _moe_body,
        grid=(n // tm, N_EXP),
        in_specs=[
            pl.BlockSpec((tm, D), lambda i, e: (i, 0)),
            pl.BlockSpec((None, mr, 3 * D), lambda i, e: (i // tiles_per_mod, 0, 0)),
            pl.BlockSpec((D, RLANES), lambda i, e: (0, 0)),
            pl.BlockSpec((1, RLANES), lambda i, e: (0, 0)),
            pl.BlockSpec((None, D, F), lambda i, e: (e, 0, 0)),
            pl.BlockSpec((None, D, F), lambda i, e: (e, 0, 0)),
            pl.BlockSpec((None, F, D), lambda i, e: (e, 0, 0)),
            pl.BlockSpec((1, D), lambda i, e: (0, 0)),
            pl.BlockSpec((1, D), lambda i, e: (0, 0)),
        ],
        out_specs=pl.BlockSpec((tm, D), lambda i, e: (i, 0)),
        out_shape=jax.ShapeDtypeStruct((n, D), F32),
        scratch_shapes=[
            pltpu.VMEM((tm, D), BF16),
            pltpu.VMEM((tm, RLANES), F32),
            pltpu.VMEM((tm, D), F32),
        ],
        compiler_params=_cp(("arbitrary", "arbitrary")),
        name="moe",
    )(x, mod, wr, br, w1, w3, w2, ln_g, ln_b)


_NT = (((1,), (1,)), ((), ()))
_TN = (((0,), (0,)), ((), ()))


def _kv_body(x_ref, w_ref, wvt_ref, k_ref, v_ref, k16_ref, vt16_ref):
    xb = x_ref[...].astype(BF16)
    kv = _dot(xb, w_ref[...])
    k_ref[...] = kv[:, :KVW]
    v_ref[...] = kv[:, KVW:]
    k16_ref[...] = kv[:, :KVW].astype(BF16)
    vt16_ref[...] = lax.dot_general(wvt_ref[...], xb, _NT, preferred_element_type=F32).astype(BF16)


def _kv(x, w, wvt, *, tm):
    n = x.shape[0]
    row = lambda i: (i, 0)
    return pl.pallas_call(
        _kv_body,
        grid=(n // tm,),
        in_specs=[pl.BlockSpec((tm, D), row), _const_spec((D, 2 * KVW)), _const_spec((KVW, D))],
        out_specs=[pl.BlockSpec((tm, KVW), row), pl.BlockSpec((tm, KVW), row),
                   pl.BlockSpec((tm, KVW), row), pl.BlockSpec((KVW, tm), lambda i: (0, i))],
        out_shape=[jax.ShapeDtypeStruct((n, KVW), F32), jax.ShapeDtypeStruct((n, KVW), F32),
                   jax.ShapeDtypeStruct((n, KVW), BF16), jax.ShapeDtypeStruct((KVW, n), BF16)],
        compiler_params=_cp(("arbitrary",)),
        name="kv_proj",
    )(x, w, wvt)


def _sink_attend(qh, kh, vh, bias, sink):
    s = lax.dot_general(qh, kh, (((1,), (1,)), ((), ())), preferred_element_type=F32) + bias
    m = jnp.maximum(jnp.max(s, axis=-1, keepdims=True), sink)
    p = jnp.exp(s - m)
    den = jnp.sum(p, axis=-1, keepdims=True) + jnp.exp(sink - m)
    return _dot(p.astype(BF16), vh) / den


def _attn_prompt_body(x_ref, mod_ref, wqt_ref, wo_ref, kc_ref, kp_ref, vtc_ref, vtp_ref,
                      bias_ref, sink_ref, g_ref, b_ref, xn_ref, *, tq):
    t = pl.program_id(1)
    x = x_ref[...]
    shift, sc1p, g1p = _split_mod(mod_ref)
    u = (x * sc1p + shift).astype(BF16)
    qt = lax.dot_general(wqt_ref[...], u, _NT, preferred_element_type=F32)
    qt = (qt * (HD ** -0.5)).astype(BF16)
    kc = kc_ref[...]
    vtc = vtc_ref[...]
    krow = lax.broadcasted_iota(jnp.int32, (2 * WIN, GRP * WIN), 0)
    first = jnp.where((krow < WIN) & (t == 0), NEG, 0.0).astype(F32)
    blocks = []
    for j in range(tq // WIN):
        qcols = slice(j * WIN, (j + 1) * WIN)
        if j == 0:
            kk = jnp.concatenate([kp_ref[...], kc[0:WIN]], axis=0)
            vvt = jnp.concatenate([vtp_ref[...], vtc[:, 0:WIN]], axis=1)
        else:
            kk = kc[(j - 1) * WIN:(j + 1) * WIN]
            vvt = vtc[:, (j - 1) * WIN:(j + 1) * WIN]
        heads = []
        for hk in range(HKV):
            hrows = slice(hk * HD, (hk + 1) * HD)
            qht = jnp.concatenate(
                [qt[(hk * GRP + g) * HD:(hk * GRP + g + 1) * HD, qcols] for g in range(GRP)], axis=1)
            st = _dot(kk[:, hrows], qht) + bias_ref[hk]
            if j == 0:
                st = st + first
            sink = sink_ref[hk]
            m = jnp.maximum(jnp.max(st, axis=0, keepdims=True), sink)
            pt = jnp.exp(st - m)
            den = jnp.sum(pt, axis=0, keepdims=True) + jnp.exp(sink - m)
            ot = _dot(vvt[hrows, :], pt.astype(BF16)) * (1.0 / den)
            for g in range(GRP):
                heads.append(ot[:, g * WIN:(g + 1) * WIN])
        blocks.append(jnp.concatenate(heads, axis=0))
    oallt = jnp.concatenate(blocks, axis=1) if len(blocks) > 1 else blocks[0]
    mix = lax.dot_general(oallt.astype(BF16), wo_ref[...], _TN, preferred_element_type=F32)
    xn_ref[...] = _layer_norm(ALPHA * x + g1p * mix, g_ref[...], b_ref[...])


def _attn_prompt(x, mod, wqt, wo, k16, vt16, bias, sink, ln_g, ln_b, *, nb, seq, tq):
    nt = seq // tq
    wpt = tq // WIN
    n = nb * seq

    def prev(b, t):
        return jnp.maximum((b * nt + t) * wpt - 1, 0)

    cur = lambda b, t: (b * nt + t, 0)
    return pl.pallas_call(
        functools.partial(_attn_prompt_body, tq=tq),
        grid=(nb, nt),
        in_specs=[
            pl.BlockSpec((tq, D), cur),
            pl.BlockSpec((None, 1, 3 * D), lambda b, t: (b, 0, 0)),
            _const_spec((HQ * HD, D)), _const_spec((HQ * HD, D)),
            pl.BlockSpec((tq, KVW), cur), pl.BlockSpec((WIN, KVW), lambda b, t: (prev(b, t), 0)),
            pl.BlockSpec((KVW, tq), lambda b, t: (0, b * nt + t)),
            pl.BlockSpec((KVW, WIN), lambda b, t: (0, prev(b, t))),
            _const_spec((HKV, 2 * WIN, GRP * WIN)), _const_spec((HKV, 1, GRP * WIN)),
            _const_spec((1, D)), _const_spec((1, D)),
        ],
        out_specs=pl.BlockSpec((tq, D), cur),
        out_shape=jax.ShapeDtypeStruct((n, D), F32),
        compiler_params=_cp(("arbitrary", "arbitrary")),
        name="attn_prompt",
    )(x, mod, wqt, wo, k16, k16, vt16, vt16, bias, sink, ln_g, ln_b)


def _attn_sample_body(x_ref, mod_ref, wq_ref, wo_ref, kb_ref, vb_ref, kn_ref, vn_ref,
                      bias_ref, sink_ref, g_ref, b_ref, xn_ref, *, ns, steps):
    x = x_ref[...]
    shift, sc1p, g1p = _split_mod(mod_ref)
    u = (x * sc1p + shift).astype(BF16)
    q = (_dot(u, wq_ref[...]) * (HD ** -0.5)).astype(BF16)
    nbuf = kb_ref.shape[1]
    k_all = jnp.concatenate([kb_ref[...].reshape(ns * nbuf, KVW), kn_ref[...]], axis=0).astype(BF16)
    v_all = jnp.concatenate([vb_ref[...].reshape(ns * nbuf, KVW), vn_ref[...]], axis=0).astype(BF16)
    nr = ns * steps
    heads = [None] * HQ
    for hk in range(HKV):
        kcols = slice(hk * HD, (hk + 1) * HD)
        qh = jnp.concatenate(
            [q[:, (hk * GRP + g) * HD:(hk * GRP + g + 1) * HD] for g in range(GRP)], axis=0)
        o = _sink_attend(qh, k_all[:, kcols], v_all[:, kcols], bias_ref[hk], sink_ref[hk])
        for g in range(GRP):
            heads[hk * GRP + g] = o[g * nr:(g + 1) * nr]
    o_all = jnp.concatenate(heads, axis=1)
    mix = _dot(o_all.astype(BF16), wo_ref[...])
    xn_ref[...] = _layer_norm(ALPHA * x + g1p * mix, g_ref[...], b_ref[...])


def _attn_sample(x, mod, wq, wo, kbuf, vbuf, kn, vn, bias, sink, ln_g, ln_b, *, ns, steps):
    n = x.shape[0]
    nr = ns * steps
    nbuf = kbuf.shape[1]
    row = lambda i: (i, 0)
    return pl.pallas_call(
        functools.partial(_attn_sample_body, ns=ns, steps=steps),
        grid=(n // nr,),
        in_specs=[
            pl.BlockSpec((nr, D), row), pl.BlockSpec((nr, 3 * D), row),
            _const_spec((D, HQ * HD)), _const_spec((HQ * HD, D)),
            pl.BlockSpec((ns, nbuf, KVW), lambda i: (i, 0, 0)),
            pl.BlockSpec((ns, nbuf, KVW), lambda i: (i, 0, 0)),
            pl.BlockSpec((nr, KVW), row), pl.BlockSpec((nr, KVW), row),
            _const_spec(bias.shape), _const_spec(sink.shape),
            _const_spec((1, D)), _const_spec((1, D)),
        ],
        out_specs=pl.BlockSpec((nr, D), row),
        out_shape=jax.ShapeDtypeStruct((n, D), F32),
        compiler_params=_cp(("arbitrary",)),
        name="attn_sample",
    )(x, mod, wq, wo, kbuf, vbuf, kn, vn, bias, sink, ln_g, ln_b)


def _t5_bucket(d):
    max_exact = N_BUCKETS // 2
    d = jnp.maximum(d, 0)
    log_ratio = jnp.log(jnp.maximum(d, 1).astype(F32) / max_exact) / math.log(WIN / max_exact)
    large = max_exact + (log_ratio * (N_BUCKETS - max_exact)).astype(jnp.int32)
    large = jnp.minimum(large, N_BUCKETS - 1)
    return jnp.where(d < max_exact, d, large)


def _bias_table(d, ok, rel_bias):
    tab = jnp.concatenate([rel_bias[_t5_bucket(jnp.arange(WIN))].astype(F32),
                           jnp.full((1, HQ), NEG, F32)], axis=0)
    sel = jax.nn.one_hot(jnp.where(ok, d, WIN), WIN + 1, dtype=F32)
    return jnp.einsum('abd,dh->hab', sel, tab, precision=lax.Precision.HIGHEST)


def _prompt_bias(rel_bias):
    d = jnp.arange(WIN)[None, :] + WIN - jnp.arange(2 * WIN)[:, None]
    b = _bias_table(d, (d >= 0) & (d < WIN), rel_bias).reshape(HKV, GRP, 2 * WIN, WIN)
    return jnp.transpose(b, (0, 2, 1, 3)).reshape(HKV, 2 * WIN, GRP * WIN)


def _sample_bias(rel_bias, ns, steps, nbuf):
    qs = jnp.repeat(jnp.arange(ns), steps)
    qt = jnp.tile(jnp.arange(steps), ns)
    ks = jnp.concatenate([jnp.repeat(jnp.arange(ns), nbuf), jnp.repeat(jnp.arange(ns), steps)])
    kpos = jnp.concatenate([jnp.tile(jnp.arange(nbuf), ns), nbuf + jnp.tile(jnp.arange(steps), ns)])
    d = qt[:, None] + nbuf - kpos[None, :]
    ok = (d >= 0) & (d < WIN) & (qs[:, None] == ks[None, :])
    return _bias_table(d, ok, rel_bias).reshape(HKV, GRP * ns * steps, kpos.shape[0])


def _sink_table(sinks, rows):
    return jnp.repeat(sinks.reshape(HKV, GRP), rows, axis=1)[:, :, None].astype(F32)


MOE_TM = 1024
LRU_TM = 256
ATTN_TQ = 256
SAMPLE_NS = 8


def kernel(x_prompt, x_sample, state_rnn_h, state_rnn_conv, cache_win_k, cache_win_v, c_prompt, c_sample, ada_w, ada_b, ln_g, ln_b, lru_w_in, lru_conv_w, lru_conv_b, lru_w_a, lru_b_a, lru_w_x, lru_b_x, lru_lambda, lru_w_out, kv_w, attn_w_q, attn_sinks, attn_w_o, rel_bias, moe_w_group, moe_b_group, moe_w_router, moe_b_router, moe_w1, moe_w3, moe_w2):
    nbp, seq, _ = x_prompt.shape
    nbs, steps, _ = x_sample.shape
    nbuf = cache_win_k.shape[1]
    npt = nbp * seq
    nst = nbs * steps

    mods = _adaln(jnp.concatenate([c_prompt, c_sample], axis=0), ada_w, ada_b)
    mod_p = mods[:, :nbp].reshape(DEPTH, 2, nbp, 1, 3 * D)
    mod_s = mods[:, nbp:].reshape(DEPTH, 2, nbs, 3 * D)

    def lru_weights(l):
        return (lru_w_in[l].astype(BF16), lru_conv_w[l], lru_conv_b[l][None], lru_w_a[l].astype(BF16),
                lru_w_x[l].astype(BF16), lru_b_a[l][None], lru_b_x[l][None], lru_lambda[l][None],
                lru_w_out[l].astype(BF16))

    def moe_weights(l):
        wr = jnp.concatenate([moe_w_group[l], moe_w_router[l].reshape(D, N_EXP)], axis=1)
        wr = jnp.pad(wr, ((0, 0), (0, RLANES - wr.shape[1]))).astype(BF16)
        br = jnp.concatenate([moe_b_group[l], moe_b_router[l].reshape(N_EXP)])
        br = jnp.pad(br, (0, RLANES - br.shape[0]))[None]
        return wr, br, moe_w1[l].astype(BF16), moe_w3[l].astype(BF16), moe_w2[l].astype(BF16)

    kvw = kv_w.astype(BF16)
    kvwt = kvw[:, KVW:].T
    bias_p = _prompt_bias(rel_bias)
    bias_s = _sample_bias(rel_bias, SAMPLE_NS, steps, nbuf)

    xp = x_prompt.reshape(npt, D)
    xs = jnp.transpose(x_sample, (1, 0, 2)).reshape(nst, D)
    hp, cp_, hs, cs = [], [], [], []
    kp = vp = kn = vn = kp16 = vpt16 = None
    for l in range(DEPTH):
        g0, b0 = ln_g[l, 0][None], ln_b[l, 0][None]
        g1, b1 = ln_g[l, 1][None], ln_b[l, 1][None]
        if l < N_A:
            lw = lru_weights(l)
            xp, h_l, c_l = _lru_prompt(xp, mod_p[l, 0], lw, g0, b0, nb=nbp, seq=seq, tm=LRU_TM)
            hp.append(h_l[:, 0])
            cp_.append(c_l)
            c0 = jnp.transpose(state_rnn_conv[l], (1, 0, 2)).reshape((CONV_W - 1) * nbs, R)
            xs, h_l, c_l = _lru_sample(xs, jnp.tile(mod_s[l, 0], (steps, 1)), state_rnn_h[l], c0,
                                       lw, g0, b0, nb=nbs, steps=steps)
            hs.append(h_l)
            cs.append(jnp.transpose(c_l.reshape(CONV_W - 1, nbs, R), (1, 0, 2)))
            mod_s1 = jnp.tile(mod_s[l, 1], (steps, 1))
        else:
            j = l - N_A
            wq, wo = attn_w_q[j].astype(BF16), attn_w_o[j].astype(BF16)
            sink_p = jnp.repeat(attn_sinks[j].reshape(HKV, GRP), WIN, axis=1)[:, None, :]
            xp = _attn_prompt(xp, mod_p[l, 0], wq.T, wo, kp16, vpt16, bias_p, sink_p,
                              g0, b0, nb=nbp, seq=seq, tq=ATTN_TQ)
            xs = _attn_sample(xs, jnp.repeat(mod_s[l, 0], steps, axis=0), wq, wo,
                              cache_win_k.reshape(nbs, nbuf, KVW), cache_win_v.reshape(nbs, nbuf, KVW),
                              kn, vn, bias_s, _sink_table(attn_sinks[j], SAMPLE_NS * steps),
                              g0, b0, ns=SAMPLE_NS, steps=steps)
            mod_s1 = jnp.repeat(mod_s[l, 1], steps, axis=0)
        mw = moe_weights(l)
        xp = _moe_sparse(xp, mod_p[l, 1], *mw, g1, b1, tiles_per_mod=seq // ROUTE_TM)
        xs = _moe(xs, mod_s1[None], *mw, g1, b1, tm=nst, tiles_per_mod=1)
        if l == N_A - 1:
            xs = jnp.transpose(xs.reshape(steps, nbs, D), (1, 0, 2)).reshape(nst, D)
            kp, vp, kp16, vpt16 = _kv(xp, kvw, kvwt, tm=1024)
            kn, vn, _, _ = _kv(xs, kvw, kvwt, tm=nst)

    lp = min(WIN, seq)
    k_p = kp.reshape(nbp, seq, HKV, HD)[:, seq - lp:]
    v_p = vp.reshape(nbp, seq, HKV, HD)[:, seq - lp:]
    k_s = jnp.concatenate([cache_win_k, kn.reshape(nbs, steps, HKV, HD)], axis=1)[:, -nbuf:]
    v_s = jnp.concatenate([cache_win_v, vn.reshape(nbs, steps, HKV, HD)], axis=1)[:, -nbuf:]
    return (xp.reshape(nbp, seq, D), xs.reshape(nbs, steps, D),
            jnp.stack(hp), jnp.stack(cp_), k_p, v_p,
            jnp.stack(hs), jnp.stack(cs), k_s, v_s)
```

```python
import functools
import math

import jax
import jax.numpy as jnp
from jax import lax
from jax.experimental import pallas as pl
from jax.experimental.pallas import tpu as pltpu

D = 1024
R = 1024
DEPTH = 4
N_A = 2
N_BLK = 4
BLK = R // N_BLK
CONV_W = 4
RG_C = 8.0
HQ = 16
HKV = 4
HD = 64
GRP = HQ // HKV
KVW = HKV * HD
WIN = 128
N_BUCKETS = 32
N_GROUPS = 4
E_PER_G = 4
N_EXP = 16
F = 512
ALPHA = (2.0 * DEPTH) ** 0.25
LN_EPS = 1e-5
NEG = -1e30
RLANES = 128
ROW_TILE = 8

F32 = jnp.float32
BF16 = jnp.bfloat16

VMEM_LIMIT = 56 * 1024 * 1024


def _cp(sem):
    return pltpu.CompilerParams(dimension_semantics=sem, vmem_limit_bytes=VMEM_LIMIT)


def _dot(a, b):
    return jnp.dot(a, b, preferred_element_type=F32)


def _layer_norm(z, g, b):
    mu = jnp.mean(z, axis=-1, keepdims=True)
    zc = z - mu
    var = jnp.mean(zc * zc, axis=-1, keepdims=True)
    return zc * lax.rsqrt(var + LN_EPS) * g + b


def _split_mod(mod_ref):
    return mod_ref[:, 0:D], mod_ref[:, D:2 * D], mod_ref[:, 2 * D:3 * D]


def _adaln_body(c_ref, w_ref, b_ref, o_ref):
    j = pl.program_id(1)
    acc = _dot(c_ref[...].astype(BF16), w_ref[...].astype(BF16)) + b_ref[...]
    o_ref[...] = acc + jnp.where(j > 0, 1.0, 0.0).astype(F32)


def _adaln(c_all, ada_w, ada_b):
    nc = c_all.shape[0]
    w = ada_w.reshape(2 * DEPTH, D, 3 * D)
    b = ada_b.reshape(2 * DEPTH, 1, 3 * D)
    return pl.pallas_call(
        _adaln_body,
        grid=(2 * DEPTH, 3),
        in_specs=[
            pl.BlockSpec((nc, D), lambda l, j: (0, 0)),
            pl.BlockSpec((None, D, D), lambda l, j: (l, 0, j)),
            pl.BlockSpec((None, 1, D), lambda l, j: (l, 0, j)),
        ],
        out_specs=pl.BlockSpec((None, nc, D), lambda l, j: (l, 0, j)),
        out_shape=jax.ShapeDtypeStruct((2 * DEPTH, nc, 3 * D), F32),
        compiler_params=_cp(("arbitrary", "arbitrary")),
        name="adaln",
    )(c_all, w, b)


def _lru_gates(xc, n, wa_ref, wx_ref, ba_ref, bx_ref, lam_ref):
    cols = slice(n * BLK, (n + 1) * BLK)
    xb = xc.astype(BF16)
    r = jax.nn.sigmoid(_dot(xb, wa_ref[n]) + ba_ref[:, cols])
    i = jax.nn.sigmoid(_dot(xb, wx_ref[n]) + bx_ref[:, cols])
    z = -lam_ref[:, cols]
    softplus = jnp.maximum(z, 0.0) + jnp.log1p(jnp.exp(-jnp.abs(z)))
    log_a = (-RG_C) * r * softplus
    a = jnp.exp(log_a)
    b = jnp.sqrt(1.0 - a * a) * i * xc
    return a, b


def _lru_prompt_body(x_ref, mod_ref, win_ref, cw_ref, cb_ref, wa_ref, wx_ref, ba_ref, bx_ref,
                     lam_ref, wout_ref, g_ref, b_ref,
                     xn_ref, hl_ref, cl_ref,
                     xbuf, abuf, hbuf, hc, *, tm):
    t = pl.program_id(1)

    @pl.when(t == 0)
    def _():
        xbuf[0:ROW_TILE, :] = jnp.zeros((ROW_TILE, R), F32)
        hc[...] = jnp.zeros((1, R), F32)

    x = x_ref[...]
    shift, sc1p, g1p = _split_mod(mod_ref)
    u = x * sc1p + shift
    yx = _dot(u.astype(BF16), win_ref[...])
    y = jax.nn.gelu(yx[:, :R])
    xbuf[ROW_TILE:ROW_TILE + tm, :] = yx[:, R:]

    rowmod = lax.broadcasted_iota(jnp.int32, (tm, BLK), 0) & (ROW_TILE - 1)
    for n in range(N_BLK):
        cols = slice(n * BLK, (n + 1) * BLK)
        xc = cb_ref[:, cols] + cw_ref[3:4, cols] * xbuf[ROW_TILE:ROW_TILE + tm, cols]
        for k in range(1, CONV_W):
            xc = xc + cw_ref[3 - k:4 - k, cols] * xbuf[ROW_TILE - k:ROW_TILE - k + tm, cols]
        a, b = _lru_gates(xc, n, wa_ref, wx_ref, ba_ref, bx_ref, lam_ref)
        s = 1
        while s < ROW_TILE:
            a_sh = pltpu.roll(a, s, 0)
            b_sh = pltpu.roll(b, s, 0)
            m = rowmod >= s
            b = jnp.where(m, a * b_sh, 0.0) + b
            a = jnp.where(m, a * a_sh, a)
            s *= 2
        abuf[:, cols] = a
        hbuf[:, cols] = b

    def group(g, h):
        r0 = pl.multiple_of(g * ROW_TILE, ROW_TILE)
        hg = abuf[pl.ds(r0, ROW_TILE), :] * h + hbuf[pl.ds(r0, ROW_TILE), :]
        hbuf[pl.ds(r0, ROW_TILE), :] = hg
        return hg[ROW_TILE - 1:ROW_TILE, :]

    h_end = lax.fori_loop(0, tm // ROW_TILE, group, hc[...])
    hc[...] = h_end
    hl_ref[...] = h_end
    tail = xbuf[tm + ROW_TILE - (CONV_W - 1):tm + ROW_TILE, :]
    cl_ref[...] = tail
    xbuf[ROW_TILE - (CONV_W - 1):ROW_TILE, :] = tail

    mix = _dot((y * hbuf[...]).astype(BF16), wout_ref[...])
    xn_ref[...] = _layer_norm(ALPHA * x + g1p * mix, g_ref[...], b_ref[...])


def _const_spec(shape):
    nd = len(shape)
    return pl.BlockSpec(shape, lambda *_: (0,) * nd)


def _lru_weight_specs():
    return [
        _const_spec((D, 2 * R)), _const_spec((CONV_W, R)), _const_spec((1, R)),
        _const_spec((N_BLK, BLK, BLK)), _const_spec((N_BLK, BLK, BLK)),
        _const_spec((1, R)), _const_spec((1, R)), _const_spec((1, R)),
        _const_spec((R, D)), _const_spec((1, D)), _const_spec((1, D)),
    ]


def _lru_prompt(x, mod, lw, ln_g, ln_b, *, nb, seq, tm):
    nt = seq // tm
    n = nb * seq
    return pl.pallas_call(
        functools.partial(_lru_prompt_body, tm=tm),
        grid=(nb, nt),
        in_specs=[
            pl.BlockSpec((tm, D), lambda b, t: (b * nt + t, 0)),
            pl.BlockSpec((None, 1, 3 * D), lambda b, t: (b, 0, 0)),
        ] + _lru_weight_specs(),
        out_specs=[
            pl.BlockSpec((tm, D), lambda b, t: (b * nt + t, 0)),
            pl.BlockSpec((None, 1, R), lambda b, t: (b, 0, 0)),
            pl.BlockSpec((None, CONV_W - 1, R), lambda b, t: (b, 0, 0)),
        ],
        out_shape=[
            jax.ShapeDtypeStruct((n, D), F32),
            jax.ShapeDtypeStruct((nb, 1, R), F32),
            jax.ShapeDtypeStruct((nb, CONV_W - 1, R), F32),
        ],
        scratch_shapes=[
            pltpu.VMEM((tm + ROW_TILE, R), F32),
            pltpu.VMEM((tm, R), F32),
            pltpu.VMEM((tm, R), F32),
            pltpu.VMEM((1, R), F32),
        ],
        compiler_params=_cp(("arbitrary", "arbitrary")),
        name="lru_prompt",
    )(x, mod, *lw, ln_g, ln_b)


def _lru_sample_body(x_ref, mod_ref, h0_ref, c0_ref, win_ref, cw_ref, cb_ref, wa_ref, wx_ref,
                     ba_ref, bx_ref, lam_ref, wout_ref, g_ref, b_ref,
                     xn_ref, hl_ref, cl_ref, hbuf, *, nb, steps):
    x = x_ref[...]
    shift, sc1p, g1p = _split_mod(mod_ref)
    u = x * sc1p + shift
    yx = _dot(u.astype(BF16), win_ref[...])
    y = jax.nn.gelu(yx[:, :R])
    xext = jnp.concatenate([c0_ref[...], yx[:, R:]], axis=0)
    cl_ref[...] = xext[steps * nb:, :]
    for n in range(N_BLK):
        cols = slice(n * BLK, (n + 1) * BLK)
        xc = cb_ref[:, cols] + cw_ref[0:1, cols] * xext[0:steps * nb, cols]
        for k in range(1, CONV_W):
            xc = xc + cw_ref[k:k + 1, cols] * xext[k * nb:(k + steps) * nb, cols]
        a, b = _lru_gates(xc, n, wa_ref, wx_ref, ba_ref, bx_ref, lam_ref)
        h = h0_ref[:, cols]
        for t in range(steps):
            h = a[t * nb:(t + 1) * nb, :] * h + b[t * nb:(t + 1) * nb, :]
            hbuf[t * nb:(t + 1) * nb, cols] = h
        hl_ref[:, cols] = h
    mix = _dot((y * hbuf[...]).astype(BF16), wout_ref[...])
    xn_ref[...] = _layer_norm(ALPHA * x + g1p * mix, g_ref[...], b_ref[...])


def _lru_sample(x, mod, h0, c0, lw, ln_g, ln_b, *, nb, steps):
    n = nb * steps
    return pl.pallas_call(
        functools.partial(_lru_sample_body, nb=nb, steps=steps),
        grid=(1,),
        in_specs=[
            _const_spec((n, D)), _const_spec((n, 3 * D)), _const_spec((nb, R)),
            _const_spec(((CONV_W - 1) * nb, R)),
        ] + _lru_weight_specs(),
        out_specs=[_const_spec((n, D)), _const_spec((nb, R)), _const_spec(((CONV_W - 1) * nb, R))],
        out_shape=[
            jax.ShapeDtypeStruct((n, D), F32),
            jax.ShapeDtypeStruct((nb, R), F32),
            jax.ShapeDtypeStruct(((CONV_W - 1) * nb, R), F32),
        ],
        scratch_shapes=[pltpu.VMEM((n, R), F32)],
        compiler_params=_cp(("arbitrary",)),
        name="lru_sample",
    )(x, mod, h0, c0, *lw, ln_g, ln_b)


def _top2(logits):
    col = lax.broadcasted_iota(jnp.int32, logits.shape, 1).astype(F32)
    big = float(RLANES)
    is_g = col < N_GROUPS
    gl = jnp.where(is_g, logits, NEG)
    gmax = jnp.max(gl, axis=-1, keepdims=True)
    gidx = jnp.min(jnp.where(gl == gmax, col, big), axis=-1, keepdims=True)
    gsum = jnp.sum(jnp.where(is_g, jnp.exp(gl - gmax), 0.0), axis=-1, keepdims=True)
    g_p = 1.0 / gsum
    lo = N_GROUPS + gidx * E_PER_G
    in_grp = (col >= lo) & (col < lo + E_PER_G)
    el = jnp.where(in_grp, logits, NEG)
    t1 = jnp.max(el, axis=-1, keepdims=True)
    i1 = jnp.min(jnp.where(el == t1, col, big), axis=-1, keepdims=True)
    el2 = jnp.where(col == i1, NEG, el)
    t2 = jnp.max(el2, axis=-1, keepdims=True)
    i2 = jnp.min(jnp.where((el2 == t2) & in_grp & (col != i1), col, big), axis=-1, keepdims=True)
    e21 = jnp.exp(t2 - t1)
    w1 = g_p / (1.0 + e21)
    w2 = w1 * e21
    return col, gidx, i1, i2, w1, w2


def _route(logits):
    col, _, i1, i2, w1, w2 = _top2(logits)
    return jnp.where(col == i1, w1, 0.0) + jnp.where(col == i2, w2, 0.0)


PAIRS = E_PER_G * (E_PER_G - 1) // 2
NCLS = N_GROUPS * PAIRS
XW = D + RLANES
FFN_TM = 256
ROUTE_TM = 512
DISPATCH_TD = 1024
COMBINE_TC = 256
DMA_UNROLL = 8


def _route_sort_body(x_ref, mod_ref, wr_ref, br_ref, urow_ref, cnt_ref, carry):
    i = pl.program_id(0)

    @pl.when(i == 0)
    def _():
        carry[...] = jnp.zeros(carry.shape, F32)

    tm = x_ref.shape[0]
    shift, sc1p, _ = _split_mod(mod_ref)
    u = x_ref[...] * sc1p + shift
    col, gidx, i1, i2, w1, w2 = _top2(_dot(u.astype(BF16), wr_ref[...]) + br_ref[...])
    first = N_GROUPS + gidx * E_PER_G
    lo = jnp.minimum(i1, i2) - first
    hi = jnp.maximum(i1, i2) - first
    w_lo = jnp.where(i1 < i2, w1, w2)
    w_hi = jnp.where(i1 < i2, w2, w1)
    cls = gidx * PAIRS + lo * (2 * E_PER_G - 1 - lo) * 0.5 + (hi - lo - 1.0)
    onehot = (col == cls).astype(F32)
    r = lax.broadcasted_iota(jnp.int32, (tm, tm), 0)
    c = lax.broadcasted_iota(jnp.int32, (tm, tm), 1)
    earlier = _dot((c < r).astype(BF16), onehot.astype(BF16)) + carry[...]
    rank = jnp.sum(onehot * earlier, axis=-1, keepdims=True)
    carry[...] += jnp.sum(onehot, axis=0, keepdims=True)
    cnt_ref[...] = carry[...]
    urow_ref[:, 0:D] = u
    urow_ref[:, D:XW] = (jnp.where(col == 0, w_lo, 0.0) + jnp.where(col == 1, w_hi, 0.0)
                         + jnp.where(col == 2, cls, 0.0) + jnp.where(col == 3, rank, 0.0))


def _route_sort(x, mod, wr, br, *, tiles_per_mod):
    n = x.shape[0]
    mr = mod.shape[1]
    return pl.pallas_call(
        _route_sort_body,
        grid=(n // ROUTE_TM,),
        in_specs=[
            pl.BlockSpec((ROUTE_TM, D), lambda i: (i, 0)),
            pl.BlockSpec((None, mr, 3 * D), lambda i: (i // tiles_per_mod, 0, 0)),
            _const_spec((D, RLANES)), _const_spec((1, RLANES)),
        ],
        out_specs=[pl.BlockSpec((ROUTE_TM, XW), lambda i: (i, 0)), _const_spec((1, RLANES))],
        out_shape=[jax.ShapeDtypeStruct((n, XW), F32), jax.ShapeDtypeStruct((1, RLANES), F32)],
        scratch_shapes=[pltpu.VMEM((1, RLANES), F32)],
        compiler_params=_cp(("arbitrary",)),
        name="moe_route",
    )(x, mod, wr, br)


def _row_copy(src, src_row, dst, dst_row, sem):
    return pltpu.make_async_copy(src.at[pl.ds(src_row, 1)], dst.at[pl.ds(dst_row, 1)], sem)


def _dispatch_body(pos_ref, urow_ref, xs_in, xs_hbm, sem):
    del xs_in
    base = pl.program_id(0) * DISPATCH_TD

    def issue(r, carry):
        _row_copy(urow_ref, r, xs_hbm, pos_ref[base + r], sem).start()
        return carry

    lax.fori_loop(0, DISPATCH_TD, issue, 0, unroll=DMA_UNROLL)
    pltpu.make_async_copy(urow_ref, xs_hbm.at[pl.ds(0, DISPATCH_TD)], sem).wait()


def _dispatch(pos, urow, xs_zero):
    n = urow.shape[0]
    any_spec = pl.BlockSpec(memory_space=pl.ANY)
    return pl.pallas_call(
        _dispatch_body,
        grid_spec=pltpu.PrefetchScalarGridSpec(
            num_scalar_prefetch=1, grid=(n // DISPATCH_TD,),
            in_specs=[pl.BlockSpec((DISPATCH_TD, XW), lambda i, pos: (i, 0)), any_spec],
            out_specs=any_spec,
            scratch_shapes=[pltpu.SemaphoreType.DMA(())]),
        out_shape=jax.ShapeDtypeStruct(xs_zero.shape, F32),
        input_output_aliases={2: 0},
        compiler_params=_cp(("arbitrary",)),
        name="moe_dispatch",
    )(pos, urow, xs_zero)


def _ffn_sorted_body(ta_ref, tb_ref, na_ref, xs_ref, w1a, w3a, w2a, w1b, w3b, w2b, ys_ref):
    del ta_ref, tb_ref
    active = pl.program_id(0) < na_ref[0]

    @pl.when(active)
    def _():
        x = xs_ref[:, 0:D].astype(BF16)
        ha = jax.nn.silu(_dot(x, w1a[...])) * _dot(x, w3a[...]) * xs_ref[:, D:D + 1]
        hb = jax.nn.silu(_dot(x, w1b[...])) * _dot(x, w3b[...]) * xs_ref[:, D + 1:D + 2]
        ys_ref[...] = _dot(ha.astype(BF16), w2a[...]) + _dot(hb.astype(BF16), w2b[...])

    @pl.when(jnp.logical_not(active))
    def _():
        ys_ref[...] = jnp.zeros(ys_ref.shape, F32)


def _ffn_sorted(ta, tb, na, xs, w1, w3, w2):
    nt = ta.shape[0]
    ea = lambda t, ta, tb, na: (ta[t], 0, 0)
    eb = lambda t, ta, tb, na: (tb[t], 0, 0)
    rows = lambda t, ta, tb, na: (t, 0)
    return pl.pallas_call(
        _ffn_sorted_body,
        grid_spec=pltpu.PrefetchScalarGridSpec(
            num_scalar_prefetch=3, grid=(nt,),
            in_specs=[
                pl.BlockSpec((FFN_TM, XW), rows),
                pl.BlockSpec((None, D, F), ea), pl.BlockSpec((None, D, F), ea), pl.BlockSpec((None, F, D), ea),
                pl.BlockSpec((None, D, F), eb), pl.BlockSpec((None, D, F), eb), pl.BlockSpec((None, F, D), eb),
            ],
            out_specs=pl.BlockSpec((FFN_TM, D), rows)),
        out_shape=jax.ShapeDtypeStruct((nt * FFN_TM, D), F32),
        compiler_params=_cp(("arbitrary",)),
        name="moe_ffn",
    )(ta, tb, na, xs, w1, w3, w2, w1, w3, w2)


def _combine_body(pos_ref, x_ref, mod_ref, ys_hbm, g_ref, b_ref, xn_ref, buf, sem):
    i = pl.program_id(0)
    nsteps = pl.num_programs(0)
    tc = x_ref.shape[0]

    def gather(step, slot):
        def issue(r, carry):
            _row_copy(ys_hbm, pos_ref[step * tc + r], buf.at[slot], r, sem.at[slot]).start()
            return carry
        lax.fori_loop(0, tc, issue, 0, unroll=DMA_UNROLL)

    @pl.when(i == 0)
    def _():
        gather(0, 0)

    @pl.when(i + 1 < nsteps)
    def _():
        gather(i + 1, (i + 1) % 2)

    slot = i % 2
    pltpu.make_async_copy(ys_hbm.at[pl.ds(0, tc)], buf.at[slot], sem.at[slot]).wait()
    g1p = mod_ref[:, 2 * D:3 * D]
    xn_ref[...] = _layer_norm(ALPHA * x_ref[...] + g1p * buf[slot], g_ref[...], b_ref[...])


def _combine(pos, x, mod, ys, ln_g, ln_b, *, tiles_per_mod):
    n = x.shape[0]
    mr = mod.shape[1]
    tc = COMBINE_TC
    tpm = tiles_per_mod * (ROUTE_TM // tc)
    return pl.pallas_call(
        _combine_body,
        grid_spec=pltpu.PrefetchScalarGridSpec(
            num_scalar_prefetch=1, grid=(n // tc,),
            in_specs=[
                pl.BlockSpec((tc, D), lambda i, pos: (i, 0)),
                pl.BlockSpec((None, mr, 3 * D), lambda i, pos: (i // tpm, 0, 0)),
                pl.BlockSpec(memory_space=pl.ANY),
                pl.BlockSpec((1, D), lambda i, pos: (0, 0)), pl.BlockSpec((1, D), lambda i, pos: (0, 0)),
            ],
            out_specs=pl.BlockSpec((tc, D), lambda i, pos: (i, 0)),
            scratch_shapes=[pltpu.VMEM((2, tc, D), F32), pltpu.SemaphoreType.DMA((2,))]),
        out_shape=jax.ShapeDtypeStruct((n, D), F32),
        compiler_params=_cp(("arbitrary",)),
        name="moe_combine",
    )(pos, x, mod, ys, ln_g, ln_b)


def _moe_sparse(x, mod, wr, br, w1, w3, w2, ln_g, ln_b, *, tiles_per_mod):
    n = x.shape[0]
    urow, cnt = _route_sort(x, mod, wr, br, tiles_per_mod=tiles_per_mod)
    cls = urow[:, D + 2].astype(jnp.int32)
    rank = urow[:, D + 3].astype(jnp.int32)
    ntile = (cnt[0, :NCLS].astype(jnp.int32) + FFN_TM - 1) // FFN_TM
    tile_end = jnp.cumsum(ntile)
    row_start = (tile_end - ntile) * FFN_TM
    pos = rank + jnp.sum(jax.nn.one_hot(cls, NCLS, dtype=jnp.int32) * row_start[None, :], axis=1)
    nt = n // FFN_TM + NCLS
    total = tile_end[NCLS - 1]
    tr = jnp.minimum(jnp.arange(nt, dtype=jnp.int32), total - 1)
    tcls = jnp.sum((tr[:, None] >= tile_end[None, :]).astype(jnp.int32), axis=1)
    grp, pair = tcls // PAIRS, tcls % PAIRS
    lo = (pair >= 3).astype(jnp.int32) + (pair >= 5).astype(jnp.int32)
    hi = pair + 1 - lo * (2 * E_PER_G - 3 - lo) // 2
    ta = grp * E_PER_G + lo
    tb = grp * E_PER_G + hi
    xs = _dispatch(pos, urow, jnp.zeros((nt * FFN_TM, XW), F32))
    ys = _ffn_sorted(ta, tb, total[None], xs, w1, w3, w2)
    return _combine(pos, x, mod, ys, ln_g, ln_b, tiles_per_mod=tiles_per_mod)


def _moe_body(x_ref, mod_ref, wr_ref, br_ref, w1_ref, w3_ref, w2_ref, g_ref, b_ref,
              xn_ref, ub, comb, acc):
    e = pl.program_id(1)

    @pl.when(e == 0)
    def _():
        shift, sc1p, _ = _split_mod(mod_ref)
        u = (x_ref[...] * sc1p + shift).astype(BF16)
        ub[...] = u
        comb[...] = _route(_dot(u, wr_ref[...]) + br_ref[...])
        acc[...] = jnp.zeros(acc.shape, F32)

    u = ub[...]
    col = lax.broadcasted_iota(jnp.int32, comb.shape, 1)
    ce = jnp.sum(jnp.where(col == e + N_GROUPS, comb[...], 0.0), axis=-1, keepdims=True)
    h = jax.nn.silu(_dot(u, w1_ref[...])) * _dot(u, w3_ref[...]) * ce
    acc[...] += _dot(h.astype(BF16), w2_ref[...])

    @pl.when(e == N_EXP - 1)
    def _():
        g1p = mod_ref[:, 2 * D:3 * D]
        xn_ref[...] = _layer_norm(ALPHA * x_ref[...] + g1p * acc[...], g_ref[...], b_ref[...])


def _moe(x, mod, wr, br, w1, w3, w2, ln_g, ln_b, *, tm, tiles_per_mod):
    n = x.shape[0]
    mr = mod.shape[1]
    return pl.pallas_call(
        _moe_body,
        grid=(n // tm, N_EXP),
        in_specs=[
            pl.BlockSpec((tm, D), lambda i, e: (i, 0)),
            pl.BlockSpec((None, mr, 3 * D), lambda i, e: (i // tiles_per_mod, 0, 0)),
            pl.BlockSpec((D, RLANES), lambda i, e: (0, 0)),
            pl.BlockSpec((1, RLANES), lambda i, e: (0, 0)),
            pl.BlockSpec((None, D, F), lambda i, e: (e, 0, 0)),
            pl.BlockSpec((None, D, F), lambda i, e: (e, 0, 0)),
            pl.BlockSpec((None, F, D), lambda i, e: (e, 0, 0)),
            pl.BlockSpec((1, D), lambda i, e: (0, 0)),
            pl.BlockSpec((1, D), lambda i, e: (0, 0)),
        ],
        out_specs=pl.BlockSpec((tm, D), lambda i, e: (i, 0)),
        out_shape=jax.ShapeDtypeStruct((n, D), F32),
        scratch_shapes=[
            pltpu.VMEM((tm, D), BF16),
            pltpu.VMEM((tm, RLANES), F32),
            pltpu.VMEM((tm, D), F32),
        ],
        compiler_params=_cp(("arbitrary", "arbitrary")),
        name="moe",
    )(x, mod, wr, br, w1, w3, w2, ln_g, ln_b)


_NT = (((1,), (1,)), ((), ()))
_TN = (((0,), (0,)), ((), ()))


def _kv_body(x_ref, w_ref, wvt_ref, k_ref, v_ref, k16_ref, vt16_ref):
    xb = x_ref[...].astype(BF16)
    kv = _dot(xb, w_ref[...])
    k_ref[...] = kv[:, :KVW]
    v_ref[...] = kv[:, KVW:]
    k16_ref[...] = kv[:, :KVW].astype(BF16)
    vt16_ref[...] = lax.dot_general(wvt_ref[...], xb, _NT, preferred_element_type=F32).astype(BF16)


def _kv(x, w, wvt, *, tm):
    n = x.shape[0]
    row = lambda i: (i, 0)
    return pl.pallas_call(
        _kv_body,
        grid=(n // tm,),
        in_specs=[pl.BlockSpec((tm, D), row), _const_spec((D, 2 * KVW)), _const_spec((KVW, D))],
        out_specs=[pl.BlockSpec((tm, KVW), row), pl.BlockSpec((tm, KVW), row),
                   pl.BlockSpec((tm, KVW), row), pl.BlockSpec((KVW, tm), lambda i: (0, i))],
        out_shape=[jax.ShapeDtypeStruct((n, KVW), F32), jax.ShapeDtypeStruct((n, KVW), F32),
                   jax.ShapeDtypeStruct((n, KVW), BF16), jax.ShapeDtypeStruct((KVW, n), BF16)],
        compiler_params=_cp(("arbitrary",)),
        name="kv_proj",
    )(x, w, wvt)


def _sink_attend(qh, kh, vh, bias, sink):
    s = lax.dot_general(qh, kh, (((1,), (1,)), ((), ())), preferred_element_type=F32) + bias
    m = jnp.maximum(jnp.max(s, axis=-1, keepdims=True), sink)
    p = jnp.exp(s - m)
    den = jnp.sum(p, axis=-1, keepdims=True) + jnp.exp(sink - m)
    return _dot(p.astype(BF16), vh) / den


def _attn_prompt_body(x_ref, mod_ref, wqt_ref, wo_ref, kc_ref, kp_ref, vtc_ref, vtp_ref,
                      bias_ref, sink_ref, g_ref, b_ref, xn_ref, *, tq):
    t = pl.program_id(1)
    x = x_ref[...]
    shift, sc1p, g1p = _split_mod(mod_ref)
    u = (x * sc1p + shift).astype(BF16)
    qt = lax.dot_general(wqt_ref[...], u, _NT, preferred_element_type=F32)
    qt = (qt * (HD ** -0.5)).astype(BF16)
    kc = kc_ref[...]
    vtc = vtc_ref[...]
    krow = lax.broadcasted_iota(jnp.int32, (2 * WIN, GRP * WIN), 0)
    first = jnp.where((krow < WIN) & (t == 0), NEG, 0.0).astype(F32)
    blocks = []
    for j in range(tq // WIN):
        qcols = slice(j * WIN, (j + 1) * WIN)
        if j == 0:
            kk = jnp.concatenate([kp_ref[...], kc[0:WIN]], axis=0)
            vvt = jnp.concatenate([vtp_ref[...], vtc[:, 0:WIN]], axis=1)
        else:
            kk = kc[(j - 1) * WIN:(j + 1) * WIN]
            vvt = vtc[:, (j - 1) * WIN:(j + 1) * WIN]
        heads = []
        for hk in range(HKV):
            hrows = slice(hk * HD, (hk + 1) * HD)
            qht = jnp.concatenate(
                [qt[(hk * GRP + g) * HD:(hk * GRP + g + 1) * HD, qcols] for g in range(GRP)], axis=1)
            st = _dot(kk[:, hrows], qht) + bias_ref[hk]
            if j == 0:
                st = st + first
            sink = sink_ref[hk]
            m = jnp.maximum(jnp.max(st, axis=0, keepdims=True), sink)
            pt = jnp.exp(st - m)
            den = jnp.sum(pt, axis=0, keepdims=True) + jnp.exp(sink - m)
            ot = _dot(vvt[hrows, :], pt.astype(BF16)) * (1.0 / den)
            for g in range(GRP):
                heads.append(ot[:, g * WIN:(g + 1) * WIN])
        blocks.append(jnp.concatenate(heads, axis=0))
    oallt = jnp.concatenate(blocks, axis=1) if len(blocks) > 1 else blocks[0]
    mix = lax.dot_general(oallt.astype(BF16), wo_ref[...], _TN, preferred_element_type=F32)
    xn_ref[...] = _layer_norm(ALPHA * x + g1p * mix, g_ref[...], b_ref[...])


def _attn_prompt(x, mod, wqt, wo, k16, vt16, bias, sink, ln_g, ln_b, *, nb, seq, tq):
    nt = seq // tq
    wpt = tq // WIN
    n = nb * seq

    def prev(b, t):
        return jnp.maximum((b * nt + t) * wpt - 1, 0)

    cur = lambda b, t: (b * nt + t, 0)
    return pl.pallas_call(
        functools.partial(_attn_prompt_body, tq=tq),
        grid=(nb, nt),
        in_specs=[
            pl.BlockSpec((tq, D), cur),
            pl.BlockSpec((None, 1, 3 * D), lambda b, t: (b, 0, 0)),
            _const_spec((HQ * HD, D)), _const_spec((HQ * HD, D)),
            pl.BlockSpec((tq, KVW), cur), pl.BlockSpec((WIN, KVW), lambda b, t: (prev(b, t), 0)),
            pl.BlockSpec((KVW, tq), lambda b, t: (0, b * nt + t)),
            pl.BlockSpec((KVW, WIN), lambda b, t: (0, prev(b, t))),
            _const_spec((HKV, 2 * WIN, GRP * WIN)), _const_spec((HKV, 1, GRP * WIN)),
            _const_spec((1, D)), _const_spec((1, D)),
        ],
        out_specs=pl.BlockSpec((tq, D), cur),
        out_shape=jax.ShapeDtypeStruct((n, D), F32),
        compiler_params=_cp(("arbitrary", "arbitrary")),
        name="attn_prompt",
    )(x, mod, wqt, wo, k16, k16, vt16, vt16, bias, sink, ln_g, ln_b)


def _attn_sample_body(x_ref, mod_ref, wq_ref, wo_ref, kb_ref, vb_ref, kn_ref, vn_ref,
                      bias_ref, sink_ref, g_ref, b_ref, xn_ref, *, ns, steps):
    x = x_ref[...]
    shift, sc1p, g1p = _split_mod(mod_ref)
    u = (x * sc1p + shift).astype(BF16)
    q = (_dot(u, wq_ref[...]) * (HD ** -0.5)).astype(BF16)
    nbuf = kb_ref.shape[1]
    k_all = jnp.concatenate([kb_ref[...].reshape(ns * nbuf, KVW), kn_ref[...]], axis=0).astype(BF16)
    v_all = jnp.concatenate([vb_ref[...].reshape(ns * nbuf, KVW), vn_ref[...]], axis=0).astype(BF16)
    nr = ns * steps
    heads = [None] * HQ
    for hk in range(HKV):
        kcols = slice(hk * HD, (hk + 1) * HD)
        qh = jnp.concatenate(
            [q[:, (hk * GRP + g) * HD:(hk * GRP + g + 1) * HD] for g in range(GRP)], axis=0)
        o = _sink_attend(qh, k_all[:, kcols], v_all[:, kcols], bias_ref[hk], sink_ref[hk])
        for g in range(GRP):
            heads[hk * GRP + g] = o[g * nr:(g + 1) * nr]
    o_all = jnp.concatenate(heads, axis=1)
    mix = _dot(o_all.astype(BF16), wo_ref[...])
    xn_ref[...] = _layer_norm(ALPHA * x + g1p * mix, g_ref[...], b_ref[...])


def _attn_sample(x, mod, wq, wo, kbuf, vbuf, kn, vn, bias, sink, ln_g, ln_b, *, ns, steps):
    n = x.shape[0]
    nr = ns * steps
    nbuf = kbuf.shape[1]
    row = lambda i: (i, 0)
    return pl.pallas_call(
        functools.partial(_attn_sample_body, ns=ns, steps=steps),
        grid=(n // nr,),
        in_specs=[
            pl.BlockSpec((nr, D), row), pl.BlockSpec((nr, 3 * D), row),
            _const_spec((D, HQ * HD)), _const_spec((HQ * HD, D)),
            pl.BlockSpec((ns, nbuf, KVW), lambda i: (i, 0, 0)),
            pl.BlockSpec((ns, nbuf, KVW), lambda i: (i, 0, 0)),
            pl.BlockSpec((nr, KVW), row), pl.BlockSpec((nr, KVW), row),
            _const_spec(bias.shape), _const_spec(sink.shape),
            _const_spec((1, D)), _const_spec((1, D)),
        ],
        out_specs=pl.BlockSpec((nr, D), row),
        out_shape=jax.ShapeDtypeStruct((n, D), F32),
        compiler_params=_cp(("arbitrary",)),
        name="attn_sample",
    )(x, mod, wq, wo, kbuf, vbuf, kn, vn, bias, sink, ln_g, ln_b)


def _t5_bucket(d):
    max_exact = N_BUCKETS // 2
    d = jnp.maximum(d, 0)
    log_ratio = jnp.log(jnp.maximum(d, 1).astype(F32) / max_exact) / math.log(WIN / max_exact)
    large = max_exact + (log_ratio * (N_BUCKETS - max_exact)).astype(jnp.int32)
    large = jnp.minimum(large, N_BUCKETS - 1)
    return jnp.where(d < max_exact, d, large)


def _bias_table(d, ok, rel_bias):
    tab = jnp.concatenate([rel_bias[_t5_bucket(jnp.arange(WIN))].astype(F32),
                           jnp.full((1, HQ), NEG, F32)], axis=0)
    sel = jax.nn.one_hot(jnp.where(ok, d, WIN), WIN + 1, dtype=F32)
    return jnp.einsum('abd,dh->hab', sel, tab, precision=lax.Precision.HIGHEST)


def _prompt_bias(rel_bias):
    d = jnp.arange(WIN)[None, :] + WIN - jnp.arange(2 * WIN)[:, None]
    b = _bias_table(d, (d >= 0) & (d < WIN), rel_bias).reshape(HKV, GRP, 2 * WIN, WIN)
    return jnp.transpose(b, (0, 2, 1, 3)).reshape(HKV, 2 * WIN, GRP * WIN)


def _sample_bias(rel_bias, ns, steps, nbuf):
    qs = jnp.repeat(jnp.arange(ns), steps)
    qt = jnp.tile(jnp.arange(steps), ns)
    ks = jnp.concatenate([jnp.repeat(jnp.arange(ns), nbuf), jnp.repeat(jnp.arange(ns), steps)])
    kpos = jnp.concatenate([jnp.tile(jnp.arange(nbuf), ns), nbuf + jnp.tile(jnp.arange(steps), ns)])
    d = qt[:, None] + nbuf - kpos[None, :]
    ok = (d >= 0) & (d < WIN) & (qs[:, None] == ks[None, :])
    return _bias_table(d, ok, rel_bias).reshape(HKV, GRP * ns * steps, kpos.shape[0])


def _sink_table(sinks, rows):
    return jnp.repeat(sinks.reshape(HKV, GRP), rows, axis=1)[:, :, None].astype(F32)


MOE_TM = 1024
LRU_TM = 256
ATTN_TQ = 256
SAMPLE_NS = 8


def kernel(x_prompt, x_sample, state_rnn_h, state_rnn_conv, cache_win_k, cache_win_v, c_prompt, c_sample, ada_w, ada_b, ln_g, ln_b, lru_w_in, lru_conv_w, lru_conv_b, lru_w_a, lru_b_a, lru_w_x, lru_b_x, lru_lambda, lru_w_out, kv_w, attn_w_q, attn_sinks, attn_w_o, rel_bias, moe_w_group, moe_b_group, moe_w_router, moe_b_router, moe_w1, moe_w3, moe_w2):
    nbp, seq, _ = x_prompt.shape
    nbs, steps, _ = x_sample.shape
    nbuf = cache_win_k.shape[1]
    npt = nbp * seq
    nst = nbs * steps

    mods = _adaln(jnp.concatenate([c_prompt, c_sample], axis=0), ada_w, ada_b)
    mod_p = mods[:, :nbp].reshape(DEPTH, 2, nbp, 1, 3 * D)
    mod_s = mods[:, nbp:].reshape(DEPTH, 2, nbs, 3 * D)

    def lru_weights(l):
        return (lru_w_in[l].astype(BF16), lru_conv_w[l], lru_conv_b[l][None], lru_w_a[l].astype(BF16),
                lru_w_x[l].astype(BF16), lru_b_a[l][None], lru_b_x[l][None], lru_lambda[l][None],
                lru_w_out[l].astype(BF16))

    def moe_weights(l):
        wr = jnp.concatenate([moe_w_group[l], moe_w_router[l].reshape(D, N_EXP)], axis=1)
        wr = jnp.pad(wr, ((0, 0), (0, RLANES - wr.shape[1]))).astype(BF16)
        br = jnp.concatenate([moe_b_group[l], moe_b_router[l].reshape(N_EXP)])
        br = jnp.pad(br, (0, RLANES - br.shape[0]))[None]
        return wr, br, moe_w1[l].astype(BF16), moe_w3[l].astype(BF16), moe_w2[l].astype(BF16)

    kvw = kv_w.astype(BF16)
    kvwt = kvw[:, KVW:].T
    bias_p = _prompt_bias(rel_bias)
    bias_s = _sample_bias(rel_bias, SAMPLE_NS, steps, nbuf)

    xp = x_prompt.reshape(npt, D)
    xs = jnp.transpose(x_sample, (1, 0, 2)).reshape(nst, D)
    hp, cp_, hs, cs = [], [], [], []
    kp = vp = kn = vn = kp16 = vpt16 = None
    for l in range(DEPTH):
        g0, b0 = ln_g[l, 0][None], ln_b[l, 0][None]
        g1, b1 = ln_g[l, 1][None], ln_b[l, 1][None]
        if l < N_A:
            lw = lru_weights(l)
            xp, h_l, c_l = _lru_prompt(xp, mod_p[l, 0], lw, g0, b0, nb=nbp, seq=seq, tm=LRU_TM)
            hp.append(h_l[:, 0])
            cp_.append(c_l)
            c0 = jnp.transpose(state_rnn_conv[l], (1, 0, 2)).reshape((CONV_W - 1) * nbs, R)
            xs, h_l, c_l = _lru_sample(xs, jnp.tile(mod_s[l, 0], (steps, 1)), state_rnn_h[l], c0,
                                       lw, g0, b0, nb=nbs, steps=steps)
            hs.append(h_l)
            cs.append(jnp.transpose(c_l.reshape(CONV_W - 1, nbs, R), (1, 0, 2)))
            mod_s1 = jnp.tile(mod_s[l, 1], (steps, 1))
        else:
            j = l - N_A
            wq, wo = attn_w_q[j].astype(BF16), attn_w_o[j].astype(BF16)
            sink_p = jnp.repeat(attn_sinks[j].reshape(HKV, GRP), WIN, axis=1)[:, None, :]
            xp = _attn_prompt(xp, mod_p[l, 0], wq.T, wo, kp16, vpt16, bias_p, sink_p,
                              g0, b0, nb=nbp, seq=seq, tq=ATTN_TQ)
            xs = _attn_sample(xs, jnp.repeat(mod_s[l, 0], steps, axis=0), wq, wo,
                              cache_win_k.reshape(nbs, nbuf, KVW), cache_win_v.reshape(nbs, nbuf, KVW),
                              kn, vn, bias_s, _sink_table(attn_sinks[j], SAMPLE_NS * steps),
                              g0, b0, ns=SAMPLE_NS, steps=steps)
            mod_s1 = jnp.repeat(mod_s[l, 1], steps, axis=0)
        mw = moe_weights(l)
        xp = _moe_sparse(xp, mod_p[l, 1], *mw, g1, b1, tiles_per_mod=seq // ROUTE_TM)
        xs = _moe(xs, mod_s1[None], *mw, g1, b1, tm=nst, tiles_per_mod=1)
        if l == N_A - 1:
            xs = jnp.transpose(xs.reshape(steps, nbs, D), (1, 0, 2)).reshape(nst, D)
            kp, vp, kp16, vpt16 = _kv(xp, kvw, kvwt, tm=1024)
            kn, vn, _, _ = _kv(xs, kvw, kvwt, tm=nst)

    lp = min(WIN, seq)
    k_p = kp.reshape(nbp, seq, HKV, HD)[:, seq - lp:]
    v_p = vp.reshape(nbp, seq, HKV, HD)[:, seq - lp:]
    k_s = jnp.concatenate([cache_win_k, kn.reshape(nbs, steps, HKV, HD)], axis=1)[:, -nbuf:]
    v_s = jnp.concatenate([cache_win_v, vn.reshape(nbs, steps, HKV, HD)], axis=1)[:, -nbuf:]
    return (xp.reshape(nbp, seq, D), xs.reshape(nbs, steps, D),
            jnp.stack(hp), jnp.stack(cp_), k_p, v_p,
            jnp.stack(hs), jnp.stack(cs), k_s, v_s)
```

```python
import functools
import math

import jax
import jax.numpy as jnp
from jax import lax
from jax.experimental import pallas as pl
from jax.experimental.pallas import tpu as pltpu

D = 1024
R = 1024
DEPTH = 4
N_A = 2
N_B = DEPTH - N_A
N_BLK = 4
BLK = R // N_BLK
CONV_W = 4
RG_C = 8.0
HQ = 16
HKV = 4
HD = 64
GRP = HQ // HKV
KVW = HKV * HD
WIN = 128
N_BUCKETS = 32
N_GROUPS = 4
E_PER_G = 4
N_EXP = 16
F = 512
ALPHA = (2.0 * DEPTH) ** 0.25
LN_EPS = 1e-5
NEG = -1e30
RLANES = 128
ROW_TILE = 8

F32 = jnp.float32
BF16 = jnp.bfloat16

VMEM_LIMIT = 56 * 1024 * 1024


def _cp(sem):
    return pltpu.CompilerParams(dimension_semantics=sem, vmem_limit_bytes=VMEM_LIMIT)


def _dot(a, b):
    return jnp.dot(a, b, preferred_element_type=F32)


def _layer_norm(z, g, b):
    mu = jnp.mean(z, axis=-1, keepdims=True)
    zc = z - mu
    var = jnp.mean(zc * zc, axis=-1, keepdims=True)
    return zc * lax.rsqrt(var + LN_EPS) * g + b


def _split_mod(mod_ref, row=None, reps=1):
    m = mod_ref[...] if row is None else mod_ref[pl.ds(row, 1), :]
    if reps > 1:
        m = jnp.concatenate([m] * reps, axis=0)
    return m[:, 0:D], m[:, D:2 * D], m[:, 2 * D:3 * D]


def _layer_spec(shape, layer):
    nd = len(shape)
    return pl.BlockSpec((None,) + tuple(shape), lambda *_: (layer,) + (0,) * nd)


def _adaln_body(cp_ref, cs_ref, w_ref, b_ref, op_ref, os_ref):
    one = jnp.where(pl.program_id(1) > 0, 1.0, 0.0).astype(F32)
    w = w_ref[...].astype(BF16)
    op_ref[...] = _dot(cp_ref[...].astype(BF16), w) + b_ref[...] + one
    os_ref[...] = _dot(cs_ref[...].astype(BF16), w) + b_ref[...] + one


def _adaln(c_p, c_s, ada_w, ada_b):
    np_, ns_ = c_p.shape[0], c_s.shape[0]
    w = ada_w.reshape(2 * DEPTH, D, 3 * D)
    b = ada_b.reshape(2 * DEPTH, 1, 3 * D)
    return pl.pallas_call(
        _adaln_body,
        grid=(2 * DEPTH, 3),
        in_specs=[
            pl.BlockSpec((np_, D), lambda l, j: (0, 0)),
            pl.BlockSpec((ns_, D), lambda l, j: (0, 0)),
            pl.BlockSpec((None, D, D), lambda l, j: (l, 0, j)),
            pl.BlockSpec((None, 1, D), lambda l, j: (l, 0, j)),
        ],
        out_specs=[pl.BlockSpec((None, np_, D), lambda l, j: (l, 0, j)),
                   pl.BlockSpec((None, ns_, D), lambda l, j: (l, 0, j))],
        out_shape=[jax.ShapeDtypeStruct((2 * DEPTH, np_, 3 * D), F32),
                   jax.ShapeDtypeStruct((2 * DEPTH, ns_, 3 * D), F32)],
        compiler_params=_cp(("arbitrary", "arbitrary")),
        name="adaln",
    )(c_p, c_s, w, b)


def _lru_gates(xc, n, wa_ref, wx_ref, ba_ref, bx_ref, lam_ref):
    cols = slice(n * BLK, (n + 1) * BLK)
    xb = xc.astype(BF16)
    r = jax.nn.sigmoid(_dot(xb, wa_ref[n]) + ba_ref[:, cols])
    i = jax.nn.sigmoid(_dot(xb, wx_ref[n]) + bx_ref[:, cols])
    z = -lam_ref[:, cols]
    softplus = jnp.maximum(z, 0.0) + jnp.log1p(jnp.exp(-jnp.abs(z)))
    log_a = (-RG_C) * r * softplus
    a = jnp.exp(log_a)
    b = jnp.sqrt(1.0 - a * a) * i * xc
    return a, b


def _lru_prompt_body(x_ref, mod_ref, win_ref, cw_ref, cb_ref, wa_ref, wx_ref, ba_ref, bx_ref,
                     lam_ref, wout_ref, g_ref, b_ref,
                     xn_ref, hl_ref, cl_ref,
                     xbuf, abuf, hbuf, hc, *, tm):
    t = pl.program_id(1)

    @pl.when(t == 0)
    def _():
        xbuf[0:ROW_TILE, :] = jnp.zeros((ROW_TILE, R), F32)
        hc[...] = jnp.zeros((1, R), F32)

    x = x_ref[...]
    shift, sc1p, g1p = _split_mod(mod_ref, row=pl.program_id(0))
    u = x * sc1p + shift
    yx = _dot(u.astype(BF16), win_ref[...])
    y = jax.nn.gelu(yx[:, :R])
    xbuf[ROW_TILE:ROW_TILE + tm, :] = yx[:, R:]

    rowmod = lax.broadcasted_iota(jnp.int32, (tm, BLK), 0) & (ROW_TILE - 1)
    for n in range(N_BLK):
        cols = slice(n * BLK, (n + 1) * BLK)
        xc = cb_ref[:, cols] + cw_ref[3:4, cols] * xbuf[ROW_TILE:ROW_TILE + tm, cols]
        for k in range(1, CONV_W):
            xc = xc + cw_ref[3 - k:4 - k, cols] * xbuf[ROW_TILE - k:ROW_TILE - k + tm, cols]
        a, b = _lru_gates(xc, n, wa_ref, wx_ref, ba_ref, bx_ref, lam_ref)
        s = 1
        while s < ROW_TILE:
            a_sh = pltpu.roll(a, s, 0)
            b_sh = pltpu.roll(b, s, 0)
            m = rowmod >= s
            b = jnp.where(m, a * b_sh, 0.0) + b
            a = jnp.where(m, a * a_sh, a)
            s *= 2
        abuf[:, cols] = a
        hbuf[:, cols] = b

    def group(g, h):
        r0 = pl.multiple_of(g * ROW_TILE, ROW_TILE)
        hg = abuf[pl.ds(r0, ROW_TILE), :] * h + hbuf[pl.ds(r0, ROW_TILE), :]
        hbuf[pl.ds(r0, ROW_TILE), :] = hg
        return hg[ROW_TILE - 1:ROW_TILE, :]

    h_end = lax.fori_loop(0, tm // ROW_TILE, group, hc[...])
    hc[...] = h_end
    hl_ref[...] = h_end
    tail = xbuf[tm + ROW_TILE - (CONV_W - 1):tm + ROW_TILE, :]
    cl_ref[...] = tail
    xbuf[ROW_TILE - (CONV_W - 1):ROW_TILE, :] = tail

    mix = _dot((y * hbuf[...]).astype(BF16), wout_ref[...])
    xn_ref[...] = _layer_norm(ALPHA * x + g1p * mix, g_ref[...], b_ref[...])


def _const_spec(shape):
    nd = len(shape)
    return pl.BlockSpec(shape, lambda *_: (0,) * nd)


def _lru_weight_specs(layer):
    sub = 2 * layer
    return [
        _layer_spec((D, 2 * R), layer), _layer_spec((CONV_W, R), layer), _layer_spec((1, R), layer),
        _layer_spec((N_BLK, BLK, BLK), layer), _layer_spec((N_BLK, BLK, BLK), layer),
        _layer_spec((1, R), layer), _layer_spec((1, R), layer), _layer_spec((1, R), layer),
        _layer_spec((R, D), layer), _layer_spec((1, D), sub), _layer_spec((1, D), sub),
    ]


def _lru_prompt(x, mod, lw, ln_g, ln_b, *, layer, nb, seq, tm):
    nt = seq // tm
    n = nb * seq
    return pl.pallas_call(
        functools.partial(_lru_prompt_body, tm=tm),
        grid=(nb, nt),
        in_specs=[
            pl.BlockSpec((tm, D), lambda b, t: (b * nt + t, 0)),
            _layer_spec((nb, 3 * D), 2 * layer),
        ] + _lru_weight_specs(layer),
        out_specs=[
            pl.BlockSpec((tm, D), lambda b, t: (b * nt + t, 0)),
            pl.BlockSpec((None, 1, R), lambda b, t: (b, 0, 0)),
            pl.BlockSpec((None, CONV_W - 1, R), lambda b, t: (b, 0, 0)),
        ],
        out_shape=[
            jax.ShapeDtypeStruct((n, D), F32),
            jax.ShapeDtypeStruct((nb, 1, R), F32),
            jax.ShapeDtypeStruct((nb, CONV_W - 1, R), F32),
        ],
        scratch_shapes=[
            pltpu.VMEM((tm + ROW_TILE, R), F32),
            pltpu.VMEM((tm, R), F32),
            pltpu.VMEM((tm, R), F32),
            pltpu.VMEM((1, R), F32),
        ],
        compiler_params=_cp(("arbitrary", "arbitrary")),
        name="lru_prompt",
    )(x, mod, *lw, ln_g, ln_b)


def _lru_sample_body(x_ref, mod_ref, h0_ref, c0_ref, win_ref, cw_ref, cb_ref, wa_ref, wx_ref,
                     ba_ref, bx_ref, lam_ref, wout_ref, g_ref, b_ref,
                     xn_ref, hl_ref, cl_ref, hbuf, *, nb, steps):
    x = x_ref[...]
    shift, sc1p, g1p = _split_mod(mod_ref, reps=steps)
    u = x * sc1p + shift
    yx = _dot(u.astype(BF16), win_ref[...])
    y = jax.nn.gelu(yx[:, :R])
    xext = jnp.concatenate([c0_ref[...], yx[:, R:]], axis=0)
    cl_ref[...] = xext[steps * nb:, :]
    for n in range(N_BLK):
        cols = slice(n * BLK, (n + 1) * BLK)
        xc = cb_ref[:, cols] + cw_ref[0:1, cols] * xext[0:steps * nb, cols]
        for k in range(1, CONV_W):
            xc = xc + cw_ref[k:k + 1, cols] * xext[k * nb:(k + steps) * nb, cols]
        a, b = _lru_gates(xc, n, wa_ref, wx_ref, ba_ref, bx_ref, lam_ref)
        h = h0_ref[:, cols]
        for t in range(steps):
            h = a[t * nb:(t + 1) * nb, :] * h + b[t * nb:(t + 1) * nb, :]
            hbuf[t * nb:(t + 1) * nb, cols] = h
        hl_ref[:, cols] = h
    mix = _dot((y * hbuf[...]).astype(BF16), wout_ref[...])
    xn_ref[...] = _layer_norm(ALPHA * x + g1p * mix, g_ref[...], b_ref[...])


def _lru_sample(x, mod, h0, c0, lw, ln_g, ln_b, *, layer, nb, steps):
    n = nb * steps
    return pl.pallas_call(
        functools.partial(_lru_sample_body, nb=nb, steps=steps),
        grid=(1,),
        in_specs=[
            _const_spec((n, D)), _layer_spec((nb, 3 * D), 2 * layer), _layer_spec((nb, R), layer),
            _const_spec(((CONV_W - 1) * nb, R)),
        ] + _lru_weight_specs(layer),
        out_specs=[_const_spec((n, D)), _const_spec((nb, R)), _const_spec(((CONV_W - 1) * nb, R))],
        out_shape=[
            jax.ShapeDtypeStruct((n, D), F32),
            jax.ShapeDtypeStruct((nb, R), F32),
            jax.ShapeDtypeStruct(((CONV_W - 1) * nb, R), F32),
        ],
        scratch_shapes=[pltpu.VMEM((n, R), F32)],
        compiler_params=_cp(("arbitrary",)),
        name="lru_sample",
    )(x, mod, h0, c0, *lw, ln_g, ln_b)


def _top2(logits):
    col = lax.broadcasted_iota(jnp.int32, logits.shape, 1).astype(F32)
    big = float(RLANES)
    is_g = col < N_GROUPS
    gl = jnp.where(is_g, logits, NEG)
    gmax = jnp.max(gl, axis=-1, keepdims=True)
    gidx = jnp.min(jnp.where(gl == gmax, col, big), axis=-1, keepdims=True)
    gsum = jnp.sum(jnp.where(is_g, jnp.exp(gl - gmax), 0.0), axis=-1, keepdims=True)
    g_p = 1.0 / gsum
    lo = N_GROUPS + gidx * E_PER_G
    in_grp = (col >= lo) & (col < lo + E_PER_G)
    el = jnp.where(in_grp, logits, NEG)
    t1 = jnp.max(el, axis=-1, keepdims=True)
    i1 = jnp.min(jnp.where(el == t1, col, big), axis=-1, keepdims=True)
    el2 = jnp.where(col == i1, NEG, el)
    t2 = jnp.max(el2, axis=-1, keepdims=True)
    i2 = jnp.min(jnp.where((el2 == t2) & in_grp & (col != i1), col, big), axis=-1, keepdims=True)
    e21 = jnp.exp(t2 - t1)
    w1 = g_p / (1.0 + e21)
    w2 = w1 * e21
    return col, gidx, i1, i2, w1, w2


def _route(logits):
    col, _, i1, i2, w1, w2 = _top2(logits)
    return jnp.where(col == i1, w1, 0.0) + jnp.where(col == i2, w2, 0.0)


PAIRS = E_PER_G * (E_PER_G - 1) // 2
NCLS = N_GROUPS * PAIRS
XW = D + RLANES
FFN_TM = 256
ROUTE_TM = 512
DISPATCH_TD = 1024
COMBINE_TC = 256
DMA_UNROLL = 8


def _route_sort_body(x_ref, mod_ref, wr_ref, br_ref, urow_ref, cnt_ref, carry, *, tiles_per_seq):
    i = pl.program_id(0)

    @pl.when(i == 0)
    def _():
        carry[...] = jnp.zeros(carry.shape, F32)

    tm = x_ref.shape[0]
    shift, sc1p, _ = _split_mod(mod_ref, row=i // tiles_per_seq)
    u = x_ref[...] * sc1p + shift
    col, gidx, i1, i2, w1, w2 = _top2(_dot(u.astype(BF16), wr_ref[...]) + br_ref[...])
    first = N_GROUPS + gidx * E_PER_G
    lo = jnp.minimum(i1, i2) - first
    hi = jnp.maximum(i1, i2) - first
    w_lo = jnp.where(i1 < i2, w1, w2)
    w_hi = jnp.where(i1 < i2, w2, w1)
    cls = gidx * PAIRS + lo * (2 * E_PER_G - 1 - lo) * 0.5 + (hi - lo - 1.0)
    onehot = (col == cls).astype(F32)
    r = lax.broadcasted_iota(jnp.int32, (tm, tm), 0)
    c = lax.broadcasted_iota(jnp.int32, (tm, tm), 1)
    earlier = _dot((c < r).astype(BF16), onehot.astype(BF16)) + carry[...]
    rank = jnp.sum(onehot * earlier, axis=-1, keepdims=True)
    carry[...] += jnp.sum(onehot, axis=0, keepdims=True)
    cnt_ref[...] = carry[...]
    urow_ref[:, 0:D] = u
    urow_ref[:, D:XW] = (jnp.where(col == 0, w_lo, 0.0) + jnp.where(col == 1, w_hi, 0.0)
                         + jnp.where(col == 2, cls, 0.0) + jnp.where(col == 3, rank, 0.0))


def _route_sort(x, mod, wr, br, *, layer, seq):
    n = x.shape[0]
    return pl.pallas_call(
        functools.partial(_route_sort_body, tiles_per_seq=seq // ROUTE_TM),
        grid=(n // ROUTE_TM,),
        in_specs=[
            pl.BlockSpec((ROUTE_TM, D), lambda i: (i, 0)),
            _layer_spec((n // seq, 3 * D), 2 * layer + 1),
            _layer_spec((D, RLANES), layer), _layer_spec((1, RLANES), layer),
        ],
        out_specs=[pl.BlockSpec((ROUTE_TM, XW), lambda i: (i, 0)), _const_spec((1, RLANES))],
        out_shape=[jax.ShapeDtypeStruct((n, XW), F32), jax.ShapeDtypeStruct((1, RLANES), F32)],
        scratch_shapes=[pltpu.VMEM((1, RLANES), F32)],
        compiler_params=_cp(("arbitrary",)),
        name="moe_route",
    )(x, mod, wr, br)


def _row_copy(src, src_row, dst, dst_row, sem):
    return pltpu.make_async_copy(src.at[pl.ds(src_row, 1)], dst.at[pl.ds(dst_row, 1)], sem)


def _dispatch_body(pos_ref, urow_ref, xs_in, xs_hbm, sem):
    del xs_in
    base = pl.program_id(0) * DISPATCH_TD

    def issue(r, carry):
        _row_copy(urow_ref, r, xs_hbm, pos_ref[base + r], sem).start()
        return carry

    lax.fori_loop(0, DISPATCH_TD, issue, 0, unroll=DMA_UNROLL)
    pltpu.make_async_copy(urow_ref, xs_hbm.at[pl.ds(0, DISPATCH_TD)], sem).wait()


def _dispatch(pos, urow, xs_zero):
    n = urow.shape[0]
    any_spec = pl.BlockSpec(memory_space=pl.ANY)
    return pl.pallas_call(
        _dispatch_body,
        grid_spec=pltpu.PrefetchScalarGridSpec(
            num_scalar_prefetch=1, grid=(n // DISPATCH_TD,),
            in_specs=[pl.BlockSpec((DISPATCH_TD, XW), lambda i, pos: (i, 0)), any_spec],
            out_specs=any_spec,
            scratch_shapes=[pltpu.SemaphoreType.DMA(())]),
        out_shape=jax.ShapeDtypeStruct(xs_zero.shape, F32),
        input_output_aliases={2: 0},
        compiler_params=_cp(("arbitrary",)),
        name="moe_dispatch",
    )(pos, urow, xs_zero)


def _ffn_sorted_body(ta_ref, tb_ref, na_ref, xs_ref, w1a, w3a, w2a, w1b, w3b, w2b, ys_ref):
    del ta_ref, tb_ref
    active = pl.program_id(0) < na_ref[0]

    @pl.when(active)
    def _():
        x = xs_ref[:, 0:D].astype(BF16)
        ha = jax.nn.silu(_dot(x, w1a[...])) * _dot(x, w3a[...]) * xs_ref[:, D:D + 1]
        hb = jax.nn.silu(_dot(x, w1b[...])) * _dot(x, w3b[...]) * xs_ref[:, D + 1:D + 2]
        ys_ref[...] = _dot(ha.astype(BF16), w2a[...]) + _dot(hb.astype(BF16), w2b[...])

    @pl.when(jnp.logical_not(active))
    def _():
        ys_ref[...] = jnp.zeros(ys_ref.shape, F32)


def _ffn_sorted(ta, tb, na, xs, w1, w3, w2, *, layer):
    nt = ta.shape[0]
    ea = lambda t, ta, tb, na: (layer, ta[t], 0, 0)
    eb = lambda t, ta, tb, na: (layer, tb[t], 0, 0)
    rows = lambda t, ta, tb, na: (t, 0)
    up, down = (None, None, D, F), (None, None, F, D)
    return pl.pallas_call(
        _ffn_sorted_body,
        grid_spec=pltpu.PrefetchScalarGridSpec(
            num_scalar_prefetch=3, grid=(nt,),
            in_specs=[
                pl.BlockSpec((FFN_TM, XW), rows),
                pl.BlockSpec(up, ea), pl.BlockSpec(up, ea), pl.BlockSpec(down, ea),
                pl.BlockSpec(up, eb), pl.BlockSpec(up, eb), pl.BlockSpec(down, eb),
            ],
            out_specs=pl.BlockSpec((FFN_TM, D), rows)),
        out_shape=jax.ShapeDtypeStruct((nt * FFN_TM, D), F32),
        compiler_params=_cp(("arbitrary",)),
        name="moe_ffn",
    )(ta, tb, na, xs, w1, w3, w2, w1, w3, w2)


def _combine_body(pos_ref, x_ref, mod_ref, ys_hbm, g_ref, b_ref, xn_ref, buf, sem, *, tiles_per_seq):
    i = pl.program_id(0)
    nsteps = pl.num_programs(0)
    tc = x_ref.shape[0]

    def gather(step, slot):
        def issue(r, carry):
            _row_copy(ys_hbm, pos_ref[step * tc + r], buf.at[slot], r, sem.at[slot]).start()
            return carry
        lax.fori_loop(0, tc, issue, 0, unroll=DMA_UNROLL)

    @pl.when(i == 0)
    def _():
        gather(0, 0)

    @pl.when(i + 1 < nsteps)
    def _():
        gather(i + 1, (i + 1) % 2)

    slot = i % 2
    pltpu.make_async_copy(ys_hbm.at[pl.ds(0, tc)], buf.at[slot], sem.at[slot]).wait()
    _, _, g1p = _split_mod(mod_ref, row=i // tiles_per_seq)
    xn_ref[...] = _layer_norm(ALPHA * x_ref[...] + g1p * buf[slot], g_ref[...], b_ref[...])


def _combine(pos, x, mod, ys, ln_g, ln_b, *, layer, seq):
    n = x.shape[0]
    tc = COMBINE_TC
    sub = 2 * layer + 1
    return pl.pallas_call(
        functools.partial(_combine_body, tiles_per_seq=seq // tc),
        grid_spec=pltpu.PrefetchScalarGridSpec(
            num_scalar_prefetch=1, grid=(n // tc,),
            in_specs=[
                pl.BlockSpec((tc, D), lambda i, pos: (i, 0)),
                _layer_spec((n // seq, 3 * D), sub),
                pl.BlockSpec(memory_space=pl.ANY),
                _layer_spec((1, D), sub), _layer_spec((1, D), sub),
            ],
            out_specs=pl.BlockSpec((tc, D), lambda i, pos: (i, 0)),
            scratch_shapes=[pltpu.VMEM((2, tc, D), F32), pltpu.SemaphoreType.DMA((2,))]),
        out_shape=jax.ShapeDtypeStruct((n, D), F32),
        compiler_params=_cp(("arbitrary",)),
        name="moe_combine",
    )(pos, x, mod, ys, ln_g, ln_b)


def _moe_sparse(x, mod, wr, br, w1, w3, w2, ln_g, ln_b, *, layer, seq):
    n = x.shape[0]
    urow, cnt = _route_sort(x, mod, wr, br, layer=layer, seq=seq)
    cls = urow[:, D + 2].astype(jnp.int32)
    rank = urow[:, D + 3].astype(jnp.int32)
    ntile = (cnt[0, :NCLS].astype(jnp.int32) + FFN_TM - 1) // FFN_TM
    tile_end = jnp.cumsum(ntile)
    row_start = (tile_end - ntile) * FFN_TM
    pos = rank + jnp.sum(jax.nn.one_hot(cls, NCLS, dtype=jnp.int32) * row_start[None, :], axis=1)
    nt = n // FFN_TM + NCLS
    total = tile_end[NCLS - 1]
    tr = jnp.minimum(jnp.arange(nt, dtype=jnp.int32), total - 1)
    tcls = jnp.sum((tr[:, None] >= tile_end[None, :]).astype(jnp.int32), axis=1)
    grp, pair = tcls // PAIRS, tcls % PAIRS
    lo = (pair >= 3).astype(jnp.int32) + (pair >= 5).astype(jnp.int32)
    hi = pair + 1 - lo * (2 * E_PER_G - 3 - lo) // 2
    ta = grp * E_PER_G + lo
    tb = grp * E_PER_G + hi
    xs = _dispatch(pos, urow, jnp.zeros((nt * FFN_TM, XW), F32))
    ys = _ffn_sorted(ta, tb, total[None], xs, w1, w3, w2, layer=layer)
    return _combine(pos, x, mod, ys, ln_g, ln_b, layer=layer, seq=seq)


def _moe_body(x_ref, mod_ref, wr_ref, br_ref, w1_ref, w3_ref, w2_ref, g_ref, b_ref,
              xn_ref, ub, comb, acc, *, reps):
    e = pl.program_id(1)

    @pl.when(e == 0)
    def _():
        shift, sc1p, _ = _split_mod(mod_ref, reps=reps)
        u = (x_ref[...] * sc1p + shift).astype(BF16)
        ub[...] = u
        comb[...] = _route(_dot(u, wr_ref[...]) + br_ref[...])
        acc[...] = jnp.zeros(acc.shape, F32)

    u = ub[...]
    col = lax.broadcasted_iota(jnp.int32, comb.shape, 1)
    ce = jnp.sum(jnp.where(col == e + N_GROUPS, comb[...], 0.0), axis=-1, keepdims=True)
    h = jax.nn.silu(_dot(u, w1_ref[...])) * _dot(u, w3_ref[...]) * ce
    acc[...] += _dot(h.astype(BF16), w2_ref[...])

    @pl.when(e == N_EXP - 1)
    def _():
        _, _, g1p = _split_mod(mod_ref, reps=reps)
        xn_ref[...] = _layer_norm(ALPHA * x_ref[...] + g1p * acc[...], g_ref[...], b_ref[...])


def _moe_dense(x, mod, wr, br, w1, w3, w2, ln_g, ln_b, *, layer, reps):
    n = x.shape[0]
    sub = 2 * layer + 1
    expert = lambda i, e: (layer, e, 0, 0)
    return pl.pallas_call(
        functools.partial(_moe_body, reps=reps),
        grid=(1, N_EXP),
        in_specs=[
            pl.BlockSpec((n, D), lambda i, e: (0, 0)),
            _layer_spec((n // reps, 3 * D), sub),
            _layer_spec((D, RLANES), layer), _layer_spec((1, RLANES), layer),
            pl.BlockSpec((None, None, D, F), expert),
            pl.BlockSpec((None, None, D, F), expert),
            pl.BlockSpec((None, None, F, D), expert),
            _layer_spec((1, D), sub), _layer_spec((1, D), sub),
        ],
        out_specs=pl.BlockSpec((n, D), lambda i, e: (0, 0)),
        out_shape=jax.ShapeDtypeStruct((n, D), F32),
        scratch_shapes=[
            pltpu.VMEM((n, D), BF16),
            pltpu.VMEM((n, RLANES), F32),
            pltpu.VMEM((n, D), F32),
        ],
        compiler_params=_cp(("arbitrary", "arbitrary")),
        name="moe",
    )(x, mod, wr, br, w1, w3, w2, ln_g, ln_b)


_NT = (((1,), (1,)), ((), ()))
_TN = (((0,), (0,)), ((), ()))


def _kv_body(x_ref, w_ref, wvt_ref, k_ref, v_ref, k16_ref, vt16_ref):
    xb = x_ref[...].astype(BF16)
    kv = _dot(xb, w_ref[...])
    k_ref[...] = kv[:, :KVW]
    v_ref[...] = kv[:, KVW:]
    k16_ref[...] = kv[:, :KVW].astype(BF16)
    vt16_ref[...] = lax.dot_general(wvt_ref[...], xb, _NT, preferred_element_type=F32).astype(BF16)


def _kv(x, w, wvt, *, tm):
    n = x.shape[0]
    row = lambda i: (i, 0)
    return pl.pallas_call(
        _kv_body,
        grid=(n // tm,),
        in_specs=[pl.BlockSpec((tm, D), row), _const_spec((D, 2 * KVW)), _const_spec((KVW, D))],
        out_specs=[pl.BlockSpec((tm, KVW), row), pl.BlockSpec((tm, KVW), row),
                   pl.BlockSpec((tm, KVW), row), pl.BlockSpec((KVW, tm), lambda i: (0, i))],
        out_shape=[jax.ShapeDtypeStruct((n, KVW), F32), jax.ShapeDtypeStruct((n, KVW), F32),
                   jax.ShapeDtypeStruct((n, KVW), BF16), jax.ShapeDtypeStruct((KVW, n), BF16)],
        compiler_params=_cp(("arbitrary",)),
        name="kv_proj",
    )(x, w, wvt)


def _sink_attend(qh, kh, vh, bias, sink):
    s = lax.dot_general(qh, kh, (((1,), (1,)), ((), ())), preferred_element_type=F32) + bias
    m = jnp.maximum(jnp.max(s, axis=-1, keepdims=True), sink)
    p = jnp.exp(s - m)
    den = jnp.sum(p, axis=-1, keepdims=True) + jnp.exp(sink - m)
    return _dot(p.astype(BF16), vh) / den


def _attn_prompt_body(x_ref, mod_ref, wqt_ref, wo_ref, kc_ref, kp_ref, vtc_ref, vtp_ref,
                      bias_ref, sink_ref, g_ref, b_ref, xn_ref, *, tq):
    t = pl.program_id(1)
    x = x_ref[...]
    shift, sc1p, g1p = _split_mod(mod_ref, row=pl.program_id(0))
    u = (x * sc1p + shift).astype(BF16)
    qt = lax.dot_general(wqt_ref[...], u, _NT, preferred_element_type=F32)
    qt = (qt * (HD ** -0.5)).astype(BF16)
    kc = kc_ref[...]
    vtc = vtc_ref[...]
    krow = lax.broadcasted_iota(jnp.int32, (2 * WIN, GRP * WIN), 0)
    first = jnp.where((krow < WIN) & (t == 0), NEG, 0.0).astype(F32)
    nwin = tq // WIN
    kks, vvts = [], []
    for j in range(nwin):
        if j == 0:
            kks.append(jnp.concatenate([kp_ref[...], kc[0:WIN]], axis=0))
            vvts.append(jnp.concatenate([vtp_ref[...], vtc[:, 0:WIN]], axis=1))
        else:
            kks.append(kc[(j - 1) * WIN:(j + 1) * WIN])
            vvts.append(vtc[:, (j - 1) * WIN:(j + 1) * WIN])

    def scores(j, hk):
        qht = jnp.concatenate(
            [qt[(hk * GRP + g) * HD:(hk * GRP + g + 1) * HD, j * WIN:(j + 1) * WIN] for g in range(GRP)],
            axis=1)
        st = _dot(kks[j][:, hk * HD:(hk + 1) * HD], qht) + bias_ref[hk]
        return st + first if j == 0 else st

    items = [(j, hk) for j in range(nwin) for hk in range(HKV)]
    heads = [[] for _ in range(nwin)]
    st_next = scores(*items[0])
    for n, (j, hk) in enumerate(items):
        st = st_next
        if n + 1 < len(items):
            st_next = scores(*items[n + 1])
        sink = sink_ref[hk]
        m = jnp.maximum(jnp.max(st, axis=0, keepdims=True), sink)
        pt = jnp.exp(st - m)
        den = jnp.sum(pt, axis=0, keepdims=True) + jnp.exp(sink - m)
        ot = _dot(vvts[j][hk * HD:(hk + 1) * HD, :], pt.astype(BF16)) * (1.0 / den)
        for g in range(GRP):
            heads[j].append(ot[:, g * WIN:(g + 1) * WIN])
    blocks = [jnp.concatenate(h, axis=0) for h in heads]
    oallt = jnp.concatenate(blocks, axis=1) if len(blocks) > 1 else blocks[0]
    mix = lax.dot_general(oallt.astype(BF16), wo_ref[...], _TN, preferred_element_type=F32)
    xn_ref[...] = _layer_norm(ALPHA * x + g1p * mix, g_ref[...], b_ref[...])


def _attn_prompt(x, mod, wqt, wo, k16, vt16, bias, sink, ln_g, ln_b, *, layer, nb, seq, tq):
    nt = seq // tq
    wpt = tq // WIN
    n = nb * seq
    j = layer - N_A
    sub = 2 * layer

    def prev(b, t):
        return jnp.maximum((b * nt + t) * wpt - 1, 0)

    cur = lambda b, t: (b * nt + t, 0)
    return pl.pallas_call(
        functools.partial(_attn_prompt_body, tq=tq),
        grid=(nb, nt),
        in_specs=[
            pl.BlockSpec((tq, D), cur),
            _layer_spec((nb, 3 * D), sub),
            _layer_spec((HQ * HD, D), j), _layer_spec((HQ * HD, D), j),
            pl.BlockSpec((tq, KVW), cur), pl.BlockSpec((WIN, KVW), lambda b, t: (prev(b, t), 0)),
            pl.BlockSpec((KVW, tq), lambda b, t: (0, b * nt + t)),
            pl.BlockSpec((KVW, WIN), lambda b, t: (0, prev(b, t))),
            _const_spec((HKV, 2 * WIN, GRP * WIN)), _layer_spec((HKV, 1, GRP * WIN), j),
            _layer_spec((1, D), sub), _layer_spec((1, D), sub),
        ],
        out_specs=pl.BlockSpec((tq, D), cur),
        out_shape=jax.ShapeDtypeStruct((n, D), F32),
        compiler_params=_cp(("arbitrary", "arbitrary")),
        name="attn_prompt",
    )(x, mod, wqt, wo, k16, k16, vt16, vt16, bias, sink, ln_g, ln_b)


def _attn_sample_body(x_ref, mod_ref, wq_ref, wo_ref, kb_ref, vb_ref, kn_ref, vn_ref,
                      bias_ref, sink_ref, g_ref, b_ref, xn_ref, *, ns, steps):
    nr = ns * steps
    x = x_ref[...].reshape(nr, D)
    shift, sc1p, g1p = _split_mod(mod_ref, reps=steps)
    u = (x * sc1p + shift).astype(BF16)
    q = (_dot(u, wq_ref[...]) * (HD ** -0.5)).astype(BF16)
    nbuf = kb_ref.shape[1]
    k_all = jnp.concatenate([kb_ref[...].reshape(ns * nbuf, KVW), kn_ref[...].reshape(nr, KVW)],
                            axis=0).astype(BF16)
    v_all = jnp.concatenate([vb_ref[...].reshape(ns * nbuf, KVW), vn_ref[...].reshape(nr, KVW)],
                            axis=0).astype(BF16)
    heads = [None] * HQ
    for hk in range(HKV):
        kcols = slice(hk * HD, (hk + 1) * HD)
        qh = jnp.concatenate(
            [q[:, (hk * GRP + g) * HD:(hk * GRP + g + 1) * HD] for g in range(GRP)], axis=0)
        o = _sink_attend(qh, k_all[:, kcols], v_all[:, kcols], bias_ref[hk], sink_ref[hk])
        for g in range(GRP):
            heads[hk * GRP + g] = o[g * nr:(g + 1) * nr]
    o_all = jnp.concatenate(heads, axis=1)
    mix = _dot(o_all.astype(BF16), wo_ref[...])
    xn_ref[...] = _layer_norm(ALPHA * x + g1p * mix, g_ref[...], b_ref[...]).reshape(steps, ns, D)


def _attn_sample(x, mod, wq, wo, kbuf, vbuf, kn, vn, bias, sink, ln_g, ln_b, *, layer, ns):
    steps, nb, _ = x.shape
    nbuf = kbuf.shape[1]
    j = layer - N_A
    sub = 2 * layer
    seqs = lambda i: (0, i, 0)
    return pl.pallas_call(
        functools.partial(_attn_sample_body, ns=ns, steps=steps),
        grid=(nb // ns,),
        in_specs=[
            pl.BlockSpec((steps, ns, D), seqs),
            pl.BlockSpec((None, ns, 3 * D), lambda i: (sub, i, 0)),
            _layer_spec((D, HQ * HD), j), _layer_spec((HQ * HD, D), j),
            pl.BlockSpec((ns, nbuf, KVW), lambda i: (i, 0, 0)),
            pl.BlockSpec((ns, nbuf, KVW), lambda i: (i, 0, 0)),
            pl.BlockSpec((steps, ns, KVW), seqs), pl.BlockSpec((steps, ns, KVW), seqs),
            _const_spec(bias.shape), _layer_spec(sink.shape[1:], j),
            _layer_spec((1, D), sub), _layer_spec((1, D), sub),
        ],
        out_specs=pl.BlockSpec((steps, ns, D), seqs),
        out_shape=jax.ShapeDtypeStruct((steps, nb, D), F32),
        compiler_params=_cp(("arbitrary",)),
        name="attn_sample",
    )(x, mod, wq, wo, kbuf, vbuf, kn, vn, bias, sink, ln_g, ln_b)


def _t5_bucket(d):
    max_exact = N_BUCKETS // 2
    d = jnp.maximum(d, 0)
    log_ratio = jnp.log(jnp.maximum(d, 1).astype(F32) / max_exact) / math.log(WIN / max_exact)
    large = max_exact + (log_ratio * (N_BUCKETS - max_exact)).astype(jnp.int32)
    large = jnp.minimum(large, N_BUCKETS - 1)
    return jnp.where(d < max_exact, d, large)


def _bias_table(d, ok, rel_bias):
    tab = jnp.concatenate([rel_bias[_t5_bucket(jnp.arange(WIN))].astype(F32),
                           jnp.full((1, HQ), NEG, F32)], axis=0)
    sel = jax.nn.one_hot(jnp.where(ok, d, WIN), WIN + 1, dtype=F32)
    return jnp.einsum('abd,dh->hab', sel, tab, precision=lax.Precision.HIGHEST)


def _prompt_bias(rel_bias):
    d = jnp.arange(WIN)[None, :] + WIN - jnp.arange(2 * WIN)[:, None]
    b = _bias_table(d, (d >= 0) & (d < WIN), rel_bias).reshape(HKV, GRP, 2 * WIN, WIN)
    return jnp.transpose(b, (0, 2, 1, 3)).reshape(HKV, 2 * WIN, GRP * WIN)


def _sample_bias(rel_bias, ns, steps, nbuf):
    qs = jnp.tile(jnp.arange(ns), steps)
    qt = jnp.repeat(jnp.arange(steps), ns)
    ks = jnp.concatenate([jnp.repeat(jnp.arange(ns), nbuf), qs])
    kpos = jnp.concatenate([jnp.tile(jnp.arange(nbuf), ns), nbuf + qt])
    d = qt[:, None] + nbuf - kpos[None, :]
    ok = (d >= 0) & (d < WIN) & (qs[:, None] == ks[None, :])
    return _bias_table(d, ok, rel_bias).reshape(HKV, GRP * ns * steps, kpos.shape[0])


LRU_TM = 512
ATTN_TQ = 512
SAMPLE_NS = 8


def kernel(x_prompt, x_sample, state_rnn_h, state_rnn_conv, cache_win_k, cache_win_v, c_prompt, c_sample, ada_w, ada_b, ln_g, ln_b, lru_w_in, lru_conv_w, lru_conv_b, lru_w_a, lru_b_a, lru_w_x, lru_b_x, lru_lambda, lru_w_out, kv_w, attn_w_q, attn_sinks, attn_w_o, rel_bias, moe_w_group, moe_b_group, moe_w_router, moe_b_router, moe_w1, moe_w3, moe_w2):
    nbp, seq, _ = x_prompt.shape
    nbs, steps, _ = x_sample.shape
    nbuf = cache_win_k.shape[1]
    npt = nbp * seq
    nst = nbs * steps

    mod_p, mod_s = _adaln(c_prompt, c_sample, ada_w, ada_b)

    ln_g3 = ln_g.reshape(2 * DEPTH, 1, D)
    ln_b3 = ln_b.reshape(2 * DEPTH, 1, D)
    lw = (lru_w_in.astype(BF16), lru_conv_w, lru_conv_b[:, None], lru_w_a.astype(BF16),
          lru_w_x.astype(BF16), lru_b_a[:, None], lru_b_x[:, None], lru_lambda[:, None],
          lru_w_out.astype(BF16))
    wr = jnp.concatenate([moe_w_group, moe_w_router.reshape(DEPTH, D, N_EXP)], axis=2)
    wr = jnp.pad(wr, ((0, 0), (0, 0), (0, RLANES - wr.shape[2]))).astype(BF16)
    br = jnp.concatenate([moe_b_group, moe_b_router.reshape(DEPTH, N_EXP)], axis=1)
    br = jnp.pad(br, ((0, 0), (0, RLANES - br.shape[1])))[:, None]
    mw = (wr, br, moe_w1.astype(BF16), moe_w3.astype(BF16), moe_w2.astype(BF16))
    wq = attn_w_q.astype(BF16)
    wqt = jnp.swapaxes(wq, 1, 2)
    wo = attn_w_o.astype(BF16)
    kvw = kv_w.astype(BF16)
    kvwt = kvw[:, KVW:].T
    bias_p = _prompt_bias(rel_bias)
    bias_s = _sample_bias(rel_bias, SAMPLE_NS, steps, nbuf)
    sink_p = jnp.repeat(attn_sinks.reshape(N_B, HKV, GRP), WIN, axis=2)[:, :, None, :]
    sink_s = jnp.repeat(attn_sinks.reshape(N_B, HKV, GRP), SAMPLE_NS * steps, axis=2)[..., None]
    kbuf = cache_win_k.reshape(nbs, nbuf, KVW)
    vbuf = cache_win_v.reshape(nbs, nbuf, KVW)

    xp = x_prompt.reshape(npt, D)
    xs = jnp.transpose(x_sample, (1, 0, 2)).reshape(nst, D)
    c0 = jnp.transpose(state_rnn_conv, (0, 2, 1, 3)).reshape(N_A, (CONV_W - 1) * nbs, R)
    hp, cp_, hs, cs = [], [], [], []
    kp = vp = kn = vn = kp16 = vpt16 = None
    for l in range(DEPTH):
        if l < N_A:
            xp, h_l, c_l = _lru_prompt(xp, mod_p, lw, ln_g3, ln_b3, layer=l, nb=nbp, seq=seq, tm=LRU_TM)
            hp.append(h_l[:, 0])
            cp_.append(c_l)
            xs, h_l, c_l = _lru_sample(xs, mod_s, state_rnn_h, c0[l], lw, ln_g3, ln_b3,
                                       layer=l, nb=nbs, steps=steps)
            hs.append(h_l)
            cs.append(jnp.transpose(c_l.reshape(CONV_W - 1, nbs, R), (1, 0, 2)))
        else:
            xp = _attn_prompt(xp, mod_p, wqt, wo, kp16, vpt16, bias_p, sink_p, ln_g3, ln_b3,
                              layer=l, nb=nbp, seq=seq, tq=ATTN_TQ)
            xs = _attn_sample(xs.reshape(steps, nbs, D), mod_s, wq, wo, kbuf, vbuf,
                              kn.reshape(steps, nbs, KVW), vn.reshape(steps, nbs, KVW),
                              bias_s, sink_s, ln_g3, ln_b3, layer=l, ns=SAMPLE_NS).reshape(nst, D)
        xp = _moe_sparse(xp, mod_p, *mw, ln_g3, ln_b3, layer=l, seq=seq)
        xs = _moe_dense(xs, mod_s, *mw, ln_g3, ln_b3, layer=l, reps=steps)
        if l == N_A - 1:
            kp, vp, kp16, vpt16 = _kv(xp, kvw, kvwt, tm=1024)
            kn, vn, _, _ = _kv(xs, kvw, kvwt, tm=nst)

    lp = min(WIN, seq)
    k_p = kp.reshape(nbp, seq, HKV, HD)[:, seq - lp:]
    v_p = vp.reshape(nbp, seq, HKV, HD)[:, seq - lp:]
    kn4 = jnp.transpose(kn.reshape(steps, nbs, HKV, HD), (1, 0, 2, 3))
    vn4 = jnp.transpose(vn.reshape(steps, nbs, HKV, HD), (1, 0, 2, 3))
    k_s = jnp.concatenate([cache_win_k, kn4], axis=1)[:, -nbuf:]
    v_s = jnp.concatenate([cache_win_v, vn4], axis=1)[:, -nbuf:]
    y_s = jnp.transpose(xs.reshape(steps, nbs, D), (1, 0, 2))
    return (xp.reshape(nbp, seq, D), y_s,
            jnp.stack(hp), jnp.stack(cp_), k_p, v_p,
            jnp.stack(hs), jnp.stack(cs), k_s, v_s)
```

```python
import functools
import math

import jax
import jax.numpy as jnp
from jax import lax
from jax.experimental import pallas as pl
from jax.experimental.pallas import tpu as pltpu

D = 1024
R = 1024
DEPTH = 4
N_A = 2
N_B = DEPTH - N_A
N_BLK = 4
BLK = R // N_BLK
CONV_W = 4
RG_C = 8.0
HQ = 16
HKV = 4
HD = 64
GRP = HQ // HKV
KVW = HKV * HD
WIN = 128
N_BUCKETS = 32
N_GROUPS = 4
E_PER_G = 4
N_EXP = 16
F = 512
ALPHA = (2.0 * DEPTH) ** 0.25
LN_EPS = 1e-5
NEG = -1e30
RLANES = 128
ROW_TILE = 8

F32 = jnp.float32
BF16 = jnp.bfloat16

VMEM_LIMIT = 56 * 1024 * 1024


def _cp(sem):
    return pltpu.CompilerParams(dimension_semantics=sem, vmem_limit_bytes=VMEM_LIMIT)


def _dot(a, b):
    return jnp.dot(a, b, preferred_element_type=F32)


def _layer_norm(z, g, b):
    mu = jnp.mean(z, axis=-1, keepdims=True)
    zc = z - mu
    var = jnp.mean(zc * zc, axis=-1, keepdims=True)
    return zc * lax.rsqrt(var + LN_EPS) * g + b


def _split_mod(mod_ref, row=None, reps=1):
    m = mod_ref[...] if row is None else mod_ref[pl.ds(row, 1), :]
    if reps > 1:
        m = jnp.concatenate([m] * reps, axis=0)
    return m[:, 0:D], m[:, D:2 * D], m[:, 2 * D:3 * D]


def _layer_spec(shape, layer):
    nd = len(shape)
    return pl.BlockSpec((None,) + tuple(shape), lambda *_: (layer,) + (0,) * nd)


def _adaln_body(cp_ref, cs_ref, w_ref, b_ref, op_ref, os_ref):
    one = jnp.where(pl.program_id(1) > 0, 1.0, 0.0).astype(F32)
    w = w_ref[...].astype(BF16)
    op_ref[...] = _dot(cp_ref[...].astype(BF16), w) + b_ref[...] + one
    os_ref[...] = _dot(cs_ref[...].astype(BF16), w) + b_ref[...] + one


def _adaln(c_p, c_s, ada_w, ada_b):
    np_, ns_ = c_p.shape[0], c_s.shape[0]
    w = ada_w.reshape(2 * DEPTH, D, 3 * D)
    b = ada_b.reshape(2 * DEPTH, 1, 3 * D)
    return pl.pallas_call(
        _adaln_body,
        grid=(2 * DEPTH, 3),
        in_specs=[
            pl.BlockSpec((np_, D), lambda l, j: (0, 0)),
            pl.BlockSpec((ns_, D), lambda l, j: (0, 0)),
            pl.BlockSpec((None, D, D), lambda l, j: (l, 0, j)),
            pl.BlockSpec((None, 1, D), lambda l, j: (l, 0, j)),
        ],
        out_specs=[pl.BlockSpec((None, np_, D), lambda l, j: (l, 0, j)),
                   pl.BlockSpec((None, ns_, D), lambda l, j: (l, 0, j))],
        out_shape=[jax.ShapeDtypeStruct((2 * DEPTH, np_, 3 * D), F32),
                   jax.ShapeDtypeStruct((2 * DEPTH, ns_, 3 * D), F32)],
        compiler_params=_cp(("arbitrary", "arbitrary")),
        name="adaln",
    )(c_p, c_s, w, b)


def _lru_gates(xc, n, wa_ref, wx_ref, ba_ref, bx_ref, lam_ref):
    cols = slice(n * BLK, (n + 1) * BLK)
    xb = xc.astype(BF16)
    r = jax.nn.sigmoid(_dot(xb, wa_ref[n]) + ba_ref[:, cols])
    i = jax.nn.sigmoid(_dot(xb, wx_ref[n]) + bx_ref[:, cols])
    z = -lam_ref[:, cols]
    softplus = jnp.maximum(z, 0.0) + jnp.log1p(jnp.exp(-jnp.abs(z)))
    log_a = (-RG_C) * r * softplus
    a = jnp.exp(log_a)
    b = jnp.sqrt(1.0 - a * a) * i * xc
    return a, b


def _lru_prompt_body(x_ref, mod_ref, win_ref, cw_ref, cb_ref, wa_ref, wx_ref, ba_ref, bx_ref,
                     lam_ref, wout_ref, g_ref, b_ref,
                     xn_ref, hl_ref, cl_ref,
                     xbuf, abuf, hbuf, hc, *, tm):
    t = pl.program_id(1)

    @pl.when(t == 0)
    def _():
        xbuf[0:ROW_TILE, :] = jnp.zeros((ROW_TILE, R), F32)
        hc[...] = jnp.zeros((1, R), F32)

    x = x_ref[...]
    shift, sc1p, g1p = _split_mod(mod_ref, row=pl.program_id(0))
    u = x * sc1p + shift
    yx = _dot(u.astype(BF16), win_ref[...])
    y = jax.nn.gelu(yx[:, :R])
    xbuf[ROW_TILE:ROW_TILE + tm, :] = yx[:, R:]

    rowmod = lax.broadcasted_iota(jnp.int32, (tm, BLK), 0) & (ROW_TILE - 1)
    for n in range(N_BLK):
        cols = slice(n * BLK, (n + 1) * BLK)
        xc = cb_ref[:, cols] + cw_ref[3:4, cols] * xbuf[ROW_TILE:ROW_TILE + tm, cols]
        for k in range(1, CONV_W):
            xc = xc + cw_ref[3 - k:4 - k, cols] * xbuf[ROW_TILE - k:ROW_TILE - k + tm, cols]
        a, b = _lru_gates(xc, n, wa_ref, wx_ref, ba_ref, bx_ref, lam_ref)
        s = 1
        while s < ROW_TILE:
            a_sh = pltpu.roll(a, s, 0)
            b_sh = pltpu.roll(b, s, 0)
            m = rowmod >= s
            b = jnp.where(m, a * b_sh, 0.0) + b
            a = jnp.where(m, a * a_sh, a)
            s *= 2
        abuf[:, cols] = a
        hbuf[:, cols] = b

    def group(g, h):
        r0 = pl.multiple_of(g * ROW_TILE, ROW_TILE)
        hg = abuf[pl.ds(r0, ROW_TILE), :] * h + hbuf[pl.ds(r0, ROW_TILE), :]
        hbuf[pl.ds(r0, ROW_TILE), :] = hg
        return hg[ROW_TILE - 1:ROW_TILE, :]

    h_end = lax.fori_loop(0, tm // ROW_TILE, group, hc[...])
    hc[...] = h_end
    hl_ref[...] = h_end
    tail = xbuf[tm + ROW_TILE - (CONV_W - 1):tm + ROW_TILE, :]
    cl_ref[...] = tail
    xbuf[ROW_TILE - (CONV_W - 1):ROW_TILE, :] = tail

    mix = _dot((y * hbuf[...]).astype(BF16), wout_ref[...])
    xn_ref[...] = _layer_norm(ALPHA * x + g1p * mix, g_ref[...], b_ref[...])


def _const_spec(shape):
    nd = len(shape)
    return pl.BlockSpec(shape, lambda *_: (0,) * nd)


def _lru_weight_specs(layer):
    sub = 2 * layer
    return [
        _layer_spec((D, 2 * R), layer), _layer_spec((CONV_W, R), layer), _layer_spec((1, R), layer),
        _layer_spec((N_BLK, BLK, BLK), layer), _layer_spec((N_BLK, BLK, BLK), layer),
        _layer_spec((1, R), layer), _layer_spec((1, R), layer), _layer_spec((1, R), layer),
        _layer_spec((R, D), layer), _layer_spec((1, D), sub), _layer_spec((1, D), sub),
    ]


def _lru_prompt(x, mod, lw, ln_g, ln_b, *, layer, nb, seq, tm):
    nt = seq // tm
    n = nb * seq
    return pl.pallas_call(
        functools.partial(_lru_prompt_body, tm=tm),
        grid=(nb, nt),
        in_specs=[
            pl.BlockSpec((tm, D), lambda b, t: (b * nt + t, 0)),
            _layer_spec((nb, 3 * D), 2 * layer),
        ] + _lru_weight_specs(layer),
        out_specs=[
            pl.BlockSpec((tm, D), lambda b, t: (b * nt + t, 0)),
            pl.BlockSpec((None, 1, R), lambda b, t: (b, 0, 0)),
            pl.BlockSpec((None, CONV_W - 1, R), lambda b, t: (b, 0, 0)),
        ],
        out_shape=[
            jax.ShapeDtypeStruct((n, D), F32),
            jax.ShapeDtypeStruct((nb, 1, R), F32),
            jax.ShapeDtypeStruct((nb, CONV_W - 1, R), F32),
        ],
        scratch_shapes=[
            pltpu.VMEM((tm + ROW_TILE, R), F32),
            pltpu.VMEM((tm, R), F32),
            pltpu.VMEM((tm, R), F32),
            pltpu.VMEM((1, R), F32),
        ],
        compiler_params=_cp(("arbitrary", "arbitrary")),
        name="lru_prompt",
    )(x, mod, *lw, ln_g, ln_b)


def _lru_sample_body(x_ref, mod_ref, h0_ref, c0_ref, win_ref, cw_ref, cb_ref, wa_ref, wx_ref,
                     ba_ref, bx_ref, lam_ref, wout_ref, g_ref, b_ref,
                     xn_ref, hl_ref, cl_ref, hbuf, *, nb, steps):
    x = x_ref[...]
    shift, sc1p, g1p = _split_mod(mod_ref, reps=steps)
    u = x * sc1p + shift
    yx = _dot(u.astype(BF16), win_ref[...])
    y = jax.nn.gelu(yx[:, :R])
    xext = jnp.concatenate([c0_ref[...], yx[:, R:]], axis=0)
    cl_ref[...] = xext[steps * nb:, :]
    for n in range(N_BLK):
        cols = slice(n * BLK, (n + 1) * BLK)
        xc = cb_ref[:, cols] + cw_ref[0:1, cols] * xext[0:steps * nb, cols]
        for k in range(1, CONV_W):
            xc = xc + cw_ref[k:k + 1, cols] * xext[k * nb:(k + steps) * nb, cols]
        a, b = _lru_gates(xc, n, wa_ref, wx_ref, ba_ref, bx_ref, lam_ref)
        h = h0_ref[:, cols]
        for t in range(steps):
            h = a[t * nb:(t + 1) * nb, :] * h + b[t * nb:(t + 1) * nb, :]
            hbuf[t * nb:(t + 1) * nb, cols] = h
        hl_ref[:, cols] = h
    mix = _dot((y * hbuf[...]).astype(BF16), wout_ref[...])
    xn_ref[...] = _layer_norm(ALPHA * x + g1p * mix, g_ref[...], b_ref[...])


def _lru_sample(x, mod, h0, c0, lw, ln_g, ln_b, *, layer, nb, steps):
    n = nb * steps
    return pl.pallas_call(
        functools.partial(_lru_sample_body, nb=nb, steps=steps),
        grid=(1,),
        in_specs=[
            _const_spec((n, D)), _layer_spec((nb, 3 * D), 2 * layer), _layer_spec((nb, R), layer),
            _const_spec(((CONV_W - 1) * nb, R)),
        ] + _lru_weight_specs(layer),
        out_specs=[_const_spec((n, D)), _const_spec((nb, R)), _const_spec(((CONV_W - 1) * nb, R))],
        out_shape=[
            jax.ShapeDtypeStruct((n, D), F32),
            jax.ShapeDtypeStruct((nb, R), F32),
            jax.ShapeDtypeStruct(((CONV_W - 1) * nb, R), F32),
        ],
        scratch_shapes=[pltpu.VMEM((n, R), F32)],
        compiler_params=_cp(("arbitrary",)),
        name="lru_sample",
    )(x, mod, h0, c0, *lw, ln_g, ln_b)


def _top2(logits):
    col = lax.broadcasted_iota(jnp.int32, logits.shape, 1).astype(F32)
    big = float(RLANES)
    is_g = col < N_GROUPS
    gl = jnp.where(is_g, logits, NEG)
    gmax = jnp.max(gl, axis=-1, keepdims=True)
    gidx = jnp.min(jnp.where(gl == gmax, col, big), axis=-1, keepdims=True)
    gsum = jnp.sum(jnp.where(is_g, jnp.exp(gl - gmax), 0.0), axis=-1, keepdims=True)
    g_p = 1.0 / gsum
    lo = N_GROUPS + gidx * E_PER_G
    in_grp = (col >= lo) & (col < lo + E_PER_G)
    el = jnp.where(in_grp, logits, NEG)
    t1 = jnp.max(el, axis=-1, keepdims=True)
    i1 = jnp.min(jnp.where(el == t1, col, big), axis=-1, keepdims=True)
    el2 = jnp.where(col == i1, NEG, el)
    t2 = jnp.max(el2, axis=-1, keepdims=True)
    i2 = jnp.min(jnp.where((el2 == t2) & in_grp & (col != i1), col, big), axis=-1, keepdims=True)
    e21 = jnp.exp(t2 - t1)
    w1 = g_p / (1.0 + e21)
    w2 = w1 * e21
    return col, gidx, i1, i2, w1, w2


def _route(logits):
    col, _, i1, i2, w1, w2 = _top2(logits)
    return jnp.where(col == i1, w1, 0.0) + jnp.where(col == i2, w2, 0.0)


PAIRS = E_PER_G * (E_PER_G - 1) // 2
NCLS = N_GROUPS * PAIRS
XW = D + RLANES
FFN_TM = 256
ROUTE_TM = 512
DISPATCH_TD = 1024
COMBINE_TC = 256


def _route_sort_body(x_ref, mod_ref, wr_ref, br_ref, urow_ref, cnt_ref, carry, *, tiles_per_seq):
    i = pl.program_id(0)

    @pl.when(i == 0)
    def _():
        carry[...] = jnp.zeros(carry.shape, F32)

    tm = x_ref.shape[0]
    shift, sc1p, _ = _split_mod(mod_ref, row=i // tiles_per_seq)
    u = x_ref[...] * sc1p + shift
    col, gidx, i1, i2, w1, w2 = _top2(_dot(u.astype(BF16), wr_ref[...]) + br_ref[...])
    first = N_GROUPS + gidx * E_PER_G
    lo = jnp.minimum(i1, i2) - first
    hi = jnp.maximum(i1, i2) - first
    w_lo = jnp.where(i1 < i2, w1, w2)
    w_hi = jnp.where(i1 < i2, w2, w1)
    cls = gidx * PAIRS + lo * (2 * E_PER_G - 1 - lo) * 0.5 + (hi - lo - 1.0)
    onehot = (col == cls).astype(F32)
    r = lax.broadcasted_iota(jnp.int32, (tm, tm), 0)
    c = lax.broadcasted_iota(jnp.int32, (tm, tm), 1)
    earlier = _dot((c < r).astype(BF16), onehot.astype(BF16)) + carry[...]
    rank = jnp.sum(onehot * earlier, axis=-1, keepdims=True)
    carry[...] += jnp.sum(onehot, axis=0, keepdims=True)
    cnt_ref[...] = carry[...]
    urow_ref[:, 0:D] = u
    urow_ref[:, D:XW] = (jnp.where(col == 0, w_lo, 0.0) + jnp.where(col == 1, w_hi, 0.0)
                         + jnp.where(col == 2, cls, 0.0) + jnp.where(col == 3, rank, 0.0))


def _route_sort(x, mod, wr, br, *, layer, seq):
    n = x.shape[0]
    return pl.pallas_call(
        functools.partial(_route_sort_body, tiles_per_seq=seq // ROUTE_TM),
        grid=(n // ROUTE_TM,),
        in_specs=[
            pl.BlockSpec((ROUTE_TM, D), lambda i: (i, 0)),
            _layer_spec((n // seq, 3 * D), 2 * layer + 1),
            _layer_spec((D, RLANES), layer), _layer_spec((1, RLANES), layer),
        ],
        out_specs=[pl.BlockSpec((ROUTE_TM, XW), lambda i: (i, 0)), _const_spec((1, RLANES))],
        out_shape=[jax.ShapeDtypeStruct((n, XW), F32), jax.ShapeDtypeStruct((1, RLANES), F32)],
        scratch_shapes=[pltpu.VMEM((1, RLANES), F32)],
        compiler_params=_cp(("arbitrary",)),
        name="moe_route",
    )(x, mod, wr, br)


def _row_copy(src, src_row, dst, dst_row, sem):
    return pltpu.make_async_copy(src.at[pl.ds(src_row, 1)], dst.at[pl.ds(dst_row, 1)], sem)


def _dispatch_body(pos_ref, zflag_ref, urow_ref, xs_hbm, zbuf, sem, zsem):
    base = pl.program_id(0) * DISPATCH_TD

    @pl.when(pl.program_id(0) == 0)
    def _():
        zbuf[...] = jnp.zeros(zbuf.shape, F32)

        def zero_copy(t):
            return pltpu.make_async_copy(zbuf, xs_hbm.at[pl.ds(t * FFN_TM, FFN_TM)], zsem)

        def start(t, carry):
            @pl.when(zflag_ref[t] != 0)
            def _():
                zero_copy(t).start()
            return carry

        def wait(t, carry):
            @pl.when(zflag_ref[t] != 0)
            def _():
                zero_copy(t).wait()
            return carry

        lax.fori_loop(0, zflag_ref.shape[0], start, 0)
        lax.fori_loop(0, zflag_ref.shape[0], wait, 0)

    def issue(g, carry):
        r0 = pl.multiple_of(g * ROW_TILE, ROW_TILE)
        for k in range(ROW_TILE):
            _row_copy(urow_ref.at[pl.ds(r0, ROW_TILE)], k, xs_hbm, pos_ref[base + r0 + k], sem).start()
        return carry

    lax.fori_loop(0, DISPATCH_TD // ROW_TILE, issue, 0)
    pltpu.make_async_copy(urow_ref, xs_hbm.at[pl.ds(0, DISPATCH_TD)], sem).wait()


def _dispatch(pos, zflag, urow):
    n = urow.shape[0]
    nt = zflag.shape[0]
    return pl.pallas_call(
        _dispatch_body,
        grid_spec=pltpu.PrefetchScalarGridSpec(
            num_scalar_prefetch=2, grid=(n // DISPATCH_TD,),
            in_specs=[pl.BlockSpec((DISPATCH_TD, XW), lambda i, pos, zflag: (i, 0))],
            out_specs=pl.BlockSpec(memory_space=pl.ANY),
            scratch_shapes=[pltpu.VMEM((FFN_TM, XW), F32), pltpu.SemaphoreType.DMA(()),
                            pltpu.SemaphoreType.DMA(())]),
        out_shape=jax.ShapeDtypeStruct((nt * FFN_TM, XW), F32),
        compiler_params=_cp(("arbitrary",)),
        name="moe_dispatch",
    )(pos, zflag, urow)


def _ffn_sorted_body(ta_ref, tb_ref, na_ref, xs_ref, w1a, w3a, w2a, w1b, w3b, w2b, ys_ref):
    del ta_ref, tb_ref
    active = pl.program_id(0) < na_ref[0]

    @pl.when(active)
    def _():
        x = xs_ref[:, 0:D].astype(BF16)
        ha = jax.nn.silu(_dot(x, w1a[...])) * _dot(x, w3a[...]) * xs_ref[:, D:D + 1]
        hb = jax.nn.silu(_dot(x, w1b[...])) * _dot(x, w3b[...]) * xs_ref[:, D + 1:D + 2]
        ys_ref[...] = _dot(ha.astype(BF16), w2a[...]) + _dot(hb.astype(BF16), w2b[...])

    @pl.when(jnp.logical_not(active))
    def _():
        ys_ref[...] = jnp.zeros(ys_ref.shape, F32)


def _ffn_sorted(ta, tb, na, xs, w1, w3, w2):
    nt = ta.shape[0]
    ea = lambda t, ta, tb, na: (ta[t], 0, 0)
    eb = lambda t, ta, tb, na: (tb[t], 0, 0)
    rows = lambda t, ta, tb, na: (t, 0)
    up, down = (None, D, F), (None, F, D)
    return pl.pallas_call(
        _ffn_sorted_body,
        grid_spec=pltpu.PrefetchScalarGridSpec(
            num_scalar_prefetch=3, grid=(nt,),
            in_specs=[
                pl.BlockSpec((FFN_TM, XW), rows),
                pl.BlockSpec(up, ea), pl.BlockSpec(up, ea), pl.BlockSpec(down, ea),
                pl.BlockSpec(up, eb), pl.BlockSpec(up, eb), pl.BlockSpec(down, eb),
            ],
            out_specs=pl.BlockSpec((FFN_TM, D), rows)),
        out_shape=jax.ShapeDtypeStruct((nt * FFN_TM, D), F32),
        compiler_params=_cp(("arbitrary",)),
        name="moe_ffn",
    )(ta, tb, na, xs, w1, w3, w2, w1, w3, w2)


def _combine_body(pos_ref, x_ref, mod_ref, ys_hbm, g_ref, b_ref, xn_ref, buf, sem, *, tiles_per_seq):
    i = pl.program_id(0)
    nsteps = pl.num_programs(0)
    tc = x_ref.shape[0]

    def gather(step, slot):
        def issue(g, carry):
            r0 = pl.multiple_of(g * ROW_TILE, ROW_TILE)
            group = buf.at[slot, pl.ds(r0, ROW_TILE)]
            for k in range(ROW_TILE):
                _row_copy(ys_hbm, pos_ref[step * tc + r0 + k], group, k, sem.at[slot]).start()
            return carry
        lax.fori_loop(0, tc // ROW_TILE, issue, 0)

    @pl.when(i == 0)
    def _():
        gather(0, 0)

    @pl.when(i + 1 < nsteps)
    def _():
        gather(i + 1, (i + 1) % 2)

    slot = i % 2
    pltpu.make_async_copy(ys_hbm.at[pl.ds(0, tc)], buf.at[slot], sem.at[slot]).wait()
    _, _, g1p = _split_mod(mod_ref, row=i // tiles_per_seq)
    xn_ref[...] = _layer_norm(ALPHA * x_ref[...] + g1p * buf[slot], g_ref[...], b_ref[...])


def _combine(pos, x, mod, ys, ln_g, ln_b, *, layer, seq):
    n = x.shape[0]
    tc = COMBINE_TC
    sub = 2 * layer + 1
    return pl.pallas_call(
        functools.partial(_combine_body, tiles_per_seq=seq // tc),
        grid_spec=pltpu.PrefetchScalarGridSpec(
            num_scalar_prefetch=1, grid=(n // tc,),
            in_specs=[
                pl.BlockSpec((tc, D), lambda i, pos: (i, 0)),
                _layer_spec((n // seq, 3 * D), sub),
                pl.BlockSpec(memory_space=pl.ANY),
                _layer_spec((1, D), sub), _layer_spec((1, D), sub),
            ],
            out_specs=pl.BlockSpec((tc, D), lambda i, pos: (i, 0)),
            scratch_shapes=[pltpu.VMEM((2, tc, D), F32), pltpu.SemaphoreType.DMA((2,))]),
        out_shape=jax.ShapeDtypeStruct((n, D), F32),
        compiler_params=_cp(("arbitrary",)),
        name="moe_combine",
    )(pos, x, mod, ys, ln_g, ln_b)


def _moe_sparse(x, mod, wr, br, w1, w3, w2, ln_g, ln_b, *, layer, seq):
    n = x.shape[0]
    urow, cnt = _route_sort(x, mod, wr, br, layer=layer, seq=seq)
    cls = urow[:, D + 2].astype(jnp.int32)
    rank = urow[:, D + 3].astype(jnp.int32)
    count = cnt[0, :NCLS].astype(jnp.int32)
    ntile = (count + FFN_TM - 1) // FFN_TM
    tile_end = jnp.cumsum(ntile)
    row_start = (tile_end - ntile) * FFN_TM
    pos = rank + jnp.sum(jax.nn.one_hot(cls, NCLS, dtype=jnp.int32) * row_start[None, :], axis=1)
    nt = n // FFN_TM + NCLS
    total = tile_end[NCLS - 1]
    tr = jnp.minimum(jnp.arange(nt, dtype=jnp.int32), total - 1)
    tcls = jnp.sum((tr[:, None] >= tile_end[None, :]).astype(jnp.int32), axis=1)
    grp, pair = tcls // PAIRS, tcls % PAIRS
    lo = (pair >= 3).astype(jnp.int32) + (pair >= 5).astype(jnp.int32)
    hi = pair + 1 - lo * (2 * E_PER_G - 3 - lo) // 2
    ta = grp * E_PER_G + lo
    tb = grp * E_PER_G + hi
    t_all = jnp.arange(nt, dtype=jnp.int32)
    partial_last = ((t_all[:, None] == tile_end[None, :] - 1) & (count % FFN_TM != 0)[None, :]).any(axis=1)
    zflag = ((t_all >= total) | partial_last).astype(jnp.int32)
    xs = _dispatch(pos, zflag, urow)
    ys = _ffn_sorted(ta, tb, total[None], xs, w1, w3, w2)
    return _combine(pos, x, mod, ys, ln_g, ln_b, layer=layer, seq=seq)


def _moe_body(x_ref, mod_ref, wr_ref, br_ref, w1f_ref, w3f_ref, w2f_ref, g_ref, b_ref,
              xn_ref, w1_ref, w3_ref, w2_ref, ub, comb, acc, *, reps):
    e = pl.program_id(1)
    w1_ref[...] = w1f_ref[...].astype(BF16)
    w3_ref[...] = w3f_ref[...].astype(BF16)
    w2_ref[...] = w2f_ref[...].astype(BF16)

    @pl.when(e == 0)
    def _():
        shift, sc1p, _ = _split_mod(mod_ref, reps=reps)
        u = (x_ref[...] * sc1p + shift).astype(BF16)
        ub[...] = u
        comb[...] = _route(_dot(u, wr_ref[...]) + br_ref[...])
        acc[...] = jnp.zeros(acc.shape, F32)

    u = ub[...]
    col = lax.broadcasted_iota(jnp.int32, comb.shape, 1)
    ce = jnp.sum(jnp.where(col == e + N_GROUPS, comb[...], 0.0), axis=-1, keepdims=True)
    h = jax.nn.silu(_dot(u, w1_ref[...])) * _dot(u, w3_ref[...]) * ce
    acc[...] += _dot(h.astype(BF16), w2_ref[...])

    @pl.when(e == N_EXP - 1)
    def _():
        _, _, g1p = _split_mod(mod_ref, reps=reps)
        xn_ref[...] = _layer_norm(ALPHA * x_ref[...] + g1p * acc[...], g_ref[...], b_ref[...])


def _moe_dense(x, mod, wr, br, w1, w3, w2, ln_g, ln_b, *, layer, reps):
    n = x.shape[0]
    sub = 2 * layer + 1
    expert = lambda i, e: (layer, e, 0, 0)
    return pl.pallas_call(
        functools.partial(_moe_body, reps=reps),
        grid=(1, N_EXP),
        in_specs=[
            pl.BlockSpec((n, D), lambda i, e: (0, 0)),
            _layer_spec((n // reps, 3 * D), sub),
            _layer_spec((D, RLANES), layer), _layer_spec((1, RLANES), layer),
            pl.BlockSpec((None, None, D, F), expert),
            pl.BlockSpec((None, None, D, F), expert),
            pl.BlockSpec((None, None, F, D), expert),
            _layer_spec((1, D), sub), _layer_spec((1, D), sub),
        ],
        out_specs=[pl.BlockSpec((n, D), lambda i, e: (0, 0)),
                   pl.BlockSpec((None, D, F), lambda i, e: (e, 0, 0)),
                   pl.BlockSpec((None, D, F), lambda i, e: (e, 0, 0)),
                   pl.BlockSpec((None, F, D), lambda i, e: (e, 0, 0))],
        out_shape=[jax.ShapeDtypeStruct((n, D), F32),
                   jax.ShapeDtypeStruct((N_EXP, D, F), BF16),
                   jax.ShapeDtypeStruct((N_EXP, D, F), BF16),
                   jax.ShapeDtypeStruct((N_EXP, F, D), BF16)],
        scratch_shapes=[
            pltpu.VMEM((n, D), BF16),
            pltpu.VMEM((n, RLANES), F32),
            pltpu.VMEM((n, D), F32),
        ],
        compiler_params=_cp(("arbitrary", "arbitrary")),
        name="moe",
    )(x, mod, wr, br, w1, w3, w2, ln_g, ln_b)


_NT = (((1,), (1,)), ((), ()))
_TN = (((0,), (0,)), ((), ()))


def _kv_body(x_ref, w_ref, wvt_ref, k_ref, v_ref, k16_ref, vt16_ref):
    xb = x_ref[...].astype(BF16)
    kv = _dot(xb, w_ref[...])
    k_ref[...] = kv[:, :KVW]
    v_ref[...] = kv[:, KVW:]
    k16_ref[...] = kv[:, :KVW].astype(BF16)
    vt16_ref[...] = lax.dot_general(wvt_ref[...], xb, _NT, preferred_element_type=F32).astype(BF16)


def _kv(x, w, wvt, *, tm):
    n = x.shape[0]
    row = lambda i: (i, 0)
    return pl.pallas_call(
        _kv_body,
        grid=(n // tm,),
        in_specs=[pl.BlockSpec((tm, D), row), _const_spec((D, 2 * KVW)), _const_spec((KVW, D))],
        out_specs=[pl.BlockSpec((tm, KVW), row), pl.BlockSpec((tm, KVW), row),
                   pl.BlockSpec((tm, KVW), row), pl.BlockSpec((KVW, tm), lambda i: (0, i))],
        out_shape=[jax.ShapeDtypeStruct((n, KVW), F32), jax.ShapeDtypeStruct((n, KVW), F32),
                   jax.ShapeDtypeStruct((n, KVW), BF16), jax.ShapeDtypeStruct((KVW, n), BF16)],
        compiler_params=_cp(("arbitrary",)),
        name="kv_proj",
    )(x, w, wvt)


def _sink_attend(qh, kh, vh, bias, sink):
    s = lax.dot_general(qh, kh, (((1,), (1,)), ((), ())), preferred_element_type=F32) + bias
    m = jnp.maximum(jnp.max(s, axis=-1, keepdims=True), sink)
    p = jnp.exp(s - m)
    den = jnp.sum(p, axis=-1, keepdims=True) + jnp.exp(sink - m)
    return _dot(p.astype(BF16), vh) / den


def _attn_prompt_body(x_ref, mod_ref, wqt_ref, wo_ref, kc_ref, kp_ref, vtc_ref, vtp_ref,
                      bias_ref, sink_ref, g_ref, b_ref, xn_ref, *, tq):
    t = pl.program_id(1)
    x = x_ref[...]
    shift, sc1p, g1p = _split_mod(mod_ref, row=pl.program_id(0))
    u = (x * sc1p + shift).astype(BF16)
    qt = lax.dot_general(wqt_ref[...], u, _NT, preferred_element_type=F32)
    qt = (qt * (HD ** -0.5)).astype(BF16)
    kc = kc_ref[...]
    vtc = vtc_ref[...]
    krow = lax.broadcasted_iota(jnp.int32, (2 * WIN, GRP * WIN), 0)
    first = jnp.where((krow < WIN) & (t == 0), NEG, 0.0).astype(F32)
    nwin = tq // WIN
    kks, vvts = [], []
    for j in range(nwin):
        if j == 0:
            kks.append(jnp.concatenate([kp_ref[...], kc[0:WIN]], axis=0))
            vvts.append(jnp.concatenate([vtp_ref[...], vtc[:, 0:WIN]], axis=1))
        else:
            kks.append(kc[(j - 1) * WIN:(j + 1) * WIN])
            vvts.append(vtc[:, (j - 1) * WIN:(j + 1) * WIN])

    def scores(j, hk):
        qht = jnp.concatenate(
            [qt[(hk * GRP + g) * HD:(hk * GRP + g + 1) * HD, j * WIN:(j + 1) * WIN] for g in range(GRP)],
            axis=1)
        st = _dot(kks[j][:, hk * HD:(hk + 1) * HD], qht) + bias_ref[hk]
        return st + first if j == 0 else st

    items = [(j, hk) for j in range(nwin) for hk in range(HKV)]
    heads = [[] for _ in range(nwin)]
    st_next = scores(*items[0])
    for n, (j, hk) in enumerate(items):
        st = st_next
        if n + 1 < len(items):
            st_next = scores(*items[n + 1])
        sink = sink_ref[hk]
        m = jnp.maximum(jnp.max(st, axis=0, keepdims=True), sink)
        pt = jnp.exp(st - m)
        den = jnp.sum(pt, axis=0, keepdims=True) + jnp.exp(sink - m)
        ot = _dot(vvts[j][hk * HD:(hk + 1) * HD, :], pt.astype(BF16)) * (1.0 / den)
        for g in range(GRP):
            heads[j].append(ot[:, g * WIN:(g + 1) * WIN])
    blocks = [jnp.concatenate(h, axis=0) for h in heads]
    oallt = jnp.concatenate(blocks, axis=1) if len(blocks) > 1 else blocks[0]
    mix = lax.dot_general(oallt.astype(BF16), wo_ref[...], _TN, preferred_element_type=F32)
    xn_ref[...] = _layer_norm(ALPHA * x + g1p * mix, g_ref[...], b_ref[...])


def _attn_prompt(x, mod, wqt, wo, k16, vt16, bias, sink, ln_g, ln_b, *, layer, nb, seq, tq):
    nt = seq // tq
    wpt = tq // WIN
    n = nb * seq
    j = layer - N_A
    sub = 2 * layer

    def prev(b, t):
        return jnp.maximum((b * nt + t) * wpt - 1, 0)

    cur = lambda b, t: (b * nt + t, 0)
    return pl.pallas_call(
        functools.partial(_attn_prompt_body, tq=tq),
        grid=(nb, nt),
        in_specs=[
            pl.BlockSpec((tq, D), cur),
            _layer_spec((nb, 3 * D), sub),
            _layer_spec((HQ * HD, D), j), _layer_spec((HQ * HD, D), j),
            pl.BlockSpec((tq, KVW), cur), pl.BlockSpec((WIN, KVW), lambda b, t: (prev(b, t), 0)),
            pl.BlockSpec((KVW, tq), lambda b, t: (0, b * nt + t)),
            pl.BlockSpec((KVW, WIN), lambda b, t: (0, prev(b, t))),
            _const_spec((HKV, 2 * WIN, GRP * WIN)), _layer_spec((HKV, 1, GRP * WIN), j),
            _layer_spec((1, D), sub), _layer_spec((1, D), sub),
        ],
        out_specs=pl.BlockSpec((tq, D), cur),
        out_shape=jax.ShapeDtypeStruct((n, D), F32),
        compiler_params=_cp(("arbitrary", "arbitrary")),
        name="attn_prompt",
    )(x, mod, wqt, wo, k16, k16, vt16, vt16, bias, sink, ln_g, ln_b)


def _attn_sample_body(x_ref, mod_ref, wq_ref, wo_ref, kb_ref, vb_ref, kn_ref, vn_ref,
                      bias_ref, sink_ref, g_ref, b_ref, xn_ref, *, ns, steps):
    nr = ns * steps
    x = x_ref[...].reshape(nr, D)
    shift, sc1p, g1p = _split_mod(mod_ref, reps=steps)
    u = (x * sc1p + shift).astype(BF16)
    q = (_dot(u, wq_ref[...]) * (HD ** -0.5)).astype(BF16)
    nbuf = kb_ref.shape[1]
    k_all = jnp.concatenate([kb_ref[...].reshape(ns * nbuf, KVW), kn_ref[...].reshape(nr, KVW)],
                            axis=0).astype(BF16)
    v_all = jnp.concatenate([vb_ref[...].reshape(ns * nbuf, KVW), vn_ref[...].reshape(nr, KVW)],
                            axis=0).astype(BF16)
    heads = [None] * HQ
    for hk in range(HKV):
        kcols = slice(hk * HD, (hk + 1) * HD)
        qh = jnp.concatenate(
            [q[:, (hk * GRP + g) * HD:(hk * GRP + g + 1) * HD] for g in range(GRP)], axis=0)
        o = _sink_attend(qh, k_all[:, kcols], v_all[:, kcols], bias_ref[hk], sink_ref[hk])
        for g in range(GRP):
            heads[hk * GRP + g] = o[g * nr:(g + 1) * nr]
    o_all = jnp.concatenate(heads, axis=1)
    mix = _dot(o_all.astype(BF16), wo_ref[...])
    xn_ref[...] = _layer_norm(ALPHA * x + g1p * mix, g_ref[...], b_ref[...]).reshape(steps, ns, D)


def _attn_sample(x, mod, wq, wo, kbuf, vbuf, kn, vn, bias, sink, ln_g, ln_b, *, layer, ns):
    steps, nb, _ = x.shape
    nbuf = kbuf.shape[1]
    j = layer - N_A
    sub = 2 * layer
    seqs = lambda i: (0, i, 0)
    return pl.pallas_call(
        functools.partial(_attn_sample_body, ns=ns, steps=steps),
        grid=(nb // ns,),
        in_specs=[
            pl.BlockSpec((steps, ns, D), seqs),
            pl.BlockSpec((None, ns, 3 * D), lambda i: (sub, i, 0)),
            _layer_spec((D, HQ * HD), j), _layer_spec((HQ * HD, D), j),
            pl.BlockSpec((ns, nbuf, KVW), lambda i: (i, 0, 0)),
            pl.BlockSpec((ns, nbuf, KVW), lambda i: (i, 0, 0)),
            pl.BlockSpec((steps, ns, KVW), seqs), pl.BlockSpec((steps, ns, KVW), seqs),
            _const_spec(bias.shape), _layer_spec(sink.shape[1:], j),
            _layer_spec((1, D), sub), _layer_spec((1, D), sub),
        ],
        out_specs=pl.BlockSpec((steps, ns, D), seqs),
        out_shape=jax.ShapeDtypeStruct((steps, nb, D), F32),
        compiler_params=_cp(("arbitrary",)),
        name="attn_sample",
    )(x, mod, wq, wo, kbuf, vbuf, kn, vn, bias, sink, ln_g, ln_b)


def _t5_bucket(d):
    max_exact = N_BUCKETS // 2
    d = jnp.maximum(d, 0)
    log_ratio = jnp.log(jnp.maximum(d, 1).astype(F32) / max_exact) / math.log(WIN / max_exact)
    large = max_exact + (log_ratio * (N_BUCKETS - max_exact)).astype(jnp.int32)
    large = jnp.minimum(large, N_BUCKETS - 1)
    return jnp.where(d < max_exact, d, large)


def _bias_table(d, ok, rel_bias):
    tab = jnp.concatenate([rel_bias[_t5_bucket(jnp.arange(WIN))].astype(F32),
                           jnp.full((1, HQ), NEG, F32)], axis=0)
    sel = jax.nn.one_hot(jnp.where(ok, d, WIN), WIN + 1, dtype=F32)
    return jnp.einsum('abd,dh->hab', sel, tab, precision=lax.Precision.HIGHEST)


def _prompt_bias(rel_bias):
    d = jnp.arange(WIN)[None, :] + WIN - jnp.arange(2 * WIN)[:, None]
    b = _bias_table(d, (d >= 0) & (d < WIN), rel_bias).reshape(HKV, GRP, 2 * WIN, WIN)
    return jnp.transpose(b, (0, 2, 1, 3)).reshape(HKV, 2 * WIN, GRP * WIN)


def _sample_bias(rel_bias, ns, steps, nbuf):
    qs = jnp.tile(jnp.arange(ns), steps)
    qt = jnp.repeat(jnp.arange(steps), ns)
    ks = jnp.concatenate([jnp.repeat(jnp.arange(ns), nbuf), qs])
    kpos = jnp.concatenate([jnp.tile(jnp.arange(nbuf), ns), nbuf + qt])
    d = qt[:, None] + nbuf - kpos[None, :]
    ok = (d >= 0) & (d < WIN) & (qs[:, None] == ks[None, :])
    return _bias_table(d, ok, rel_bias).reshape(HKV, GRP * ns * steps, kpos.shape[0])


LRU_TM = 512
ATTN_TQ = 512
SAMPLE_NS = 8


def kernel(x_prompt, x_sample, state_rnn_h, state_rnn_conv, cache_win_k, cache_win_v, c_prompt, c_sample, ada_w, ada_b, ln_g, ln_b, lru_w_in, lru_conv_w, lru_conv_b, lru_w_a, lru_b_a, lru_w_x, lru_b_x, lru_lambda, lru_w_out, kv_w, attn_w_q, attn_sinks, attn_w_o, rel_bias, moe_w_group, moe_b_group, moe_w_router, moe_b_router, moe_w1, moe_w3, moe_w2):
    nbp, seq, _ = x_prompt.shape
    nbs, steps, _ = x_sample.shape
    nbuf = cache_win_k.shape[1]
    npt = nbp * seq
    nst = nbs * steps

    mod_p, mod_s = _adaln(c_prompt, c_sample, ada_w, ada_b)

    ln_g3 = ln_g.reshape(2 * DEPTH, 1, D)
    ln_b3 = ln_b.reshape(2 * DEPTH, 1, D)
    lw = (lru_w_in.astype(BF16), lru_conv_w, lru_conv_b[:, None], lru_w_a.astype(BF16),
          lru_w_x.astype(BF16), lru_b_a[:, None], lru_b_x[:, None], lru_lambda[:, None],
          lru_w_out.astype(BF16))
    wr = jnp.concatenate([moe_w_group, moe_w_router.reshape(DEPTH, D, N_EXP)], axis=2)
    wr = jnp.pad(wr, ((0, 0), (0, 0), (0, RLANES - wr.shape[2]))).astype(BF16)
    br = jnp.concatenate([moe_b_group, moe_b_router.reshape(DEPTH, N_EXP)], axis=1)
    br = jnp.pad(br, ((0, 0), (0, RLANES - br.shape[1])))[:, None]
    wq = attn_w_q.astype(BF16)
    wqt = jnp.swapaxes(wq, 1, 2)
    wo = attn_w_o.astype(BF16)
    kvw = kv_w.astype(BF16)
    kvwt = kvw[:, KVW:].T
    bias_p = _prompt_bias(rel_bias)
    bias_s = _sample_bias(rel_bias, SAMPLE_NS, steps, nbuf)
    sink_p = jnp.repeat(attn_sinks.reshape(N_B, HKV, GRP), WIN, axis=2)[:, :, None, :]
    sink_s = jnp.repeat(attn_sinks.reshape(N_B, HKV, GRP), SAMPLE_NS * steps, axis=2)[..., None]
    kbuf = cache_win_k.reshape(nbs, nbuf, KVW)
    vbuf = cache_win_v.reshape(nbs, nbuf, KVW)

    xp = x_prompt.reshape(npt, D)
    xs = jnp.transpose(x_sample, (1, 0, 2)).reshape(nst, D)
    c0 = jnp.transpose(state_rnn_conv, (0, 2, 1, 3)).reshape(N_A, (CONV_W - 1) * nbs, R)
    hp, cp_, hs, cs = [], [], [], []
    kp = vp = kn = vn = kp16 = vpt16 = None
    for l in range(DEPTH):
        if l < N_A:
            xp, h_l, c_l = _lru_prompt(xp, mod_p, lw, ln_g3, ln_b3, layer=l, nb=nbp, seq=seq, tm=LRU_TM)
            hp.append(h_l[:, 0])
            cp_.append(c_l)
            xs, h_l, c_l = _lru_sample(xs, mod_s, state_rnn_h, c0[l], lw, ln_g3, ln_b3,
                                       layer=l, nb=nbs, steps=steps)
            hs.append(h_l)
            cs.append(jnp.transpose(c_l.reshape(CONV_W - 1, nbs, R), (1, 0, 2)))
        else:
            xp = _attn_prompt(xp, mod_p, wqt, wo, kp16, vpt16, bias_p, sink_p, ln_g3, ln_b3,
                              layer=l, nb=nbp, seq=seq, tq=ATTN_TQ)
            xs = _attn_sample(xs.reshape(steps, nbs, D), mod_s, wq, wo, kbuf, vbuf,
                              kn.reshape(steps, nbs, KVW), vn.reshape(steps, nbs, KVW),
                              bias_s, sink_s, ln_g3, ln_b3, layer=l, ns=SAMPLE_NS).reshape(nst, D)
        xs, w1, w3, w2 = _moe_dense(xs, mod_s, wr, br, moe_w1, moe_w3, moe_w2, ln_g3, ln_b3,
                                    layer=l, reps=steps)
        xp = _moe_sparse(xp, mod_p, wr, br, w1, w3, w2, ln_g3, ln_b3, layer=l, seq=seq)
        if l == N_A - 1:
            kp, vp, kp16, vpt16 = _kv(xp, kvw, kvwt, tm=1024)
            kn, vn, _, _ = _kv(xs, kvw, kvwt, tm=nst)

    lp = min(WIN, seq)
    k_p = kp.reshape(nbp, seq, HKV, HD)[:, seq - lp:]
    v_p = vp.reshape(nbp, seq, HKV, HD)[:, seq - lp:]
    kn4 = jnp.transpose(kn.reshape(steps, nbs, HKV, HD), (1, 0, 2, 3))
    vn4 = jnp.transpose(vn.reshape(steps, nbs, HKV, HD), (1, 0, 2, 3))
    k_s = jnp.concatenate([cache_win_k, kn4], axis=1)[:, -nbuf:]
    v_s = jnp.concatenate([cache_win_v, vn4], axis=1)[:, -nbuf:]
    y_s = jnp.transpose(xs.reshape(steps, nbs, D), (1, 0, 2))
    return (xp.reshape(nbp, seq, D), y_s,
            jnp.stack(hp), jnp.stack(cp_), k_p, v_p,
            jnp.stack(hs), jnp.stack(cs), k_s, v_s)
```

```python
import functools
import math

import jax
import jax.numpy as jnp
from jax import lax
from jax.experimental import pallas as pl
from jax.experimental.pallas import tpu as pltpu

D = 1024
R = 1024
DEPTH = 4
N_A = 2
N_B = DEPTH - N_A
N_BLK = 4
BLK = R // N_BLK
CONV_W = 4
RG_C = 8.0
HQ = 16
HKV = 4
HD = 64
GRP = HQ // HKV
KVW = HKV * HD
WIN = 128
N_BUCKETS = 32
N_GROUPS = 4
E_PER_G = 4
N_EXP = 16
F = 512
ALPHA = (2.0 * DEPTH) ** 0.25
LN_EPS = 1e-5
NEG = -1e30
RLANES = 128
ROW_TILE = 8

F32 = jnp.float32
BF16 = jnp.bfloat16

VMEM_LIMIT = 56 * 1024 * 1024


def _cp(sem):
    return pltpu.CompilerParams(dimension_semantics=sem, vmem_limit_bytes=VMEM_LIMIT)


def _dot(a, b):
    return jnp.dot(a, b, preferred_element_type=F32)


def _layer_norm(z, g, b):
    mu = jnp.mean(z, axis=-1, keepdims=True)
    zc = z - mu
    var = jnp.mean(zc * zc, axis=-1, keepdims=True)
    return zc * lax.rsqrt(var + LN_EPS) * g + b


def _split_mod(mod_ref, row=None, reps=1):
    m = mod_ref[...] if row is None else mod_ref[pl.ds(row, 1), :]
    if reps > 1:
        m = jnp.concatenate([m] * reps, axis=0)
    return m[:, 0:D], m[:, D:2 * D], m[:, 2 * D:3 * D]


def _layer_spec(shape, layer):
    nd = len(shape)
    return pl.BlockSpec((None,) + tuple(shape), lambda *_: (layer,) + (0,) * nd)


def _adaln_body(cp_ref, cs_ref, w_ref, b_ref, op_ref, os_ref):
    one = jnp.where(pl.program_id(1) > 0, 1.0, 0.0).astype(F32)
    w = w_ref[...].astype(BF16)
    op_ref[...] = _dot(cp_ref[...].astype(BF16), w) + b_ref[...] + one
    os_ref[...] = _dot(cs_ref[...].astype(BF16), w) + b_ref[...] + one


def _adaln(c_p, c_s, ada_w, ada_b):
    np_, ns_ = c_p.shape[0], c_s.shape[0]
    w = ada_w.reshape(2 * DEPTH, D, 3 * D)
    b = ada_b.reshape(2 * DEPTH, 1, 3 * D)
    return pl.pallas_call(
        _adaln_body,
        grid=(2 * DEPTH, 3),
        in_specs=[
            pl.BlockSpec((np_, D), lambda l, j: (0, 0)),
            pl.BlockSpec((ns_, D), lambda l, j: (0, 0)),
            pl.BlockSpec((None, D, D), lambda l, j: (l, 0, j)),
            pl.BlockSpec((None, 1, D), lambda l, j: (l, 0, j)),
        ],
        out_specs=[pl.BlockSpec((None, np_, D), lambda l, j: (l, 0, j)),
                   pl.BlockSpec((None, ns_, D), lambda l, j: (l, 0, j))],
        out_shape=[jax.ShapeDtypeStruct((2 * DEPTH, np_, 3 * D), F32),
                   jax.ShapeDtypeStruct((2 * DEPTH, ns_, 3 * D), F32)],
        compiler_params=_cp(("arbitrary", "arbitrary")),
        name="adaln",
    )(c_p, c_s, w, b)


def _lru_gates(xc, n, wa_ref, wx_ref, ba_ref, bx_ref, lam_ref):
    cols = slice(n * BLK, (n + 1) * BLK)
    xb = xc.astype(BF16)
    r = jax.nn.sigmoid(_dot(xb, wa_ref[n]) + ba_ref[:, cols])
    i = jax.nn.sigmoid(_dot(xb, wx_ref[n]) + bx_ref[:, cols])
    z = -lam_ref[:, cols]
    softplus = jnp.maximum(z, 0.0) + jnp.log1p(jnp.exp(-jnp.abs(z)))
    log_a = (-RG_C) * r * softplus
    a = jnp.exp(log_a)
    b = jnp.sqrt(1.0 - a * a) * i * xc
    return a, b


def _lru_prompt_body(x_ref, mod_ref, win_ref, cw_ref, cb_ref, wa_ref, wx_ref, ba_ref, bx_ref,
                     lam_ref, wout_ref, g_ref, b_ref,
                     xn_ref, hl_ref, cl_ref,
                     xbuf, abuf, hbuf, hc, *, tm):
    t = pl.program_id(1)

    @pl.when(t == 0)
    def _():
        xbuf[0:ROW_TILE, :] = jnp.zeros((ROW_TILE, R), F32)
        hc[...] = jnp.zeros((1, R), F32)

    x = x_ref[...]
    shift, sc1p, g1p = _split_mod(mod_ref, row=pl.program_id(0))
    u = x * sc1p + shift
    yx = _dot(u.astype(BF16), win_ref[...])
    y = jax.nn.gelu(yx[:, :R])
    xbuf[ROW_TILE:ROW_TILE + tm, :] = yx[:, R:]

    rowmod = lax.broadcasted_iota(jnp.int32, (tm, BLK), 0) & (ROW_TILE - 1)
    for n in range(N_BLK):
        cols = slice(n * BLK, (n + 1) * BLK)
        xc = cb_ref[:, cols] + cw_ref[3:4, cols] * xbuf[ROW_TILE:ROW_TILE + tm, cols]
        for k in range(1, CONV_W):
            xc = xc + cw_ref[3 - k:4 - k, cols] * xbuf[ROW_TILE - k:ROW_TILE - k + tm, cols]
        a, b = _lru_gates(xc, n, wa_ref, wx_ref, ba_ref, bx_ref, lam_ref)
        s = 1
        while s < ROW_TILE:
            a_sh = pltpu.roll(a, s, 0)
            b_sh = pltpu.roll(b, s, 0)
            m = rowmod >= s
            b = jnp.where(m, a * b_sh, 0.0) + b
            a = jnp.where(m, a * a_sh, a)
            s *= 2
        abuf[:, cols] = a
        hbuf[:, cols] = b

    def group(g, h):
        r0 = pl.multiple_of(g * ROW_TILE, ROW_TILE)
        hg = abuf[pl.ds(r0, ROW_TILE), :] * h + hbuf[pl.ds(r0, ROW_TILE), :]
        hbuf[pl.ds(r0, ROW_TILE), :] = hg
        return hg[ROW_TILE - 1:ROW_TILE, :]

    h_end = lax.fori_loop(0, tm // ROW_TILE, group, hc[...])
    hc[...] = h_end
    hl_ref[...] = h_end
    tail = xbuf[tm + ROW_TILE - (CONV_W - 1):tm + ROW_TILE, :]
    cl_ref[...] = tail
    xbuf[ROW_TILE - (CONV_W - 1):ROW_TILE, :] = tail

    mix = _dot((y * hbuf[...]).astype(BF16), wout_ref[...])
    xn_ref[...] = _layer_norm(ALPHA * x + g1p * mix, g_ref[...], b_ref[...])


def _const_spec(shape):
    nd = len(shape)
    return pl.BlockSpec(shape, lambda *_: (0,) * nd)


def _lru_weight_specs(layer):
    sub = 2 * layer
    return [
        _layer_spec((D, 2 * R), layer), _layer_spec((CONV_W, R), layer), _layer_spec((1, R), layer),
        _layer_spec((N_BLK, BLK, BLK), layer), _layer_spec((N_BLK, BLK, BLK), layer),
        _layer_spec((1, R), layer), _layer_spec((1, R), layer), _layer_spec((1, R), layer),
        _layer_spec((R, D), layer), _layer_spec((1, D), sub), _layer_spec((1, D), sub),
    ]


def _lru_prompt(x, mod, lw, ln_g, ln_b, *, layer, nb, seq, tm):
    nt = seq // tm
    n = nb * seq
    return pl.pallas_call(
        functools.partial(_lru_prompt_body, tm=tm),
        grid=(nb, nt),
        in_specs=[
            pl.BlockSpec((tm, D), lambda b, t: (b * nt + t, 0)),
            _layer_spec((nb, 3 * D), 2 * layer),
        ] + _lru_weight_specs(layer),
        out_specs=[
            pl.BlockSpec((tm, D), lambda b, t: (b * nt + t, 0)),
            pl.BlockSpec((None, 1, R), lambda b, t: (b, 0, 0)),
            pl.BlockSpec((None, CONV_W - 1, R), lambda b, t: (b, 0, 0)),
        ],
        out_shape=[
            jax.ShapeDtypeStruct((n, D), F32),
            jax.ShapeDtypeStruct((nb, 1, R), F32),
            jax.ShapeDtypeStruct((nb, CONV_W - 1, R), F32),
        ],
        scratch_shapes=[
            pltpu.VMEM((tm + ROW_TILE, R), F32),
            pltpu.VMEM((tm, R), F32),
            pltpu.VMEM((tm, R), F32),
            pltpu.VMEM((1, R), F32),
        ],
        compiler_params=_cp(("arbitrary", "arbitrary")),
        name="lru_prompt",
    )(x, mod, *lw, ln_g, ln_b)


def _lru_sample_body(x_ref, mod_ref, h0_ref, c0_ref, win_ref, cw_ref, cb_ref, wa_ref, wx_ref,
                     ba_ref, bx_ref, lam_ref, wout_ref, g_ref, b_ref,
                     xn_ref, hl_ref, cl_ref, hbuf, *, nb, steps):
    x = x_ref[...]
    shift, sc1p, g1p = _split_mod(mod_ref, reps=steps)
    u = x * sc1p + shift
    yx = _dot(u.astype(BF16), win_ref[...])
    y = jax.nn.gelu(yx[:, :R])
    xext = jnp.concatenate([c0_ref[...], yx[:, R:]], axis=0)
    cl_ref[...] = xext[steps * nb:, :]
    for n in range(N_BLK):
        cols = slice(n * BLK, (n + 1) * BLK)
        xc = cb_ref[:, cols] + cw_ref[0:1, cols] * xext[0:steps * nb, cols]
        for k in range(1, CONV_W):
            xc = xc + cw_ref[k:k + 1, cols] * xext[k * nb:(k + steps) * nb, cols]
        a, b = _lru_gates(xc, n, wa_ref, wx_ref, ba_ref, bx_ref, lam_ref)
        h = h0_ref[:, cols]
        for t in range(steps):
            h = a[t * nb:(t + 1) * nb, :] * h + b[t * nb:(t + 1) * nb, :]
            hbuf[t * nb:(t + 1) * nb, cols] = h
        hl_ref[:, cols] = h
    mix = _dot((y * hbuf[...]).astype(BF16), wout_ref[...])
    xn_ref[...] = _layer_norm(ALPHA * x + g1p * mix, g_ref[...], b_ref[...])


def _lru_sample(x, mod, h0, c0, lw, ln_g, ln_b, *, layer, nb, steps):
    n = nb * steps
    return pl.pallas_call(
        functools.partial(_lru_sample_body, nb=nb, steps=steps),
        grid=(1,),
        in_specs=[
            _const_spec((n, D)), _layer_spec((nb, 3 * D), 2 * layer), _layer_spec((nb, R), layer),
            _const_spec(((CONV_W - 1) * nb, R)),
        ] + _lru_weight_specs(layer),
        out_specs=[_const_spec((n, D)), _const_spec((nb, R)), _const_spec(((CONV_W - 1) * nb, R))],
        out_shape=[
            jax.ShapeDtypeStruct((n, D), F32),
            jax.ShapeDtypeStruct((nb, R), F32),
            jax.ShapeDtypeStruct(((CONV_W - 1) * nb, R), F32),
        ],
        scratch_shapes=[pltpu.VMEM((n, R), F32)],
        compiler_params=_cp(("arbitrary",)),
        name="lru_sample",
    )(x, mod, h0, c0, *lw, ln_g, ln_b)


RROWS = 32
EROW = 8
_NT = (((1,), (1,)), ((), ()))
_TN = (((0,), (0,)), ((), ()))


def _top2(lt):
    nt = lt.shape[1]
    gl = lt[0:EROW, :]
    gid = lax.broadcasted_iota(jnp.int32, (EROW, nt), 0).astype(F32)
    is_g = gid < N_GROUPS
    gl = jnp.where(is_g, gl, NEG)
    gmax = jnp.max(gl, axis=0, keepdims=True)
    gidx = jnp.min(jnp.where(gl == gmax, gid, float(EROW)), axis=0, keepdims=True)
    gsum = jnp.sum(jnp.where(is_g, jnp.exp(gl - gmax), 0.0), axis=0, keepdims=True)
    g_p = 1.0 / gsum
    el = lt[EROW:EROW + N_EXP, :]
    eid = lax.broadcasted_iota(jnp.int32, (N_EXP, nt), 0).astype(F32)
    first = gidx * E_PER_G
    in_grp = (eid >= first) & (eid < first + E_PER_G)
    el = jnp.where(in_grp, el, NEG)
    big = float(N_EXP)
    t1 = jnp.max(el, axis=0, keepdims=True)
    i1 = jnp.min(jnp.where(el == t1, eid, big), axis=0, keepdims=True)
    el2 = jnp.where(eid == i1, NEG, el)
    t2 = jnp.max(el2, axis=0, keepdims=True)
    i2 = jnp.min(jnp.where((el2 == t2) & in_grp & (eid != i1), eid, big), axis=0, keepdims=True)
    e21 = jnp.exp(t2 - t1)
    w1 = g_p / (1.0 + e21)
    w2 = w1 * e21
    return eid, gidx, i1, i2, w1, w2


def _router_logits(u16, wrt_ref, brt_ref):
    return lax.dot_general(wrt_ref[...], u16, _NT, preferred_element_type=F32) + brt_ref[:, 0:1]


def _rows_to_cols(rows):
    r, nt = rows.shape
    padded = jnp.concatenate([rows, jnp.zeros((RLANES - r, nt), F32)], axis=0)
    return jnp.transpose(padded)


PAIRS = E_PER_G * (E_PER_G - 1) // 2
NCLS = N_GROUPS * PAIRS
XW = D + RLANES
FFN_TM = 256
ROUTE_TM = 512
DISPATCH_TD = 1024
COMBINE_TC = 256
COMBINE_CHUNK = 64


def _route_sort_body(x_ref, mod_ref, wr_ref, br_ref, urow_ref, meta_ref, cnt_ref, carry, *, tiles_per_seq):
    i = pl.program_id(0)

    @pl.when(i == 0)
    def _():
        carry[...] = jnp.zeros(carry.shape, F32)

    tm = x_ref.shape[0]
    shift, sc1p, _ = _split_mod(mod_ref, row=i // tiles_per_seq)
    u = x_ref[...] * sc1p + shift
    _, gidx, i1, i2, w1, w2 = _top2(_router_logits(u.astype(BF16), wr_ref, br_ref))
    first = gidx * E_PER_G
    lo = jnp.minimum(i1, i2) - first
    hi = jnp.maximum(i1, i2) - first
    w_lo = jnp.where(i1 < i2, w1, w2)
    w_hi = jnp.where(i1 < i2, w2, w1)
    cls = gidx * PAIRS + lo * (2 * E_PER_G - 1 - lo) * 0.5 + (hi - lo - 1.0)
    cid = lax.broadcasted_iota(jnp.int32, (RROWS, tm), 0).astype(F32)
    onehot = (cid == cls).astype(F32)
    r = lax.broadcasted_iota(jnp.int32, (tm, tm), 0)
    c = lax.broadcasted_iota(jnp.int32, (tm, tm), 1)
    earlier = _dot(onehot.astype(BF16), (r < c).astype(BF16)) + carry[:, 0:1]
    rank = jnp.sum(onehot * earlier, axis=0, keepdims=True)
    carry[...] += jnp.sum(onehot, axis=1, keepdims=True)
    cnt_ref[...] = carry[...]
    meta = jnp.concatenate([w_lo, w_hi, cls, rank, jnp.zeros((ROW_TILE - 4, tm), F32)], axis=0)
    meta_ref[...] = meta
    urow_ref[:, 0:D] = u
    urow_ref[:, D:XW] = _rows_to_cols(meta)


def _route_sort(x, mod, wrt, brt, *, layer, seq):
    n = x.shape[0]
    nt = n // ROUTE_TM
    return pl.pallas_call(
        functools.partial(_route_sort_body, tiles_per_seq=seq // ROUTE_TM),
        grid=(nt,),
        in_specs=[
            pl.BlockSpec((ROUTE_TM, D), lambda i: (i, 0)),
            _layer_spec((n // seq, 3 * D), 2 * layer + 1),
            _layer_spec((RROWS, D), layer), _layer_spec((RROWS, RLANES), layer),
        ],
        out_specs=[pl.BlockSpec((ROUTE_TM, XW), lambda i: (i, 0)),
                   pl.BlockSpec((None, ROW_TILE, ROUTE_TM), lambda i: (i, 0, 0)),
                   _const_spec((RROWS, RLANES))],
        out_shape=[jax.ShapeDtypeStruct((n, XW), F32),
                   jax.ShapeDtypeStruct((nt, ROW_TILE, ROUTE_TM), F32),
                   jax.ShapeDtypeStruct((RROWS, RLANES), F32)],
        scratch_shapes=[pltpu.VMEM((RROWS, RLANES), F32)],
        compiler_params=_cp(("arbitrary",)),
        name="moe_route",
    )(x, mod, wrt, brt)


def _row_copy(src, src_row, dst, dst_row, sem):
    return pltpu.make_async_copy(src.at[pl.ds(src_row, 1)], dst.at[pl.ds(dst_row, 1)], sem)


def _dispatch_body(pos_ref, zflag_ref, urow_ref, xs_hbm, zbuf, sem, zsem):
    base = pl.program_id(0) * DISPATCH_TD

    @pl.when(pl.program_id(0) == 0)
    def _():
        zbuf[...] = jnp.zeros(zbuf.shape, F32)

        def zero_copy(t):
            return pltpu.make_async_copy(zbuf, xs_hbm.at[pl.ds(t * FFN_TM, FFN_TM)], zsem)

        def start(t, carry):
            @pl.when(zflag_ref[t] != 0)
            def _():
                zero_copy(t).start()
            return carry

        def wait(t, carry):
            @pl.when(zflag_ref[t] != 0)
            def _():
                zero_copy(t).wait()
            return carry

        lax.fori_loop(0, zflag_ref.shape[0], start, 0)
        lax.fori_loop(0, zflag_ref.shape[0], wait, 0)

    def issue(g, carry):
        r0 = pl.multiple_of(g * ROW_TILE, ROW_TILE)
        for k in range(ROW_TILE):
            _row_copy(urow_ref.at[pl.ds(r0, ROW_TILE)], k, xs_hbm, pos_ref[base + r0 + k], sem).start()
        return carry

    lax.fori_loop(0, DISPATCH_TD // ROW_TILE, issue, 0)
    pltpu.make_async_copy(urow_ref, xs_hbm.at[pl.ds(0, DISPATCH_TD)], sem).wait()


def _dispatch(pos, zflag, urow):
    n = urow.shape[0]
    nt = zflag.shape[0]
    return pl.pallas_call(
        _dispatch_body,
        grid_spec=pltpu.PrefetchScalarGridSpec(
            num_scalar_prefetch=2, grid=(n // DISPATCH_TD,),
            in_specs=[pl.BlockSpec((DISPATCH_TD, XW), lambda i, pos, zflag: (i, 0))],
            out_specs=pl.BlockSpec(memory_space=pl.ANY),
            scratch_shapes=[pltpu.VMEM((FFN_TM, XW), F32), pltpu.SemaphoreType.DMA(()),
                            pltpu.SemaphoreType.DMA(())]),
        out_shape=jax.ShapeDtypeStruct((nt * FFN_TM, XW), F32),
        compiler_params=_cp(("arbitrary",)),
        name="moe_dispatch",
    )(pos, zflag, urow)


def _ffn_sorted_body(ta_ref, tb_ref, na_ref, xs_ref, w1a, w3a, w2a, w1b, w3b, w2b, ys_ref):
    del ta_ref, tb_ref
    active = pl.program_id(0) < na_ref[0]

    @pl.when(active)
    def _():
        x = xs_ref[:, 0:D].astype(BF16)
        ha = jax.nn.silu(_dot(x, w1a[...])) * _dot(x, w3a[...]) * xs_ref[:, D:D + 1]
        hb = jax.nn.silu(_dot(x, w1b[...])) * _dot(x, w3b[...]) * xs_ref[:, D + 1:D + 2]
        ys_ref[...] = _dot(ha.astype(BF16), w2a[...]) + _dot(hb.astype(BF16), w2b[...])

    @pl.when(jnp.logical_not(active))
    def _():
        ys_ref[...] = jnp.zeros(ys_ref.shape, F32)


def _ffn_sorted(ta, tb, na, xs, w1, w3, w2):
    nt = ta.shape[0]
    ea = lambda t, ta, tb, na: (ta[t], 0, 0)
    eb = lambda t, ta, tb, na: (tb[t], 0, 0)
    rows = lambda t, ta, tb, na: (t, 0)
    up, down = (None, D, F), (None, F, D)
    return pl.pallas_call(
        _ffn_sorted_body,
        grid_spec=pltpu.PrefetchScalarGridSpec(
            num_scalar_prefetch=3, grid=(nt,),
            in_specs=[
                pl.BlockSpec((FFN_TM, XW), rows),
                pl.BlockSpec(up, ea), pl.BlockSpec(up, ea), pl.BlockSpec(down, ea),
                pl.BlockSpec(up, eb), pl.BlockSpec(up, eb), pl.BlockSpec(down, eb),
            ],
            out_specs=pl.BlockSpec((FFN_TM, D), rows)),
        out_shape=jax.ShapeDtypeStruct((nt * FFN_TM, D), F32),
        compiler_params=_cp(("arbitrary",)),
        name="moe_ffn",
    )(ta, tb, na, xs, w1, w3, w2, w1, w3, w2)


def _combine_body(pos_ref, x_ref, mod_ref, ys_hbm, g_ref, b_ref, xn_ref, buf, sem, *, tiles_per_seq):
    i = pl.program_id(0)
    nsteps = pl.num_programs(0)
    tc = x_ref.shape[0]

    def issue_group(step, slot, r0):
        group = buf.at[slot, pl.ds(r0, ROW_TILE)]
        for k in range(ROW_TILE):
            _row_copy(ys_hbm, pos_ref[step * tc + r0 + k], group, k, sem.at[slot]).start()

    def wait_tile(slot):
        pltpu.make_async_copy(ys_hbm.at[pl.ds(0, tc)], buf.at[slot], sem.at[slot]).wait()

    @pl.when(i == 0)
    def _():
        def prime(g, carry):
            issue_group(0, 0, pl.multiple_of(g * ROW_TILE, ROW_TILE))
            return carry
        lax.fori_loop(0, tc // ROW_TILE, prime, 0)

    slot = i % 2
    wait_tile(slot)
    _, _, g1p = _split_mod(mod_ref, row=i // tiles_per_seq)
    gain, bias = g_ref[...], b_ref[...]
    nxt = jnp.minimum(i + 1, nsteps - 1)

    for c in range(tc // COMBINE_CHUNK):
        for g in range(COMBINE_CHUNK // ROW_TILE):
            issue_group(nxt, 1 - slot, c * COMBINE_CHUNK + g * ROW_TILE)
        rows = pl.ds(c * COMBINE_CHUNK, COMBINE_CHUNK)
        z = ALPHA * x_ref[rows, :] + g1p * buf[slot, rows, :]
        xn_ref[rows, :] = _layer_norm(z, gain, bias)

    @pl.when(i == nsteps - 1)
    def _():
        wait_tile(1 - slot)


def _combine(pos, x, mod, ys, ln_g, ln_b, *, layer, seq):
    n = x.shape[0]
    tc = COMBINE_TC
    sub = 2 * layer + 1
    return pl.pallas_call(
        functools.partial(_combine_body, tiles_per_seq=seq // tc),
        grid_spec=pltpu.PrefetchScalarGridSpec(
            num_scalar_prefetch=1, grid=(n // tc,),
            in_specs=[
                pl.BlockSpec((tc, D), lambda i, pos: (i, 0)),
                _layer_spec((n // seq, 3 * D), sub),
                pl.BlockSpec(memory_space=pl.ANY),
                _layer_spec((1, D), sub), _layer_spec((1, D), sub),
            ],
            out_specs=pl.BlockSpec((tc, D), lambda i, pos: (i, 0)),
            scratch_shapes=[pltpu.VMEM((2, tc, D), F32), pltpu.SemaphoreType.DMA((2,))]),
        out_shape=jax.ShapeDtypeStruct((n, D), F32),
        compiler_params=_cp(("arbitrary",)),
        name="moe_combine",
    )(pos, x, mod, ys, ln_g, ln_b)


def _moe_sparse(x, mod, wr, br, w1, w3, w2, ln_g, ln_b, *, layer, seq):
    n = x.shape[0]
    urow, meta, cnt = _route_sort(x, mod, wr, br, layer=layer, seq=seq)
    cls = meta[:, 2, :].reshape(n).astype(jnp.int32)
    rank = meta[:, 3, :].reshape(n).astype(jnp.int32)
    count = cnt[:NCLS, 0].astype(jnp.int32)
    ntile = (count + FFN_TM - 1) // FFN_TM
    tile_end = jnp.cumsum(ntile)
    row_start = (tile_end - ntile) * FFN_TM
    pos = rank + jnp.sum(jax.nn.one_hot(cls, NCLS, dtype=jnp.int32) * row_start[None, :], axis=1)
    nt = n // FFN_TM + NCLS
    total = tile_end[NCLS - 1]
    tr = jnp.minimum(jnp.arange(nt, dtype=jnp.int32), total - 1)
    tcls = jnp.sum((tr[:, None] >= tile_end[None, :]).astype(jnp.int32), axis=1)
    grp, pair = tcls // PAIRS, tcls % PAIRS
    lo = (pair >= 3).astype(jnp.int32) + (pair >= 5).astype(jnp.int32)
    hi = pair + 1 - lo * (2 * E_PER_G - 3 - lo) // 2
    ta = grp * E_PER_G + lo
    tb = grp * E_PER_G + hi
    t_all = jnp.arange(nt, dtype=jnp.int32)
    partial_last = ((t_all[:, None] == tile_end[None, :] - 1) & (count % FFN_TM != 0)[None, :]).any(axis=1)
    zflag = ((t_all >= total) | partial_last).astype(jnp.int32)
    xs = _dispatch(pos, zflag, urow)
    ys = _ffn_sorted(ta, tb, total[None], xs, w1, w3, w2)
    return _combine(pos, x, mod, ys, ln_g, ln_b, layer=layer, seq=seq)


def _moe_body(x_ref, mod_ref, wr_ref, br_ref, w1f_ref, w3f_ref, w2f_ref, g_ref, b_ref,
              xn_ref, w1_ref, w3_ref, w2_ref, ub, comb, acc, *, reps):
    e = pl.program_id(1)
    w1_ref[...] = w1f_ref[...].astype(BF16)
    w3_ref[...] = w3f_ref[...].astype(BF16)
    w2_ref[...] = w2f_ref[...].astype(BF16)

    @pl.when(e == 0)
    def _():
        shift, sc1p, _ = _split_mod(mod_ref, reps=reps)
        u = (x_ref[...] * sc1p + shift).astype(BF16)
        ub[...] = u
        eid, _, i1, i2, w1, w2 = _top2(_router_logits(u, wr_ref, br_ref))
        comb[...] = _rows_to_cols(jnp.where(eid == i1, w1, 0.0) + jnp.where(eid == i2, w2, 0.0))
        acc[...] = jnp.zeros(acc.shape, F32)

    u = ub[...]
    col = lax.broadcasted_iota(jnp.int32, comb.shape, 1)
    ce = jnp.sum(jnp.where(col == e, comb[...], 0.0), axis=-1, keepdims=True)
    h = jax.nn.silu(_dot(u, w1_ref[...])) * _dot(u, w3_ref[...]) * ce
    acc[...] += _dot(h.astype(BF16), w2_ref[...])

    @pl.when(e == N_EXP - 1)
    def _():
        _, _, g1p = _split_mod(mod_ref, reps=reps)
        xn_ref[...] = _layer_norm(ALPHA * x_ref[...] + g1p * acc[...], g_ref[...], b_ref[...])


def _moe_dense(x, mod, wr, br, w1, w3, w2, ln_g, ln_b, *, layer, reps):
    n = x.shape[0]
    sub = 2 * layer + 1
    expert = lambda i, e: (layer, e, 0, 0)
    return pl.pallas_call(
        functools.partial(_moe_body, reps=reps),
        grid=(1, N_EXP),
        in_specs=[
            pl.BlockSpec((n, D), lambda i, e: (0, 0)),
            _layer_spec((n // reps, 3 * D), sub),
            _layer_spec((RROWS, D), layer), _layer_spec((RROWS, RLANES), layer),
            pl.BlockSpec((None, None, D, F), expert),
            pl.BlockSpec((None, None, D, F), expert),
            pl.BlockSpec((None, None, F, D), expert),
            _layer_spec((1, D), sub), _layer_spec((1, D), sub),
        ],
        out_specs=[pl.BlockSpec((n, D), lambda i, e: (0, 0)),
                   pl.BlockSpec((None, D, F), lambda i, e: (e, 0, 0)),
                   pl.BlockSpec((None, D, F), lambda i, e: (e, 0, 0)),
                   pl.BlockSpec((None, F, D), lambda i, e: (e, 0, 0))],
        out_shape=[jax.ShapeDtypeStruct((n, D), F32),
                   jax.ShapeDtypeStruct((N_EXP, D, F), BF16),
                   jax.ShapeDtypeStruct((N_EXP, D, F), BF16),
                   jax.ShapeDtypeStruct((N_EXP, F, D), BF16)],
        scratch_shapes=[
            pltpu.VMEM((n, D), BF16),
            pltpu.VMEM((n, RLANES), F32),
            pltpu.VMEM((n, D), F32),
        ],
        compiler_params=_cp(("arbitrary", "arbitrary")),
        name="moe",
    )(x, mod, wr, br, w1, w3, w2, ln_g, ln_b)


def _kv_body(x_ref, w_ref, wvt_ref, k_ref, v_ref, k16_ref, vt16_ref):
    xb = x_ref[...].astype(BF16)
    kv = _dot(xb, w_ref[...])
    k_ref[...] = kv[:, :KVW]
    v_ref[...] = kv[:, KVW:]
    k16_ref[...] = kv[:, :KVW].astype(BF16)
    vt16_ref[...] = lax.dot_general(wvt_ref[...], xb, _NT, preferred_element_type=F32).astype(BF16)


def _kv(x, w, wvt, *, tm):
    n = x.shape[0]
    row = lambda i: (i, 0)
    return pl.pallas_call(
        _kv_body,
        grid=(n // tm,),
        in_specs=[pl.BlockSpec((tm, D), row), _const_spec((D, 2 * KVW)), _const_spec((KVW, D))],
        out_specs=[pl.BlockSpec((tm, KVW), row), pl.BlockSpec((tm, KVW), row),
                   pl.BlockSpec((tm, KVW), row), pl.BlockSpec((KVW, tm), lambda i: (0, i))],
        out_shape=[jax.ShapeDtypeStruct((n, KVW), F32), jax.ShapeDtypeStruct((n, KVW), F32),
                   jax.ShapeDtypeStruct((n, KVW), BF16), jax.ShapeDtypeStruct((KVW, n), BF16)],
        compiler_params=_cp(("arbitrary",)),
        name="kv_proj",
    )(x, w, wvt)


def _sink_attend(qh, kh, vh, bias, sink):
    s = lax.dot_general(qh, kh, (((1,), (1,)), ((), ())), preferred_element_type=F32) + bias
    m = jnp.maximum(jnp.max(s, axis=-1, keepdims=True), sink)
    p = jnp.exp(s - m)
    den = jnp.sum(p, axis=-1, keepdims=True) + jnp.exp(sink - m)
    return _dot(p.astype(BF16), vh) / den


def _attn_prompt_body(x_ref, mod_ref, wqt_ref, wo_ref, kc_ref, kp_ref, vtc_ref, vtp_ref,
                      bias_ref, sink_ref, g_ref, b_ref, xn_ref, *, tq):
    t = pl.program_id(1)
    x = x_ref[...]
    shift, sc1p, g1p = _split_mod(mod_ref, row=pl.program_id(0))
    u = (x * sc1p + shift).astype(BF16)
    qt = lax.dot_general(wqt_ref[...], u, _NT, preferred_element_type=F32)
    qt = (qt * (HD ** -0.5)).astype(BF16)
    kc = kc_ref[...]
    vtc = vtc_ref[...]
    krow = lax.broadcasted_iota(jnp.int32, (2 * WIN, GRP * WIN), 0)
    first = jnp.where((krow < WIN) & (t == 0), NEG, 0.0).astype(F32)
    nwin = tq // WIN
    kks, vvts = [], []
    for j in range(nwin):
        if j == 0:
            kks.append(jnp.concatenate([kp_ref[...], kc[0:WIN]], axis=0))
            vvts.append(jnp.concatenate([vtp_ref[...], vtc[:, 0:WIN]], axis=1))
        else:
            kks.append(kc[(j - 1) * WIN:(j + 1) * WIN])
            vvts.append(vtc[:, (j - 1) * WIN:(j + 1) * WIN])

    def scores(j, hk):
        qht = jnp.concatenate(
            [qt[(hk * GRP + g) * HD:(hk * GRP + g + 1) * HD, j * WIN:(j + 1) * WIN] for g in range(GRP)],
            axis=1)
        st = _dot(kks[j][:, hk * HD:(hk + 1) * HD], qht) + bias_ref[hk]
        return st + first if j == 0 else st

    items = [(j, hk) for j in range(nwin) for hk in range(HKV)]
    heads = [[] for _ in range(nwin)]
    st_next = scores(*items[0])
    for n, (j, hk) in enumerate(items):
        st = st_next
        if n + 1 < len(items):
            st_next = scores(*items[n + 1])
        sink = sink_ref[hk]
        m = jnp.maximum(jnp.max(st, axis=0, keepdims=True), sink)
        pt = jnp.exp(st - m)
        den = jnp.sum(pt, axis=0, keepdims=True) + jnp.exp(sink - m)
        ot = _dot(vvts[j][hk * HD:(hk + 1) * HD, :], pt.astype(BF16)) * (1.0 / den)
        for g in range(GRP):
            heads[j].append(ot[:, g * WIN:(g + 1) * WIN])
    blocks = [jnp.concatenate(h, axis=0) for h in heads]
    oallt = jnp.concatenate(blocks, axis=1) if len(blocks) > 1 else blocks[0]
    mix = lax.dot_general(oallt.astype(BF16), wo_ref[...], _TN, preferred_element_type=F32)
    xn_ref[...] = _layer_norm(ALPHA * x + g1p * mix, g_ref[...], b_ref[...])


def _attn_prompt(x, mod, wqt, wo, k16, vt16, bias, sink, ln_g, ln_b, *, layer, nb, seq, tq):
    nt = seq // tq
    wpt = tq // WIN
    n = nb * seq
    j = layer - N_A
    sub = 2 * layer

    def prev(b, t):
        return jnp.maximum((b * nt + t) * wpt - 1, 0)

    cur = lambda b, t: (b * nt + t, 0)
    return pl.pallas_call(
        functools.partial(_attn_prompt_body, tq=tq),
        grid=(nb, nt),
        in_specs=[
            pl.BlockSpec((tq, D), cur),
            _layer_spec((nb, 3 * D), sub),
            _layer_spec((HQ * HD, D), j), _layer_spec((HQ * HD, D), j),
            pl.BlockSpec((tq, KVW), cur), pl.BlockSpec((WIN, KVW), lambda b, t: (prev(b, t), 0)),
            pl.BlockSpec((KVW, tq), lambda b, t: (0, b * nt + t)),
            pl.BlockSpec((KVW, WIN), lambda b, t: (0, prev(b, t))),
            _const_spec((HKV, 2 * WIN, GRP * WIN)), _layer_spec((HKV, 1, GRP * WIN), j),
            _layer_spec((1, D), sub), _layer_spec((1, D), sub),
        ],
        out_specs=pl.BlockSpec((tq, D), cur),
        out_shape=jax.ShapeDtypeStruct((n, D), F32),
        compiler_params=_cp(("arbitrary", "arbitrary")),
        name="attn_prompt",
    )(x, mod, wqt, wo, k16, k16, vt16, vt16, bias, sink, ln_g, ln_b)


def _attn_sample_body(x_ref, mod_ref, wq_ref, wo_ref, kb_ref, vb_ref, kn_ref, vn_ref,
                      bias_ref, sink_ref, g_ref, b_ref, xn_ref, *, ns, steps):
    nr = ns * steps
    x = x_ref[...].reshape(nr, D)
    shift, sc1p, g1p = _split_mod(mod_ref, reps=steps)
    u = (x * sc1p + shift).astype(BF16)
    q = (_dot(u, wq_ref[...]) * (HD ** -0.5)).astype(BF16)
    nbuf = kb_ref.shape[1]
    k_all = jnp.concatenate([kb_ref[...].reshape(ns * nbuf, KVW), kn_ref[...].reshape(nr, KVW)],
                            axis=0).astype(BF16)
    v_all = jnp.concatenate([vb_ref[...].reshape(ns * nbuf, KVW), vn_ref[...].reshape(nr, KVW)],
                            axis=0).astype(BF16)
    heads = [None] * HQ
    for hk in range(HKV):
        kcols = slice(hk * HD, (hk + 1) * HD)
        qh = jnp.concatenate(
            [q[:, (hk * GRP + g) * HD:(hk * GRP + g + 1) * HD] for g in range(GRP)], axis=0)
        o = _sink_attend(qh, k_all[:, kcols], v_all[:, kcols], bias_ref[hk], sink_ref[hk])
        for g in range(GRP):
            heads[hk * GRP + g] = o[g * nr:(g + 1) * nr]
    o_all = jnp.concatenate(heads, axis=1)
    mix = _dot(o_all.astype(BF16), wo_ref[...])
    xn_ref[...] = _layer_norm(ALPHA * x + g1p * mix, g_ref[...], b_ref[...]).reshape(steps, ns, D)


def _attn_sample(x, mod, wq, wo, kbuf, vbuf, kn, vn, bias, sink, ln_g, ln_b, *, layer, ns):
    steps, nb, _ = x.shape
    nbuf = kbuf.shape[1]
    j = layer - N_A
    sub = 2 * layer
    seqs = lambda i: (0, i, 0)
    return pl.pallas_call(
        functools.partial(_attn_sample_body, ns=ns, steps=steps),
        grid=(nb // ns,),
        in_specs=[
            pl.BlockSpec((steps, ns, D), seqs),
            pl.BlockSpec((None, ns, 3 * D), lambda i: (sub, i, 0)),
            _layer_spec((D, HQ * HD), j), _layer_spec((HQ * HD, D), j),
            pl.BlockSpec((ns, nbuf, KVW), lambda i: (i, 0, 0)),
            pl.BlockSpec((ns, nbuf, KVW), lambda i: (i, 0, 0)),
            pl.BlockSpec((steps, ns, KVW), seqs), pl.BlockSpec((steps, ns, KVW), seqs),
            _const_spec(bias.shape), _layer_spec(sink.shape[1:], j),
            _layer_spec((1, D), sub), _layer_spec((1, D), sub),
        ],
        out_specs=pl.BlockSpec((steps, ns, D), seqs),
        out_shape=jax.ShapeDtypeStruct((steps, nb, D), F32),
        compiler_params=_cp(("arbitrary",)),
        name="attn_sample",
    )(x, mod, wq, wo, kbuf, vbuf, kn, vn, bias, sink, ln_g, ln_b)


def _t5_bucket(d):
    max_exact = N_BUCKETS // 2
    d = jnp.maximum(d, 0)
    log_ratio = jnp.log(jnp.maximum(d, 1).astype(F32) / max_exact) / math.log(WIN / max_exact)
    large = max_exact + (log_ratio * (N_BUCKETS - max_exact)).astype(jnp.int32)
    large = jnp.minimum(large, N_BUCKETS - 1)
    return jnp.where(d < max_exact, d, large)


def _bias_table(d, ok, rel_bias):
    tab = jnp.concatenate([rel_bias[_t5_bucket(jnp.arange(WIN))].astype(F32),
                           jnp.full((1, HQ), NEG, F32)], axis=0)
    sel = jax.nn.one_hot(jnp.where(ok, d, WIN), WIN + 1, dtype=F32)
    return jnp.einsum('abd,dh->hab', sel, tab, precision=lax.Precision.HIGHEST)


def _prompt_bias(rel_bias):
    d = jnp.arange(WIN)[None, :] + WIN - jnp.arange(2 * WIN)[:, None]
    b = _bias_table(d, (d >= 0) & (d < WIN), rel_bias).reshape(HKV, GRP, 2 * WIN, WIN)
    return jnp.transpose(b, (0, 2, 1, 3)).reshape(HKV, 2 * WIN, GRP * WIN)


def _sample_bias(rel_bias, ns, steps, nbuf):
    qs = jnp.tile(jnp.arange(ns), steps)
    qt = jnp.repeat(jnp.arange(steps), ns)
    ks = jnp.concatenate([jnp.repeat(jnp.arange(ns), nbuf), qs])
    kpos = jnp.concatenate([jnp.tile(jnp.arange(nbuf), ns), nbuf + qt])
    d = qt[:, None] + nbuf - kpos[None, :]
    ok = (d >= 0) & (d < WIN) & (qs[:, None] == ks[None, :])
    return _bias_table(d, ok, rel_bias).reshape(HKV, GRP * ns * steps, kpos.shape[0])


LRU_TM = 512
ATTN_TQ = 512
SAMPLE_NS = 8


def kernel(x_prompt, x_sample, state_rnn_h, state_rnn_conv, cache_win_k, cache_win_v, c_prompt, c_sample, ada_w, ada_b, ln_g, ln_b, lru_w_in, lru_conv_w, lru_conv_b, lru_w_a, lru_b_a, lru_w_x, lru_b_x, lru_lambda, lru_w_out, kv_w, attn_w_q, attn_sinks, attn_w_o, rel_bias, moe_w_group, moe_b_group, moe_w_router, moe_b_router, moe_w1, moe_w3, moe_w2):
    nbp, seq, _ = x_prompt.shape
    nbs, steps, _ = x_sample.shape
    nbuf = cache_win_k.shape[1]
    npt = nbp * seq
    nst = nbs * steps

    mod_p, mod_s = _adaln(c_prompt, c_sample, ada_w, ada_b)

    ln_g3 = ln_g.reshape(2 * DEPTH, 1, D)
    ln_b3 = ln_b.reshape(2 * DEPTH, 1, D)
    lw = (lru_w_in.astype(BF16), lru_conv_w, lru_conv_b[:, None], lru_w_a.astype(BF16),
          lru_w_x.astype(BF16), lru_b_a[:, None], lru_b_x[:, None], lru_lambda[:, None],
          lru_w_out.astype(BF16))
    zrow = lambda k: jnp.zeros((DEPTH, k, D), F32)
    wr = jnp.concatenate([jnp.swapaxes(moe_w_group, 1, 2), zrow(EROW - N_GROUPS),
                          jnp.swapaxes(moe_w_router.reshape(DEPTH, D, N_EXP), 1, 2),
                          zrow(RROWS - EROW - N_EXP)], axis=1).astype(BF16)
    zb = lambda k: jnp.zeros((DEPTH, k), F32)
    br = jnp.concatenate([moe_b_group, zb(EROW - N_GROUPS), moe_b_router.reshape(DEPTH, N_EXP),
                          zb(RROWS - EROW - N_EXP)], axis=1)
    br = jnp.broadcast_to(br[:, :, None], (DEPTH, RROWS, RLANES))
    wq = attn_w_q.astype(BF16)
    wqt = jnp.swapaxes(wq, 1, 2)
    wo = attn_w_o.astype(BF16)
    kvw = kv_w.astype(BF16)
    kvwt = kvw[:, KVW:].T
    bias_p = _prompt_bias(rel_bias)
    bias_s = _sample_bias(rel_bias, SAMPLE_NS, steps, nbuf)
    sink_p = jnp.repeat(attn_sinks.reshape(N_B, HKV, GRP), WIN, axis=2)[:, :, None, :]
    sink_s = jnp.repeat(attn_sinks.reshape(N_B, HKV, GRP), SAMPLE_NS * steps, axis=2)[..., None]
    kbuf = cache_win_k.reshape(nbs, nbuf, KVW)
    vbuf = cache_win_v.reshape(nbs, nbuf, KVW)

    xp = x_prompt.reshape(npt, D)
    xs = jnp.transpose(x_sample, (1, 0, 2)).reshape(nst, D)
    c0 = jnp.transpose(state_rnn_conv, (0, 2, 1, 3)).reshape(N_A, (CONV_W - 1) * nbs, R)
    hp, cp_, hs, cs = [], [], [], []
    kp = vp = kn = vn = kp16 = vpt16 = None
    for l in range(DEPTH):
        if l < N_A:
            xp, h_l, c_l = _lru_prompt(xp, mod_p, lw, ln_g3, ln_b3, layer=l, nb=nbp, seq=seq, tm=LRU_TM)
            hp.append(h_l[:, 0])
            cp_.append(c_l)
            xs, h_l, c_l = _lru_sample(xs, mod_s, state_rnn_h, c0[l], lw, ln_g3, ln_b3,
                                       layer=l, nb=nbs, steps=steps)
            hs.append(h_l)
            cs.append(jnp.transpose(c_l.reshape(CONV_W - 1, nbs, R), (1, 0, 2)))
        else:
            xp = _attn_prompt(xp, mod_p, wqt, wo, kp16, vpt16, bias_p, sink_p, ln_g3, ln_b3,
                              layer=l, nb=nbp, seq=seq, tq=ATTN_TQ)
            xs = _attn_sample(xs.reshape(steps, nbs, D), mod_s, wq, wo, kbuf, vbuf,
                              kn.reshape(steps, nbs, KVW), vn.reshape(steps, nbs, KVW),
                              bias_s, sink_s, ln_g3, ln_b3, layer=l, ns=SAMPLE_NS).reshape(nst, D)
        xs, w1, w3, w2 = _moe_dense(xs, mod_s, wr, br, moe_w1, moe_w3, moe_w2, ln_g3, ln_b3,
                                    layer=l, reps=steps)
        xp = _moe_sparse(xp, mod_p, wr, br, w1, w3, w2, ln_g3, ln_b3, layer=l, seq=seq)
        if l == N_A - 1:
            kp, vp, kp16, vpt16 = _kv(xp, kvw, kvwt, tm=1024)
            kn, vn, _, _ = _kv(xs, kvw, kvwt, tm=nst)

    lp = min(WIN, seq)
    k_p = kp.reshape(nbp, seq, HKV, HD)[:, seq - lp:]
    v_p = vp.reshape(nbp, seq, HKV, HD)[:, seq - lp:]
    kn4 = jnp.transpose(kn.reshape(steps, nbs, HKV, HD), (1, 0, 2, 3))
    vn4 = jnp.transpose(vn.reshape(steps, nbs, HKV, HD), (1, 0, 2, 3))
    k_s = jnp.concatenate([cache_win_k, kn4], axis=1)[:, -nbuf:]
    v_s = jnp.concatenate([cache_win_v, vn4], axis=1)[:, -nbuf:]
    y_s = jnp.transpose(xs.reshape(steps, nbs, D), (1, 0, 2))
    return (xp.reshape(nbp, seq, D), y_s,
            jnp.stack(hp), jnp.stack(cp_), k_p, v_p,
            jnp.stack(hs), jnp.stack(cs), k_s, v_s)
```

```python
import functools
import math

import jax
import jax.numpy as jnp
import numpy as np
from jax import lax
from jax.experimental import pallas as pl
from jax.experimental.pallas import tpu as pltpu

D = 1024
R = 1024
DEPTH = 4
N_A = 2
N_B = DEPTH - N_A
N_BLK = 4
BLK = R // N_BLK
CONV_W = 4
RG_C = 8.0
HQ = 16
HKV = 4
HD = 64
GRP = HQ // HKV
KVW = HKV * HD
WIN = 128
N_BUCKETS = 32
N_GROUPS = 4
E_PER_G = 4
N_EXP = 16
F = 512
ALPHA = (2.0 * DEPTH) ** 0.25
LN_EPS = 1e-5
NEG = -1e30
RLANES = 128
ROW_TILE = 8

F32 = jnp.float32
BF16 = jnp.bfloat16

VMEM_LIMIT = 56 * 1024 * 1024


def _cp(sem):
    return pltpu.CompilerParams(dimension_semantics=sem, vmem_limit_bytes=VMEM_LIMIT)


def _dot(a, b):
    return jnp.dot(a, b, preferred_element_type=F32)


def _layer_norm(z, g, b):
    mu = jnp.mean(z, axis=-1, keepdims=True)
    zc = z - mu
    var = jnp.mean(zc * zc, axis=-1, keepdims=True)
    return zc * lax.rsqrt(var + LN_EPS) * g + b


def _split_mod(mod_ref, row=None, reps=1):
    m = mod_ref[...] if row is None else mod_ref[pl.ds(row, 1), :]
    if reps > 1:
        m = jnp.concatenate([m] * reps, axis=0)
    return m[:, 0:D], m[:, D:2 * D], m[:, 2 * D:3 * D]


def _layer_spec(shape, layer):
    nd = len(shape)
    return pl.BlockSpec((None,) + tuple(shape), lambda *_: (layer,) + (0,) * nd)


def _adaln_body(cp_ref, cs_ref, w_ref, b_ref, op_ref, os_ref):
    one = jnp.where(pl.program_id(1) > 0, 1.0, 0.0).astype(F32)
    w = w_ref[...].astype(BF16)
    op_ref[...] = _dot(cp_ref[...].astype(BF16), w) + b_ref[...] + one
    os_ref[...] = _dot(cs_ref[...].astype(BF16), w) + b_ref[...] + one


def _adaln(c_p, c_s, ada_w, ada_b):
    np_, ns_ = c_p.shape[0], c_s.shape[0]
    w = ada_w.reshape(2 * DEPTH, D, 3 * D)
    b = ada_b.reshape(2 * DEPTH, 1, 3 * D)
    return pl.pallas_call(
        _adaln_body,
        grid=(2 * DEPTH, 3),
        in_specs=[
            pl.BlockSpec((np_, D), lambda l, j: (0, 0)),
            pl.BlockSpec((ns_, D), lambda l, j: (0, 0)),
            pl.BlockSpec((None, D, D), lambda l, j: (l, 0, j)),
            pl.BlockSpec((None, 1, D), lambda l, j: (l, 0, j)),
        ],
        out_specs=[pl.BlockSpec((None, np_, D), lambda l, j: (l, 0, j)),
                   pl.BlockSpec((None, ns_, D), lambda l, j: (l, 0, j))],
        out_shape=[jax.ShapeDtypeStruct((2 * DEPTH, np_, 3 * D), F32),
                   jax.ShapeDtypeStruct((2 * DEPTH, ns_, 3 * D), F32)],
        compiler_params=_cp(("arbitrary", "arbitrary")),
        name="adaln",
    )(c_p, c_s, w, b)


def _lru_gates(xc, n, wa_ref, wx_ref, ba_ref, bx_ref, lam_ref):
    cols = slice(n * BLK, (n + 1) * BLK)
    xb = xc.astype(BF16)
    r = jax.nn.sigmoid(_dot(xb, wa_ref[n]) + ba_ref[:, cols])
    i = jax.nn.sigmoid(_dot(xb, wx_ref[n]) + bx_ref[:, cols])
    z = -lam_ref[:, cols]
    softplus = jnp.maximum(z, 0.0) + jnp.log1p(jnp.exp(-jnp.abs(z)))
    log_a = (-RG_C) * r * softplus
    a = jnp.exp(log_a)
    v = 1.0 - a * a
    b = jnp.where(v > 0.0, v * lax.rsqrt(v), 0.0) * i * xc
    return a, b


def _lru_prompt_body(x_ref, mod_ref, win_ref, cw_ref, cb_ref, wa_ref, wx_ref, ba_ref, bx_ref,
                     lam_ref, wout_ref, g_ref, b_ref,
                     xn_ref, hl_ref, cl_ref,
                     xbuf, abuf, hbuf, hc, *, tm):
    t = pl.program_id(1)

    @pl.when(t == 0)
    def _():
        xbuf[0:ROW_TILE, :] = jnp.zeros((ROW_TILE, R), F32)
        hc[...] = jnp.zeros((1, R), F32)

    x = x_ref[...]
    shift, sc1p, g1p = _split_mod(mod_ref, row=pl.program_id(0))
    u = x * sc1p + shift
    yx = _dot(u.astype(BF16), win_ref[...])
    y = jax.nn.gelu(yx[:, :R])
    xbuf[ROW_TILE:ROW_TILE + tm, :] = yx[:, R:]

    ngrp = tm // ROW_TILE
    rowmod = lax.broadcasted_iota(jnp.int32, (ngrp, ROW_TILE, BLK), 1)
    for n in range(N_BLK):
        cols = slice(n * BLK, (n + 1) * BLK)
        xc = cb_ref[:, cols] + cw_ref[3:4, cols] * xbuf[ROW_TILE:ROW_TILE + tm, cols]
        for k in range(1, CONV_W):
            xc = xc + cw_ref[3 - k:4 - k, cols] * xbuf[ROW_TILE - k:ROW_TILE - k + tm, cols]
        a, b = _lru_gates(xc, n, wa_ref, wx_ref, ba_ref, bx_ref, lam_ref)
        a = a.reshape(ngrp, ROW_TILE, BLK)
        b = b.reshape(ngrp, ROW_TILE, BLK)
        s = 1
        while s < ROW_TILE:
            a_sh = pltpu.roll(a, s, 1)
            b_sh = pltpu.roll(b, s, 1)
            m = rowmod >= s
            b = jnp.where(m, a * b_sh, 0.0) + b
            a = jnp.where(m, a * a_sh, a)
            s *= 2
        abuf[:, cols] = a.reshape(tm, BLK)
        hbuf[:, cols] = b.reshape(tm, BLK)

    def group(g, h):
        r0 = pl.multiple_of(g * ROW_TILE, ROW_TILE)
        hg = abuf[pl.ds(r0, ROW_TILE), :] * h + hbuf[pl.ds(r0, ROW_TILE), :]
        hbuf[pl.ds(r0, ROW_TILE), :] = hg
        return hg[ROW_TILE - 1:ROW_TILE, :]

    h_end = lax.fori_loop(0, tm // ROW_TILE, group, hc[...])
    hc[...] = h_end
    hl_ref[...] = h_end
    tail = xbuf[tm + ROW_TILE - (CONV_W - 1):tm + ROW_TILE, :]
    cl_ref[...] = tail
    xbuf[ROW_TILE - (CONV_W - 1):ROW_TILE, :] = tail

    mix = _dot((y * hbuf[...]).astype(BF16), wout_ref[...])
    xn_ref[...] = _layer_norm(ALPHA * x + g1p * mix, g_ref[...], b_ref[...])


def _const_spec(shape):
    nd = len(shape)
    return pl.BlockSpec(shape, lambda *_: (0,) * nd)


def _lru_weight_specs(layer):
    sub = 2 * layer
    return [
        _layer_spec((D, 2 * R), layer), _layer_spec((CONV_W, R), layer), _layer_spec((1, R), layer),
        _layer_spec((N_BLK, BLK, BLK), layer), _layer_spec((N_BLK, BLK, BLK), layer),
        _layer_spec((1, R), layer), _layer_spec((1, R), layer), _layer_spec((1, R), layer),
        _layer_spec((R, D), layer), _layer_spec((1, D), sub), _layer_spec((1, D), sub),
    ]


def _lru_prompt(x, mod, lw, ln_g, ln_b, *, layer, nb, seq, tm):
    nt = seq // tm
    n = nb * seq
    return pl.pallas_call(
        functools.partial(_lru_prompt_body, tm=tm),
        grid=(nb, nt),
        in_specs=[
            pl.BlockSpec((tm, D), lambda b, t: (b * nt + t, 0)),
            _layer_spec((nb, 3 * D), 2 * layer),
        ] + _lru_weight_specs(layer),
        out_specs=[
            pl.BlockSpec((tm, D), lambda b, t: (b * nt + t, 0)),
            pl.BlockSpec((None, 1, R), lambda b, t: (b, 0, 0)),
            pl.BlockSpec((None, CONV_W - 1, R), lambda b, t: (b, 0, 0)),
        ],
        out_shape=[
            jax.ShapeDtypeStruct((n, D), F32),
            jax.ShapeDtypeStruct((nb, 1, R), F32),
            jax.ShapeDtypeStruct((nb, CONV_W - 1, R), F32),
        ],
        scratch_shapes=[
            pltpu.VMEM((tm + ROW_TILE, R), F32),
            pltpu.VMEM((tm, R), F32),
            pltpu.VMEM((tm, R), F32),
            pltpu.VMEM((1, R), F32),
        ],
        compiler_params=_cp(("arbitrary", "arbitrary")),
        name="lru_prompt",
    )(x, mod, *lw, ln_g, ln_b)


def _lru_sample_body(x_ref, mod_ref, h0_ref, c0_ref, win_ref, cw_ref, cb_ref, wa_ref, wx_ref,
                     ba_ref, bx_ref, lam_ref, wout_ref, g_ref, b_ref,
                     xn_ref, hl_ref, cl_ref, hbuf, *, nb, steps):
    x = x_ref[...]
    shift, sc1p, g1p = _split_mod(mod_ref, reps=steps)
    u = x * sc1p + shift
    yx = _dot(u.astype(BF16), win_ref[...])
    y = jax.nn.gelu(yx[:, :R])
    xext = jnp.concatenate([c0_ref[...], yx[:, R:]], axis=0)
    cl_ref[...] = xext[steps * nb:, :]
    for n in range(N_BLK):
        cols = slice(n * BLK, (n + 1) * BLK)
        xc = cb_ref[:, cols] + cw_ref[0:1, cols] * xext[0:steps * nb, cols]
        for k in range(1, CONV_W):
            xc = xc + cw_ref[k:k + 1, cols] * xext[k * nb:(k + steps) * nb, cols]
        a, b = _lru_gates(xc, n, wa_ref, wx_ref, ba_ref, bx_ref, lam_ref)
        h = h0_ref[:, cols]
        for t in range(steps):
            h = a[t * nb:(t + 1) * nb, :] * h + b[t * nb:(t + 1) * nb, :]
            hbuf[t * nb:(t + 1) * nb, cols] = h
        hl_ref[:, cols] = h
    mix = _dot((y * hbuf[...]).astype(BF16), wout_ref[...])
    xn_ref[...] = _layer_norm(ALPHA * x + g1p * mix, g_ref[...], b_ref[...])


def _lru_sample(x, mod, h0, c0, lw, ln_g, ln_b, *, layer, nb, steps):
    n = nb * steps
    return pl.pallas_call(
        functools.partial(_lru_sample_body, nb=nb, steps=steps),
        grid=(1,),
        in_specs=[
            _const_spec((n, D)), _layer_spec((nb, 3 * D), 2 * layer), _layer_spec((nb, R), layer),
            _const_spec(((CONV_W - 1) * nb, R)),
        ] + _lru_weight_specs(layer),
        out_specs=[_const_spec((n, D)), _const_spec((nb, R)), _const_spec(((CONV_W - 1) * nb, R))],
        out_shape=[
            jax.ShapeDtypeStruct((n, D), F32),
            jax.ShapeDtypeStruct((nb, R), F32),
            jax.ShapeDtypeStruct(((CONV_W - 1) * nb, R), F32),
        ],
        scratch_shapes=[pltpu.VMEM((n, R), F32)],
        compiler_params=_cp(("arbitrary",)),
        name="lru_sample",
    )(x, mod, h0, c0, *lw, ln_g, ln_b)


RROWS = 32
EROW = 8
_NT = (((1,), (1,)), ((), ()))
_TN = (((0,), (0,)), ((), ()))


def _top2(lt):
    nt = lt.shape[1]
    gl = lt[0:EROW, :]
    gid = lax.broadcasted_iota(jnp.int32, (EROW, nt), 0).astype(F32)
    is_g = gid < N_GROUPS
    gl = jnp.where(is_g, gl, NEG)
    gmax = jnp.max(gl, axis=0, keepdims=True)
    gidx = jnp.min(jnp.where(gl == gmax, gid, float(EROW)), axis=0, keepdims=True)
    gsum = jnp.sum(jnp.where(is_g, jnp.exp(gl - gmax), 0.0), axis=0, keepdims=True)
    g_p = 1.0 / gsum
    el = lt[EROW:EROW + N_EXP, :]
    eid = lax.broadcasted_iota(jnp.int32, (N_EXP, nt), 0).astype(F32)
    first = gidx * E_PER_G
    in_grp = (eid >= first) & (eid < first + E_PER_G)
    el = jnp.where(in_grp, el, NEG)
    big = float(N_EXP)
    t1 = jnp.max(el, axis=0, keepdims=True)
    i1 = jnp.min(jnp.where(el == t1, eid, big), axis=0, keepdims=True)
    el2 = jnp.where(eid == i1, NEG, el)
    t2 = jnp.max(el2, axis=0, keepdims=True)
    i2 = jnp.min(jnp.where((el2 == t2) & in_grp & (eid != i1), eid, big), axis=0, keepdims=True)
    e21 = jnp.exp(t2 - t1)
    w1 = g_p / (1.0 + e21)
    w2 = w1 * e21
    return eid, gidx, i1, i2, w1, w2


def _router_logits(u16, wrt_ref, brt_ref):
    return lax.dot_general(wrt_ref[...], u16, _NT, preferred_element_type=F32) + brt_ref[:, 0:1]


def _rows_to_cols(rows):
    r, nt = rows.shape
    padded = jnp.concatenate([rows, jnp.zeros((RLANES - r, nt), F32)], axis=0)
    return jnp.transpose(padded)


PAIRS = E_PER_G * (E_PER_G - 1) // 2
NCLS = N_GROUPS * PAIRS
XW = D + RLANES
FFN_TM = 256
ROUTE_TM = 512
DISPATCH_TD = 1024
COMBINE_TC = 256
COMBINE_CHUNK = 64
COMBINE_SLOTS = 3


def _route_sort_body(x_ref, mod_ref, wr_ref, br_ref, urow_ref, meta_ref, cnt_ref, carry, *, tiles_per_seq):
    i = pl.program_id(0)

    @pl.when(i == 0)
    def _():
        carry[...] = jnp.zeros(carry.shape, F32)

    tm = x_ref.shape[0]
    shift, sc1p, _ = _split_mod(mod_ref, row=i // tiles_per_seq)
    u = x_ref[...] * sc1p + shift
    _, gidx, i1, i2, w1, w2 = _top2(_router_logits(u.astype(BF16), wr_ref, br_ref))
    first = gidx * E_PER_G
    lo = jnp.minimum(i1, i2) - first
    hi = jnp.maximum(i1, i2) - first
    w_lo = jnp.where(i1 < i2, w1, w2)
    w_hi = jnp.where(i1 < i2, w2, w1)
    cls = gidx * PAIRS + lo * (2 * E_PER_G - 1 - lo) * 0.5 + (hi - lo - 1.0)
    cid = lax.broadcasted_iota(jnp.int32, (RROWS, tm), 0).astype(F32)
    onehot = (cid == cls).astype(F32)
    r = lax.broadcasted_iota(jnp.int32, (tm, tm), 0)
    c = lax.broadcasted_iota(jnp.int32, (tm, tm), 1)
    earlier = _dot(onehot.astype(BF16), (r < c).astype(BF16)) + carry[:, 0:1]
    rank = jnp.sum(onehot * earlier, axis=0, keepdims=True)
    carry[...] += jnp.sum(onehot, axis=1, keepdims=True)
    cnt_ref[...] = carry[...]
    meta = jnp.concatenate([w_lo, w_hi, cls, rank, jnp.zeros((ROW_TILE - 4, tm), F32)], axis=0)
    meta_ref[...] = meta
    urow_ref[:, 0:D] = u
    urow_ref[:, D:XW] = _rows_to_cols(meta)


def _route_sort(x, mod, wrt, brt, *, layer, seq):
    n = x.shape[0]
    nt = n // ROUTE_TM
    return pl.pallas_call(
        functools.partial(_route_sort_body, tiles_per_seq=seq // ROUTE_TM),
        grid=(nt,),
        in_specs=[
            pl.BlockSpec((ROUTE_TM, D), lambda i: (i, 0)),
            _layer_spec((n // seq, 3 * D), 2 * layer + 1),
            _layer_spec((RROWS, D), layer), _layer_spec((RROWS, RLANES), layer),
        ],
        out_specs=[pl.BlockSpec((ROUTE_TM, XW), lambda i: (i, 0)),
                   pl.BlockSpec((None, ROW_TILE, ROUTE_TM), lambda i: (i, 0, 0)),
                   _const_spec((RROWS, RLANES))],
        out_shape=[jax.ShapeDtypeStruct((n, XW), F32),
                   jax.ShapeDtypeStruct((nt, ROW_TILE, ROUTE_TM), F32),
                   jax.ShapeDtypeStruct((RROWS, RLANES), F32)],
        scratch_shapes=[pltpu.VMEM((RROWS, RLANES), F32)],
        compiler_params=_cp(("arbitrary",)),
        name="moe_route",
    )(x, mod, wrt, brt)


def _row_copy(src, src_row, dst, dst_row, sem):
    return pltpu.make_async_copy(src.at[pl.ds(src_row, 1)], dst.at[pl.ds(dst_row, 1)], sem)


def _dispatch_body(pos_ref, zflag_ref, urow_ref, xs_hbm, zbuf, sem, zsem):
    base = pl.program_id(0) * DISPATCH_TD

    @pl.when(pl.program_id(0) == 0)
    def _():
        zbuf[...] = jnp.zeros(zbuf.shape, F32)

        def zero_copy(t):
            return pltpu.make_async_copy(zbuf, xs_hbm.at[pl.ds(t * FFN_TM, FFN_TM)], zsem)

        def start(t, carry):
            @pl.when(zflag_ref[t] != 0)
            def _():
                zero_copy(t).start()
            return carry

        def wait(t, carry):
            @pl.when(zflag_ref[t] != 0)
            def _():
                zero_copy(t).wait()
            return carry

        lax.fori_loop(0, zflag_ref.shape[0], start, 0)
        lax.fori_loop(0, zflag_ref.shape[0], wait, 0)

    def issue(g, carry):
        r0 = pl.multiple_of(g * ROW_TILE, ROW_TILE)
        for k in range(ROW_TILE):
            _row_copy(urow_ref.at[pl.ds(r0, ROW_TILE)], k, xs_hbm, pos_ref[base + r0 + k], sem).start()
        return carry

    lax.fori_loop(0, DISPATCH_TD // ROW_TILE, issue, 0)
    pltpu.make_async_copy(urow_ref, xs_hbm.at[pl.ds(0, DISPATCH_TD)], sem).wait()


def _dispatch(pos, zflag, urow):
    n = urow.shape[0]
    nt = zflag.shape[0]
    return pl.pallas_call(
        _dispatch_body,
        grid_spec=pltpu.PrefetchScalarGridSpec(
            num_scalar_prefetch=2, grid=(n // DISPATCH_TD,),
            in_specs=[pl.BlockSpec((DISPATCH_TD, XW), lambda i, pos, zflag: (i, 0))],
            out_specs=pl.BlockSpec(memory_space=pl.ANY),
            scratch_shapes=[pltpu.VMEM((FFN_TM, XW), F32), pltpu.SemaphoreType.DMA(()),
                            pltpu.SemaphoreType.DMA(())]),
        out_shape=jax.ShapeDtypeStruct((nt * FFN_TM, XW), F32),
        compiler_params=_cp(("arbitrary",)),
        name="moe_dispatch",
    )(pos, zflag, urow)


def _ffn_sorted_body(ta_ref, tb_ref, na_ref, xs_ref, w1a, w3a, w2a, w1b, w3b, w2b, ys_ref):
    del ta_ref, tb_ref
    active = pl.program_id(0) < na_ref[0]

    @pl.when(active)
    def _():
        x = xs_ref[:, 0:D].astype(BF16)
        ha = jax.nn.silu(_dot(x, w1a[...])) * _dot(x, w3a[...]) * xs_ref[:, D:D + 1]
        hb = jax.nn.silu(_dot(x, w1b[...])) * _dot(x, w3b[...]) * xs_ref[:, D + 1:D + 2]
        ys_ref[...] = _dot(ha.astype(BF16), w2a[...]) + _dot(hb.astype(BF16), w2b[...])

    @pl.when(jnp.logical_not(active))
    def _():
        ys_ref[...] = jnp.zeros(ys_ref.shape, F32)


def _ffn_sorted(ta, tb, na, xs, w1, w3, w2):
    nt = ta.shape[0]
    ea = lambda t, ta, tb, na: (ta[t], 0, 0)
    eb = lambda t, ta, tb, na: (tb[t], 0, 0)
    rows = lambda t, ta, tb, na: (t, 0)
    up, down = (None, D, F), (None, F, D)
    return pl.pallas_call(
        _ffn_sorted_body,
        grid_spec=pltpu.PrefetchScalarGridSpec(
            num_scalar_prefetch=3, grid=(nt,),
            in_specs=[
                pl.BlockSpec((FFN_TM, XW), rows),
                pl.BlockSpec(up, ea), pl.BlockSpec(up, ea), pl.BlockSpec(down, ea),
                pl.BlockSpec(up, eb), pl.BlockSpec(up, eb), pl.BlockSpec(down, eb),
            ],
            out_specs=pl.BlockSpec((FFN_TM, D), rows)),
        out_shape=jax.ShapeDtypeStruct((nt * FFN_TM, D), F32),
        compiler_params=_cp(("arbitrary",)),
        name="moe_ffn",
    )(ta, tb, na, xs, w1, w3, w2, w1, w3, w2)


def _combine_body(pos_ref, x_ref, mod_ref, ys_hbm, g_ref, b_ref, xn_ref, buf, sem, *, tiles_per_seq):
    i = pl.program_id(0)
    nsteps = pl.num_programs(0)
    tc = x_ref.shape[0]

    def issue_group(step, slot, r0):
        group = buf.at[slot, pl.ds(r0, ROW_TILE)]
        for k in range(ROW_TILE):
            _row_copy(ys_hbm, pos_ref[step * tc + r0 + k], group, k, sem.at[slot]).start()

    def wait_tile(slot):
        pltpu.make_async_copy(ys_hbm.at[pl.ds(0, tc)], buf.at[slot], sem.at[slot]).wait()

    ahead = COMBINE_SLOTS - 1

    @pl.when(i == 0)
    def _():
        for t in range(ahead):
            def prime(g, carry, t=t):
                issue_group(t, t, pl.multiple_of(g * ROW_TILE, ROW_TILE))
                return carry
            lax.fori_loop(0, tc // ROW_TILE, prime, 0)

    slot = i % COMBINE_SLOTS
    wait_tile(slot)
    _, _, g1p = _split_mod(mod_ref, row=i // tiles_per_seq)
    gain, bias = g_ref[...], b_ref[...]
    nxt = jnp.minimum(i + ahead, nsteps - 1)
    nslot = (i + ahead) % COMBINE_SLOTS

    for c in range(tc // COMBINE_CHUNK):
        for g in range(COMBINE_CHUNK // ROW_TILE):
            issue_group(nxt, nslot, c * COMBINE_CHUNK + g * ROW_TILE)
        rows = pl.ds(c * COMBINE_CHUNK, COMBINE_CHUNK)
        z = ALPHA * x_ref[rows, :] + g1p * buf[slot, rows, :]
        xn_ref[rows, :] = _layer_norm(z, gain, bias)

    @pl.when(i == nsteps - 1)
    def _():
        for k in range(1, COMBINE_SLOTS):
            wait_tile((i + k) % COMBINE_SLOTS)


def _combine(pos, x, mod, ys, ln_g, ln_b, *, layer, seq):
    n = x.shape[0]
    tc = COMBINE_TC
    sub = 2 * layer + 1
    return pl.pallas_call(
        functools.partial(_combine_body, tiles_per_seq=seq // tc),
        grid_spec=pltpu.PrefetchScalarGridSpec(
            num_scalar_prefetch=1, grid=(n // tc,),
            in_specs=[
                pl.BlockSpec((tc, D), lambda i, pos: (i, 0)),
                _layer_spec((n // seq, 3 * D), sub),
                pl.BlockSpec(memory_space=pl.ANY),
                _layer_spec((1, D), sub), _layer_spec((1, D), sub),
            ],
            out_specs=pl.BlockSpec((tc, D), lambda i, pos: (i, 0)),
            scratch_shapes=[pltpu.VMEM((COMBINE_SLOTS, tc, D), F32),
                            pltpu.SemaphoreType.DMA((COMBINE_SLOTS,))]),
        out_shape=jax.ShapeDtypeStruct((n, D), F32),
        compiler_params=_cp(("arbitrary",)),
        name="moe_combine",
    )(pos, x, mod, ys, ln_g, ln_b)


def _moe_sparse(x, mod, wr, br, w1, w3, w2, ln_g, ln_b, *, layer, seq):
    n = x.shape[0]
    urow, meta, cnt = _route_sort(x, mod, wr, br, layer=layer, seq=seq)
    cls = meta[:, 2, :].reshape(n).astype(jnp.int32)
    rank = meta[:, 3, :].reshape(n).astype(jnp.int32)
    count = cnt[:NCLS, 0].astype(jnp.int32)
    ntile = (count + FFN_TM - 1) // FFN_TM
    tile_end = jnp.cumsum(ntile)
    row_start = (tile_end - ntile) * FFN_TM
    pos = rank + jnp.sum(jax.nn.one_hot(cls, NCLS, dtype=jnp.int32) * row_start[None, :], axis=1)
    nt = n // FFN_TM + NCLS
    total = tile_end[NCLS - 1]
    tr = jnp.minimum(jnp.arange(nt, dtype=jnp.int32), total - 1)
    tcls = jnp.sum((tr[:, None] >= tile_end[None, :]).astype(jnp.int32), axis=1)
    grp, pair = tcls // PAIRS, tcls % PAIRS
    lo = (pair >= 3).astype(jnp.int32) + (pair >= 5).astype(jnp.int32)
    hi = pair + 1 - lo * (2 * E_PER_G - 3 - lo) // 2
    ta = grp * E_PER_G + lo
    tb = grp * E_PER_G + hi
    t_all = jnp.arange(nt, dtype=jnp.int32)
    partial_last = ((t_all[:, None] == tile_end[None, :] - 1) & (count % FFN_TM != 0)[None, :]).any(axis=1)
    zflag = ((t_all >= total) | partial_last).astype(jnp.int32)
    xs = _dispatch(pos, zflag, urow)
    ys = _ffn_sorted(ta, tb, total[None], xs, w1, w3, w2)
    return _combine(pos, x, mod, ys, ln_g, ln_b, layer=layer, seq=seq)


def _moe_body(x_ref, mod_ref, wr_ref, br_ref, w1f_ref, w3f_ref, w2f_ref, g_ref, b_ref,
              xn_ref, w1_ref, w3_ref, w2_ref, ub, comb, acc, *, reps):
    e = pl.program_id(1)
    w1_ref[...] = w1f_ref[...].astype(BF16)
    w3_ref[...] = w3f_ref[...].astype(BF16)
    w2_ref[...] = w2f_ref[...].astype(BF16)

    @pl.when(e == 0)
    def _():
        shift, sc1p, _ = _split_mod(mod_ref, reps=reps)
        u = (x_ref[...] * sc1p + shift).astype(BF16)
        ub[...] = u
        eid, _, i1, i2, w1, w2 = _top2(_router_logits(u, wr_ref, br_ref))
        comb[...] = _rows_to_cols(jnp.where(eid == i1, w1, 0.0) + jnp.where(eid == i2, w2, 0.0))
        acc[...] = jnp.zeros(acc.shape, F32)

    u = ub[...]
    col = lax.broadcasted_iota(jnp.int32, comb.shape, 1)
    ce = jnp.sum(jnp.where(col == e, comb[...], 0.0), axis=-1, keepdims=True)
    h = jax.nn.silu(_dot(u, w1_ref[...])) * _dot(u, w3_ref[...]) * ce
    acc[...] += _dot(h.astype(BF16), w2_ref[...])

    @pl.when(e == N_EXP - 1)
    def _():
        _, _, g1p = _split_mod(mod_ref, reps=reps)
        xn_ref[...] = _layer_norm(ALPHA * x_ref[...] + g1p * acc[...], g_ref[...], b_ref[...])


def _moe_dense(x, mod, wr, br, w1, w3, w2, ln_g, ln_b, *, layer, reps):
    n = x.shape[0]
    sub = 2 * layer + 1
    expert = lambda i, e: (layer, e, 0, 0)
    return pl.pallas_call(
        functools.partial(_moe_body, reps=reps),
        grid=(1, N_EXP),
        in_specs=[
            pl.BlockSpec((n, D), lambda i, e: (0, 0)),
            _layer_spec((n // reps, 3 * D), sub),
            _layer_spec((RROWS, D), layer), _layer_spec((RROWS, RLANES), layer),
            pl.BlockSpec((None, None, D, F), expert),
            pl.BlockSpec((None, None, D, F), expert),
            pl.BlockSpec((None, None, F, D), expert),
            _layer_spec((1, D), sub), _layer_spec((1, D), sub),
        ],
        out_specs=[pl.BlockSpec((n, D), lambda i, e: (0, 0)),
                   pl.BlockSpec((None, D, F), lambda i, e: (e, 0, 0)),
                   pl.BlockSpec((None, D, F), lambda i, e: (e, 0, 0)),
                   pl.BlockSpec((None, F, D), lambda i, e: (e, 0, 0))],
        out_shape=[jax.ShapeDtypeStruct((n, D), F32),
                   jax.ShapeDtypeStruct((N_EXP, D, F), BF16),
                   jax.ShapeDtypeStruct((N_EXP, D, F), BF16),
                   jax.ShapeDtypeStruct((N_EXP, F, D), BF16)],
        scratch_shapes=[
            pltpu.VMEM((n, D), BF16),
            pltpu.VMEM((n, RLANES), F32),
            pltpu.VMEM((n, D), F32),
        ],
        compiler_params=_cp(("arbitrary", "arbitrary")),
        name="moe",
    )(x, mod, wr, br, w1, w3, w2, ln_g, ln_b)


def _kv_body(x_ref, w_ref, wvt_ref, k_ref, v_ref, k16_ref, vt16_ref):
    xb = x_ref[...].astype(BF16)
    kv = _dot(xb, w_ref[...])
    k_ref[...] = kv[:, :KVW]
    v_ref[...] = kv[:, KVW:]
    k16_ref[...] = kv[:, :KVW].astype(BF16)
    vt16_ref[...] = lax.dot_general(wvt_ref[...], xb, _NT, preferred_element_type=F32).astype(BF16)


def _kv(x, w, wvt, *, tm):
    n = x.shape[0]
    row = lambda i: (i, 0)
    return pl.pallas_call(
        _kv_body,
        grid=(n // tm,),
        in_specs=[pl.BlockSpec((tm, D), row), _const_spec((D, 2 * KVW)), _const_spec((KVW, D))],
        out_specs=[pl.BlockSpec((tm, KVW), row), pl.BlockSpec((tm, KVW), row),
                   pl.BlockSpec((tm, KVW), row), pl.BlockSpec((KVW, tm), lambda i: (0, i))],
        out_shape=[jax.ShapeDtypeStruct((n, KVW), F32), jax.ShapeDtypeStruct((n, KVW), F32),
                   jax.ShapeDtypeStruct((n, KVW), BF16), jax.ShapeDtypeStruct((KVW, n), BF16)],
        compiler_params=_cp(("arbitrary",)),
        name="kv_proj",
    )(x, w, wvt)


def _sink_attend(qh, kh, vh, bias, sink):
    s = lax.dot_general(qh, kh, (((1,), (1,)), ((), ())), preferred_element_type=F32) + bias
    m = jnp.maximum(jnp.max(s, axis=-1, keepdims=True), sink)
    p = jnp.exp(s - m)
    den = jnp.sum(p, axis=-1, keepdims=True) + jnp.exp(sink - m)
    return _dot(p.astype(BF16), vh) / den


def _attn_prompt_body(x_ref, mod_ref, wqt_ref, wo_ref, kc_ref, kp_ref, vtc_ref, vtp_ref,
                      bias_ref, sink_ref, g_ref, b_ref, xn_ref, *, tq):
    t = pl.program_id(1)
    x = x_ref[...]
    shift, sc1p, g1p = _split_mod(mod_ref, row=pl.program_id(0))
    u = (x * sc1p + shift).astype(BF16)
    qt = lax.dot_general(wqt_ref[...], u, _NT, preferred_element_type=F32)
    qt = (qt * (HD ** -0.5)).astype(BF16)
    kc = kc_ref[...]
    vtc = vtc_ref[...]
    krow = lax.broadcasted_iota(jnp.int32, (2 * WIN, GRP * WIN), 0)
    first = jnp.where((krow < WIN) & (t == 0), NEG, 0.0).astype(F32)
    nwin = tq // WIN
    kks, vvts = [], []
    for j in range(nwin):
        if j == 0:
            kks.append(jnp.concatenate([kp_ref[...], kc[0:WIN]], axis=0))
            vvts.append(jnp.concatenate([vtp_ref[...], vtc[:, 0:WIN]], axis=1))
        else:
            kks.append(kc[(j - 1) * WIN:(j + 1) * WIN])
            vvts.append(vtc[:, (j - 1) * WIN:(j + 1) * WIN])

    def scores(j, hk):
        qht = jnp.concatenate(
            [qt[(hk * GRP + g) * HD:(hk * GRP + g + 1) * HD, j * WIN:(j + 1) * WIN] for g in range(GRP)],
            axis=1)
        st = _dot(kks[j][:, hk * HD:(hk + 1) * HD], qht) + bias_ref[hk]
        return st + first if j == 0 else st

    items = [(j, hk) for j in range(nwin) for hk in range(HKV)]
    heads = [[] for _ in range(nwin)]
    st_next = scores(*items[0])
    for n, (j, hk) in enumerate(items):
        st = st_next
        if n + 1 < len(items):
            st_next = scores(*items[n + 1])
        sink = sink_ref[hk]
        m = jnp.maximum(jnp.max(st, axis=0, keepdims=True), sink)
        pt = jnp.exp(st - m)
        den = jnp.sum(pt, axis=0, keepdims=True) + jnp.exp(sink - m)
        ot = _dot(vvts[j][hk * HD:(hk + 1) * HD, :], pt.astype(BF16)) * (1.0 / den)
        for g in range(GRP):
            heads[j].append(ot[:, g * WIN:(g + 1) * WIN])
    blocks = [jnp.concatenate(h, axis=0) for h in heads]
    oallt = jnp.concatenate(blocks, axis=1) if len(blocks) > 1 else blocks[0]
    mix = lax.dot_general(oallt.astype(BF16), wo_ref[...], _TN, preferred_element_type=F32)
    xn_ref[...] = _layer_norm(ALPHA * x + g1p * mix, g_ref[...], b_ref[...])


def _attn_prompt(x, mod, wqt, wo, k16, vt16, bias, sink, ln_g, ln_b, *, layer, nb, seq, tq):
    nt = seq // tq
    wpt = tq // WIN
    n = nb * seq
    j = layer - N_A
    sub = 2 * layer

    def prev(b, t):
        return jnp.maximum((b * nt + t) * wpt - 1, 0)

    cur = lambda b, t: (b * nt + t, 0)
    return pl.pallas_call(
        functools.partial(_attn_prompt_body, tq=tq),
        grid=(nb, nt),
        in_specs=[
            pl.BlockSpec((tq, D), cur),
            _layer_spec((nb, 3 * D), sub),
            _layer_spec((HQ * HD, D), j), _layer_spec((HQ * HD, D), j),
            pl.BlockSpec((tq, KVW), cur), pl.BlockSpec((WIN, KVW), lambda b, t: (prev(b, t), 0)),
            pl.BlockSpec((KVW, tq), lambda b, t: (0, b * nt + t)),
            pl.BlockSpec((KVW, WIN), lambda b, t: (0, prev(b, t))),
            _const_spec((HKV, 2 * WIN, GRP * WIN)), _layer_spec((HKV, 1, GRP * WIN), j),
            _layer_spec((1, D), sub), _layer_spec((1, D), sub),
        ],
        out_specs=pl.BlockSpec((tq, D), cur),
        out_shape=jax.ShapeDtypeStruct((n, D), F32),
        compiler_params=_cp(("arbitrary", "arbitrary")),
        name="attn_prompt",
    )(x, mod, wqt, wo, k16, k16, vt16, vt16, bias, sink, ln_g, ln_b)


def _attn_sample_body(x_ref, mod_ref, wq_ref, wo_ref, kb_ref, vb_ref, kn_ref, vn_ref,
                      bias_ref, sink_ref, g_ref, b_ref, xn_ref, *, ns, steps):
    nr = ns * steps
    x = x_ref[...].reshape(nr, D)
    shift, sc1p, g1p = _split_mod(mod_ref, reps=steps)
    u = (x * sc1p + shift).astype(BF16)
    q = (_dot(u, wq_ref[...]) * (HD ** -0.5)).astype(BF16)
    nbuf = kb_ref.shape[1]
    k_all = jnp.concatenate([kb_ref[...].reshape(ns * nbuf, KVW), kn_ref[...].reshape(nr, KVW)],
                            axis=0).astype(BF16)
    v_all = jnp.concatenate([vb_ref[...].reshape(ns * nbuf, KVW), vn_ref[...].reshape(nr, KVW)],
                            axis=0).astype(BF16)
    heads = [None] * HQ
    for hk in range(HKV):
        kcols = slice(hk * HD, (hk + 1) * HD)
        qh = jnp.concatenate(
            [q[:, (hk * GRP + g) * HD:(hk * GRP + g + 1) * HD] for g in range(GRP)], axis=0)
        o = _sink_attend(qh, k_all[:, kcols], v_all[:, kcols], bias_ref[hk], sink_ref[hk])
        for g in range(GRP):
            heads[hk * GRP + g] = o[g * nr:(g + 1) * nr]
    o_all = jnp.concatenate(heads, axis=1)
    mix = _dot(o_all.astype(BF16), wo_ref[...])
    xn_ref[...] = _layer_norm(ALPHA * x + g1p * mix, g_ref[...], b_ref[...]).reshape(steps, ns, D)


def _attn_sample(x, mod, wq, wo, kbuf, vbuf, kn, vn, bias, sink, ln_g, ln_b, *, layer, ns):
    steps, nb, _ = x.shape
    nbuf = kbuf.shape[1]
    j = layer - N_A
    sub = 2 * layer
    seqs = lambda i: (0, i, 0)
    return pl.pallas_call(
        functools.partial(_attn_sample_body, ns=ns, steps=steps),
        grid=(nb // ns,),
        in_specs=[
            pl.BlockSpec((steps, ns, D), seqs),
            pl.BlockSpec((None, ns, 3 * D), lambda i: (sub, i, 0)),
            _layer_spec((D, HQ * HD), j), _layer_spec((HQ * HD, D), j),
            pl.BlockSpec((ns, nbuf, KVW), lambda i: (i, 0, 0)),
            pl.BlockSpec((ns, nbuf, KVW), lambda i: (i, 0, 0)),
            pl.BlockSpec((steps, ns, KVW), seqs), pl.BlockSpec((steps, ns, KVW), seqs),
            _const_spec(bias.shape), _layer_spec(sink.shape[1:], j),
            _layer_spec((1, D), sub), _layer_spec((1, D), sub),
        ],
        out_specs=pl.BlockSpec((steps, ns, D), seqs),
        out_shape=jax.ShapeDtypeStruct((steps, nb, D), F32),
        compiler_params=_cp(("arbitrary",)),
        name="attn_sample",
    )(x, mod, wq, wo, kbuf, vbuf, kn, vn, bias, sink, ln_g, ln_b)


def _t5_bucket(d):
    max_exact = N_BUCKETS // 2
    d = np.maximum(d, 0)
    log_ratio = (np.log(np.maximum(d, 1).astype(np.float32) / np.float32(max_exact))
                 / np.float32(math.log(WIN / max_exact)))
    large = max_exact + (log_ratio * (N_BUCKETS - max_exact)).astype(np.int32)
    large = np.minimum(large, N_BUCKETS - 1)
    return np.where(d < max_exact, d, large)


def _bias_table(d, ok, rel_bias):
    sel = np.where(ok, _t5_bucket(d), N_BUCKETS)
    onehot = (sel[..., None] == np.arange(N_BUCKETS + 1)).astype(np.float32)
    tab = jnp.concatenate([rel_bias.astype(F32), jnp.full((1, HQ), NEG, F32)], axis=0)
    return jnp.einsum('abd,dh->hab', jnp.asarray(onehot), tab, precision=lax.Precision.HIGHEST)


def _prompt_bias(rel_bias):
    d = np.arange(WIN)[None, :] + WIN - np.arange(2 * WIN)[:, None]
    b = _bias_table(d, (d >= 0) & (d < WIN), rel_bias).reshape(HKV, GRP, 2 * WIN, WIN)
    return jnp.transpose(b, (0, 2, 1, 3)).reshape(HKV, 2 * WIN, GRP * WIN)


def _sample_bias(rel_bias, ns, steps, nbuf):
    qs = np.tile(np.arange(ns), steps)
    qt = np.repeat(np.arange(steps), ns)
    ks = np.concatenate([np.repeat(np.arange(ns), nbuf), qs])
    kpos = np.concatenate([np.tile(np.arange(nbuf), ns), nbuf + qt])
    d = qt[:, None] + nbuf - kpos[None, :]
    ok = (d >= 0) & (d < WIN) & (qs[:, None] == ks[None, :])
    return _bias_table(d, ok, rel_bias).reshape(HKV, GRP * ns * steps, kpos.shape[0])


LRU_TM = 512
ATTN_TQ = 512
SAMPLE_NS = 8


def kernel(x_prompt, x_sample, state_rnn_h, state_rnn_conv, cache_win_k, cache_win_v, c_prompt, c_sample, ada_w, ada_b, ln_g, ln_b, lru_w_in, lru_conv_w, lru_conv_b, lru_w_a, lru_b_a, lru_w_x, lru_b_x, lru_lambda, lru_w_out, kv_w, attn_w_q, attn_sinks, attn_w_o, rel_bias, moe_w_group, moe_b_group, moe_w_router, moe_b_router, moe_w1, moe_w3, moe_w2):
    nbp, seq, _ = x_prompt.shape
    nbs, steps, _ = x_sample.shape
    nbuf = cache_win_k.shape[1]
    npt = nbp * seq
    nst = nbs * steps

    mod_p, mod_s = _adaln(c_prompt, c_sample, ada_w, ada_b)

    ln_g3 = ln_g.reshape(2 * DEPTH, 1, D)
    ln_b3 = ln_b.reshape(2 * DEPTH, 1, D)
    lw = (lru_w_in.astype(BF16), lru_conv_w, lru_conv_b[:, None], lru_w_a.astype(BF16),
          lru_w_x.astype(BF16), lru_b_a[:, None], lru_b_x[:, None], lru_lambda[:, None],
          lru_w_out.astype(BF16))
    zrow = lambda k: jnp.zeros((DEPTH, k, D), F32)
    wr = jnp.concatenate([jnp.swapaxes(moe_w_group, 1, 2), zrow(EROW - N_GROUPS),
                          jnp.swapaxes(moe_w_router.reshape(DEPTH, D, N_EXP), 1, 2),
                          zrow(RROWS - EROW - N_EXP)], axis=1).astype(BF16)
    zb = lambda k: jnp.zeros((DEPTH, k), F32)
    br = jnp.concatenate([moe_b_group, zb(EROW - N_GROUPS), moe_b_router.reshape(DEPTH, N_EXP),
                          zb(RROWS - EROW - N_EXP)], axis=1)
    br = jnp.broadcast_to(br[:, :, None], (DEPTH, RROWS, RLANES))
    wq = attn_w_q.astype(BF16)
    wqt = jnp.swapaxes(wq, 1, 2)
    wo = attn_w_o.astype(BF16)
    kvw = kv_w.astype(BF16)
    kvwt = kvw[:, KVW:].T
    bias_p = _prompt_bias(rel_bias)
    bias_s = _sample_bias(rel_bias, SAMPLE_NS, steps, nbuf)
    sink_p = jnp.repeat(attn_sinks.reshape(N_B, HKV, GRP), WIN, axis=2)[:, :, None, :]
    sink_s = jnp.repeat(attn_sinks.reshape(N_B, HKV, GRP), SAMPLE_NS * steps, axis=2)[..., None]
    kbuf = cache_win_k.reshape(nbs, nbuf, KVW)
    vbuf = cache_win_v.reshape(nbs, nbuf, KVW)

    xp = x_prompt.reshape(npt, D)
    xs = jnp.transpose(x_sample, (1, 0, 2)).reshape(nst, D)
    c0 = jnp.transpose(state_rnn_conv, (0, 2, 1, 3)).reshape(N_A, (CONV_W - 1) * nbs, R)
    hp, cp_, hs, cs = [], [], [], []
    kp = vp = kn = vn = kp16 = vpt16 = None
    for l in range(DEPTH):
        if l < N_A:
            xp, h_l, c_l = _lru_prompt(xp, mod_p, lw, ln_g3, ln_b3, layer=l, nb=nbp, seq=seq, tm=LRU_TM)
            hp.append(h_l[:, 0])
            cp_.append(c_l)
            xs, h_l, c_l = _lru_sample(xs, mod_s, state_rnn_h, c0[l], lw, ln_g3, ln_b3,
                                       layer=l, nb=nbs, steps=steps)
            hs.append(h_l)
            cs.append(jnp.transpose(c_l.reshape(CONV_W - 1, nbs, R), (1, 0, 2)))
        else:
            xp = _attn_prompt(xp, mod_p, wqt, wo, kp16, vpt16, bias_p, sink_p, ln_g3, ln_b3,
                              layer=l, nb=nbp, seq=seq, tq=ATTN_TQ)
            xs = _attn_sample(xs.reshape(steps, nbs, D), mod_s, wq, wo, kbuf, vbuf,
                              kn.reshape(steps, nbs, KVW), vn.reshape(steps, nbs, KVW),
                              bias_s, sink_s, ln_g3, ln_b3, layer=l, ns=SAMPLE_NS).reshape(nst, D)
        xs, w1, w3, w2 = _moe_dense(xs, mod_s, wr, br, moe_w1, moe_w3, moe_w2, ln_g3, ln_b3,
                                    layer=l, reps=steps)
        xp = _moe_sparse(xp, mod_p, wr, br, w1, w3, w2, ln_g3, ln_b3, layer=l, seq=seq)
        if l == N_A - 1:
            kp, vp, kp16, vpt16 = _kv(xp, kvw, kvwt, tm=1024)
            kn, vn, _, _ = _kv(xs, kvw, kvwt, tm=nst)

    lp = min(WIN, seq)
    k_p = kp.reshape(nbp, seq, HKV, HD)[:, seq - lp:]
    v_p = vp.reshape(nbp, seq, HKV, HD)[:, seq - lp:]
    kn4 = jnp.transpose(kn.reshape(steps, nbs, HKV, HD), (1, 0, 2, 3))
    vn4 = jnp.transpose(vn.reshape(steps, nbs, HKV, HD), (1, 0, 2, 3))
    k_s = jnp.concatenate([cache_win_k, kn4], axis=1)[:, -nbuf:]
    v_s = jnp.concatenate([cache_win_v, vn4], axis=1)[:, -nbuf:]
    y_s = jnp.transpose(xs.reshape(steps, nbs, D), (1, 0, 2))
    return (xp.reshape(nbp, seq, D), y_s,
            jnp.stack(hp), jnp.stack(cp_), k_p, v_p,
            jnp.stack(hs), jnp.stack(cs), k_s, v_s)
```

```python
import functools
import math

import jax
import jax.numpy as jnp
import numpy as np
from jax import lax
from jax.experimental import pallas as pl
from jax.experimental.pallas import tpu as pltpu

D = 1024
R = 1024
DEPTH = 4
N_A = 2
N_B = DEPTH - N_A
N_BLK = 4
BLK = R // N_BLK
CONV_W = 4
RG_C = 8.0
HQ = 16
HKV = 4
HD = 64
GRP = HQ // HKV
KVW = HKV * HD
WIN = 128
N_BUCKETS = 32
N_GROUPS = 4
E_PER_G = 4
N_EXP = 16
F = 512
ALPHA = (2.0 * DEPTH) ** 0.25
LN_EPS = 1e-5
NEG = -1e30
RLANES = 128
ROW_TILE = 8

F32 = jnp.float32
BF16 = jnp.bfloat16

VMEM_LIMIT = 56 * 1024 * 1024


def _cp(sem):
    return pltpu.CompilerParams(dimension_semantics=sem, vmem_limit_bytes=VMEM_LIMIT)


def _dot(a, b):
    return jnp.dot(a, b, preferred_element_type=F32)


def _layer_norm(z, g, b):
    mu = jnp.mean(z, axis=-1, keepdims=True)
    zc = z - mu
    var = jnp.mean(zc * zc, axis=-1, keepdims=True)
    return zc * lax.rsqrt(var + LN_EPS) * g + b


def _split_mod(mod_ref, row=None, reps=1):
    m = mod_ref[...] if row is None else mod_ref[pl.ds(row, 1), :]
    if reps > 1:
        m = jnp.concatenate([m] * reps, axis=0)
    return m[:, 0:D], m[:, D:2 * D], m[:, 2 * D:3 * D]


def _layer_spec(shape, layer):
    nd = len(shape)
    return pl.BlockSpec((None,) + tuple(shape), lambda *_: (layer,) + (0,) * nd)


def _adaln_body(cp_ref, cs_ref, w_ref, b_ref, op_ref, os_ref):
    one = jnp.where(pl.program_id(1) > 0, 1.0, 0.0).astype(F32)
    w = w_ref[...].astype(BF16)
    op_ref[...] = _dot(cp_ref[...].astype(BF16), w) + b_ref[...] + one
    os_ref[...] = _dot(cs_ref[...].astype(BF16), w) + b_ref[...] + one


def _adaln(c_p, c_s, ada_w, ada_b):
    np_, ns_ = c_p.shape[0], c_s.shape[0]
    w = ada_w.reshape(2 * DEPTH, D, 3 * D)
    b = ada_b.reshape(2 * DEPTH, 1, 3 * D)
    return pl.pallas_call(
        _adaln_body,
        grid=(2 * DEPTH, 3),
        in_specs=[
            pl.BlockSpec((np_, D), lambda l, j: (0, 0)),
            pl.BlockSpec((ns_, D), lambda l, j: (0, 0)),
            pl.BlockSpec((None, D, D), lambda l, j: (l, 0, j)),
            pl.BlockSpec((None, 1, D), lambda l, j: (l, 0, j)),
        ],
        out_specs=[pl.BlockSpec((None, np_, D), lambda l, j: (l, 0, j)),
                   pl.BlockSpec((None, ns_, D), lambda l, j: (l, 0, j))],
        out_shape=[jax.ShapeDtypeStruct((2 * DEPTH, np_, 3 * D), F32),
                   jax.ShapeDtypeStruct((2 * DEPTH, ns_, 3 * D), F32)],
        compiler_params=_cp(("arbitrary", "arbitrary")),
        name="adaln",
    )(c_p, c_s, w, b)


def _lru_gates(xc, n, wa_ref, wx_ref, ba_ref, bx_ref, lam_ref):
    cols = slice(n * BLK, (n + 1) * BLK)
    xb = xc.astype(BF16)
    r = jax.nn.sigmoid(_dot(xb, wa_ref[n]) + ba_ref[:, cols])
    i = jax.nn.sigmoid(_dot(xb, wx_ref[n]) + bx_ref[:, cols])
    z = -lam_ref[:, cols]
    softplus = jnp.maximum(z, 0.0) + jnp.log1p(jnp.exp(-jnp.abs(z)))
    log_a = (-RG_C) * r * softplus
    a = jnp.exp(log_a)
    v = 1.0 - a * a
    b = jnp.where(v > 0.0, v * lax.rsqrt(v), 0.0) * i * xc
    return a, b


def _lru_prompt_body(x_ref, mod_ref, win_ref, cw_ref, cb_ref, wa_ref, wx_ref, ba_ref, bx_ref,
                     lam_ref, wout_ref, g_ref, b_ref, mod1_ref, wr_ref, br_ref,
                     xn_ref, hl_ref, cl_ref, urow_ref, meta_ref, cnt_ref,
                     xbuf, abuf, hbuf, hc, carry, *, tm):
    t = pl.program_id(1)

    @pl.when(t == 0)
    def _():
        xbuf[0:ROW_TILE, :] = jnp.zeros((ROW_TILE, R), F32)
        hc[...] = jnp.zeros((1, R), F32)

    x = x_ref[...]
    shift, sc1p, g1p = _split_mod(mod_ref, row=pl.program_id(0))
    u = x * sc1p + shift
    yx = _dot(u.astype(BF16), win_ref[...])
    y = jax.nn.gelu(yx[:, :R])
    xbuf[ROW_TILE:ROW_TILE + tm, :] = yx[:, R:]

    ngrp = tm // ROW_TILE
    rowmod = lax.broadcasted_iota(jnp.int32, (ngrp, ROW_TILE, BLK), 1)
    for n in range(N_BLK):
        cols = slice(n * BLK, (n + 1) * BLK)
        xc = cb_ref[:, cols] + cw_ref[3:4, cols] * xbuf[ROW_TILE:ROW_TILE + tm, cols]
        for k in range(1, CONV_W):
            xc = xc + cw_ref[3 - k:4 - k, cols] * xbuf[ROW_TILE - k:ROW_TILE - k + tm, cols]
        a, b = _lru_gates(xc, n, wa_ref, wx_ref, ba_ref, bx_ref, lam_ref)
        a = a.reshape(ngrp, ROW_TILE, BLK)
        b = b.reshape(ngrp, ROW_TILE, BLK)
        s = 1
        while s < ROW_TILE:
            a_sh = pltpu.roll(a, s, 1)
            b_sh = pltpu.roll(b, s, 1)
            m = rowmod >= s
            b = jnp.where(m, a * b_sh, 0.0) + b
            a = jnp.where(m, a * a_sh, a)
            s *= 2
        abuf[:, cols] = a.reshape(tm, BLK)
        hbuf[:, cols] = b.reshape(tm, BLK)

    def group(g, h):
        r0 = pl.multiple_of(g * ROW_TILE, ROW_TILE)
        hg = abuf[pl.ds(r0, ROW_TILE), :] * h + hbuf[pl.ds(r0, ROW_TILE), :]
        hbuf[pl.ds(r0, ROW_TILE), :] = hg
        return hg[ROW_TILE - 1:ROW_TILE, :]

    h_end = lax.fori_loop(0, tm // ROW_TILE, group, hc[...])
    hc[...] = h_end
    hl_ref[...] = h_end
    tail = xbuf[tm + ROW_TILE - (CONV_W - 1):tm + ROW_TILE, :]
    cl_ref[...] = tail
    xbuf[ROW_TILE - (CONV_W - 1):ROW_TILE, :] = tail

    mix = _dot((y * hbuf[...]).astype(BF16), wout_ref[...])
    xn = _layer_norm(ALPHA * x + g1p * mix, g_ref[...], b_ref[...])
    xn_ref[...] = xn
    b = pl.program_id(0)
    _route_rows(xn, b, (b == 0) & (t == 0), mod1_ref, wr_ref, br_ref, urow_ref, meta_ref, cnt_ref, carry)


def _const_spec(shape):
    nd = len(shape)
    return pl.BlockSpec(shape, lambda *_: (0,) * nd)


def _lru_weight_specs(layer):
    sub = 2 * layer
    return [
        _layer_spec((D, 2 * R), layer), _layer_spec((CONV_W, R), layer), _layer_spec((1, R), layer),
        _layer_spec((N_BLK, BLK, BLK), layer), _layer_spec((N_BLK, BLK, BLK), layer),
        _layer_spec((1, R), layer), _layer_spec((1, R), layer), _layer_spec((1, R), layer),
        _layer_spec((R, D), layer), _layer_spec((1, D), sub), _layer_spec((1, D), sub),
    ]


def _lru_prompt(x, mod, lw, ln_g, ln_b, wr, br, *, layer, nb, seq, tm):
    nt = seq // tm
    n = nb * seq
    r_in, r_out, r_shape, r_scratch = _route_specs(layer, nb, nt, tm)
    return pl.pallas_call(
        functools.partial(_lru_prompt_body, tm=tm),
        grid=(nb, nt),
        in_specs=[
            pl.BlockSpec((tm, D), lambda b, t: (b * nt + t, 0)),
            _layer_spec((nb, 3 * D), 2 * layer),
        ] + _lru_weight_specs(layer) + r_in,
        out_specs=[
            pl.BlockSpec((tm, D), lambda b, t: (b * nt + t, 0)),
            pl.BlockSpec((None, 1, R), lambda b, t: (b, 0, 0)),
            pl.BlockSpec((None, CONV_W - 1, R), lambda b, t: (b, 0, 0)),
        ] + r_out,
        out_shape=[
            jax.ShapeDtypeStruct((n, D), F32),
            jax.ShapeDtypeStruct((nb, 1, R), F32),
            jax.ShapeDtypeStruct((nb, CONV_W - 1, R), F32),
        ] + r_shape,
        scratch_shapes=[
            pltpu.VMEM((tm + ROW_TILE, R), F32),
            pltpu.VMEM((tm, R), F32),
            pltpu.VMEM((tm, R), F32),
            pltpu.VMEM((1, R), F32),
        ] + r_scratch,
        compiler_params=_cp(("arbitrary", "arbitrary")),
        name="lru_prompt",
    )(x, mod, *lw, ln_g, ln_b, mod, wr, br)


def _lru_sample_body(x_ref, mod_ref, h0_ref, c0_ref, win_ref, cw_ref, cb_ref, wa_ref, wx_ref,
                     ba_ref, bx_ref, lam_ref, wout_ref, g_ref, b_ref,
                     xn_ref, hl_ref, cl_ref, hbuf, *, nb, steps):
    x = x_ref[...]
    shift, sc1p, g1p = _split_mod(mod_ref, reps=steps)
    u = x * sc1p + shift
    yx = _dot(u.astype(BF16), win_ref[...])
    y = jax.nn.gelu(yx[:, :R])
    xext = jnp.concatenate([c0_ref[...], yx[:, R:]], axis=0)
    cl_ref[...] = xext[steps * nb:, :]
    for n in range(N_BLK):
        cols = slice(n * BLK, (n + 1) * BLK)
        xc = cb_ref[:, cols] + cw_ref[0:1, cols] * xext[0:steps * nb, cols]
        for k in range(1, CONV_W):
            xc = xc + cw_ref[k:k + 1, cols] * xext[k * nb:(k + steps) * nb, cols]
        a, b = _lru_gates(xc, n, wa_ref, wx_ref, ba_ref, bx_ref, lam_ref)
        h = h0_ref[:, cols]
        for t in range(steps):
            h = a[t * nb:(t + 1) * nb, :] * h + b[t * nb:(t + 1) * nb, :]
            hbuf[t * nb:(t + 1) * nb, cols] = h
        hl_ref[:, cols] = h
    mix = _dot((y * hbuf[...]).astype(BF16), wout_ref[...])
    xn_ref[...] = _layer_norm(ALPHA * x + g1p * mix, g_ref[...], b_ref[...])


def _lru_sample(x, mod, h0, c0, lw, ln_g, ln_b, *, layer, nb, steps):
    n = nb * steps
    return pl.pallas_call(
        functools.partial(_lru_sample_body, nb=nb, steps=steps),
        grid=(1,),
        in_specs=[
            _const_spec((n, D)), _layer_spec((nb, 3 * D), 2 * layer), _layer_spec((nb, R), layer),
            _const_spec(((CONV_W - 1) * nb, R)),
        ] + _lru_weight_specs(layer),
        out_specs=[_const_spec((n, D)), _const_spec((nb, R)), _const_spec(((CONV_W - 1) * nb, R))],
        out_shape=[
            jax.ShapeDtypeStruct((n, D), F32),
            jax.ShapeDtypeStruct((nb, R), F32),
            jax.ShapeDtypeStruct(((CONV_W - 1) * nb, R), F32),
        ],
        scratch_shapes=[pltpu.VMEM((n, R), F32)],
        compiler_params=_cp(("arbitrary",)),
        name="lru_sample",
    )(x, mod, h0, c0, *lw, ln_g, ln_b)


RROWS = 32
EROW = 8
_NT = (((1,), (1,)), ((), ()))
_TN = (((0,), (0,)), ((), ()))


def _top2(lt):
    nt = lt.shape[1]
    gl = lt[0:EROW, :]
    gid = lax.broadcasted_iota(jnp.int32, (EROW, nt), 0).astype(F32)
    is_g = gid < N_GROUPS
    gl = jnp.where(is_g, gl, NEG)
    gmax = jnp.max(gl, axis=0, keepdims=True)
    gidx = jnp.min(jnp.where(gl == gmax, gid, float(EROW)), axis=0, keepdims=True)
    gsum = jnp.sum(jnp.where(is_g, jnp.exp(gl - gmax), 0.0), axis=0, keepdims=True)
    g_p = 1.0 / gsum
    el = lt[EROW:EROW + N_EXP, :]
    eid = lax.broadcasted_iota(jnp.int32, (N_EXP, nt), 0).astype(F32)
    first = gidx * E_PER_G
    in_grp = (eid >= first) & (eid < first + E_PER_G)
    el = jnp.where(in_grp, el, NEG)
    big = float(N_EXP)
    t1 = jnp.max(el, axis=0, keepdims=True)
    i1 = jnp.min(jnp.where(el == t1, eid, big), axis=0, keepdims=True)
    el2 = jnp.where(eid == i1, NEG, el)
    t2 = jnp.max(el2, axis=0, keepdims=True)
    i2 = jnp.min(jnp.where((el2 == t2) & in_grp & (eid != i1), eid, big), axis=0, keepdims=True)
    e21 = jnp.exp(t2 - t1)
    w1 = g_p / (1.0 + e21)
    w2 = w1 * e21
    return eid, gidx, i1, i2, w1, w2


def _router_logits(u16, wrt_ref, brt_ref):
    return lax.dot_general(wrt_ref[...], u16, _NT, preferred_element_type=F32) + brt_ref[:, 0:1]


def _rows_to_cols(rows):
    r, nt = rows.shape
    padded = jnp.concatenate([rows, jnp.zeros((RLANES - r, nt), F32)], axis=0)
    return jnp.transpose(padded)


PAIRS = E_PER_G * (E_PER_G - 1) // 2
NCLS = N_GROUPS * PAIRS
XW = D + RLANES
FFN_TM = 256
DISPATCH_TD = 2048
COMBINE_TC = 256
COMBINE_CHUNK = 64
COMBINE_SLOTS = 3


def _route_rows(x, row, first, mod_ref, wr_ref, br_ref, urow_ref, meta_ref, cnt_ref, carry):
    @pl.when(first)
    def _():
        carry[...] = jnp.zeros(carry.shape, F32)

    tm = x.shape[0]
    shift, sc1p, _ = _split_mod(mod_ref, row=row)
    u = x * sc1p + shift
    _, gidx, i1, i2, w1, w2 = _top2(_router_logits(u.astype(BF16), wr_ref, br_ref))
    first = gidx * E_PER_G
    lo = jnp.minimum(i1, i2) - first
    hi = jnp.maximum(i1, i2) - first
    w_lo = jnp.where(i1 < i2, w1, w2)
    w_hi = jnp.where(i1 < i2, w2, w1)
    cls = gidx * PAIRS + lo * (2 * E_PER_G - 1 - lo) * 0.5 + (hi - lo - 1.0)
    cid = lax.broadcasted_iota(jnp.int32, (RROWS, tm), 0).astype(F32)
    onehot = (cid == cls).astype(F32)
    r = lax.broadcasted_iota(jnp.int32, (tm, tm), 0)
    c = lax.broadcasted_iota(jnp.int32, (tm, tm), 1)
    earlier = _dot(onehot.astype(BF16), (r < c).astype(BF16)) + carry[:, 0:1]
    rank = jnp.sum(onehot * earlier, axis=0, keepdims=True)
    carry[...] += jnp.sum(onehot, axis=1, keepdims=True)
    cnt_ref[...] = carry[...]
    meta = jnp.concatenate([w_lo, w_hi, cls, rank, jnp.zeros((ROW_TILE - 4, tm), F32)], axis=0)
    meta_ref[...] = meta
    urow_ref[:, 0:D] = u
    urow_ref[:, D:XW] = _rows_to_cols(meta)


def _route_specs(layer, nb, nt, tm):
    tile = lambda b, t: b * nt + t
    n = nb * nt * tm
    in_specs = [_layer_spec((nb, 3 * D), 2 * layer + 1),
                _layer_spec((RROWS, D), layer), _layer_spec((RROWS, RLANES), layer)]
    out_specs = [pl.BlockSpec((tm, XW), lambda b, t: (tile(b, t), 0)),
                 pl.BlockSpec((None, ROW_TILE, tm), lambda b, t: (tile(b, t), 0, 0)),
                 _const_spec((RROWS, RLANES))]
    out_shape = [jax.ShapeDtypeStruct((n, XW), F32),
                 jax.ShapeDtypeStruct((nb * nt, ROW_TILE, tm), F32),
                 jax.ShapeDtypeStruct((RROWS, RLANES), F32)]
    return in_specs, out_specs, out_shape, [pltpu.VMEM((RROWS, RLANES), F32)]


def _row_copy(src, src_row, dst, dst_row, sem):
    return pltpu.make_async_copy(src.at[pl.ds(src_row, 1)], dst.at[pl.ds(dst_row, 1)], sem)


def _dispatch_body(pos_ref, zflag_ref, urow_ref, xs_hbm, zbuf, sem, zsem):
    base = pl.program_id(0) * DISPATCH_TD

    @pl.when(pl.program_id(0) == 0)
    def _():
        zbuf[...] = jnp.zeros(zbuf.shape, F32)

        def zero_copy(t):
            return pltpu.make_async_copy(zbuf, xs_hbm.at[pl.ds(t * FFN_TM, FFN_TM)], zsem)

        def start(t, carry):
            @pl.when(zflag_ref[t] != 0)
            def _():
                zero_copy(t).start()
            return carry

        def wait(t, carry):
            @pl.when(zflag_ref[t] != 0)
            def _():
                zero_copy(t).wait()
            return carry

        lax.fori_loop(0, zflag_ref.shape[0], start, 0)
        lax.fori_loop(0, zflag_ref.shape[0], wait, 0)

    def issue(g, carry):
        r0 = pl.multiple_of(g * ROW_TILE, ROW_TILE)
        for k in range(ROW_TILE):
            _row_copy(urow_ref.at[pl.ds(r0, ROW_TILE)], k, xs_hbm, pos_ref[base + r0 + k], sem).start()
        return carry

    lax.fori_loop(0, DISPATCH_TD // ROW_TILE, issue, 0)
    pltpu.make_async_copy(urow_ref, xs_hbm.at[pl.ds(0, DISPATCH_TD)], sem).wait()


def _dispatch(pos, zflag, urow):
    n = urow.shape[0]
    nt = zflag.shape[0]
    return pl.pallas_call(
        _dispatch_body,
        grid_spec=pltpu.PrefetchScalarGridSpec(
            num_scalar_prefetch=2, grid=(n // DISPATCH_TD,),
            in_specs=[pl.BlockSpec((DISPATCH_TD, XW), lambda i, pos, zflag: (i, 0))],
            out_specs=pl.BlockSpec(memory_space=pl.ANY),
            scratch_shapes=[pltpu.VMEM((FFN_TM, XW), F32), pltpu.SemaphoreType.DMA(()),
                            pltpu.SemaphoreType.DMA(())]),
        out_shape=jax.ShapeDtypeStruct((nt * FFN_TM, XW), F32),
        compiler_params=_cp(("arbitrary",)),
        name="moe_dispatch",
    )(pos, zflag, urow)


def _ffn_sorted_body(ta_ref, tb_ref, na_ref, xs_ref, w1a, w3a, w2a, w1b, w3b, w2b, ys_ref):
    del ta_ref, tb_ref
    active = pl.program_id(0) < na_ref[0]

    @pl.when(active)
    def _():
        x = xs_ref[:, 0:D].astype(BF16)
        ha = jax.nn.silu(_dot(x, w1a[...])) * _dot(x, w3a[...]) * xs_ref[:, D:D + 1]
        hb = jax.nn.silu(_dot(x, w1b[...])) * _dot(x, w3b[...]) * xs_ref[:, D + 1:D + 2]
        ys_ref[...] = _dot(ha.astype(BF16), w2a[...]) + _dot(hb.astype(BF16), w2b[...])

    @pl.when(jnp.logical_not(active))
    def _():
        ys_ref[...] = jnp.zeros(ys_ref.shape, F32)


def _ffn_sorted(ta, tb, na, xs, w1, w3, w2):
    nt = ta.shape[0]
    ea = lambda t, ta, tb, na: (ta[t], 0, 0)
    eb = lambda t, ta, tb, na: (tb[t], 0, 0)
    rows = lambda t, ta, tb, na: (t, 0)
    up, down = (None, D, F), (None, F, D)
    return pl.pallas_call(
        _ffn_sorted_body,
        grid_spec=pltpu.PrefetchScalarGridSpec(
            num_scalar_prefetch=3, grid=(nt,),
            in_specs=[
                pl.BlockSpec((FFN_TM, XW), rows),
                pl.BlockSpec(up, ea), pl.BlockSpec(up, ea), pl.BlockSpec(down, ea),
                pl.BlockSpec(up, eb), pl.BlockSpec(up, eb), pl.BlockSpec(down, eb),
            ],
            out_specs=pl.BlockSpec((FFN_TM, D), rows)),
        out_shape=jax.ShapeDtypeStruct((nt * FFN_TM, D), F32),
        compiler_params=_cp(("arbitrary",)),
        name="moe_ffn",
    )(ta, tb, na, xs, w1, w3, w2, w1, w3, w2)


def _combine_body(pos_ref, x_ref, mod_ref, ys_hbm, g_ref, b_ref, xn_ref, buf, sem, *, tiles_per_seq):
    i = pl.program_id(0)
    nsteps = pl.num_programs(0)
    tc = x_ref.shape[0]

    def issue_group(step, slot, r0):
        group = buf.at[slot, pl.ds(r0, ROW_TILE)]
        for k in range(ROW_TILE):
            _row_copy(ys_hbm, pos_ref[step * tc + r0 + k], group, k, sem.at[slot]).start()

    def wait_tile(slot):
        pltpu.make_async_copy(ys_hbm.at[pl.ds(0, tc)], buf.at[slot], sem.at[slot]).wait()

    ahead = COMBINE_SLOTS - 1

    @pl.when(i == 0)
    def _():
        for t in range(ahead):
            def prime(g, carry, t=t):
                issue_group(t, t, pl.multiple_of(g * ROW_TILE, ROW_TILE))
                return carry
            lax.fori_loop(0, tc // ROW_TILE, prime, 0)

    slot = i % COMBINE_SLOTS
    wait_tile(slot)
    _, _, g1p = _split_mod(mod_ref, row=i // tiles_per_seq)
    gain, bias = g_ref[...], b_ref[...]
    nxt = jnp.minimum(i + ahead, nsteps - 1)
    nslot = (i + ahead) % COMBINE_SLOTS

    for c in range(tc // COMBINE_CHUNK):
        for g in range(COMBINE_CHUNK // ROW_TILE):
            issue_group(nxt, nslot, c * COMBINE_CHUNK + g * ROW_TILE)
        rows = pl.ds(c * COMBINE_CHUNK, COMBINE_CHUNK)
        z = ALPHA * x_ref[rows, :] + g1p * buf[slot, rows, :]
        xn_ref[rows, :] = _layer_norm(z, gain, bias)

    @pl.when(i == nsteps - 1)
    def _():
        for k in range(1, COMBINE_SLOTS):
            wait_tile((i + k) % COMBINE_SLOTS)


def _combine(pos, x, mod, ys, ln_g, ln_b, *, layer, seq):
    n = x.shape[0]
    tc = COMBINE_TC
    sub = 2 * layer + 1
    return pl.pallas_call(
        functools.partial(_combine_body, tiles_per_seq=seq // tc),
        grid_spec=pltpu.PrefetchScalarGridSpec(
            num_scalar_prefetch=1, grid=(n // tc,),
            in_specs=[
                pl.BlockSpec((tc, D), lambda i, pos: (i, 0)),
                _layer_spec((n // seq, 3 * D), sub),
                pl.BlockSpec(memory_space=pl.ANY),
                _layer_spec((1, D), sub), _layer_spec((1, D), sub),
            ],
            out_specs=pl.BlockSpec((tc, D), lambda i, pos: (i, 0)),
            scratch_shapes=[pltpu.VMEM((COMBINE_SLOTS, tc, D), F32),
                            pltpu.SemaphoreType.DMA((COMBINE_SLOTS,))]),
        out_shape=jax.ShapeDtypeStruct((n, D), F32),
        compiler_params=_cp(("arbitrary",)),
        name="moe_combine",
    )(pos, x, mod, ys, ln_g, ln_b)


def _moe_sparse(x, route, mod, w1, w3, w2, ln_g, ln_b, *, layer, seq):
    n = x.shape[0]
    urow, meta, cnt = route
    cls = meta[:, 2, :].reshape(n).astype(jnp.int32)
    rank = meta[:, 3, :].reshape(n).astype(jnp.int32)
    count = cnt[:NCLS, 0].astype(jnp.int32)
    ntile = (count + FFN_TM - 1) // FFN_TM
    tile_end = jnp.cumsum(ntile)
    row_start = (tile_end - ntile) * FFN_TM
    pos = rank + jnp.sum(jax.nn.one_hot(cls, NCLS, dtype=jnp.int32) * row_start[None, :], axis=1)
    nt = n // FFN_TM + NCLS
    total = tile_end[NCLS - 1]
    tr = jnp.minimum(jnp.arange(nt, dtype=jnp.int32), total - 1)
    tcls = jnp.sum((tr[:, None] >= tile_end[None, :]).astype(jnp.int32), axis=1)
    grp, pair = tcls // PAIRS, tcls % PAIRS
    lo = (pair >= 3).astype(jnp.int32) + (pair >= 5).astype(jnp.int32)
    hi = pair + 1 - lo * (2 * E_PER_G - 3 - lo) // 2
    ta = grp * E_PER_G + lo
    tb = grp * E_PER_G + hi
    t_all = jnp.arange(nt, dtype=jnp.int32)
    partial_last = ((t_all[:, None] == tile_end[None, :] - 1) & (count % FFN_TM != 0)[None, :]).any(axis=1)
    zflag = ((t_all >= total) | partial_last).astype(jnp.int32)
    xs = _dispatch(pos, zflag, urow)
    ys = _ffn_sorted(ta, tb, total[None], xs, w1, w3, w2)
    return _combine(pos, x, mod, ys, ln_g, ln_b, layer=layer, seq=seq)


def _moe_body(x_ref, mod_ref, wr_ref, br_ref, w1f_ref, w3f_ref, w2f_ref, g_ref, b_ref,
              xn_ref, w1_ref, w3_ref, w2_ref, ub, comb, acc, *, reps):
    e = pl.program_id(1)
    w1_ref[...] = w1f_ref[...].astype(BF16)
    w3_ref[...] = w3f_ref[...].astype(BF16)
    w2_ref[...] = w2f_ref[...].astype(BF16)

    @pl.when(e == 0)
    def _():
        shift, sc1p, _ = _split_mod(mod_ref, reps=reps)
        u = (x_ref[...] * sc1p + shift).astype(BF16)
        ub[...] = u
        eid, _, i1, i2, w1, w2 = _top2(_router_logits(u, wr_ref, br_ref))
        comb[...] = _rows_to_cols(jnp.where(eid == i1, w1, 0.0) + jnp.where(eid == i2, w2, 0.0))
        acc[...] = jnp.zeros(acc.shape, F32)

    u = ub[...]
    col = lax.broadcasted_iota(jnp.int32, comb.shape, 1)
    ce = jnp.sum(jnp.where(col == e, comb[...], 0.0), axis=-1, keepdims=True)
    h = jax.nn.silu(_dot(u, w1_ref[...])) * _dot(u, w3_ref[...]) * ce
    acc[...] += _dot(h.astype(BF16), w2_ref[...])

    @pl.when(e == N_EXP - 1)
    def _():
        _, _, g1p = _split_mod(mod_ref, reps=reps)
        xn_ref[...] = _layer_norm(ALPHA * x_ref[...] + g1p * acc[...], g_ref[...], b_ref[...])


def _moe_dense(x, mod, wr, br, w1, w3, w2, ln_g, ln_b, *, layer, reps):
    n = x.shape[0]
    sub = 2 * layer + 1
    expert = lambda i, e: (layer, e, 0, 0)
    return pl.pallas_call(
        functools.partial(_moe_body, reps=reps),
        grid=(1, N_EXP),
        in_specs=[
            pl.BlockSpec((n, D), lambda i, e: (0, 0)),
            _layer_spec((n // reps, 3 * D), sub),
            _layer_spec((RROWS, D), layer), _layer_spec((RROWS, RLANES), layer),
            pl.BlockSpec((None, None, D, F), expert),
            pl.BlockSpec((None, None, D, F), expert),
            pl.BlockSpec((None, None, F, D), expert),
            _layer_spec((1, D), sub), _layer_spec((1, D), sub),
        ],
        out_specs=[pl.BlockSpec((n, D), lambda i, e: (0, 0)),
                   pl.BlockSpec((None, D, F), lambda i, e: (e, 0, 0)),
                   pl.BlockSpec((None, D, F), lambda i, e: (e, 0, 0)),
                   pl.BlockSpec((None, F, D), lambda i, e: (e, 0, 0))],
        out_shape=[jax.ShapeDtypeStruct((n, D), F32),
                   jax.ShapeDtypeStruct((N_EXP, D, F), BF16),
                   jax.ShapeDtypeStruct((N_EXP, D, F), BF16),
                   jax.ShapeDtypeStruct((N_EXP, F, D), BF16)],
        scratch_shapes=[
            pltpu.VMEM((n, D), BF16),
            pltpu.VMEM((n, RLANES), F32),
            pltpu.VMEM((n, D), F32),
        ],
        compiler_params=_cp(("arbitrary", "arbitrary")),
        name="moe",
    )(x, mod, wr, br, w1, w3, w2, ln_g, ln_b)


def _kv_body(x_ref, w_ref, wvt_ref, k_ref, v_ref, k16_ref, vt16_ref):
    xb = x_ref[...].astype(BF16)
    kv = _dot(xb, w_ref[...])
    k_ref[...] = kv[:, :KVW]
    v_ref[...] = kv[:, KVW:]
    k16_ref[...] = kv[:, :KVW].astype(BF16)
    vt16_ref[...] = lax.dot_general(wvt_ref[...], xb, _NT, preferred_element_type=F32).astype(BF16)


def _kv(x, w, wvt, *, tm):
    n = x.shape[0]
    row = lambda i: (i, 0)
    return pl.pallas_call(
        _kv_body,
        grid=(n // tm,),
        in_specs=[pl.BlockSpec((tm, D), row), _const_spec((D, 2 * KVW)), _const_spec((KVW, D))],
        out_specs=[pl.BlockSpec((tm, KVW), row), pl.BlockSpec((tm, KVW), row),
                   pl.BlockSpec((tm, KVW), row), pl.BlockSpec((KVW, tm), lambda i: (0, i))],
        out_shape=[jax.ShapeDtypeStruct((n, KVW), F32), jax.ShapeDtypeStruct((n, KVW), F32),
                   jax.ShapeDtypeStruct((n, KVW), BF16), jax.ShapeDtypeStruct((KVW, n), BF16)],
        compiler_params=_cp(("arbitrary",)),
        name="kv_proj",
    )(x, w, wvt)


def _sink_attend(qh, kh, vh, bias, sink):
    s = lax.dot_general(qh, kh, (((1,), (1,)), ((), ())), preferred_element_type=F32) + bias
    m = jnp.maximum(jnp.max(s, axis=-1, keepdims=True), sink)
    p = jnp.exp(s - m)
    den = jnp.sum(p, axis=-1, keepdims=True) + jnp.exp(sink - m)
    return _dot(p.astype(BF16), vh) / den


def _attn_prompt_body(x_ref, mod_ref, wqt_ref, wo_ref, kc_ref, kp_ref, vtc_ref, vtp_ref,
                      bias_ref, sink_ref, g_ref, b_ref, mod1_ref, wr_ref, br_ref,
                      xn_ref, urow_ref, meta_ref, cnt_ref, carry, *, tq):
    t = pl.program_id(1)
    x = x_ref[...]
    shift, sc1p, g1p = _split_mod(mod_ref, row=pl.program_id(0))
    u = (x * sc1p + shift).astype(BF16)
    qt = lax.dot_general(wqt_ref[...], u, _NT, preferred_element_type=F32)
    qt = (qt * (HD ** -0.5)).astype(BF16)
    kc = kc_ref[...]
    vtc = vtc_ref[...]
    krow = lax.broadcasted_iota(jnp.int32, (2 * WIN, GRP * WIN), 0)
    first = jnp.where((krow < WIN) & (t == 0), NEG, 0.0).astype(F32)
    nwin = tq // WIN
    kks, vvts = [], []
    for j in range(nwin):
        if j == 0:
            kks.append(jnp.concatenate([kp_ref[...], kc[0:WIN]], axis=0))
            vvts.append(jnp.concatenate([vtp_ref[...], vtc[:, 0:WIN]], axis=1))
        else:
            kks.append(kc[(j - 1) * WIN:(j + 1) * WIN])
            vvts.append(vtc[:, (j - 1) * WIN:(j + 1) * WIN])

    def scores(j, hk):
        qht = jnp.concatenate(
            [qt[(hk * GRP + g) * HD:(hk * GRP + g + 1) * HD, j * WIN:(j + 1) * WIN] for g in range(GRP)],
            axis=1)
        st = _dot(kks[j][:, hk * HD:(hk + 1) * HD], qht) + bias_ref[hk]
        return st + first if j == 0 else st

    items = [(j, hk) for j in range(nwin) for hk in range(HKV)]
    heads = [[] for _ in range(nwin)]
    st_next = scores(*items[0])
    for n, (j, hk) in enumerate(items):
        st = st_next
        if n + 1 < len(items):
            st_next = scores(*items[n + 1])
        sink = sink_ref[hk]
        m = jnp.maximum(jnp.max(st, axis=0, keepdims=True), sink)
        pt = jnp.exp(st - m)
        den = jnp.sum(pt, axis=0, keepdims=True) + jnp.exp(sink - m)
        ot = _dot(vvts[j][hk * HD:(hk + 1) * HD, :], pt.astype(BF16)) * (1.0 / den)
        for g in range(GRP):
            heads[j].append(ot[:, g * WIN:(g + 1) * WIN])
    blocks = [jnp.concatenate(h, axis=0) for h in heads]
    oallt = jnp.concatenate(blocks, axis=1) if len(blocks) > 1 else blocks[0]
    mix = lax.dot_general(oallt.astype(BF16), wo_ref[...], _TN, preferred_element_type=F32)
    xn = _layer_norm(ALPHA * x + g1p * mix, g_ref[...], b_ref[...])
    xn_ref[...] = xn
    b = pl.program_id(0)
    _route_rows(xn, b, (b == 0) & (t == 0), mod1_ref, wr_ref, br_ref, urow_ref, meta_ref, cnt_ref, carry)


def _attn_prompt(x, mod, wqt, wo, k16, vt16, bias, sink, ln_g, ln_b, wr, br, *, layer, nb, seq, tq):
    nt = seq // tq
    r_in, r_out, r_shape, r_scratch = _route_specs(layer, nb, nt, tq)
    wpt = tq // WIN
    n = nb * seq
    j = layer - N_A
    sub = 2 * layer

    def prev(b, t):
        return jnp.maximum((b * nt + t) * wpt - 1, 0)

    cur = lambda b, t: (b * nt + t, 0)
    return pl.pallas_call(
        functools.partial(_attn_prompt_body, tq=tq),
        grid=(nb, nt),
        in_specs=[
            pl.BlockSpec((tq, D), cur),
            _layer_spec((nb, 3 * D), sub),
            _layer_spec((HQ * HD, D), j), _layer_spec((HQ * HD, D), j),
            pl.BlockSpec((tq, KVW), cur), pl.BlockSpec((WIN, KVW), lambda b, t: (prev(b, t), 0)),
            pl.BlockSpec((KVW, tq), lambda b, t: (0, b * nt + t)),
            pl.BlockSpec((KVW, WIN), lambda b, t: (0, prev(b, t))),
            _const_spec((HKV, 2 * WIN, GRP * WIN)), _layer_spec((HKV, 1, GRP * WIN), j),
            _layer_spec((1, D), sub), _layer_spec((1, D), sub),
        ] + r_in,
        out_specs=[pl.BlockSpec((tq, D), cur)] + r_out,
        out_shape=[jax.ShapeDtypeStruct((n, D), F32)] + r_shape,
        scratch_shapes=r_scratch,
        compiler_params=_cp(("arbitrary", "arbitrary")),
        name="attn_prompt",
    )(x, mod, wqt, wo, k16, k16, vt16, vt16, bias, sink, ln_g, ln_b, mod, wr, br)


def _attn_sample_body(x_ref, mod_ref, wq_ref, wo_ref, kb_ref, vb_ref, kn_ref, vn_ref,
                      bias_ref, sink_ref, g_ref, b_ref, xn_ref, *, ns, steps):
    nr = ns * steps
    x = x_ref[...].reshape(nr, D)
    shift, sc1p, g1p = _split_mod(mod_ref, reps=steps)
    u = (x * sc1p + shift).astype(BF16)
    q = (_dot(u, wq_ref[...]) * (HD ** -0.5)).astype(BF16)
    nbuf = kb_ref.shape[1]
    k_all = jnp.concatenate([kb_ref[...].reshape(ns * nbuf, KVW), kn_ref[...].reshape(nr, KVW)],
                            axis=0).astype(BF16)
    v_all = jnp.concatenate([vb_ref[...].reshape(ns * nbuf, KVW), vn_ref[...].reshape(nr, KVW)],
                            axis=0).astype(BF16)
    heads = [None] * HQ
    for hk in range(HKV):
        kcols = slice(hk * HD, (hk + 1) * HD)
        qh = jnp.concatenate(
            [q[:, (hk * GRP + g) * HD:(hk * GRP + g + 1) * HD] for g in range(GRP)], axis=0)
        o = _sink_attend(qh, k_all[:, kcols], v_all[:, kcols], bias_ref[hk], sink_ref[hk])
        for g in range(GRP):
            heads[hk * GRP + g] = o[g * nr:(g + 1) * nr]
    o_all = jnp.concatenate(heads, axis=1)
    mix = _dot(o_all.astype(BF16), wo_ref[...])
    xn_ref[...] = _layer_norm(ALPHA * x + g1p * mix, g_ref[...], b_ref[...]).reshape(steps, ns, D)


def _attn_sample(x, mod, wq, wo, kbuf, vbuf, kn, vn, bias, sink, ln_g, ln_b, *, layer, ns):
    steps, nb, _ = x.shape
    nbuf = kbuf.shape[1]
    j = layer - N_A
    sub = 2 * layer
    seqs = lambda i: (0, i, 0)
    return pl.pallas_call(
        functools.partial(_attn_sample_body, ns=ns, steps=steps),
        grid=(nb // ns,),
        in_specs=[
            pl.BlockSpec((steps, ns, D), seqs),
            pl.BlockSpec((None, ns, 3 * D), lambda i: (sub, i, 0)),
            _layer_spec((D, HQ * HD), j), _layer_spec((HQ * HD, D), j),
            pl.BlockSpec((ns, nbuf, KVW), lambda i: (i, 0, 0)),
            pl.BlockSpec((ns, nbuf, KVW), lambda i: (i, 0, 0)),
            pl.BlockSpec((steps, ns, KVW), seqs), pl.BlockSpec((steps, ns, KVW), seqs),
            _const_spec(bias.shape), _layer_spec(sink.shape[1:], j),
            _layer_spec((1, D), sub), _layer_spec((1, D), sub),
        ],
        out_specs=pl.BlockSpec((steps, ns, D), seqs),
        out_shape=jax.ShapeDtypeStruct((steps, nb, D), F32),
        compiler_params=_cp(("arbitrary",)),
        name="attn_sample",
    )(x, mod, wq, wo, kbuf, vbuf, kn, vn, bias, sink, ln_g, ln_b)


def _t5_bucket(d):
    max_exact = N_BUCKETS // 2
    d = np.maximum(d, 0)
    log_ratio = (np.log(np.maximum(d, 1).astype(np.float32) / np.float32(max_exact))
                 / np.float32(math.log(WIN / max_exact)))
    large = max_exact + (log_ratio * (N_BUCKETS - max_exact)).astype(np.int32)
    large = np.minimum(large, N_BUCKETS - 1)
    return np.where(d < max_exact, d, large)


def _bias_table(d, ok, rel_bias):
    sel = np.where(ok, _t5_bucket(d), N_BUCKETS)
    onehot = (sel[..., None] == np.arange(N_BUCKETS + 1)).astype(np.float32)
    tab = jnp.concatenate([rel_bias.astype(F32), jnp.full((1, HQ), NEG, F32)], axis=0)
    return jnp.einsum('abd,dh->hab', jnp.asarray(onehot), tab, precision=lax.Precision.HIGHEST)


def _prompt_bias(rel_bias):
    d = np.arange(WIN)[None, :] + WIN - np.arange(2 * WIN)[:, None]
    b = _bias_table(d, (d >= 0) & (d < WIN), rel_bias).reshape(HKV, GRP, 2 * WIN, WIN)
    return jnp.transpose(b, (0, 2, 1, 3)).reshape(HKV, 2 * WIN, GRP * WIN)


def _sample_bias(rel_bias, ns, steps, nbuf):
    qs = np.tile(np.arange(ns), steps)
    qt = np.repeat(np.arange(steps), ns)
    ks = np.concatenate([np.repeat(np.arange(ns), nbuf), qs])
    kpos = np.concatenate([np.tile(np.arange(nbuf), ns), nbuf + qt])
    d = qt[:, None] + nbuf - kpos[None, :]
    ok = (d >= 0) & (d < WIN) & (qs[:, None] == ks[None, :])
    return _bias_table(d, ok, rel_bias).reshape(HKV, GRP * ns * steps, kpos.shape[0])


LRU_TM = 512
ATTN_TQ = 1024
SAMPLE_NS = 8


def kernel(x_prompt, x_sample, state_rnn_h, state_rnn_conv, cache_win_k, cache_win_v, c_prompt, c_sample, ada_w, ada_b, ln_g, ln_b, lru_w_in, lru_conv_w, lru_conv_b, lru_w_a, lru_b_a, lru_w_x, lru_b_x, lru_lambda, lru_w_out, kv_w, attn_w_q, attn_sinks, attn_w_o, rel_bias, moe_w_group, moe_b_group, moe_w_router, moe_b_router, moe_w1, moe_w3, moe_w2):
    nbp, seq, _ = x_prompt.shape
    nbs, steps, _ = x_sample.shape
    nbuf = cache_win_k.shape[1]
    npt = nbp * seq
    nst = nbs * steps

    mod_p, mod_s = _adaln(c_prompt, c_sample, ada_w, ada_b)

    ln_g3 = ln_g.reshape(2 * DEPTH, 1, D)
    ln_b3 = ln_b.reshape(2 * DEPTH, 1, D)
    lw = (lru_w_in.astype(BF16), lru_conv_w, lru_conv_b[:, None], lru_w_a.astype(BF16),
          lru_w_x.astype(BF16), lru_b_a[:, None], lru_b_x[:, None], lru_lambda[:, None],
          lru_w_out.astype(BF16))
    zrow = lambda k: jnp.zeros((DEPTH, k, D), F32)
    wr = jnp.concatenate([jnp.swapaxes(moe_w_group, 1, 2), zrow(EROW - N_GROUPS),
                          jnp.swapaxes(moe_w_router.reshape(DEPTH, D, N_EXP), 1, 2),
                          zrow(RROWS - EROW - N_EXP)], axis=1).astype(BF16)
    zb = lambda k: jnp.zeros((DEPTH, k), F32)
    br = jnp.concatenate([moe_b_group, zb(EROW - N_GROUPS), moe_b_router.reshape(DEPTH, N_EXP),
                          zb(RROWS - EROW - N_EXP)], axis=1)
    br = jnp.broadcast_to(br[:, :, None], (DEPTH, RROWS, RLANES))
    wq = attn_w_q.astype(BF16)
    wqt = jnp.swapaxes(wq, 1, 2)
    wo = attn_w_o.astype(BF16)
    kvw = kv_w.astype(BF16)
    kvwt = kvw[:, KVW:].T
    bias_p = _prompt_bias(rel_bias)
    bias_s = _sample_bias(rel_bias, SAMPLE_NS, steps, nbuf)
    sink_p = jnp.repeat(attn_sinks.reshape(N_B, HKV, GRP), WIN, axis=2)[:, :, None, :]
    sink_s = jnp.repeat(attn_sinks.reshape(N_B, HKV, GRP), SAMPLE_NS * steps, axis=2)[..., None]
    kbuf = cache_win_k.reshape(nbs, nbuf, KVW)
    vbuf = cache_win_v.reshape(nbs, nbuf, KVW)

    xp = x_prompt.reshape(npt, D)
    xs = jnp.transpose(x_sample, (1, 0, 2)).reshape(nst, D)
    c0 = jnp.transpose(state_rnn_conv, (0, 2, 1, 3)).reshape(N_A, (CONV_W - 1) * nbs, R)
    hp, cp_, hs, cs = [], [], [], []
    kp = vp = kn = vn = kp16 = vpt16 = None
    for l in range(DEPTH):
        if l < N_A:
            xp, h_l, c_l, *route = _lru_prompt(xp, mod_p, lw, ln_g3, ln_b3, wr, br,
                                               layer=l, nb=nbp, seq=seq, tm=LRU_TM)
            hp.append(h_l[:, 0])
            cp_.append(c_l)
            xs, h_l, c_l = _lru_sample(xs, mod_s, state_rnn_h, c0[l], lw, ln_g3, ln_b3,
                                       layer=l, nb=nbs, steps=steps)
            hs.append(h_l)
            cs.append(jnp.transpose(c_l.reshape(CONV_W - 1, nbs, R), (1, 0, 2)))
        else:
            xp, *route = _attn_prompt(xp, mod_p, wqt, wo, kp16, vpt16, bias_p, sink_p, ln_g3, ln_b3, wr, br,
                                      layer=l, nb=nbp, seq=seq, tq=ATTN_TQ)
            xs = _attn_sample(xs.reshape(steps, nbs, D), mod_s, wq, wo, kbuf, vbuf,
                              kn.reshape(steps, nbs, KVW), vn.reshape(steps, nbs, KVW),
                              bias_s, sink_s, ln_g3, ln_b3, layer=l, ns=SAMPLE_NS).reshape(nst, D)
        xs, w1, w3, w2 = _moe_dense(xs, mod_s, wr, br, moe_w1, moe_w3, moe_w2, ln_g3, ln_b3,
                                    layer=l, reps=steps)
        xp = _moe_sparse(xp, route, mod_p, w1, w3, w2, ln_g3, ln_b3, layer=l, seq=seq)
        if l == N_A - 1:
            kp, vp, kp16, vpt16 = _kv(xp, kvw, kvwt, tm=1024)
            kn, vn, _, _ = _kv(xs, kvw, kvwt, tm=nst)

    lp = min(WIN, seq)
    k_p = kp.reshape(nbp, seq, HKV, HD)[:, seq - lp:]
    v_p = vp.reshape(nbp, seq, HKV, HD)[:, seq - lp:]
    kn4 = jnp.transpose(kn.reshape(steps, nbs, HKV, HD), (1, 0, 2, 3))
    vn4 = jnp.transpose(vn.reshape(steps, nbs, HKV, HD), (1, 0, 2, 3))
    k_s = jnp.concatenate([cache_win_k, kn4], axis=1)[:, -nbuf:]
    v_s = jnp.concatenate([cache_win_v, vn4], axis=1)[:, -nbuf:]
    y_s = jnp.transpose(xs.reshape(steps, nbs, D), (1, 0, 2))
    return (xp.reshape(nbp, seq, D), y_s,
            jnp.stack(hp), jnp.stack(cp_), k_p, v_p,
            jnp.stack(hs), jnp.stack(cs), k_s, v_s)
```

```python
import functools
import math

import jax
import jax.numpy as jnp
import numpy as np
from jax import lax
from jax.experimental import pallas as pl
from jax.experimental.pallas import tpu as pltpu

D = 1024
R = 1024
DEPTH = 4
N_A = 2
N_B = DEPTH - N_A
N_BLK = 4
BLK = R // N_BLK
CONV_W = 4
RG_C = 8.0
HQ = 16
HKV = 4
HD = 64
GRP = HQ // HKV
KVW = HKV * HD
WIN = 128
N_BUCKETS = 32
N_GROUPS = 4
E_PER_G = 4
N_EXP = 16
F = 512
ALPHA = (2.0 * DEPTH) ** 0.25
LN_EPS = 1e-5
NEG = -1e30
RLANES = 128
ROW_TILE = 8

F32 = jnp.float32
BF16 = jnp.bfloat16

VMEM_LIMIT = 56 * 1024 * 1024


def _cp(sem):
    return pltpu.CompilerParams(dimension_semantics=sem, vmem_limit_bytes=VMEM_LIMIT)


def _dot(a, b):
    return jnp.dot(a, b, preferred_element_type=F32)


def _layer_norm(z, g, b):
    mu = jnp.mean(z, axis=-1, keepdims=True)
    zc = z - mu
    var = jnp.mean(zc * zc, axis=-1, keepdims=True)
    return zc * lax.rsqrt(var + LN_EPS) * g + b


def _split_mod(mod_ref, row=None, reps=1):
    m = mod_ref[...] if row is None else mod_ref[pl.ds(row, 1), :]
    if reps > 1:
        m = jnp.concatenate([m] * reps, axis=0)
    return m[:, 0:D], m[:, D:2 * D], m[:, 2 * D:3 * D]


def _layer_spec(shape, layer):
    nd = len(shape)
    return pl.BlockSpec((None,) + tuple(shape), lambda *_: (layer,) + (0,) * nd)


def _adaln_body(cp_ref, cs_ref, w_ref, b_ref, op_ref, os_ref):
    one = jnp.where(pl.program_id(1) > 0, 1.0, 0.0).astype(F32)
    w = w_ref[...].astype(BF16)
    op_ref[...] = _dot(cp_ref[...].astype(BF16), w) + b_ref[...] + one
    os_ref[...] = _dot(cs_ref[...].astype(BF16), w) + b_ref[...] + one


def _adaln(c_p, c_s, ada_w, ada_b):
    np_, ns_ = c_p.shape[0], c_s.shape[0]
    w = ada_w.reshape(2 * DEPTH, D, 3 * D)
    b = ada_b.reshape(2 * DEPTH, 1, 3 * D)
    return pl.pallas_call(
        _adaln_body,
        grid=(2 * DEPTH, 3),
        in_specs=[
            pl.BlockSpec((np_, D), lambda l, j: (0, 0)),
            pl.BlockSpec((ns_, D), lambda l, j: (0, 0)),
            pl.BlockSpec((None, D, D), lambda l, j: (l, 0, j)),
            pl.BlockSpec((None, 1, D), lambda l, j: (l, 0, j)),
        ],
        out_specs=[pl.BlockSpec((None, np_, D), lambda l, j: (l, 0, j)),
                   pl.BlockSpec((None, ns_, D), lambda l, j: (l, 0, j))],
        out_shape=[jax.ShapeDtypeStruct((2 * DEPTH, np_, 3 * D), F32),
                   jax.ShapeDtypeStruct((2 * DEPTH, ns_, 3 * D), F32)],
        compiler_params=_cp(("arbitrary", "arbitrary")),
        name="adaln",
    )(c_p, c_s, w, b)


def _lru_gates(xc, n, wa_ref, wx_ref, ba_ref, bx_ref, lam_ref):
    cols = slice(n * BLK, (n + 1) * BLK)
    xb = xc.astype(BF16)
    r = jax.nn.sigmoid(_dot(xb, wa_ref[n]) + ba_ref[:, cols])
    i = jax.nn.sigmoid(_dot(xb, wx_ref[n]) + bx_ref[:, cols])
    z = -lam_ref[:, cols]
    softplus = jnp.maximum(z, 0.0) + jnp.log1p(jnp.exp(-jnp.abs(z)))
    log_a = (-RG_C) * r * softplus
    a = jnp.exp(log_a)
    v = 1.0 - a * a
    b = jnp.where(v > 0.0, v * lax.rsqrt(v), 0.0) * i * xc
    return a, b


def _lru_prompt_body(x_ref, mod_ref, win_ref, cw_ref, cb_ref, wa_ref, wx_ref, ba_ref, bx_ref,
                     lam_ref, wout_ref, g_ref, b_ref, mod1_ref, wr_ref, br_ref,
                     xn_ref, hl_ref, cl_ref, urow_ref, meta_ref, cnt_ref,
                     xbuf, abuf, hbuf, hc, carry, *, tm):
    t = pl.program_id(1)

    @pl.when(t == 0)
    def _():
        xbuf[0:ROW_TILE, :] = jnp.zeros((ROW_TILE, R), F32)
        hc[...] = jnp.zeros((1, R), F32)

    x = x_ref[...]
    shift, sc1p, g1p = _split_mod(mod_ref, row=pl.program_id(0))
    u = x * sc1p + shift
    yx = _dot(u.astype(BF16), win_ref[...])
    y = jax.nn.gelu(yx[:, :R])
    xbuf[ROW_TILE:ROW_TILE + tm, :] = yx[:, R:]

    ngrp = tm // ROW_TILE
    rowmod = lax.broadcasted_iota(jnp.int32, (ngrp, ROW_TILE, BLK), 1)
    for n in range(N_BLK):
        cols = slice(n * BLK, (n + 1) * BLK)
        xc = cb_ref[:, cols] + cw_ref[3:4, cols] * xbuf[ROW_TILE:ROW_TILE + tm, cols]
        for k in range(1, CONV_W):
            xc = xc + cw_ref[3 - k:4 - k, cols] * xbuf[ROW_TILE - k:ROW_TILE - k + tm, cols]
        a, b = _lru_gates(xc, n, wa_ref, wx_ref, ba_ref, bx_ref, lam_ref)
        a = a.reshape(ngrp, ROW_TILE, BLK)
        b = b.reshape(ngrp, ROW_TILE, BLK)
        s = 1
        while s < ROW_TILE:
            a_sh = pltpu.roll(a, s, 1)
            b_sh = pltpu.roll(b, s, 1)
            m = rowmod >= s
            b = jnp.where(m, a * b_sh, 0.0) + b
            a = jnp.where(m, a * a_sh, a)
            s *= 2
        abuf[:, cols] = a.reshape(tm, BLK)
        hbuf[:, cols] = b.reshape(tm, BLK)

    def group(g, h):
        r0 = pl.multiple_of(g * ROW_TILE, ROW_TILE)
        hg = abuf[pl.ds(r0, ROW_TILE), :] * h + hbuf[pl.ds(r0, ROW_TILE), :]
        hbuf[pl.ds(r0, ROW_TILE), :] = hg
        return hg[ROW_TILE - 1:ROW_TILE, :]

    h_end = lax.fori_loop(0, tm // ROW_TILE, group, hc[...])
    hc[...] = h_end
    hl_ref[...] = h_end
    tail = xbuf[tm + ROW_TILE - (CONV_W - 1):tm + ROW_TILE, :]
    cl_ref[...] = tail
    xbuf[ROW_TILE - (CONV_W - 1):ROW_TILE, :] = tail

    mix = _dot((y * hbuf[...]).astype(BF16), wout_ref[...])
    xn = _layer_norm(ALPHA * x + g1p * mix, g_ref[...], b_ref[...])
    xn_ref[...] = xn
    b = pl.program_id(0)
    _route_rows(xn, b, (b == 0) & (t == 0), mod1_ref, wr_ref, br_ref, urow_ref, meta_ref, cnt_ref, carry)


def _const_spec(shape):
    nd = len(shape)
    return pl.BlockSpec(shape, lambda *_: (0,) * nd)


def _lru_weight_specs(layer):
    sub = 2 * layer
    return [
        _layer_spec((D, 2 * R), layer), _layer_spec((CONV_W, R), layer), _layer_spec((1, R), layer),
        _layer_spec((N_BLK, BLK, BLK), layer), _layer_spec((N_BLK, BLK, BLK), layer),
        _layer_spec((1, R), layer), _layer_spec((1, R), layer), _layer_spec((1, R), layer),
        _layer_spec((R, D), layer), _layer_spec((1, D), sub), _layer_spec((1, D), sub),
    ]


def _lru_prompt(x, mod, lw, ln_g, ln_b, wr, br, *, layer, nb, seq, tm):
    nt = seq // tm
    n = nb * seq
    r_in, r_out, r_shape, r_scratch = _route_specs(layer, nb, nt, tm)
    return pl.pallas_call(
        functools.partial(_lru_prompt_body, tm=tm),
        grid=(nb, nt),
        in_specs=[
            pl.BlockSpec((tm, D), lambda b, t: (b * nt + t, 0)),
            _layer_spec((nb, 3 * D), 2 * layer),
        ] + _lru_weight_specs(layer) + r_in,
        out_specs=[
            pl.BlockSpec((tm, D), lambda b, t: (b * nt + t, 0)),
            pl.BlockSpec((None, 1, R), lambda b, t: (b, 0, 0)),
            pl.BlockSpec((None, CONV_W - 1, R), lambda b, t: (b, 0, 0)),
        ] + r_out,
        out_shape=[
            jax.ShapeDtypeStruct((n, D), F32),
            jax.ShapeDtypeStruct((nb, 1, R), F32),
            jax.ShapeDtypeStruct((nb, CONV_W - 1, R), F32),
        ] + r_shape,
        scratch_shapes=[
            pltpu.VMEM((tm + ROW_TILE, R), F32),
            pltpu.VMEM((tm, R), F32),
            pltpu.VMEM((tm, R), F32),
            pltpu.VMEM((1, R), F32),
        ] + r_scratch,
        compiler_params=_cp(("arbitrary", "arbitrary")),
        name="lru_prompt",
    )(x, mod, *lw, ln_g, ln_b, mod, wr, br)


def _lru_sample_body(x_ref, mod_ref, h0_ref, c0_ref, win_ref, cw_ref, cb_ref, wa_ref, wx_ref,
                     ba_ref, bx_ref, lam_ref, wout_ref, g_ref, b_ref,
                     xn_ref, hl_ref, cl_ref, hbuf, *, nb, steps):
    x = x_ref[...]
    shift, sc1p, g1p = _split_mod(mod_ref, reps=steps)
    u = x * sc1p + shift
    yx = _dot(u.astype(BF16), win_ref[...])
    y = jax.nn.gelu(yx[:, :R])
    xext = jnp.concatenate([c0_ref[...], yx[:, R:]], axis=0)
    cl_ref[...] = xext[steps * nb:, :]
    for n in range(N_BLK):
        cols = slice(n * BLK, (n + 1) * BLK)
        xc = cb_ref[:, cols] + cw_ref[0:1, cols] * xext[0:steps * nb, cols]
        for k in range(1, CONV_W):
            xc = xc + cw_ref[k:k + 1, cols] * xext[k * nb:(k + steps) * nb, cols]
        a, b = _lru_gates(xc, n, wa_ref, wx_ref, ba_ref, bx_ref, lam_ref)
        h = h0_ref[:, cols]
        for t in range(steps):
            h = a[t * nb:(t + 1) * nb, :] * h + b[t * nb:(t + 1) * nb, :]
            hbuf[t * nb:(t + 1) * nb, cols] = h
        hl_ref[:, cols] = h
    mix = _dot((y * hbuf[...]).astype(BF16), wout_ref[...])
    xn_ref[...] = _layer_norm(ALPHA * x + g1p * mix, g_ref[...], b_ref[...])


def _lru_sample(x, mod, h0, c0, lw, ln_g, ln_b, *, layer, nb, steps):
    n = nb * steps
    return pl.pallas_call(
        functools.partial(_lru_sample_body, nb=nb, steps=steps),
        grid=(1,),
        in_specs=[
            _const_spec((n, D)), _layer_spec((nb, 3 * D), 2 * layer), _layer_spec((nb, R), layer),
            _const_spec(((CONV_W - 1) * nb, R)),
        ] + _lru_weight_specs(layer),
        out_specs=[_const_spec((n, D)), _const_spec((nb, R)), _const_spec(((CONV_W - 1) * nb, R))],
        out_shape=[
            jax.ShapeDtypeStruct((n, D), F32),
            jax.ShapeDtypeStruct((nb, R), F32),
            jax.ShapeDtypeStruct(((CONV_W - 1) * nb, R), F32),
        ],
        scratch_shapes=[pltpu.VMEM((n, R), F32)],
        compiler_params=_cp(("arbitrary",)),
        name="lru_sample",
    )(x, mod, h0, c0, *lw, ln_g, ln_b)


RROWS = 32
EROW = 8
_NT = (((1,), (1,)), ((), ()))
_TN = (((0,), (0,)), ((), ()))


def _top2(lt):
    nt = lt.shape[1]
    gl = lt[0:EROW, :]
    gid = lax.broadcasted_iota(jnp.int32, (EROW, nt), 0).astype(F32)
    is_g = gid < N_GROUPS
    gl = jnp.where(is_g, gl, NEG)
    gmax = jnp.max(gl, axis=0, keepdims=True)
    gidx = jnp.min(jnp.where(gl == gmax, gid, float(EROW)), axis=0, keepdims=True)
    gsum = jnp.sum(jnp.where(is_g, jnp.exp(gl - gmax), 0.0), axis=0, keepdims=True)
    g_p = 1.0 / gsum
    el = lt[EROW:EROW + N_EXP, :]
    eid = lax.broadcasted_iota(jnp.int32, (N_EXP, nt), 0).astype(F32)
    first = gidx * E_PER_G
    in_grp = (eid >= first) & (eid < first + E_PER_G)
    el = jnp.where(in_grp, el, NEG)
    big = float(N_EXP)
    t1 = jnp.max(el, axis=0, keepdims=True)
    i1 = jnp.min(jnp.where(el == t1, eid, big), axis=0, keepdims=True)
    el2 = jnp.where(eid == i1, NEG, el)
    t2 = jnp.max(el2, axis=0, keepdims=True)
    i2 = jnp.min(jnp.where((el2 == t2) & in_grp & (eid != i1), eid, big), axis=0, keepdims=True)
    e21 = jnp.exp(t2 - t1)
    w1 = g_p / (1.0 + e21)
    w2 = w1 * e21
    return eid, gidx, i1, i2, w1, w2


def _router_logits(u16, wrt_ref, brt_ref):
    return lax.dot_general(wrt_ref[...], u16, _NT, preferred_element_type=F32) + brt_ref[:, 0:1]


def _rows_to_cols(rows):
    r, nt = rows.shape
    padded = jnp.concatenate([rows, jnp.zeros((RLANES - r, nt), F32)], axis=0)
    return jnp.transpose(padded)


PAIRS = E_PER_G * (E_PER_G - 1) // 2
NCLS = N_GROUPS * PAIRS
XW = D + RLANES
FFN_TM = 256
DISPATCH_TD = 2048
COMBINE_TC = 256
COMBINE_CHUNK = 64
COMBINE_SLOTS = 3


def _route_rows(x, row, first, mod_ref, wr_ref, br_ref, urow_ref, meta_ref, cnt_ref, carry):
    @pl.when(first)
    def _():
        carry[...] = jnp.zeros(carry.shape, F32)

    tm = x.shape[0]
    shift, sc1p, _ = _split_mod(mod_ref, row=row)
    u = x * sc1p + shift
    _, gidx, i1, i2, w1, w2 = _top2(_router_logits(u.astype(BF16), wr_ref, br_ref))
    first = gidx * E_PER_G
    lo = jnp.minimum(i1, i2) - first
    hi = jnp.maximum(i1, i2) - first
    w_lo = jnp.where(i1 < i2, w1, w2)
    w_hi = jnp.where(i1 < i2, w2, w1)
    cls = gidx * PAIRS + lo * (2 * E_PER_G - 1 - lo) * 0.5 + (hi - lo - 1.0)
    cid = lax.broadcasted_iota(jnp.int32, (RROWS, tm), 0).astype(F32)
    onehot = (cid == cls).astype(F32)
    r = lax.broadcasted_iota(jnp.int32, (tm, tm), 0)
    c = lax.broadcasted_iota(jnp.int32, (tm, tm), 1)
    earlier = _dot(onehot.astype(BF16), (r < c).astype(BF16)) + carry[:, 0:1]
    rank = jnp.sum(onehot * earlier, axis=0, keepdims=True)
    carry[...] += jnp.sum(onehot, axis=1, keepdims=True)
    cnt_ref[...] = carry[...]
    meta = jnp.concatenate([w_lo, w_hi, cls, rank, jnp.zeros((ROW_TILE - 4, tm), F32)], axis=0)
    meta_ref[...] = meta
    urow_ref[:, 0:D] = u
    urow_ref[:, D:XW] = _rows_to_cols(meta)


def _route_specs(layer, nb, nt, tm):
    tile = lambda b, t: b * nt + t
    n = nb * nt * tm
    in_specs = [_layer_spec((nb, 3 * D), 2 * layer + 1),
                _layer_spec((RROWS, D), layer), _layer_spec((RROWS, RLANES), layer)]
    out_specs = [pl.BlockSpec((tm, XW), lambda b, t: (tile(b, t), 0)),
                 pl.BlockSpec((None, ROW_TILE, tm), lambda b, t: (tile(b, t), 0, 0)),
                 _const_spec((RROWS, RLANES))]
    out_shape = [jax.ShapeDtypeStruct((n, XW), F32),
                 jax.ShapeDtypeStruct((nb * nt, ROW_TILE, tm), F32),
                 jax.ShapeDtypeStruct((RROWS, RLANES), F32)]
    return in_specs, out_specs, out_shape, [pltpu.VMEM((RROWS, RLANES), F32)]


def _row_copy(src, src_row, dst, dst_row, sem):
    return pltpu.make_async_copy(src.at[pl.ds(src_row, 1)], dst.at[pl.ds(dst_row, 1)], sem)


def _dispatch_body(pos_ref, zflag_ref, urow_ref, xs_hbm, zbuf, sem, zsem):
    base = pl.program_id(0) * DISPATCH_TD

    @pl.when(pl.program_id(0) == 0)
    def _():
        zbuf[...] = jnp.zeros(zbuf.shape, F32)

        def zero_copy(t):
            return pltpu.make_async_copy(zbuf, xs_hbm.at[pl.ds(t * FFN_TM, FFN_TM)], zsem)

        def start(t, carry):
            @pl.when(zflag_ref[t] != 0)
            def _():
                zero_copy(t).start()
            return carry

        def wait(t, carry):
            @pl.when(zflag_ref[t] != 0)
            def _():
                zero_copy(t).wait()
            return carry

        lax.fori_loop(0, zflag_ref.shape[0], start, 0)
        lax.fori_loop(0, zflag_ref.shape[0], wait, 0)

    def issue(g, carry):
        r0 = pl.multiple_of(g * ROW_TILE, ROW_TILE)
        for k in range(ROW_TILE):
            _row_copy(urow_ref.at[pl.ds(r0, ROW_TILE)], k, xs_hbm, pos_ref[base + r0 + k],
                      sem).start(priority=k % 2)
        return carry

    lax.fori_loop(0, DISPATCH_TD // ROW_TILE, issue, 0)
    pltpu.make_async_copy(urow_ref, xs_hbm.at[pl.ds(0, DISPATCH_TD)], sem).wait()


def _dispatch(pos, zflag, urow):
    n = urow.shape[0]
    nt = zflag.shape[0]
    return pl.pallas_call(
        _dispatch_body,
        grid_spec=pltpu.PrefetchScalarGridSpec(
            num_scalar_prefetch=2, grid=(n // DISPATCH_TD,),
            in_specs=[pl.BlockSpec((DISPATCH_TD, XW), lambda i, pos, zflag: (i, 0))],
            out_specs=pl.BlockSpec(memory_space=pl.ANY),
            scratch_shapes=[pltpu.VMEM((FFN_TM, XW), F32), pltpu.SemaphoreType.DMA(()),
                            pltpu.SemaphoreType.DMA(())]),
        out_shape=jax.ShapeDtypeStruct((nt * FFN_TM, XW), F32),
        compiler_params=_cp(("arbitrary",)),
        name="moe_dispatch",
    )(pos, zflag, urow)


def _ffn_sorted_body(ta_ref, tb_ref, na_ref, xs_ref, w1a, w3a, w2a, w1b, w3b, w2b, ys_ref):
    del ta_ref, tb_ref
    active = pl.program_id(0) < na_ref[0]

    @pl.when(active)
    def _():
        x = xs_ref[:, 0:D].astype(BF16)
        ha = jax.nn.silu(_dot(x, w1a[...])) * _dot(x, w3a[...]) * xs_ref[:, D:D + 1]
        hb = jax.nn.silu(_dot(x, w1b[...])) * _dot(x, w3b[...]) * xs_ref[:, D + 1:D + 2]
        ys_ref[...] = _dot(ha.astype(BF16), w2a[...]) + _dot(hb.astype(BF16), w2b[...])

    @pl.when(jnp.logical_not(active))
    def _():
        ys_ref[...] = jnp.zeros(ys_ref.shape, F32)


def _ffn_sorted(ta, tb, na, xs, w1, w3, w2):
    nt = ta.shape[0]
    ea = lambda t, ta, tb, na: (ta[t], 0, 0)
    eb = lambda t, ta, tb, na: (tb[t], 0, 0)
    rows = lambda t, ta, tb, na: (t, 0)
    up, down = (None, D, F), (None, F, D)
    return pl.pallas_call(
        _ffn_sorted_body,
        grid_spec=pltpu.PrefetchScalarGridSpec(
            num_scalar_prefetch=3, grid=(nt,),
            in_specs=[
                pl.BlockSpec((FFN_TM, XW), rows),
                pl.BlockSpec(up, ea), pl.BlockSpec(up, ea), pl.BlockSpec(down, ea),
                pl.BlockSpec(up, eb), pl.BlockSpec(up, eb), pl.BlockSpec(down, eb),
            ],
            out_specs=pl.BlockSpec((FFN_TM, D), rows)),
        out_shape=jax.ShapeDtypeStruct((nt * FFN_TM, D), F32),
        compiler_params=_cp(("arbitrary",)),
        name="moe_ffn",
    )(ta, tb, na, xs, w1, w3, w2, w1, w3, w2)


def _combine_body(pos_ref, x_ref, mod_ref, ys_hbm, g_ref, b_ref, xn_ref, buf, sem, *, tiles_per_seq):
    i = pl.program_id(0)
    nsteps = pl.num_programs(0)
    tc = x_ref.shape[0]

    def issue_group(step, slot, r0):
        group = buf.at[slot, pl.ds(r0, ROW_TILE)]
        for k in range(ROW_TILE):
            _row_copy(ys_hbm, pos_ref[step * tc + r0 + k], group, k, sem.at[slot]).start(priority=k % 2)

    def wait_tile(slot):
        pltpu.make_async_copy(ys_hbm.at[pl.ds(0, tc)], buf.at[slot], sem.at[slot]).wait()

    ahead = COMBINE_SLOTS - 1

    @pl.when(i == 0)
    def _():
        for t in range(ahead):
            def prime(g, carry, t=t):
                issue_group(t, t, pl.multiple_of(g * ROW_TILE, ROW_TILE))
                return carry
            lax.fori_loop(0, tc // ROW_TILE, prime, 0)

    slot = i % COMBINE_SLOTS
    wait_tile(slot)
    _, _, g1p = _split_mod(mod_ref, row=i // tiles_per_seq)
    gain, bias = g_ref[...], b_ref[...]
    nxt = jnp.minimum(i + ahead, nsteps - 1)
    nslot = (i + ahead) % COMBINE_SLOTS

    for c in range(tc // COMBINE_CHUNK):
        for g in range(COMBINE_CHUNK // ROW_TILE):
            issue_group(nxt, nslot, c * COMBINE_CHUNK + g * ROW_TILE)
        rows = pl.ds(c * COMBINE_CHUNK, COMBINE_CHUNK)
        z = ALPHA * x_ref[rows, :] + g1p * buf[slot, rows, :]
        xn_ref[rows, :] = _layer_norm(z, gain, bias)

    @pl.when(i == nsteps - 1)
    def _():
        for k in range(1, COMBINE_SLOTS):
            wait_tile((i + k) % COMBINE_SLOTS)


def _combine(pos, x, mod, ys, ln_g, ln_b, *, layer, seq):
    n = x.shape[0]
    tc = COMBINE_TC
    sub = 2 * layer + 1
    return pl.pallas_call(
        functools.partial(_combine_body, tiles_per_seq=seq // tc),
        grid_spec=pltpu.PrefetchScalarGridSpec(
            num_scalar_prefetch=1, grid=(n // tc,),
            in_specs=[
                pl.BlockSpec((tc, D), lambda i, pos: (i, 0)),
                _layer_spec((n // seq, 3 * D), sub),
                pl.BlockSpec(memory_space=pl.ANY),
                _layer_spec((1, D), sub), _layer_spec((1, D), sub),
            ],
            out_specs=pl.BlockSpec((tc, D), lambda i, pos: (i, 0)),
            scratch_shapes=[pltpu.VMEM((COMBINE_SLOTS, tc, D), F32),
                            pltpu.SemaphoreType.DMA((COMBINE_SLOTS,))]),
        out_shape=jax.ShapeDtypeStruct((n, D), F32),
        compiler_params=_cp(("arbitrary",)),
        name="moe_combine",
    )(pos, x, mod, ys, ln_g, ln_b)


def _moe_sparse(x, route, mod, w1, w3, w2, ln_g, ln_b, *, layer, seq):
    n = x.shape[0]
    urow, meta, cnt = route
    cls = meta[:, 2, :].reshape(n).astype(jnp.int32)
    rank = meta[:, 3, :].reshape(n).astype(jnp.int32)
    count = cnt[:NCLS, 0].astype(jnp.int32)
    ntile = (count + FFN_TM - 1) // FFN_TM
    tile_end = jnp.cumsum(ntile)
    row_start = (tile_end - ntile) * FFN_TM
    pos = rank + jnp.sum(jax.nn.one_hot(cls, NCLS, dtype=jnp.int32) * row_start[None, :], axis=1)
    nt = n // FFN_TM + NCLS
    total = tile_end[NCLS - 1]
    tr = jnp.minimum(jnp.arange(nt, dtype=jnp.int32), total - 1)
    tcls = jnp.sum((tr[:, None] >= tile_end[None, :]).astype(jnp.int32), axis=1)
    grp, pair = tcls // PAIRS, tcls % PAIRS
    lo = (pair >= 3).astype(jnp.int32) + (pair >= 5).astype(jnp.int32)
    hi = pair + 1 - lo * (2 * E_PER_G - 3 - lo) // 2
    ta = grp * E_PER_G + lo
    tb = grp * E_PER_G + hi
    t_all = jnp.arange(nt, dtype=jnp.int32)
    partial_last = ((t_all[:, None] == tile_end[None, :] - 1) & (count % FFN_TM != 0)[None, :]).any(axis=1)
    zflag = ((t_all >= total) | partial_last).astype(jnp.int32)
    xs = _dispatch(pos, zflag, urow)
    ys = _ffn_sorted(ta, tb, total[None], xs, w1, w3, w2)
    return _combine(pos, x, mod, ys, ln_g, ln_b, layer=layer, seq=seq)


def _moe_body(x_ref, mod_ref, wr_ref, br_ref, w1f_ref, w3f_ref, w2f_ref, g_ref, b_ref,
              xn_ref, w1_ref, w3_ref, w2_ref, ub, comb, acc, *, reps):
    e = pl.program_id(1)
    w1_ref[...] = w1f_ref[...].astype(BF16)
    w3_ref[...] = w3f_ref[...].astype(BF16)
    w2_ref[...] = w2f_ref[...].astype(BF16)

    @pl.when(e == 0)
    def _():
        shift, sc1p, _ = _split_mod(mod_ref, reps=reps)
        u = (x_ref[...] * sc1p + shift).astype(BF16)
        ub[...] = u
        eid, _, i1, i2, w1, w2 = _top2(_router_logits(u, wr_ref, br_ref))
        comb[...] = _rows_to_cols(jnp.where(eid == i1, w1, 0.0) + jnp.where(eid == i2, w2, 0.0))
        acc[...] = jnp.zeros(acc.shape, F32)

    u = ub[...]
    col = lax.broadcasted_iota(jnp.int32, comb.shape, 1)
    ce = jnp.sum(jnp.where(col == e, comb[...], 0.0), axis=-1, keepdims=True)
    h = jax.nn.silu(_dot(u, w1_ref[...])) * _dot(u, w3_ref[...]) * ce
    acc[...] += _dot(h.astype(BF16), w2_ref[...])

    @pl.when(e == N_EXP - 1)
    def _():
        _, _, g1p = _split_mod(mod_ref, reps=reps)
        xn_ref[...] = _layer_norm(ALPHA * x_ref[...] + g1p * acc[...], g_ref[...], b_ref[...])


def _moe_dense(x, mod, wr, br, w1, w3, w2, ln_g, ln_b, *, layer, reps):
    n = x.shape[0]
    sub = 2 * layer + 1
    expert = lambda i, e: (layer, e, 0, 0)
    return pl.pallas_call(
        functools.partial(_moe_body, reps=reps),
        grid=(1, N_EXP),
        in_specs=[
            pl.BlockSpec((n, D), lambda i, e: (0, 0)),
            _layer_spec((n // reps, 3 * D), sub),
            _layer_spec((RROWS, D), layer), _layer_spec((RROWS, RLANES), layer),
            pl.BlockSpec((None, None, D, F), expert),
            pl.BlockSpec((None, None, D, F), expert),
            pl.BlockSpec((None, None, F, D), expert),
            _layer_spec((1, D), sub), _layer_spec((1, D), sub),
        ],
        out_specs=[pl.BlockSpec((n, D), lambda i, e: (0, 0)),
                   pl.BlockSpec((None, D, F), lambda i, e: (e, 0, 0)),
                   pl.BlockSpec((None, D, F), lambda i, e: (e, 0, 0)),
                   pl.BlockSpec((None, F, D), lambda i, e: (e, 0, 0))],
        out_shape=[jax.ShapeDtypeStruct((n, D), F32),
                   jax.ShapeDtypeStruct((N_EXP, D, F), BF16),
                   jax.ShapeDtypeStruct((N_EXP, D, F), BF16),
                   jax.ShapeDtypeStruct((N_EXP, F, D), BF16)],
        scratch_shapes=[
            pltpu.VMEM((n, D), BF16),
            pltpu.VMEM((n, RLANES), F32),
            pltpu.VMEM((n, D), F32),
        ],
        compiler_params=_cp(("arbitrary", "arbitrary")),
        name="moe",
    )(x, mod, wr, br, w1, w3, w2, ln_g, ln_b)


def _kv_body(x_ref, w_ref, wvt_ref, k_ref, v_ref, k16_ref, vt16_ref):
    xb = x_ref[...].astype(BF16)
    kv = _dot(xb, w_ref[...])
    k_ref[...] = kv[:, :KVW]
    v_ref[...] = kv[:, KVW:]
    k16_ref[...] = kv[:, :KVW].astype(BF16)
    vt16_ref[...] = lax.dot_general(wvt_ref[...], xb, _NT, preferred_element_type=F32).astype(BF16)


def _kv(x, w, wvt, *, tm):
    n = x.shape[0]
    row = lambda i: (i, 0)
    return pl.pallas_call(
        _kv_body,
        grid=(n // tm,),
        in_specs=[pl.BlockSpec((tm, D), row), _const_spec((D, 2 * KVW)), _const_spec((KVW, D))],
        out_specs=[pl.BlockSpec((tm, KVW), row), pl.BlockSpec((tm, KVW), row),
                   pl.BlockSpec((tm, KVW), row), pl.BlockSpec((KVW, tm), lambda i: (0, i))],
        out_shape=[jax.ShapeDtypeStruct((n, KVW), F32), jax.ShapeDtypeStruct((n, KVW), F32),
                   jax.ShapeDtypeStruct((n, KVW), BF16), jax.ShapeDtypeStruct((KVW, n), BF16)],
        compiler_params=_cp(("arbitrary",)),
        name="kv_proj",
    )(x, w, wvt)


def _sink_attend(qh, kh, vh, bias, sink):
    s = lax.dot_general(qh, kh, (((1,), (1,)), ((), ())), preferred_element_type=F32) + bias
    m = jnp.maximum(jnp.max(s, axis=-1, keepdims=True), sink)
    p = jnp.exp(s - m)
    den = jnp.sum(p, axis=-1, keepdims=True) + jnp.exp(sink - m)
    return _dot(p.astype(BF16), vh) / den


def _attn_prompt_body(x_ref, mod_ref, wqt_ref, wo_ref, kc_ref, kp_ref, vtc_ref, vtp_ref,
                      bias_ref, sink_ref, g_ref, b_ref, mod1_ref, wr_ref, br_ref,
                      xn_ref, urow_ref, meta_ref, cnt_ref, carry, *, tq):
    t = pl.program_id(1)
    x = x_ref[...]
    shift, sc1p, g1p = _split_mod(mod_ref, row=pl.program_id(0))
    u = (x * sc1p + shift).astype(BF16)
    qt = lax.dot_general(wqt_ref[...], u, _NT, preferred_element_type=F32)
    qt = (qt * (HD ** -0.5)).astype(BF16)
    kc = kc_ref[...]
    vtc = vtc_ref[...]
    krow = lax.broadcasted_iota(jnp.int32, (2 * WIN, GRP * WIN), 0)
    first = jnp.where((krow < WIN) & (t == 0), NEG, 0.0).astype(F32)
    nwin = tq // WIN
    kks, vvts = [], []
    for j in range(nwin):
        if j == 0:
            kks.append(jnp.concatenate([kp_ref[...], kc[0:WIN]], axis=0))
            vvts.append(jnp.concatenate([vtp_ref[...], vtc[:, 0:WIN]], axis=1))
        else:
            kks.append(kc[(j - 1) * WIN:(j + 1) * WIN])
            vvts.append(vtc[:, (j - 1) * WIN:(j + 1) * WIN])

    def scores(j, hk):
        qht = jnp.concatenate(
            [qt[(hk * GRP + g) * HD:(hk * GRP + g + 1) * HD, j * WIN:(j + 1) * WIN] for g in range(GRP)],
            axis=1)
        st = _dot(kks[j][:, hk * HD:(hk + 1) * HD], qht) + bias_ref[hk]
        return st + first if j == 0 else st

    items = [(j, hk) for j in range(nwin) for hk in range(HKV)]
    heads = [[] for _ in range(nwin)]
    st_next = scores(*items[0])
    for n, (j, hk) in enumerate(items):
        st = st_next
        if n + 1 < len(items):
            st_next = scores(*items[n + 1])
        sink = sink_ref[hk]
        m = jnp.maximum(jnp.max(st, axis=0, keepdims=True), sink)
        pt = jnp.exp(st - m)
        den = jnp.sum(pt, axis=0, keepdims=True) + jnp.exp(sink - m)
        ot = _dot(vvts[j][hk * HD:(hk + 1) * HD, :], pt.astype(BF16)) * (1.0 / den)
        for g in range(GRP):
            heads[j].append(ot[:, g * WIN:(g + 1) * WIN])
    blocks = [jnp.concatenate(h, axis=0) for h in heads]
    oallt = jnp.concatenate(blocks, axis=1) if len(blocks) > 1 else blocks[0]
    mix = lax.dot_general(oallt.astype(BF16), wo_ref[...], _TN, preferred_element_type=F32)
    xn = _layer_norm(ALPHA * x + g1p * mix, g_ref[...], b_ref[...])
    xn_ref[...] = xn
    b = pl.program_id(0)
    _route_rows(xn, b, (b == 0) & (t == 0), mod1_ref, wr_ref, br_ref, urow_ref, meta_ref, cnt_ref, carry)


def _attn_prompt(x, mod, wqt, wo, k16, vt16, bias, sink, ln_g, ln_b, wr, br, *, layer, nb, seq, tq):
    nt = seq // tq
    r_in, r_out, r_shape, r_scratch = _route_specs(layer, nb, nt, tq)
    wpt = tq // WIN
    n = nb * seq
    j = layer - N_A
    sub = 2 * layer

    def prev(b, t):
        return jnp.maximum((b * nt + t) * wpt - 1, 0)

    cur = lambda b, t: (b * nt + t, 0)
    return pl.pallas_call(
        functools.partial(_attn_prompt_body, tq=tq),
        grid=(nb, nt),
        in_specs=[
            pl.BlockSpec((tq, D), cur),
            _layer_spec((nb, 3 * D), sub),
            _layer_spec((HQ * HD, D), j), _layer_spec((HQ * HD, D), j),
            pl.BlockSpec((tq, KVW), cur), pl.BlockSpec((WIN, KVW), lambda b, t: (prev(b, t), 0)),
            pl.BlockSpec((KVW, tq), lambda b, t: (0, b * nt + t)),
            pl.BlockSpec((KVW, WIN), lambda b, t: (0, prev(b, t))),
            _const_spec((HKV, 2 * WIN, GRP * WIN)), _layer_spec((HKV, 1, GRP * WIN), j),
            _layer_spec((1, D), sub), _layer_spec((1, D), sub),
        ] + r_in,
        out_specs=[pl.BlockSpec((tq, D), cur)] + r_out,
        out_shape=[jax.ShapeDtypeStruct((n, D), F32)] + r_shape,
        scratch_shapes=r_scratch,
        compiler_params=_cp(("arbitrary", "arbitrary")),
        name="attn_prompt",
    )(x, mod, wqt, wo, k16, k16, vt16, vt16, bias, sink, ln_g, ln_b, mod, wr, br)


def _attn_sample_body(x_ref, mod_ref, wq_ref, wo_ref, kb_ref, vb_ref, kn_ref, vn_ref,
                      bias_ref, sink_ref, g_ref, b_ref, xn_ref, *, ns, steps):
    nr = ns * steps
    x = x_ref[...].reshape(nr, D)
    shift, sc1p, g1p = _split_mod(mod_ref, reps=steps)
    u = (x * sc1p + shift).astype(BF16)
    q = (_dot(u, wq_ref[...]) * (HD ** -0.5)).astype(BF16)
    nbuf = kb_ref.shape[1]
    k_all = jnp.concatenate([kb_ref[...].reshape(ns * nbuf, KVW), kn_ref[...].reshape(nr, KVW)],
                            axis=0).astype(BF16)
    v_all = jnp.concatenate([vb_ref[...].reshape(ns * nbuf, KVW), vn_ref[...].reshape(nr, KVW)],
                            axis=0).astype(BF16)
    heads = [None] * HQ
    for hk in range(HKV):
        kcols = slice(hk * HD, (hk + 1) * HD)
        qh = jnp.concatenate(
            [q[:, (hk * GRP + g) * HD:(hk * GRP + g + 1) * HD] for g in range(GRP)], axis=0)
        o = _sink_attend(qh, k_all[:, kcols], v_all[:, kcols], bias_ref[hk], sink_ref[hk])
        for g in range(GRP):
            heads[hk * GRP + g] = o[g * nr:(g + 1) * nr]
    o_all = jnp.concatenate(heads, axis=1)
    mix = _dot(o_all.astype(BF16), wo_ref[...])
    xn_ref[...] = _layer_norm(ALPHA * x + g1p * mix, g_ref[...], b_ref[...]).reshape(steps, ns, D)


def _attn_sample(x, mod, wq, wo, kbuf, vbuf, kn, vn, bias, sink, ln_g, ln_b, *, layer, ns):
    steps, nb, _ = x.shape
    nbuf = kbuf.shape[1]
    j = layer - N_A
    sub = 2 * layer
    seqs = lambda i: (0, i, 0)
    return pl.pallas_call(
        functools.partial(_attn_sample_body, ns=ns, steps=steps),
        grid=(nb // ns,),
        in_specs=[
            pl.BlockSpec((steps, ns, D), seqs),
            pl.BlockSpec((None, ns, 3 * D), lambda i: (sub, i, 0)),
            _layer_spec((D, HQ * HD), j), _layer_spec((HQ * HD, D), j),
            pl.BlockSpec((ns, nbuf, KVW), lambda i: (i, 0, 0)),
            pl.BlockSpec((ns, nbuf, KVW), lambda i: (i, 0, 0)),
            pl.BlockSpec((steps, ns, KVW), seqs), pl.BlockSpec((steps, ns, KVW), seqs),
            _const_spec(bias.shape), _layer_spec(sink.shape[1:], j),
            _layer_spec((1, D), sub), _layer_spec((1, D), sub),
        ],
        out_specs=pl.BlockSpec((steps, ns, D), seqs),
        out_shape=jax.ShapeDtypeStruct((steps, nb, D), F32),
        compiler_params=_cp(("arbitrary",)),
        name="attn_sample",
    )(x, mod, wq, wo, kbuf, vbuf, kn, vn, bias, sink, ln_g, ln_b)


def _t5_bucket(d):
    max_exact = N_BUCKETS // 2
    d = np.maximum(d, 0)
    log_ratio = (np.log(np.maximum(d, 1).astype(np.float32) / np.float32(max_exact))
                 / np.float32(math.log(WIN / max_exact)))
    large = max_exact + (log_ratio * (N_BUCKETS - max_exact)).astype(np.int32)
    large = np.minimum(large, N_BUCKETS - 1)
    return np.where(d < max_exact, d, large)


def _bias_table(d, ok, rel_bias):
    sel = np.where(ok, _t5_bucket(d), N_BUCKETS)
    onehot = (sel[..., None] == np.arange(N_BUCKETS + 1)).astype(np.float32)
    tab = jnp.concatenate([rel_bias.astype(F32), jnp.full((1, HQ), NEG, F32)], axis=0)
    return jnp.einsum('abd,dh->hab', jnp.asarray(onehot), tab, precision=lax.Precision.HIGHEST)


def _prompt_bias(rel_bias):
    d = np.arange(WIN)[None, :] + WIN - np.arange(2 * WIN)[:, None]
    b = _bias_table(d, (d >= 0) & (d < WIN), rel_bias).reshape(HKV, GRP, 2 * WIN, WIN)
    return jnp.transpose(b, (0, 2, 1, 3)).reshape(HKV, 2 * WIN, GRP * WIN)


def _sample_bias(rel_bias, ns, steps, nbuf):
    qs = np.tile(np.arange(ns), steps)
    qt = np.repeat(np.arange(steps), ns)
    ks = np.concatenate([np.repeat(np.arange(ns), nbuf), qs])
    kpos = np.concatenate([np.tile(np.arange(nbuf), ns), nbuf + qt])
    d = qt[:, None] + nbuf - kpos[None, :]
    ok = (d >= 0) & (d < WIN) & (qs[:, None] == ks[None, :])
    return _bias_table(d, ok, rel_bias).reshape(HKV, GRP * ns * steps, kpos.shape[0])


LRU_TM = 512
ATTN_TQ = 1024
SAMPLE_NS = 8


def kernel(x_prompt, x_sample, state_rnn_h, state_rnn_conv, cache_win_k, cache_win_v, c_prompt, c_sample, ada_w, ada_b, ln_g, ln_b, lru_w_in, lru_conv_w, lru_conv_b, lru_w_a, lru_b_a, lru_w_x, lru_b_x, lru_lambda, lru_w_out, kv_w, attn_w_q, attn_sinks, attn_w_o, rel_bias, moe_w_group, moe_b_group, moe_w_router, moe_b_router, moe_w1, moe_w3, moe_w2):
    nbp, seq, _ = x_prompt.shape
    nbs, steps, _ = x_sample.shape
    nbuf = cache_win_k.shape[1]
    npt = nbp * seq
    nst = nbs * steps

    mod_p, mod_s = _adaln(c_prompt, c_sample, ada_w, ada_b)

    ln_g3 = ln_g.reshape(2 * DEPTH, 1, D)
    ln_b3 = ln_b.reshape(2 * DEPTH, 1, D)
    lw = (lru_w_in.astype(BF16), lru_conv_w, lru_conv_b[:, None], lru_w_a.astype(BF16),
          lru_w_x.astype(BF16), lru_b_a[:, None], lru_b_x[:, None], lru_lambda[:, None],
          lru_w_out.astype(BF16))
    zrow = lambda k: jnp.zeros((DEPTH, k, D), F32)
    wr = jnp.concatenate([jnp.swapaxes(moe_w_group, 1, 2), zrow(EROW - N_GROUPS),
                          jnp.swapaxes(moe_w_router.reshape(DEPTH, D, N_EXP), 1, 2),
                          zrow(RROWS - EROW - N_EXP)], axis=1).astype(BF16)
    zb = lambda k: jnp.zeros((DEPTH, k), F32)
    br = jnp.concatenate([moe_b_group, zb(EROW - N_GROUPS), moe_b_router.reshape(DEPTH, N_EXP),
                          zb(RROWS - EROW - N_EXP)], axis=1)
    br = jnp.broadcast_to(br[:, :, None], (DEPTH, RROWS, RLANES))
    wq = attn_w_q.astype(BF16)
    wqt = jnp.swapaxes(wq, 1, 2)
    wo = attn_w_o.astype(BF16)
    kvw = kv_w.astype(BF16)
    kvwt = kvw[:, KVW:].T
    bias_p = _prompt_bias(rel_bias)
    bias_s = _sample_bias(rel_bias, SAMPLE_NS, steps, nbuf)
    sink_p = jnp.repeat(attn_sinks.reshape(N_B, HKV, GRP), WIN, axis=2)[:, :, None, :]
    sink_s = jnp.repeat(attn_sinks.reshape(N_B, HKV, GRP), SAMPLE_NS * steps, axis=2)[..., None]
    kbuf = cache_win_k.reshape(nbs, nbuf, KVW)
    vbuf = cache_win_v.reshape(nbs, nbuf, KVW)

    xp = x_prompt.reshape(npt, D)
    xs = jnp.transpose(x_sample, (1, 0, 2)).reshape(nst, D)
    c0 = jnp.transpose(state_rnn_conv, (0, 2, 1, 3)).reshape(N_A, (CONV_W - 1) * nbs, R)
    hp, cp_, hs, cs = [], [], [], []
    kp = vp = kn = vn = kp16 = vpt16 = None
    for l in range(DEPTH):
        if l < N_A:
            xp, h_l, c_l, *route = _lru_prompt(xp, mod_p, lw, ln_g3, ln_b3, wr, br,
                                               layer=l, nb=nbp, seq=seq, tm=LRU_TM)
            hp.append(h_l[:, 0])
            cp_.append(c_l)
            xs, h_l, c_l = _lru_sample(xs, mod_s, state_rnn_h, c0[l], lw, ln_g3, ln_b3,
                                       layer=l, nb=nbs, steps=steps)
            hs.append(h_l)
            cs.append(jnp.transpose(c_l.reshape(CONV_W - 1, nbs, R), (1, 0, 2)))
        else:
            xp, *route = _attn_prompt(xp, mod_p, wqt, wo, kp16, vpt16, bias_p, sink_p, ln_g3, ln_b3, wr, br,
                                      layer=l, nb=nbp, seq=seq, tq=ATTN_TQ)
            xs = _attn_sample(xs.reshape(steps, nbs, D), mod_s, wq, wo, kbuf, vbuf,
                              kn.reshape(steps, nbs, KVW), vn.reshape(steps, nbs, KVW),
                              bias_s, sink_s, ln_g3, ln_b3, layer=l, ns=SAMPLE_NS).reshape(nst, D)
        xs, w1, w3, w2 = _moe_dense(xs, mod_s, wr, br, moe_w1, moe_w3, moe_w2, ln_g3, ln_b3,
                                    layer=l, reps=steps)
        xp = _moe_sparse(xp, route, mod_p, w1, w3, w2, ln_g3, ln_b3, layer=l, seq=seq)
        if l == N_A - 1:
            kp, vp, kp16, vpt16 = _kv(xp, kvw, kvwt, tm=1024)
            kn, vn, _, _ = _kv(xs, kvw, kvwt, tm=nst)

    lp = min(WIN, seq)
    k_p = kp.reshape(nbp, seq, HKV, HD)[:, seq - lp:]
    v_p = vp.reshape(nbp, seq, HKV, HD)[:, seq - lp:]
    kn4 = jnp.transpose(kn.reshape(steps, nbs, HKV, HD), (1, 0, 2, 3))
    vn4 = jnp.transpose(vn.reshape(steps, nbs, HKV, HD), (1, 0, 2, 3))
    k_s = jnp.concatenate([cache_win_k, kn4], axis=1)[:, -nbuf:]
    v_s = jnp.concatenate([cache_win_v, vn4], axis=1)[:, -nbuf:]
    y_s = jnp.transpose(xs.reshape(steps, nbs, D), (1, 0, 2))
    return (xp.reshape(nbp, seq, D), y_s,
            jnp.stack(hp), jnp.stack(cp_), k_p, v_p,
            jnp.stack(hs), jnp.stack(cs), k_s, v_s)
```

```python
import functools
import math

import jax
import jax.numpy as jnp
import numpy as np
from jax import lax
from jax.experimental import pallas as pl
from jax.experimental.pallas import tpu as pltpu

D = 1024
R = 1024
DEPTH = 4
N_A = 2
N_B = DEPTH - N_A
N_BLK = 4
BLK = R // N_BLK
CONV_W = 4
RG_C = 8.0
HQ = 16
HKV = 4
HD = 64
GRP = HQ // HKV
KVW = HKV * HD
WIN = 128
N_BUCKETS = 32
N_GROUPS = 4
E_PER_G = 4
N_EXP = 16
F = 512
ALPHA = (2.0 * DEPTH) ** 0.25
LN_EPS = 1e-5
NEG = -1e30
LANES = 128
RLANES = LANES
ROW_TILE = 8

F32 = jnp.float32
BF16 = jnp.bfloat16

VMEM_LIMIT = 56 * 1024 * 1024


def _cp(sem):
    return pltpu.CompilerParams(dimension_semantics=sem, vmem_limit_bytes=VMEM_LIMIT)


def _dot(a, b):
    return jnp.dot(a, b, preferred_element_type=F32)


def _layer_norm(z, g, b):
    mu = jnp.mean(z, axis=-1, keepdims=True)
    zc = z - mu
    var = jnp.mean(zc * zc, axis=-1, keepdims=True)
    return zc * lax.rsqrt(var + LN_EPS) * g + b


def _split_mod(mod_ref, row=None, reps=1):
    m = mod_ref[...] if row is None else mod_ref[pl.ds(row, 1), :]
    if reps > 1:
        m = jnp.concatenate([m] * reps, axis=0)
    return m[:, 0:D], m[:, D:2 * D], m[:, 2 * D:3 * D]


def _layer_spec(shape, layer):
    nd = len(shape)
    return pl.BlockSpec((None,) + tuple(shape), lambda *_: (layer,) + (0,) * nd)


def _adaln_body(cp_ref, cs_ref, w_ref, b_ref, op_ref, os_ref):
    one = jnp.where(pl.program_id(1) > 0, 1.0, 0.0).astype(F32)
    w = w_ref[...].astype(BF16)
    op_ref[...] = _dot(cp_ref[...].astype(BF16), w) + b_ref[...] + one
    os_ref[...] = _dot(cs_ref[...].astype(BF16), w) + b_ref[...] + one


def _adaln(c_p, c_s, ada_w, ada_b):
    np_, ns_ = c_p.shape[0], c_s.shape[0]
    w = ada_w.reshape(2 * DEPTH, D, 3 * D)
    b = ada_b.reshape(2 * DEPTH, 1, 3 * D)
    return pl.pallas_call(
        _adaln_body,
        grid=(2 * DEPTH, 3),
        in_specs=[
            pl.BlockSpec((np_, D), lambda l, j: (0, 0)),
            pl.BlockSpec((ns_, D), lambda l, j: (0, 0)),
            pl.BlockSpec((None, D, D), lambda l, j: (l, 0, j)),
            pl.BlockSpec((None, 1, D), lambda l, j: (l, 0, j)),
        ],
        out_specs=[pl.BlockSpec((None, np_, D), lambda l, j: (l, 0, j)),
                   pl.BlockSpec((None, ns_, D), lambda l, j: (l, 0, j))],
        out_shape=[jax.ShapeDtypeStruct((2 * DEPTH, np_, 3 * D), F32),
                   jax.ShapeDtypeStruct((2 * DEPTH, ns_, 3 * D), F32)],
        compiler_params=_cp(("arbitrary", "arbitrary")),
        name="adaln",
    )(c_p, c_s, w, b)


def _lru_gates(xc, n, wa_ref, wx_ref, ba_ref, bx_ref, lam_ref):
    cols = slice(n * BLK, (n + 1) * BLK)
    xb = xc.astype(BF16)
    r = jax.nn.sigmoid(_dot(xb, wa_ref[n]) + ba_ref[:, cols])
    i = jax.nn.sigmoid(_dot(xb, wx_ref[n]) + bx_ref[:, cols])
    z = -lam_ref[:, cols]
    softplus = jnp.maximum(z, 0.0) + jnp.log1p(jnp.exp(-jnp.abs(z)))
    log_a = (-RG_C) * r * softplus
    a = jnp.exp(log_a)
    v = 1.0 - a * a
    b = jnp.where(v > 0.0, v * lax.rsqrt(v), 0.0) * i * xc
    return a, b


def _lru_prompt_body(x_ref, mod_ref, win_ref, cw_ref, cb_ref, wa_ref, wx_ref, ba_ref, bx_ref,
                     lam_ref, wout_ref, g_ref, b_ref, mod1_ref, wr_ref, br_ref,
                     xn_ref, hl_ref, cl_ref, urow_ref, meta_ref, cnt_ref,
                     xbuf, abuf, hbuf, hc, carry, *, tm):
    t = pl.program_id(1)

    @pl.when(t == 0)
    def _():
        xbuf[0:ROW_TILE, :] = jnp.zeros((ROW_TILE, R), F32)
        hc[...] = jnp.zeros((1, R), F32)

    x = x_ref[...]
    shift, sc1p, g1p = _split_mod(mod_ref, row=pl.program_id(0))
    u = x * sc1p + shift
    yx = _dot(u.astype(BF16), win_ref[...])
    y = jax.nn.gelu(yx[:, :R])
    xbuf[ROW_TILE:ROW_TILE + tm, :] = yx[:, R:]

    ngrp = tm // ROW_TILE
    rowmod = lax.broadcasted_iota(jnp.int32, (ngrp, ROW_TILE, BLK), 1)
    for n in range(N_BLK):
        cols = slice(n * BLK, (n + 1) * BLK)
        xc = cb_ref[:, cols] + cw_ref[3:4, cols] * xbuf[ROW_TILE:ROW_TILE + tm, cols]
        for k in range(1, CONV_W):
            xc = xc + cw_ref[3 - k:4 - k, cols] * xbuf[ROW_TILE - k:ROW_TILE - k + tm, cols]
        a, b = _lru_gates(xc, n, wa_ref, wx_ref, ba_ref, bx_ref, lam_ref)
        a = a.reshape(ngrp, ROW_TILE, BLK)
        b = b.reshape(ngrp, ROW_TILE, BLK)
        s = 1
        while s < ROW_TILE:
            a_sh = pltpu.roll(a, s, 1)
            b_sh = pltpu.roll(b, s, 1)
            m = rowmod >= s
            b = jnp.where(m, a * b_sh, 0.0) + b
            a = jnp.where(m, a * a_sh, a)
            s *= 2
        abuf[:, cols] = a.reshape(tm, BLK)
        hbuf[:, cols] = b.reshape(tm, BLK)

    def group(g, h):
        r0 = pl.multiple_of(g * ROW_TILE, ROW_TILE)
        hg = abuf[pl.ds(r0, ROW_TILE), :] * h + hbuf[pl.ds(r0, ROW_TILE), :]
        hbuf[pl.ds(r0, ROW_TILE), :] = hg
        return hg[ROW_TILE - 1:ROW_TILE, :]

    h_end = lax.fori_loop(0, tm // ROW_TILE, group, hc[...])
    hc[...] = h_end
    hl_ref[...] = h_end
    tail = xbuf[tm + ROW_TILE - (CONV_W - 1):tm + ROW_TILE, :]
    cl_ref[...] = tail
    xbuf[ROW_TILE - (CONV_W - 1):ROW_TILE, :] = tail

    mix = _dot((y * hbuf[...]).astype(BF16), wout_ref[...])
    xn = _layer_norm(ALPHA * x + g1p * mix, g_ref[...], b_ref[...])
    xn_ref[...] = xn
    b = pl.program_id(0)
    _route_rows(xn, b, (b == 0) & (t == 0), mod1_ref, wr_ref, br_ref, urow_ref, meta_ref, cnt_ref, carry)


def _const_spec(shape):
    nd = len(shape)
    return pl.BlockSpec(shape, lambda *_: (0,) * nd)


def _lru_weight_specs(layer):
    sub = 2 * layer
    return [
        _layer_spec((D, 2 * R), layer), _layer_spec((CONV_W, R), layer), _layer_spec((1, R), layer),
        _layer_spec((N_BLK, BLK, BLK), layer), _layer_spec((N_BLK, BLK, BLK), layer),
        _layer_spec((1, R), layer), _layer_spec((1, R), layer), _layer_spec((1, R), layer),
        _layer_spec((R, D), layer), _layer_spec((1, D), sub), _layer_spec((1, D), sub),
    ]


def _lru_prompt(x, mod, lw, ln_g, ln_b, wr, br, *, layer, nb, seq, tm):
    nt = seq // tm
    n = nb * seq
    r_in, r_out, r_shape, r_scratch = _route_specs(layer, nb, nt, tm)
    return pl.pallas_call(
        functools.partial(_lru_prompt_body, tm=tm),
        grid=(nb, nt),
        in_specs=[
            pl.BlockSpec((tm, D), lambda b, t: (b * nt + t, 0)),
            _layer_spec((nb, 3 * D), 2 * layer),
        ] + _lru_weight_specs(layer) + r_in,
        out_specs=[
            pl.BlockSpec((tm, D), lambda b, t: (b * nt + t, 0)),
            pl.BlockSpec((None, 1, R), lambda b, t: (b, 0, 0)),
            pl.BlockSpec((None, CONV_W - 1, R), lambda b, t: (b, 0, 0)),
        ] + r_out,
        out_shape=[
            jax.ShapeDtypeStruct((n, D), F32),
            jax.ShapeDtypeStruct((nb, 1, R), F32),
            jax.ShapeDtypeStruct((nb, CONV_W - 1, R), F32),
        ] + r_shape,
        scratch_shapes=[
            pltpu.VMEM((tm + ROW_TILE, R), F32),
            pltpu.VMEM((tm, R), F32),
            pltpu.VMEM((tm, R), F32),
            pltpu.VMEM((1, R), F32),
        ] + r_scratch,
        compiler_params=_cp(("arbitrary", "arbitrary")),
        name="lru_prompt",
    )(x, mod, *lw, ln_g, ln_b, mod, wr, br)


def _lru_sample_body(x_ref, mod_ref, h0_ref, c0_ref, win_ref, cw_ref, cb_ref, wa_ref, wx_ref,
                     ba_ref, bx_ref, lam_ref, wout_ref, g_ref, b_ref,
                     xn_ref, hl_ref, cl_ref, hbuf, *, nb, steps):
    x = x_ref[...]
    shift, sc1p, g1p = _split_mod(mod_ref, reps=steps)
    u = x * sc1p + shift
    yx = _dot(u.astype(BF16), win_ref[...])
    y = jax.nn.gelu(yx[:, :R])
    xext = jnp.concatenate([c0_ref[...], yx[:, R:]], axis=0)
    cl_ref[...] = xext[steps * nb:, :]
    for n in range(N_BLK):
        cols = slice(n * BLK, (n + 1) * BLK)
        xc = cb_ref[:, cols] + cw_ref[0:1, cols] * xext[0:steps * nb, cols]
        for k in range(1, CONV_W):
            xc = xc + cw_ref[k:k + 1, cols] * xext[k * nb:(k + steps) * nb, cols]
        a, b = _lru_gates(xc, n, wa_ref, wx_ref, ba_ref, bx_ref, lam_ref)
        h = h0_ref[:, cols]
        for t in range(steps):
            h = a[t * nb:(t + 1) * nb, :] * h + b[t * nb:(t + 1) * nb, :]
            hbuf[t * nb:(t + 1) * nb, cols] = h
        hl_ref[:, cols] = h
    mix = _dot((y * hbuf[...]).astype(BF16), wout_ref[...])
    xn_ref[...] = _layer_norm(ALPHA * x + g1p * mix, g_ref[...], b_ref[...])


def _lru_sample(x, mod, h0, c0, lw, ln_g, ln_b, *, layer, nb, steps):
    n = nb * steps
    return pl.pallas_call(
        functools.partial(_lru_sample_body, nb=nb, steps=steps),
        grid=(1,),
        in_specs=[
            _const_spec((n, D)), _layer_spec((nb, 3 * D), 2 * layer), _layer_spec((nb, R), layer),
            _const_spec(((CONV_W - 1) * nb, R)),
        ] + _lru_weight_specs(layer),
        out_specs=[_const_spec((n, D)), _const_spec((nb, R)), _const_spec(((CONV_W - 1) * nb, R))],
        out_shape=[
            jax.ShapeDtypeStruct((n, D), F32),
            jax.ShapeDtypeStruct((nb, R), F32),
            jax.ShapeDtypeStruct(((CONV_W - 1) * nb, R), F32),
        ],
        scratch_shapes=[pltpu.VMEM((n, R), F32)],
        compiler_params=_cp(("arbitrary",)),
        name="lru_sample",
    )(x, mod, h0, c0, *lw, ln_g, ln_b)


RROWS = 32
EROW = 8
_NT = (((1,), (1,)), ((), ()))
_TN = (((0,), (0,)), ((), ()))


def _top2(lt):
    nt = lt.shape[1]
    gl = lt[0:EROW, :]
    gid = lax.broadcasted_iota(jnp.int32, (EROW, nt), 0).astype(F32)
    is_g = gid < N_GROUPS
    gl = jnp.where(is_g, gl, NEG)
    gmax = jnp.max(gl, axis=0, keepdims=True)
    gidx = jnp.min(jnp.where(gl == gmax, gid, float(EROW)), axis=0, keepdims=True)
    gsum = jnp.sum(jnp.where(is_g, jnp.exp(gl - gmax), 0.0), axis=0, keepdims=True)
    g_p = 1.0 / gsum
    el = lt[EROW:EROW + N_EXP, :]
    eid = lax.broadcasted_iota(jnp.int32, (N_EXP, nt), 0).astype(F32)
    first = gidx * E_PER_G
    in_grp = (eid >= first) & (eid < first + E_PER_G)
    el = jnp.where(in_grp, el, NEG)
    big = float(N_EXP)
    t1 = jnp.max(el, axis=0, keepdims=True)
    i1 = jnp.min(jnp.where(el == t1, eid, big), axis=0, keepdims=True)
    el2 = jnp.where(eid == i1, NEG, el)
    t2 = jnp.max(el2, axis=0, keepdims=True)
    i2 = jnp.min(jnp.where((el2 == t2) & in_grp & (eid != i1), eid, big), axis=0, keepdims=True)
    e21 = jnp.exp(t2 - t1)
    w1 = g_p / (1.0 + e21)
    w2 = w1 * e21
    return eid, gidx, i1, i2, w1, w2


def _router_logits(u16, wrt_ref, brt_ref):
    return lax.dot_general(wrt_ref[...], u16, _NT, preferred_element_type=F32) + brt_ref[:, 0:1]


def _rows_to_cols(rows):
    r, nt = rows.shape
    padded = jnp.concatenate([rows, jnp.zeros((RLANES - r, nt), F32)], axis=0)
    return jnp.transpose(padded)


PAIRS = E_PER_G * (E_PER_G - 1) // 2
NCLS = N_GROUPS * PAIRS
HALF = D // 2
ROW_SHAPE = (D // LANES, LANES)
HI_MASK = -65536
FFN_TM = 256
DISPATCH_TD = 2048
COMBINE_TC = 256
COMBINE_CHUNK = 64
COMBINE_SLOTS = 3


def _route_rows(x, row, first, mod_ref, wr_ref, br_ref, urow_ref, meta_ref, cnt_ref, carry):
    @pl.when(first)
    def _():
        carry[...] = jnp.zeros(carry.shape, F32)

    tm = x.shape[0]
    shift, sc1p, _ = _split_mod(mod_ref, row=row)
    u = x * sc1p + shift
    _, gidx, i1, i2, w1, w2 = _top2(_router_logits(u.astype(BF16), wr_ref, br_ref))
    first = gidx * E_PER_G
    lo = jnp.minimum(i1, i2) - first
    hi = jnp.maximum(i1, i2) - first
    w_lo = jnp.where(i1 < i2, w1, w2)
    w_hi = jnp.where(i1 < i2, w2, w1)
    cls = gidx * PAIRS + lo * (2 * E_PER_G - 1 - lo) * 0.5 + (hi - lo - 1.0)
    cid = lax.broadcasted_iota(jnp.int32, (RROWS, tm), 0).astype(F32)
    onehot = (cid == cls).astype(F32)
    r = lax.broadcasted_iota(jnp.int32, (tm, tm), 0)
    c = lax.broadcasted_iota(jnp.int32, (tm, tm), 1)
    earlier = _dot(onehot.astype(BF16), (r < c).astype(BF16)) + carry[:, 0:1]
    rank = jnp.sum(onehot * earlier, axis=0, keepdims=True)
    carry[...] += jnp.sum(onehot, axis=1, keepdims=True)
    cnt_ref[...] = carry[...]
    meta = jnp.concatenate([w_lo, w_hi, cls, rank, jnp.zeros((ROW_TILE - 4, tm), F32)], axis=0)
    meta_ref[...] = meta
    ub = u.astype(BF16)
    lo = lax.shift_right_logical(pltpu.bitcast(ub[:, 0:HALF].astype(F32), jnp.int32), 16)
    hi = pltpu.bitcast(ub[:, HALF:D].astype(F32), jnp.int32) & HI_MASK
    ext = pltpu.bitcast(_rows_to_cols(meta), jnp.int32)
    row = jnp.concatenate([hi | lo, ext, jnp.zeros((tm, D - HALF - RLANES), jnp.int32)], axis=1)
    urow_ref[...] = row.reshape((tm,) + ROW_SHAPE)


def _route_specs(layer, nb, nt, tm):
    tile = lambda b, t: b * nt + t
    n = nb * nt * tm
    in_specs = [_layer_spec((nb, 3 * D), 2 * layer + 1),
                _layer_spec((RROWS, D), layer), _layer_spec((RROWS, RLANES), layer)]
    out_specs = [pl.BlockSpec((tm,) + ROW_SHAPE, lambda b, t: (tile(b, t), 0, 0)),
                 pl.BlockSpec((None, ROW_TILE, tm), lambda b, t: (tile(b, t), 0, 0)),
                 _const_spec((RROWS, RLANES))]
    out_shape = [jax.ShapeDtypeStruct((n,) + ROW_SHAPE, jnp.int32),
                 jax.ShapeDtypeStruct((nb * nt, ROW_TILE, tm), F32),
                 jax.ShapeDtypeStruct((RROWS, RLANES), F32)]
    return in_specs, out_specs, out_shape, [pltpu.VMEM((RROWS, RLANES), F32)]


def _row_copy(src, src_row, dst, dst_row, sem):
    return pltpu.make_async_copy(src.at[pl.ds(src_row, 1)], dst.at[pl.ds(dst_row, 1)], sem)


def _dispatch_body(pos_ref, zflag_ref, urow_ref, xs_hbm, zbuf, sem, zsem):
    base = pl.program_id(0) * DISPATCH_TD

    @pl.when(pl.program_id(0) == 0)
    def _():
        zbuf[...] = jnp.zeros(zbuf.shape, jnp.int32)

        def zero_copy(t):
            return pltpu.make_async_copy(zbuf, xs_hbm.at[pl.ds(t * FFN_TM, FFN_TM)], zsem)

        def start(t, carry):
            @pl.when(zflag_ref[t] != 0)
            def _():
                zero_copy(t).start()
            return carry

        def wait(t, carry):
            @pl.when(zflag_ref[t] != 0)
            def _():
                zero_copy(t).wait()
            return carry

        lax.fori_loop(0, zflag_ref.shape[0], start, 0)
        lax.fori_loop(0, zflag_ref.shape[0], wait, 0)

    def issue(g, carry):
        r0 = pl.multiple_of(g * ROW_TILE, ROW_TILE)
        for k in range(ROW_TILE):
            _row_copy(urow_ref.at[pl.ds(r0, ROW_TILE)], k, xs_hbm, pos_ref[base + r0 + k],
                      sem).start(priority=k % 2)
        return carry

    lax.fori_loop(0, DISPATCH_TD // ROW_TILE, issue, 0)
    pltpu.make_async_copy(urow_ref, xs_hbm.at[pl.ds(0, DISPATCH_TD)], sem).wait()


def _dispatch(pos, zflag, urow):
    n = urow.shape[0]
    nt = zflag.shape[0]
    return pl.pallas_call(
        _dispatch_body,
        grid_spec=pltpu.PrefetchScalarGridSpec(
            num_scalar_prefetch=2, grid=(n // DISPATCH_TD,),
            in_specs=[pl.BlockSpec((DISPATCH_TD,) + ROW_SHAPE, lambda i, pos, zflag: (i, 0, 0))],
            out_specs=pl.BlockSpec(memory_space=pl.ANY),
            scratch_shapes=[pltpu.VMEM((FFN_TM,) + ROW_SHAPE, jnp.int32), pltpu.SemaphoreType.DMA(()),
                            pltpu.SemaphoreType.DMA(())]),
        out_shape=jax.ShapeDtypeStruct((nt * FFN_TM,) + ROW_SHAPE, jnp.int32),
        compiler_params=_cp(("arbitrary",)),
        name="moe_dispatch",
    )(pos, zflag, urow)


def _ffn_sorted_body(ta_ref, tb_ref, na_ref, xs_ref, w1a, w3a, w2a, w1b, w3b, w2b, ys_ref):
    del ta_ref, tb_ref
    active = pl.program_id(0) < na_ref[0]

    @pl.when(active)
    def _():
        words = xs_ref[...].reshape(FFN_TM, D)
        pair = words[:, 0:HALF]
        x = jnp.concatenate([pltpu.bitcast(lax.shift_left(pair, 16), F32).astype(BF16),
                             pltpu.bitcast(pair & HI_MASK, F32).astype(BF16)], axis=1)
        ext = pltpu.bitcast(words[:, HALF:HALF + RLANES], F32)
        ha = jax.nn.silu(_dot(x, w1a[...])) * _dot(x, w3a[...]) * ext[:, 0:1]
        hb = jax.nn.silu(_dot(x, w1b[...])) * _dot(x, w3b[...]) * ext[:, 1:2]
        y = _dot(ha.astype(BF16), w2a[...]) + _dot(hb.astype(BF16), w2b[...])
        ys_ref[...] = y.reshape(FFN_TM, D // LANES, LANES)

    @pl.when(jnp.logical_not(active))
    def _():
        ys_ref[...] = jnp.zeros(ys_ref.shape, F32)


def _ffn_sorted(ta, tb, na, xs, w1, w3, w2):
    nt = ta.shape[0]
    ea = lambda t, ta, tb, na: (ta[t], 0, 0)
    eb = lambda t, ta, tb, na: (tb[t], 0, 0)
    rows = lambda t, ta, tb, na: (t, 0)
    up, down = (None, D, F), (None, F, D)
    return pl.pallas_call(
        _ffn_sorted_body,
        grid_spec=pltpu.PrefetchScalarGridSpec(
            num_scalar_prefetch=3, grid=(nt,),
            in_specs=[
                pl.BlockSpec((FFN_TM,) + ROW_SHAPE, lambda t, ta, tb, na: (t, 0, 0)),
                pl.BlockSpec(up, ea), pl.BlockSpec(up, ea), pl.BlockSpec(down, ea),
                pl.BlockSpec(up, eb), pl.BlockSpec(up, eb), pl.BlockSpec(down, eb),
            ],
            out_specs=pl.BlockSpec((FFN_TM, D // LANES, LANES), lambda t, ta, tb, na: (t, 0, 0))),
        out_shape=jax.ShapeDtypeStruct((nt * FFN_TM, D // LANES, LANES), F32),
        compiler_params=_cp(("arbitrary",)),
        name="moe_ffn",
    )(ta, tb, na, xs, w1, w3, w2, w1, w3, w2)


def _combine_body(pos_ref, x_ref, mod_ref, ys_hbm, g_ref, b_ref, xn_ref, buf, sem, *, tiles_per_seq):
    i = pl.program_id(0)
    nsteps = pl.num_programs(0)
    tc = x_ref.shape[0]

    def issue_group(step, slot, r0):
        group = buf.at[slot, pl.ds(r0, ROW_TILE)]
        for k in range(ROW_TILE):
            _row_copy(ys_hbm, pos_ref[step * tc + r0 + k], group, k, sem.at[slot]).start(priority=k % 2)

    def wait_tile(slot):
        pltpu.make_async_copy(ys_hbm.at[pl.ds(0, tc)], buf.at[slot], sem.at[slot]).wait()

    ahead = COMBINE_SLOTS - 1

    @pl.when(i == 0)
    def _():
        for t in range(ahead):
            def prime(g, carry, t=t):
                issue_group(t, t, pl.multiple_of(g * ROW_TILE, ROW_TILE))
                return carry
            lax.fori_loop(0, tc // ROW_TILE, prime, 0)

    slot = i % COMBINE_SLOTS
    wait_tile(slot)
    _, _, g1p = _split_mod(mod_ref, row=i // tiles_per_seq)
    gain, bias = g_ref[...], b_ref[...]
    nxt = jnp.minimum(i + ahead, nsteps - 1)
    nslot = (i + ahead) % COMBINE_SLOTS

    for c in range(tc // COMBINE_CHUNK):
        for g in range(COMBINE_CHUNK // ROW_TILE):
            issue_group(nxt, nslot, c * COMBINE_CHUNK + g * ROW_TILE)
        rows = pl.ds(c * COMBINE_CHUNK, COMBINE_CHUNK)
        z = ALPHA * x_ref[rows, :] + g1p * buf[slot, rows].reshape(COMBINE_CHUNK, D)
        xn_ref[rows, :] = _layer_norm(z, gain, bias)

    @pl.when(i == nsteps - 1)
    def _():
        for k in range(1, COMBINE_SLOTS):
            wait_tile((i + k) % COMBINE_SLOTS)


def _combine(pos, x, mod, ys, ln_g, ln_b, *, layer, seq):
    n = x.shape[0]
    tc = COMBINE_TC
    sub = 2 * layer + 1
    return pl.pallas_call(
        functools.partial(_combine_body, tiles_per_seq=seq // tc),
        grid_spec=pltpu.PrefetchScalarGridSpec(
            num_scalar_prefetch=1, grid=(n // tc,),
            in_specs=[
                pl.BlockSpec((tc, D), lambda i, pos: (i, 0)),
                _layer_spec((n // seq, 3 * D), sub),
                pl.BlockSpec(memory_space=pl.ANY),
                _layer_spec((1, D), sub), _layer_spec((1, D), sub),
            ],
            out_specs=pl.BlockSpec((tc, D), lambda i, pos: (i, 0)),
            scratch_shapes=[pltpu.VMEM((COMBINE_SLOTS, tc, D // LANES, LANES), F32),
                            pltpu.SemaphoreType.DMA((COMBINE_SLOTS,))]),
        out_shape=jax.ShapeDtypeStruct((n, D), F32),
        compiler_params=_cp(("arbitrary",)),
        name="moe_combine",
    )(pos, x, mod, ys, ln_g, ln_b)


def _moe_sparse(x, route, mod, w1, w3, w2, ln_g, ln_b, *, layer, seq):
    n = x.shape[0]
    urow, meta, cnt = route
    cls = meta[:, 2, :].reshape(n).astype(jnp.int32)
    rank = meta[:, 3, :].reshape(n).astype(jnp.int32)
    count = cnt[:NCLS, 0].astype(jnp.int32)
    ntile = (count + FFN_TM - 1) // FFN_TM
    tile_end = jnp.cumsum(ntile)
    row_start = (tile_end - ntile) * FFN_TM
    pos = rank + jnp.sum(jax.nn.one_hot(cls, NCLS, dtype=jnp.int32) * row_start[None, :], axis=1)
    nt = n // FFN_TM + NCLS
    total = tile_end[NCLS - 1]
    tr = jnp.minimum(jnp.arange(nt, dtype=jnp.int32), total - 1)
    tcls = jnp.sum((tr[:, None] >= tile_end[None, :]).astype(jnp.int32), axis=1)
    grp, pair = tcls // PAIRS, tcls % PAIRS
    lo = (pair >= 3).astype(jnp.int32) + (pair >= 5).astype(jnp.int32)
    hi = pair + 1 - lo * (2 * E_PER_G - 3 - lo) // 2
    ta = grp * E_PER_G + lo
    tb = grp * E_PER_G + hi
    t_all = jnp.arange(nt, dtype=jnp.int32)
    partial_last = ((t_all[:, None] == tile_end[None, :] - 1) & (count % FFN_TM != 0)[None, :]).any(axis=1)
    zflag = ((t_all >= total) | partial_last).astype(jnp.int32)
    xs = _dispatch(pos, zflag, urow)
    ys = _ffn_sorted(ta, tb, total[None], xs, w1, w3, w2)
    return _combine(pos, x, mod, ys, ln_g, ln_b, layer=layer, seq=seq)


def _moe_body(x_ref, mod_ref, wr_ref, br_ref, w1f_ref, w3f_ref, w2f_ref, g_ref, b_ref,
              xn_ref, w1_ref, w3_ref, w2_ref, ub, comb, acc, *, reps):
    e = pl.program_id(1)
    w1_ref[...] = w1f_ref[...].astype(BF16)
    w3_ref[...] = w3f_ref[...].astype(BF16)
    w2_ref[...] = w2f_ref[...].astype(BF16)

    @pl.when(e == 0)
    def _():
        shift, sc1p, _ = _split_mod(mod_ref, reps=reps)
        u = (x_ref[...] * sc1p + shift).astype(BF16)
        ub[...] = u
        eid, _, i1, i2, w1, w2 = _top2(_router_logits(u, wr_ref, br_ref))
        comb[...] = _rows_to_cols(jnp.where(eid == i1, w1, 0.0) + jnp.where(eid == i2, w2, 0.0))
        acc[...] = jnp.zeros(acc.shape, F32)

    u = ub[...]
    col = lax.broadcasted_iota(jnp.int32, comb.shape, 1)
    ce = jnp.sum(jnp.where(col == e, comb[...], 0.0), axis=-1, keepdims=True)
    h = jax.nn.silu(_dot(u, w1_ref[...])) * _dot(u, w3_ref[...]) * ce
    acc[...] += _dot(h.astype(BF16), w2_ref[...])

    @pl.when(e == N_EXP - 1)
    def _():
        _, _, g1p = _split_mod(mod_ref, reps=reps)
        xn_ref[...] = _layer_norm(ALPHA * x_ref[...] + g1p * acc[...], g_ref[...], b_ref[...])


def _moe_dense(x, mod, wr, br, w1, w3, w2, ln_g, ln_b, *, layer, reps):
    n = x.shape[0]
    sub = 2 * layer + 1
    expert = lambda i, e: (layer, e, 0, 0)
    return pl.pallas_call(
        functools.partial(_moe_body, reps=reps),
        grid=(1, N_EXP),
        in_specs=[
            pl.BlockSpec((n, D), lambda i, e: (0, 0)),
            _layer_spec((n // reps, 3 * D), sub),
            _layer_spec((RROWS, D), layer), _layer_spec((RROWS, RLANES), layer),
            pl.BlockSpec((None, None, D, F), expert),
            pl.BlockSpec((None, None, D, F), expert),
            pl.BlockSpec((None, None, F, D), expert),
            _layer_spec((1, D), sub), _layer_spec((1, D), sub),
        ],
        out_specs=[pl.BlockSpec((n, D), lambda i, e: (0, 0)),
                   pl.BlockSpec((None, D, F), lambda i, e: (e, 0, 0)),
                   pl.BlockSpec((None, D, F), lambda i, e: (e, 0, 0)),
                   pl.BlockSpec((None, F, D), lambda i, e: (e, 0, 0))],
        out_shape=[jax.ShapeDtypeStruct((n, D), F32),
                   jax.ShapeDtypeStruct((N_EXP, D, F), BF16),
                   jax.ShapeDtypeStruct((N_EXP, D, F), BF16),
                   jax.ShapeDtypeStruct((N_EXP, F, D), BF16)],
        scratch_shapes=[
            pltpu.VMEM((n, D), BF16),
            pltpu.VMEM((n, RLANES), F32),
            pltpu.VMEM((n, D), F32),
        ],
        compiler_params=_cp(("arbitrary", "arbitrary")),
        name="moe",
    )(x, mod, wr, br, w1, w3, w2, ln_g, ln_b)


def _kv_body(x_ref, w_ref, wvt_ref, k_ref, v_ref, k16_ref, vt16_ref):
    xb = x_ref[...].astype(BF16)
    kv = _dot(xb, w_ref[...])
    k_ref[...] = kv[:, :KVW]
    v_ref[...] = kv[:, KVW:]
    k16_ref[...] = kv[:, :KVW].astype(BF16)
    vt16_ref[...] = lax.dot_general(wvt_ref[...], xb, _NT, preferred_element_type=F32).astype(BF16)


def _kv(x, w, wvt, *, tm):
    n = x.shape[0]
    row = lambda i: (i, 0)
    return pl.pallas_call(
        _kv_body,
        grid=(n // tm,),
        in_specs=[pl.BlockSpec((tm, D), row), _const_spec((D, 2 * KVW)), _const_spec((KVW, D))],
        out_specs=[pl.BlockSpec((tm, KVW), row), pl.BlockSpec((tm, KVW), row),
                   pl.BlockSpec((tm, KVW), row), pl.BlockSpec((KVW, tm), lambda i: (0, i))],
        out_shape=[jax.ShapeDtypeStruct((n, KVW), F32), jax.ShapeDtypeStruct((n, KVW), F32),
                   jax.ShapeDtypeStruct((n, KVW), BF16), jax.ShapeDtypeStruct((KVW, n), BF16)],
        compiler_params=_cp(("arbitrary",)),
        name="kv_proj",
    )(x, w, wvt)


def _sink_attend(qh, kh, vh, bias, sink):
    s = lax.dot_general(qh, kh, (((1,), (1,)), ((), ())), preferred_element_type=F32) + bias
    m = jnp.maximum(jnp.max(s, axis=-1, keepdims=True), sink)
    p = jnp.exp(s - m)
    den = jnp.sum(p, axis=-1, keepdims=True) + jnp.exp(sink - m)
    return _dot(p.astype(BF16), vh) / den


def _attn_prompt_body(x_ref, mod_ref, wqt_ref, wo_ref, kc_ref, kp_ref, vtc_ref, vtp_ref,
                      bias_ref, sink_ref, g_ref, b_ref, mod1_ref, wr_ref, br_ref,
                      xn_ref, urow_ref, meta_ref, cnt_ref, carry, *, tq):
    t = pl.program_id(1)
    x = x_ref[...]
    shift, sc1p, g1p = _split_mod(mod_ref, row=pl.program_id(0))
    u = (x * sc1p + shift).astype(BF16)
    qt = lax.dot_general(wqt_ref[...], u, _NT, preferred_element_type=F32)
    qt = (qt * (HD ** -0.5)).astype(BF16)
    kc = kc_ref[...]
    vtc = vtc_ref[...]
    krow = lax.broadcasted_iota(jnp.int32, (2 * WIN, GRP * WIN), 0)
    first = jnp.where((krow < WIN) & (t == 0), NEG, 0.0).astype(F32)
    nwin = tq // WIN
    kks, vvts = [], []
    for j in range(nwin):
        if j == 0:
            kks.append(jnp.concatenate([kp_ref[...], kc[0:WIN]], axis=0))
            vvts.append(jnp.concatenate([vtp_ref[...], vtc[:, 0:WIN]], axis=1))
        else:
            kks.append(kc[(j - 1) * WIN:(j + 1) * WIN])
            vvts.append(vtc[:, (j - 1) * WIN:(j + 1) * WIN])

    def scores(j, hk):
        qht = jnp.concatenate(
            [qt[(hk * GRP + g) * HD:(hk * GRP + g + 1) * HD, j * WIN:(j + 1) * WIN] for g in range(GRP)],
            axis=1)
        st = _dot(kks[j][:, hk * HD:(hk + 1) * HD], qht) + bias_ref[hk]
        return st + first if j == 0 else st

    items = [(j, hk) for j in range(nwin) for hk in range(HKV)]
    heads = [[] for _ in range(nwin)]
    st_next = scores(*items[0])
    for n, (j, hk) in enumerate(items):
        st = st_next
        if n + 1 < len(items):
            st_next = scores(*items[n + 1])
        sink = sink_ref[hk]
        m = jnp.maximum(jnp.max(st, axis=0, keepdims=True), sink)
        pt = jnp.exp(st - m)
        den = jnp.sum(pt, axis=0, keepdims=True) + jnp.exp(sink - m)
        ot = _dot(vvts[j][hk * HD:(hk + 1) * HD, :], pt.astype(BF16)) * (1.0 / den)
        for g in range(GRP):
            heads[j].append(ot[:, g * WIN:(g + 1) * WIN])
    blocks = [jnp.concatenate(h, axis=0) for h in heads]
    oallt = jnp.concatenate(blocks, axis=1) if len(blocks) > 1 else blocks[0]
    mix = lax.dot_general(oallt.astype(BF16), wo_ref[...], _TN, preferred_element_type=F32)
    xn = _layer_norm(ALPHA * x + g1p * mix, g_ref[...], b_ref[...])
    xn_ref[...] = xn
    b = pl.program_id(0)
    _route_rows(xn, b, (b == 0) & (t == 0), mod1_ref, wr_ref, br_ref, urow_ref, meta_ref, cnt_ref, carry)


def _attn_prompt(x, mod, wqt, wo, k16, vt16, bias, sink, ln_g, ln_b, wr, br, *, layer, nb, seq, tq):
    nt = seq // tq
    r_in, r_out, r_shape, r_scratch = _route_specs(layer, nb, nt, tq)
    wpt = tq // WIN
    n = nb * seq
    j = layer - N_A
    sub = 2 * layer

    def prev(b, t):
        return jnp.maximum((b * nt + t) * wpt - 1, 0)

    cur = lambda b, t: (b * nt + t, 0)
    return pl.pallas_call(
        functools.partial(_attn_prompt_body, tq=tq),
        grid=(nb, nt),
        in_specs=[
            pl.BlockSpec((tq, D), cur),
            _layer_spec((nb, 3 * D), sub),
            _layer_spec((HQ * HD, D), j), _layer_spec((HQ * HD, D), j),
            pl.BlockSpec((tq, KVW), cur), pl.BlockSpec((WIN, KVW), lambda b, t: (prev(b, t), 0)),
            pl.BlockSpec((KVW, tq), lambda b, t: (0, b * nt + t)),
            pl.BlockSpec((KVW, WIN), lambda b, t: (0, prev(b, t))),
            _const_spec((HKV, 2 * WIN, GRP * WIN)), _layer_spec((HKV, 1, GRP * WIN), j),
            _layer_spec((1, D), sub), _layer_spec((1, D), sub),
        ] + r_in,
        out_specs=[pl.BlockSpec((tq, D), cur)] + r_out,
        out_shape=[jax.ShapeDtypeStruct((n, D), F32)] + r_shape,
        scratch_shapes=r_scratch,
        compiler_params=_cp(("arbitrary", "arbitrary")),
        name="attn_prompt",
    )(x, mod, wqt, wo, k16, k16, vt16, vt16, bias, sink, ln_g, ln_b, mod, wr, br)


def _attn_sample_body(x_ref, mod_ref, wq_ref, wo_ref, kb_ref, vb_ref, kn_ref, vn_ref,
                      bias_ref, sink_ref, g_ref, b_ref, xn_ref, *, ns, steps):
    nr = ns * steps
    x = x_ref[...].reshape(nr, D)
    shift, sc1p, g1p = _split_mod(mod_ref, reps=steps)
    u = (x * sc1p + shift).astype(BF16)
    q = (_dot(u, wq_ref[...]) * (HD ** -0.5)).astype(BF16)
    nbuf = kb_ref.shape[1]
    k_all = jnp.concatenate([kb_ref[...].reshape(ns * nbuf, KVW), kn_ref[...].reshape(nr, KVW)],
                            axis=0).astype(BF16)
    v_all = jnp.concatenate([vb_ref[...].reshape(ns * nbuf, KVW), vn_ref[...].reshape(nr, KVW)],
                            axis=0).astype(BF16)
    heads = [None] * HQ
    for hk in range(HKV):
        kcols = slice(hk * HD, (hk + 1) * HD)
        qh = jnp.concatenate(
            [q[:, (hk * GRP + g) * HD:(hk * GRP + g + 1) * HD] for g in range(GRP)], axis=0)
        o = _sink_attend(qh, k_all[:, kcols], v_all[:, kcols], bias_ref[hk], sink_ref[hk])
        for g in range(GRP):
            heads[hk * GRP + g] = o[g * nr:(g + 1) * nr]
    o_all = jnp.concatenate(heads, axis=1)
    mix = _dot(o_all.astype(BF16), wo_ref[...])
    xn_ref[...] = _layer_norm(ALPHA * x + g1p * mix, g_ref[...], b_ref[...]).reshape(steps, ns, D)


def _attn_sample(x, mod, wq, wo, kbuf, vbuf, kn, vn, bias, sink, ln_g, ln_b, *, layer, ns):
    steps, nb, _ = x.shape
    nbuf = kbuf.shape[1]
    j = layer - N_A
    sub = 2 * layer
    seqs = lambda i: (0, i, 0)
    return pl.pallas_call(
        functools.partial(_attn_sample_body, ns=ns, steps=steps),
        grid=(nb // ns,),
        in_specs=[
            pl.BlockSpec((steps, ns, D), seqs),
            pl.BlockSpec((None, ns, 3 * D), lambda i: (sub, i, 0)),
            _layer_spec((D, HQ * HD), j), _layer_spec((HQ * HD, D), j),
            pl.BlockSpec((ns, nbuf, KVW), lambda i: (i, 0, 0)),
            pl.BlockSpec((ns, nbuf, KVW), lambda i: (i, 0, 0)),
            pl.BlockSpec((steps, ns, KVW), seqs), pl.BlockSpec((steps, ns, KVW), seqs),
            _const_spec(bias.shape), _layer_spec(sink.shape[1:], j),
            _layer_spec((1, D), sub), _layer_spec((1, D), sub),
        ],
        out_specs=pl.BlockSpec((steps, ns, D), seqs),
        out_shape=jax.ShapeDtypeStruct((steps, nb, D), F32),
        compiler_params=_cp(("arbitrary",)),
        name="attn_sample",
    )(x, mod, wq, wo, kbuf, vbuf, kn, vn, bias, sink, ln_g, ln_b)


def _t5_bucket(d):
    max_exact = N_BUCKETS // 2
    d = np.maximum(d, 0)
    log_ratio = (np.log(np.maximum(d, 1).astype(np.float32) / np.float32(max_exact))
                 / np.float32(math.log(WIN / max_exact)))
    large = max_exact + (log_ratio * (N_BUCKETS - max_exact)).astype(np.int32)
    large = np.minimum(large, N_BUCKETS - 1)
    return np.where(d < max_exact, d, large)


def _bias_table(d, ok, rel_bias):
    sel = np.where(ok, _t5_bucket(d), N_BUCKETS)
    onehot = (sel[..., None] == np.arange(N_BUCKETS + 1)).astype(np.float32)
    tab = jnp.concatenate([rel_bias.astype(F32), jnp.full((1, HQ), NEG, F32)], axis=0)
    return jnp.einsum('abd,dh->hab', jnp.asarray(onehot), tab, precision=lax.Precision.HIGHEST)


def _prompt_bias(rel_bias):
    d = np.arange(WIN)[None, :] + WIN - np.arange(2 * WIN)[:, None]
    b = _bias_table(d, (d >= 0) & (d < WIN), rel_bias).reshape(HKV, GRP, 2 * WIN, WIN)
    return jnp.transpose(b, (0, 2, 1, 3)).reshape(HKV, 2 * WIN, GRP * WIN)


def _sample_bias(rel_bias, ns, steps, nbuf):
    qs = np.tile(np.arange(ns), steps)
    qt = np.repeat(np.arange(steps), ns)
    ks = np.concatenate([np.repeat(np.arange(ns), nbuf), qs])
    kpos = np.concatenate([np.tile(np.arange(nbuf), ns), nbuf + qt])
    d = qt[:, None] + nbuf - kpos[None, :]
    ok = (d >= 0) & (d < WIN) & (qs[:, None] == ks[None, :])
    return _bias_table(d, ok, rel_bias).reshape(HKV, GRP * ns * steps, kpos.shape[0])


LRU_TM = 512
ATTN_TQ = 1024
SAMPLE_NS = 8


def kernel(x_prompt, x_sample, state_rnn_h, state_rnn_conv, cache_win_k, cache_win_v, c_prompt, c_sample, ada_w, ada_b, ln_g, ln_b, lru_w_in, lru_conv_w, lru_conv_b, lru_w_a, lru_b_a, lru_w_x, lru_b_x, lru_lambda, lru_w_out, kv_w, attn_w_q, attn_sinks, attn_w_o, rel_bias, moe_w_group, moe_b_group, moe_w_router, moe_b_router, moe_w1, moe_w3, moe_w2):
    nbp, seq, _ = x_prompt.shape
    nbs, steps, _ = x_sample.shape
    nbuf = cache_win_k.shape[1]
    npt = nbp * seq
    nst = nbs * steps

    mod_p, mod_s = _adaln(c_prompt, c_sample, ada_w, ada_b)

    ln_g3 = ln_g.reshape(2 * DEPTH, 1, D)
    ln_b3 = ln_b.reshape(2 * DEPTH, 1, D)
    lw = (lru_w_in.astype(BF16), lru_conv_w, lru_conv_b[:, None], lru_w_a.astype(BF16),
          lru_w_x.astype(BF16), lru_b_a[:, None], lru_b_x[:, None], lru_lambda[:, None],
          lru_w_out.astype(BF16))
    zrow = lambda k: jnp.zeros((DEPTH, k, D), F32)
    wr = jnp.concatenate([jnp.swapaxes(moe_w_group, 1, 2), zrow(EROW - N_GROUPS),
                          jnp.swapaxes(moe_w_router.reshape(DEPTH, D, N_EXP), 1, 2),
                          zrow(RROWS - EROW - N_EXP)], axis=1).astype(BF16)
    zb = lambda k: jnp.zeros((DEPTH, k), F32)
    br = jnp.concatenate([moe_b_group, zb(EROW - N_GROUPS), moe_b_router.reshape(DEPTH, N_EXP),
                          zb(RROWS - EROW - N_EXP)], axis=1)
    br = jnp.broadcast_to(br[:, :, None], (DEPTH, RROWS, RLANES))
    wq = attn_w_q.astype(BF16)
    wqt = jnp.swapaxes(wq, 1, 2)
    wo = attn_w_o.astype(BF16)
    kvw = kv_w.astype(BF16)
    kvwt = kvw[:, KVW:].T
    bias_p = _prompt_bias(rel_bias)
    bias_s = _sample_bias(rel_bias, SAMPLE_NS, steps, nbuf)
    sink_p = jnp.repeat(attn_sinks.reshape(N_B, HKV, GRP), WIN, axis=2)[:, :, None, :]
    sink_s = jnp.repeat(attn_sinks.reshape(N_B, HKV, GRP), SAMPLE_NS * steps, axis=2)[..., None]
    kbuf = cache_win_k.reshape(nbs, nbuf, KVW)
    vbuf = cache_win_v.reshape(nbs, nbuf, KVW)

    xp = x_prompt.reshape(npt, D)
    xs = jnp.transpose(x_sample, (1, 0, 2)).reshape(nst, D)
    c0 = jnp.transpose(state_rnn_conv, (0, 2, 1, 3)).reshape(N_A, (CONV_W - 1) * nbs, R)
    hp, cp_, hs, cs = [], [], [], []
    kp = vp = kn = vn = kp16 = vpt16 = None
    for l in range(DEPTH):
        if l < N_A:
            xp, h_l, c_l, *route = _lru_prompt(xp, mod_p, lw, ln_g3, ln_b3, wr, br,
                                               layer=l, nb=nbp, seq=seq, tm=LRU_TM)
            hp.append(h_l[:, 0])
            cp_.append(c_l)
            xs, h_l, c_l = _lru_sample(xs, mod_s, state_rnn_h, c0[l], lw, ln_g3, ln_b3,
                                       layer=l, nb=nbs, steps=steps)
            hs.append(h_l)
            cs.append(jnp.transpose(c_l.reshape(CONV_W - 1, nbs, R), (1, 0, 2)))
        else:
            xp, *route = _attn_prompt(xp, mod_p, wqt, wo, kp16, vpt16, bias_p, sink_p, ln_g3, ln_b3, wr, br,
                                      layer=l, nb=nbp, seq=seq, tq=ATTN_TQ)
            xs = _attn_sample(xs.reshape(steps, nbs, D), mod_s, wq, wo, kbuf, vbuf,
                              kn.reshape(steps, nbs, KVW), vn.reshape(steps, nbs, KVW),
                              bias_s, sink_s, ln_g3, ln_b3, layer=l, ns=SAMPLE_NS).reshape(nst, D)
        xs, w1, w3, w2 = _moe_dense(xs, mod_s, wr, br, moe_w1, moe_w3, moe_w2, ln_g3, ln_b3,
                                    layer=l, reps=steps)
        xp = _moe_sparse(xp, route, mod_p, w1, w3, w2, ln_g3, ln_b3, layer=l, seq=seq)
        if l == N_A - 1:
            kp, vp, kp16, vpt16 = _kv(xp, kvw, kvwt, tm=1024)
            kn, vn, _, _ = _kv(xs, kvw, kvwt, tm=nst)

    lp = min(WIN, seq)
    k_p = kp.reshape(nbp, seq, HKV, HD)[:, seq - lp:]
    v_p = vp.reshape(nbp, seq, HKV, HD)[:, seq - lp:]
    kn4 = jnp.transpose(kn.reshape(steps, nbs, HKV, HD), (1, 0, 2, 3))
    vn4 = jnp.transpose(vn.reshape(steps, nbs, HKV, HD), (1, 0, 2, 3))
    k_s = jnp.concatenate([cache_win_k, kn4], axis=1)[:, -nbuf:]
    v_s = jnp.concatenate([cache_win_v, vn4], axis=1)[:, -nbuf:]
    y_s = jnp.transpose(xs.reshape(steps, nbs, D), (1, 0, 2))
    return (xp.reshape(nbp, seq, D), y_s,
            jnp.stack(hp), jnp.stack(cp_), k_p, v_p,
            jnp.stack(hs), jnp.stack(cs), k_s, v_s)
```

```python
import functools
import math

import jax
import jax.numpy as jnp
import numpy as np
from jax import lax
from jax.experimental import pallas as pl
from jax.experimental.pallas import tpu as pltpu

D = 1024
R = 1024
DEPTH = 4
N_A = 2
N_B = DEPTH - N_A
N_BLK = 4
BLK = R // N_BLK
CONV_W = 4
RG_C = 8.0
HQ = 16
HKV = 4
HD = 64
GRP = HQ // HKV
KVW = HKV * HD
WIN = 128
N_BUCKETS = 32
N_GROUPS = 4
E_PER_G = 4
N_EXP = 16
F = 512
ALPHA = (2.0 * DEPTH) ** 0.25
LN_EPS = 1e-5
NEG = -1e30
LANES = 128
RLANES = LANES
ROW_TILE = 8

F32 = jnp.float32
BF16 = jnp.bfloat16

VMEM_LIMIT = 56 * 1024 * 1024


def _cp(sem):
    return pltpu.CompilerParams(dimension_semantics=sem, vmem_limit_bytes=VMEM_LIMIT)


def _dot(a, b):
    return jnp.dot(a, b, preferred_element_type=F32)


def _layer_norm(z, g, b):
    mu = jnp.mean(z, axis=-1, keepdims=True)
    zc = z - mu
    var = jnp.mean(zc * zc, axis=-1, keepdims=True)
    return zc * lax.rsqrt(var + LN_EPS) * g + b


def _split_mod(mod_ref, row=None, reps=1):
    m = mod_ref[...] if row is None else mod_ref[pl.ds(row, 1), :]
    if reps > 1:
        m = jnp.concatenate([m] * reps, axis=0)
    return m[:, 0:D], m[:, D:2 * D], m[:, 2 * D:3 * D]


def _layer_spec(shape, layer):
    nd = len(shape)
    return pl.BlockSpec((None,) + tuple(shape), lambda *_: (layer,) + (0,) * nd)


def _adaln_body(cp_ref, cs_ref, w_ref, b_ref, op_ref, os_ref):
    one = jnp.where(pl.program_id(1) > 0, 1.0, 0.0).astype(F32)
    w = w_ref[...].astype(BF16)
    op_ref[...] = _dot(cp_ref[...].astype(BF16), w) + b_ref[...] + one
    os_ref[...] = _dot(cs_ref[...].astype(BF16), w) + b_ref[...] + one


def _adaln(c_p, c_s, ada_w, ada_b):
    np_, ns_ = c_p.shape[0], c_s.shape[0]
    w = ada_w.reshape(2 * DEPTH, D, 3 * D)
    b = ada_b.reshape(2 * DEPTH, 1, 3 * D)
    return pl.pallas_call(
        _adaln_body,
        grid=(2 * DEPTH, 3),
        in_specs=[
            pl.BlockSpec((np_, D), lambda l, j: (0, 0)),
            pl.BlockSpec((ns_, D), lambda l, j: (0, 0)),
            pl.BlockSpec((None, D, D), lambda l, j: (l, 0, j)),
            pl.BlockSpec((None, 1, D), lambda l, j: (l, 0, j)),
        ],
        out_specs=[pl.BlockSpec((None, np_, D), lambda l, j: (l, 0, j)),
                   pl.BlockSpec((None, ns_, D), lambda l, j: (l, 0, j))],
        out_shape=[jax.ShapeDtypeStruct((2 * DEPTH, np_, 3 * D), F32),
                   jax.ShapeDtypeStruct((2 * DEPTH, ns_, 3 * D), F32)],
        compiler_params=_cp(("arbitrary", "arbitrary")),
        name="adaln",
    )(c_p, c_s, w, b)


def _lru_gates(xc, n, wa_ref, wx_ref, ba_ref, bx_ref, lam_ref):
    cols = slice(n * BLK, (n + 1) * BLK)
    xb = xc.astype(BF16)
    r = jax.nn.sigmoid(_dot(xb, wa_ref[n]) + ba_ref[:, cols])
    i = jax.nn.sigmoid(_dot(xb, wx_ref[n]) + bx_ref[:, cols])
    z = -lam_ref[:, cols]
    softplus = jnp.maximum(z, 0.0) + jnp.log1p(jnp.exp(-jnp.abs(z)))
    log_a = (-RG_C) * r * softplus
    a = jnp.exp(log_a)
    v = 1.0 - a * a
    b = jnp.where(v > 0.0, v * lax.rsqrt(v), 0.0) * i * xc
    return a, b


def _lru_prompt_body(x_ref, mod_ref, win_ref, cw_ref, cb_ref, wa_ref, wx_ref, ba_ref, bx_ref,
                     lam_ref, wout_ref, g_ref, b_ref, mod1_ref, wr_ref, br_ref,
                     xn_ref, hl_ref, cl_ref, urow_ref, meta_ref, cnt_ref,
                     xbuf, abuf, hbuf, hc, carry, *, tm):
    t = pl.program_id(1)

    @pl.when(t == 0)
    def _():
        xbuf[0:ROW_TILE, :] = jnp.zeros((ROW_TILE, R), F32)
        hc[...] = jnp.zeros((1, R), F32)

    x = x_ref[...]
    shift, sc1p, g1p = _split_mod(mod_ref, row=pl.program_id(0))
    u = x * sc1p + shift
    yx = _dot(u.astype(BF16), win_ref[...])
    y = jax.nn.gelu(yx[:, :R])
    xbuf[ROW_TILE:ROW_TILE + tm, :] = yx[:, R:]

    ngrp = tm // ROW_TILE
    rowmod = lax.broadcasted_iota(jnp.int32, (ngrp, ROW_TILE, BLK), 1)
    for n in range(N_BLK):
        cols = slice(n * BLK, (n + 1) * BLK)
        xc = cb_ref[:, cols] + cw_ref[3:4, cols] * xbuf[ROW_TILE:ROW_TILE + tm, cols]
        for k in range(1, CONV_W):
            xc = xc + cw_ref[3 - k:4 - k, cols] * xbuf[ROW_TILE - k:ROW_TILE - k + tm, cols]
        a, b = _lru_gates(xc, n, wa_ref, wx_ref, ba_ref, bx_ref, lam_ref)
        a = a.reshape(ngrp, ROW_TILE, BLK)
        b = b.reshape(ngrp, ROW_TILE, BLK)
        s = 1
        while s < ROW_TILE:
            a_sh = pltpu.roll(a, s, 1)
            b_sh = pltpu.roll(b, s, 1)
            m = rowmod >= s
            b = jnp.where(m, a * b_sh, 0.0) + b
            a = jnp.where(m, a * a_sh, a)
            s *= 2
        abuf[:, cols] = a.reshape(tm, BLK)
        hbuf[:, cols] = b.reshape(tm, BLK)

    def group(g, h):
        r0 = pl.multiple_of(g * ROW_TILE, ROW_TILE)
        hg = abuf[pl.ds(r0, ROW_TILE), :] * h + hbuf[pl.ds(r0, ROW_TILE), :]
        hbuf[pl.ds(r0, ROW_TILE), :] = hg
        return hg[ROW_TILE - 1:ROW_TILE, :]

    h_end = lax.fori_loop(0, tm // ROW_TILE, group, hc[...])
    hc[...] = h_end
    hl_ref[...] = h_end
    tail = xbuf[tm + ROW_TILE - (CONV_W - 1):tm + ROW_TILE, :]
    cl_ref[...] = tail
    xbuf[ROW_TILE - (CONV_W - 1):ROW_TILE, :] = tail

    mix = _dot((y * hbuf[...]).astype(BF16), wout_ref[...])
    xn = _layer_norm(ALPHA * x + g1p * mix, g_ref[...], b_ref[...])
    xn_ref[...] = xn
    b = pl.program_id(0)
    _route_rows(xn, b, (b == 0) & (t == 0), mod1_ref, wr_ref, br_ref, urow_ref, meta_ref, cnt_ref, carry)


def _const_spec(shape):
    nd = len(shape)
    return pl.BlockSpec(shape, lambda *_: (0,) * nd)


def _lru_weight_specs(layer):
    sub = 2 * layer
    return [
        _layer_spec((D, 2 * R), layer), _layer_spec((CONV_W, R), layer), _layer_spec((1, R), layer),
        _layer_spec((N_BLK, BLK, BLK), layer), _layer_spec((N_BLK, BLK, BLK), layer),
        _layer_spec((1, R), layer), _layer_spec((1, R), layer), _layer_spec((1, R), layer),
        _layer_spec((R, D), layer), _layer_spec((1, D), sub), _layer_spec((1, D), sub),
    ]


def _lru_prompt(x, mod, lw, ln_g, ln_b, wr, br, *, layer, nb, seq, tm):
    nt = seq // tm
    n = nb * seq
    r_in, r_out, r_shape, r_scratch = _route_specs(layer, nb, nt, tm)
    return pl.pallas_call(
        functools.partial(_lru_prompt_body, tm=tm),
        grid=(nb, nt),
        in_specs=[
            pl.BlockSpec((tm, D), lambda b, t: (b * nt + t, 0)),
            _layer_spec((nb, 3 * D), 2 * layer),
        ] + _lru_weight_specs(layer) + r_in,
        out_specs=[
            pl.BlockSpec((tm, D), lambda b, t: (b * nt + t, 0)),
            pl.BlockSpec((None, 1, R), lambda b, t: (b, 0, 0)),
            pl.BlockSpec((None, CONV_W - 1, R), lambda b, t: (b, 0, 0)),
        ] + r_out,
        out_shape=[
            jax.ShapeDtypeStruct((n, D), F32),
            jax.ShapeDtypeStruct((nb, 1, R), F32),
            jax.ShapeDtypeStruct((nb, CONV_W - 1, R), F32),
        ] + r_shape,
        scratch_shapes=[
            pltpu.VMEM((tm + ROW_TILE, R), F32),
            pltpu.VMEM((tm, R), F32),
            pltpu.VMEM((tm, R), F32),
            pltpu.VMEM((1, R), F32),
        ] + r_scratch,
        compiler_params=_cp(("arbitrary", "arbitrary")),
        name="lru_prompt",
    )(x, mod, *lw, ln_g, ln_b, mod, wr, br)


def _lru_sample_body(x_ref, mod_ref, h0_ref, c0_ref, win_ref, cw_ref, cb_ref, wa_ref, wx_ref,
                     ba_ref, bx_ref, lam_ref, wout_ref, g_ref, b_ref,
                     xn_ref, hl_ref, cl_ref, hbuf, *, nb, steps):
    x = x_ref[...]
    shift, sc1p, g1p = _split_mod(mod_ref, reps=steps)
    u = x * sc1p + shift
    yx = _dot(u.astype(BF16), win_ref[...])
    y = jax.nn.gelu(yx[:, :R])
    xext = jnp.concatenate([c0_ref[...], yx[:, R:]], axis=0)
    cl_ref[...] = xext[steps * nb:, :]
    for n in range(N_BLK):
        cols = slice(n * BLK, (n + 1) * BLK)
        xc = cb_ref[:, cols] + cw_ref[0:1, cols] * xext[0:steps * nb, cols]
        for k in range(1, CONV_W):
            xc = xc + cw_ref[k:k + 1, cols] * xext[k * nb:(k + steps) * nb, cols]
        a, b = _lru_gates(xc, n, wa_ref, wx_ref, ba_ref, bx_ref, lam_ref)
        h = h0_ref[:, cols]
        for t in range(steps):
            h = a[t * nb:(t + 1) * nb, :] * h + b[t * nb:(t + 1) * nb, :]
            hbuf[t * nb:(t + 1) * nb, cols] = h
        hl_ref[:, cols] = h
    mix = _dot((y * hbuf[...]).astype(BF16), wout_ref[...])
    xn_ref[...] = _layer_norm(ALPHA * x + g1p * mix, g_ref[...], b_ref[...])


def _lru_sample(x, mod, h0, c0, lw, ln_g, ln_b, *, layer, nb, steps):
    n = nb * steps
    return pl.pallas_call(
        functools.partial(_lru_sample_body, nb=nb, steps=steps),
        grid=(1,),
        in_specs=[
            _const_spec((n, D)), _layer_spec((nb, 3 * D), 2 * layer), _layer_spec((nb, R), layer),
            _const_spec(((CONV_W - 1) * nb, R)),
        ] + _lru_weight_specs(layer),
        out_specs=[_const_spec((n, D)), _const_spec((nb, R)), _const_spec(((CONV_W - 1) * nb, R))],
        out_shape=[
            jax.ShapeDtypeStruct((n, D), F32),
            jax.ShapeDtypeStruct((nb, R), F32),
            jax.ShapeDtypeStruct(((CONV_W - 1) * nb, R), F32),
        ],
        scratch_shapes=[pltpu.VMEM((n, R), F32)],
        compiler_params=_cp(("arbitrary",)),
        name="lru_sample",
    )(x, mod, h0, c0, *lw, ln_g, ln_b)


RROWS = 32
EROW = 8
_NT = (((1,), (1,)), ((), ()))
_TN = (((0,), (0,)), ((), ()))


def _top2(lt):
    nt = lt.shape[1]
    gl = lt[0:EROW, :]
    gid = lax.broadcasted_iota(jnp.int32, (EROW, nt), 0).astype(F32)
    is_g = gid < N_GROUPS
    gl = jnp.where(is_g, gl, NEG)
    gmax = jnp.max(gl, axis=0, keepdims=True)
    gidx = jnp.min(jnp.where(gl == gmax, gid, float(EROW)), axis=0, keepdims=True)
    gsum = jnp.sum(jnp.where(is_g, jnp.exp(gl - gmax), 0.0), axis=0, keepdims=True)
    g_p = 1.0 / gsum
    el = lt[EROW:EROW + N_EXP, :]
    eid = lax.broadcasted_iota(jnp.int32, (N_EXP, nt), 0).astype(F32)
    first = gidx * E_PER_G
    in_grp = (eid >= first) & (eid < first + E_PER_G)
    el = jnp.where(in_grp, el, NEG)
    big = float(N_EXP)
    t1 = jnp.max(el, axis=0, keepdims=True)
    i1 = jnp.min(jnp.where(el == t1, eid, big), axis=0, keepdims=True)
    el2 = jnp.where(eid == i1, NEG, el)
    t2 = jnp.max(el2, axis=0, keepdims=True)
    i2 = jnp.min(jnp.where((el2 == t2) & in_grp & (eid != i1), eid, big), axis=0, keepdims=True)
    e21 = jnp.exp(t2 - t1)
    w1 = g_p / (1.0 + e21)
    w2 = w1 * e21
    return eid, gidx, i1, i2, w1, w2


def _router_logits(u16, wrt_ref, brt_ref):
    return lax.dot_general(wrt_ref[...], u16, _NT, preferred_element_type=F32) + brt_ref[:, 0:1]


def _rows_to_cols(rows):
    r, nt = rows.shape
    padded = jnp.concatenate([rows, jnp.zeros((RLANES - r, nt), F32)], axis=0)
    return jnp.transpose(padded)


PAIRS = E_PER_G * (E_PER_G - 1) // 2
NCLS = N_GROUPS * PAIRS
HALF = D // 2
ROW_SHAPE = (D // LANES, LANES)
HI_MASK = -65536
FFN_TM = 256
DISPATCH_TD = 2048
COMBINE_TC = 512
COMBINE_CHUNK = 64
COMBINE_SLOTS = 3


def _route_rows(x, row, first, mod_ref, wr_ref, br_ref, urow_ref, meta_ref, cnt_ref, carry):
    @pl.when(first)
    def _():
        carry[...] = jnp.zeros(carry.shape, F32)

    tm = x.shape[0]
    shift, sc1p, _ = _split_mod(mod_ref, row=row)
    u = x * sc1p + shift
    _, gidx, i1, i2, w1, w2 = _top2(_router_logits(u.astype(BF16), wr_ref, br_ref))
    first = gidx * E_PER_G
    lo = jnp.minimum(i1, i2) - first
    hi = jnp.maximum(i1, i2) - first
    w_lo = jnp.where(i1 < i2, w1, w2)
    w_hi = jnp.where(i1 < i2, w2, w1)
    cls = gidx * PAIRS + lo * (2 * E_PER_G - 1 - lo) * 0.5 + (hi - lo - 1.0)
    cid = lax.broadcasted_iota(jnp.int32, (RROWS, tm), 0).astype(F32)
    onehot = (cid == cls).astype(F32)
    r = lax.broadcasted_iota(jnp.int32, (tm, tm), 0)
    c = lax.broadcasted_iota(jnp.int32, (tm, tm), 1)
    earlier = _dot(onehot.astype(BF16), (r < c).astype(BF16)) + carry[:, 0:1]
    rank = jnp.sum(onehot * earlier, axis=0, keepdims=True)
    carry[...] += jnp.sum(onehot, axis=1, keepdims=True)
    cnt_ref[...] = carry[...]
    meta = jnp.concatenate([w_lo, w_hi, cls, rank, jnp.zeros((ROW_TILE - 4, tm), F32)], axis=0)
    meta_ref[...] = meta
    ub = u.astype(BF16)
    lo = lax.shift_right_logical(pltpu.bitcast(ub[:, 0:HALF].astype(F32), jnp.int32), 16)
    hi = pltpu.bitcast(ub[:, HALF:D].astype(F32), jnp.int32) & HI_MASK
    ext = pltpu.bitcast(_rows_to_cols(meta), jnp.int32)
    row = jnp.concatenate([hi | lo, ext, jnp.zeros((tm, D - HALF - RLANES), jnp.int32)], axis=1)
    urow_ref[...] = row.reshape((tm,) + ROW_SHAPE)


def _route_specs(layer, nb, nt, tm):
    tile = lambda b, t: b * nt + t
    n = nb * nt * tm
    in_specs = [_layer_spec((nb, 3 * D), 2 * layer + 1),
                _layer_spec((RROWS, D), layer), _layer_spec((RROWS, RLANES), layer)]
    out_specs = [pl.BlockSpec((tm,) + ROW_SHAPE, lambda b, t: (tile(b, t), 0, 0)),
                 pl.BlockSpec((None, ROW_TILE, tm), lambda b, t: (tile(b, t), 0, 0)),
                 _const_spec((RROWS, RLANES))]
    out_shape = [jax.ShapeDtypeStruct((n,) + ROW_SHAPE, jnp.int32),
                 jax.ShapeDtypeStruct((nb * nt, ROW_TILE, tm), F32),
                 jax.ShapeDtypeStruct((RROWS, RLANES), F32)]
    return in_specs, out_specs, out_shape, [pltpu.VMEM((RROWS, RLANES), F32)]


def _row_copy(src, src_row, dst, dst_row, sem):
    return pltpu.make_async_copy(src.at[pl.ds(src_row, 1)], dst.at[pl.ds(dst_row, 1)], sem)


def _dispatch_body(pos_ref, zflag_ref, urow_ref, xs_hbm, zbuf, sem, zsem):
    base = pl.program_id(0) * DISPATCH_TD

    @pl.when(pl.program_id(0) == 0)
    def _():
        zbuf[...] = jnp.zeros(zbuf.shape, jnp.int32)

        def zero_copy(t):
            return pltpu.make_async_copy(zbuf, xs_hbm.at[pl.ds(t * FFN_TM, FFN_TM)], zsem)

        def start(t, carry):
            @pl.when(zflag_ref[t] != 0)
            def _():
                zero_copy(t).start()
            return carry

        def wait(t, carry):
            @pl.when(zflag_ref[t] != 0)
            def _():
                zero_copy(t).wait()
            return carry

        lax.fori_loop(0, zflag_ref.shape[0], start, 0)
        lax.fori_loop(0, zflag_ref.shape[0], wait, 0)

    def issue(g, carry):
        r0 = pl.multiple_of(g * ROW_TILE, ROW_TILE)
        for k in range(ROW_TILE):
            _row_copy(urow_ref.at[pl.ds(r0, ROW_TILE)], k, xs_hbm, pos_ref[base + r0 + k],
                      sem).start(priority=k % 2)
        return carry

    lax.fori_loop(0, DISPATCH_TD // ROW_TILE, issue, 0)
    pltpu.make_async_copy(urow_ref, xs_hbm.at[pl.ds(0, DISPATCH_TD)], sem).wait()


def _dispatch(pos, zflag, urow):
    n = urow.shape[0]
    nt = zflag.shape[0]
    return pl.pallas_call(
        _dispatch_body,
        grid_spec=pltpu.PrefetchScalarGridSpec(
            num_scalar_prefetch=2, grid=(n // DISPATCH_TD,),
            in_specs=[pl.BlockSpec((DISPATCH_TD,) + ROW_SHAPE, lambda i, pos, zflag: (i, 0, 0))],
            out_specs=pl.BlockSpec(memory_space=pl.ANY),
            scratch_shapes=[pltpu.VMEM((FFN_TM,) + ROW_SHAPE, jnp.int32), pltpu.SemaphoreType.DMA(()),
                            pltpu.SemaphoreType.DMA(())]),
        out_shape=jax.ShapeDtypeStruct((nt * FFN_TM,) + ROW_SHAPE, jnp.int32),
        compiler_params=_cp(("arbitrary",)),
        name="moe_dispatch",
    )(pos, zflag, urow)


def _ffn_sorted_body(ta_ref, tb_ref, na_ref, xs_ref, w1a, w3a, w2a, w1b, w3b, w2b, ys_ref):
    del ta_ref, tb_ref
    active = pl.program_id(0) < na_ref[0]

    @pl.when(active)
    def _():
        words = xs_ref[...].reshape(FFN_TM, D)
        pair = words[:, 0:HALF]
        x = jnp.concatenate([pltpu.bitcast(lax.shift_left(pair, 16), F32).astype(BF16),
                             pltpu.bitcast(pair & HI_MASK, F32).astype(BF16)], axis=1)
        ext = pltpu.bitcast(words[:, HALF:HALF + RLANES], F32)
        ha = jax.nn.silu(_dot(x, w1a[...])) * _dot(x, w3a[...]) * ext[:, 0:1]
        hb = jax.nn.silu(_dot(x, w1b[...])) * _dot(x, w3b[...]) * ext[:, 1:2]
        y = _dot(ha.astype(BF16), w2a[...]) + _dot(hb.astype(BF16), w2b[...])
        ys_ref[...] = y.reshape(FFN_TM, D // LANES, LANES)

    @pl.when(jnp.logical_not(active))
    def _():
        ys_ref[...] = jnp.zeros(ys_ref.shape, F32)


def _ffn_sorted(ta, tb, na, xs, w1, w3, w2):
    nt = ta.shape[0]
    ea = lambda t, ta, tb, na: (ta[t], 0, 0)
    eb = lambda t, ta, tb, na: (tb[t], 0, 0)
    rows = lambda t, ta, tb, na: (t, 0)
    up, down = (None, D, F), (None, F, D)
    return pl.pallas_call(
        _ffn_sorted_body,
        grid_spec=pltpu.PrefetchScalarGridSpec(
            num_scalar_prefetch=3, grid=(nt,),
            in_specs=[
                pl.BlockSpec((FFN_TM,) + ROW_SHAPE, lambda t, ta, tb, na: (t, 0, 0)),
                pl.BlockSpec(up, ea), pl.BlockSpec(up, ea), pl.BlockSpec(down, ea),
                pl.BlockSpec(up, eb), pl.BlockSpec(up, eb), pl.BlockSpec(down, eb),
            ],
            out_specs=pl.BlockSpec((FFN_TM, D // LANES, LANES), lambda t, ta, tb, na: (t, 0, 0))),
        out_shape=jax.ShapeDtypeStruct((nt * FFN_TM, D // LANES, LANES), F32),
        compiler_params=_cp(("arbitrary",)),
        name="moe_ffn",
    )(ta, tb, na, xs, w1, w3, w2, w1, w3, w2)


def _combine_body(pos_ref, x_ref, mod_ref, ys_hbm, g_ref, b_ref, *rest, tiles_per_seq, with_kv):
    if with_kv:
        kvw_ref, kvwt_ref, xn_ref, *kv_out, buf, sem = rest
    else:
        xn_ref, buf, sem = rest
    i = pl.program_id(0)
    nsteps = pl.num_programs(0)
    tc = x_ref.shape[0]

    def issue_group(step, slot, r0):
        group = buf.at[slot, pl.ds(r0, ROW_TILE)]
        for k in range(ROW_TILE):
            _row_copy(ys_hbm, pos_ref[step * tc + r0 + k], group, k, sem.at[slot]).start(priority=k % 2)

    def wait_tile(slot):
        pltpu.make_async_copy(ys_hbm.at[pl.ds(0, tc)], buf.at[slot], sem.at[slot]).wait()

    ahead = COMBINE_SLOTS - 1

    @pl.when(i == 0)
    def _():
        for t in range(ahead):
            def prime(g, carry, t=t):
                issue_group(t, t, pl.multiple_of(g * ROW_TILE, ROW_TILE))
                return carry
            lax.fori_loop(0, tc // ROW_TILE, prime, 0)

    slot = i % COMBINE_SLOTS
    wait_tile(slot)
    _, _, g1p = _split_mod(mod_ref, row=i // tiles_per_seq)
    gain, bias = g_ref[...], b_ref[...]
    nxt = jnp.minimum(i + ahead, nsteps - 1)
    nslot = (i + ahead) % COMBINE_SLOTS

    for c in range(tc // COMBINE_CHUNK):
        for g in range(COMBINE_CHUNK // ROW_TILE):
            issue_group(nxt, nslot, c * COMBINE_CHUNK + g * ROW_TILE)
        rows = pl.ds(c * COMBINE_CHUNK, COMBINE_CHUNK)
        z = ALPHA * x_ref[rows, :] + g1p * buf[slot, rows].reshape(COMBINE_CHUNK, D)
        xn_ref[rows, :] = _layer_norm(z, gain, bias)

    if with_kv:
        _kv_rows(xn_ref[...].astype(BF16), kvw_ref, kvwt_ref, *kv_out)

    @pl.when(i == nsteps - 1)
    def _():
        for k in range(1, COMBINE_SLOTS):
            wait_tile((i + k) % COMBINE_SLOTS)


def _combine(pos, x, mod, ys, ln_g, ln_b, kv_weights=None, *, layer, seq):
    n = x.shape[0]
    tc = COMBINE_TC
    sub = 2 * layer + 1
    row = lambda i, pos: (i, 0)
    in_specs = [
        pl.BlockSpec((tc, D), row),
        _layer_spec((n // seq, 3 * D), sub),
        pl.BlockSpec(memory_space=pl.ANY),
        _layer_spec((1, D), sub), _layer_spec((1, D), sub),
    ]
    out_specs = [pl.BlockSpec((tc, D), row)]
    out_shape = [jax.ShapeDtypeStruct((n, D), F32)]
    args = [pos, x, mod, ys, ln_g, ln_b]
    if kv_weights is not None:
        in_specs += [pl.BlockSpec((D, 2 * KVW), lambda i, pos: (0, 0)),
                     pl.BlockSpec((KVW, D), lambda i, pos: (0, 0))]
        out_specs += [pl.BlockSpec((tc, KVW), row), pl.BlockSpec((tc, KVW), row),
                      pl.BlockSpec((tc, KVW), row), pl.BlockSpec((KVW, tc), lambda i, pos: (0, i))]
        out_shape += [jax.ShapeDtypeStruct((n, KVW), F32), jax.ShapeDtypeStruct((n, KVW), F32),
                      jax.ShapeDtypeStruct((n, KVW), BF16), jax.ShapeDtypeStruct((KVW, n), BF16)]
        args += list(kv_weights)
    return pl.pallas_call(
        functools.partial(_combine_body, tiles_per_seq=seq // tc, with_kv=kv_weights is not None),
        grid_spec=pltpu.PrefetchScalarGridSpec(
            num_scalar_prefetch=1, grid=(n // tc,),
            in_specs=in_specs, out_specs=out_specs,
            scratch_shapes=[pltpu.VMEM((COMBINE_SLOTS, tc, D // LANES, LANES), F32),
                            pltpu.SemaphoreType.DMA((COMBINE_SLOTS,))]),
        out_shape=out_shape,
        compiler_params=_cp(("arbitrary",)),
        name="moe_combine",
    )(*args)


def _moe_sparse(x, route, mod, w1, w3, w2, ln_g, ln_b, kv_weights=None, *, layer, seq):
    n = x.shape[0]
    urow, meta, cnt = route
    cls = meta[:, 2, :].reshape(n).astype(jnp.int32)
    rank = meta[:, 3, :].reshape(n).astype(jnp.int32)
    count = cnt[:NCLS, 0].astype(jnp.int32)
    ntile = (count + FFN_TM - 1) // FFN_TM
    tile_end = jnp.cumsum(ntile)
    row_start = (tile_end - ntile) * FFN_TM
    pos = rank + jnp.sum(jax.nn.one_hot(cls, NCLS, dtype=jnp.int32) * row_start[None, :], axis=1)
    nt = n // FFN_TM + NCLS
    total = tile_end[NCLS - 1]
    tr = jnp.minimum(jnp.arange(nt, dtype=jnp.int32), total - 1)
    tcls = jnp.sum((tr[:, None] >= tile_end[None, :]).astype(jnp.int32), axis=1)
    grp, pair = tcls // PAIRS, tcls % PAIRS
    lo = (pair >= 3).astype(jnp.int32) + (pair >= 5).astype(jnp.int32)
    hi = pair + 1 - lo * (2 * E_PER_G - 3 - lo) // 2
    ta = grp * E_PER_G + lo
    tb = grp * E_PER_G + hi
    t_all = jnp.arange(nt, dtype=jnp.int32)
    partial_last = ((t_all[:, None] == tile_end[None, :] - 1) & (count % FFN_TM != 0)[None, :]).any(axis=1)
    zflag = ((t_all >= total) | partial_last).astype(jnp.int32)
    xs = _dispatch(pos, zflag, urow)
    ys = _ffn_sorted(ta, tb, total[None], xs, w1, w3, w2)
    return _combine(pos, x, mod, ys, ln_g, ln_b, kv_weights, layer=layer, seq=seq)


def _moe_body(x_ref, mod_ref, wr_ref, br_ref, w1f_ref, w3f_ref, w2f_ref, g_ref, b_ref,
              xn_ref, w1_ref, w3_ref, w2_ref, ub, comb, acc, *, reps):
    e = pl.program_id(1)
    w1_ref[...] = w1f_ref[...].astype(BF16)
    w3_ref[...] = w3f_ref[...].astype(BF16)
    w2_ref[...] = w2f_ref[...].astype(BF16)

    @pl.when(e == 0)
    def _():
        shift, sc1p, _ = _split_mod(mod_ref, reps=reps)
        u = (x_ref[...] * sc1p + shift).astype(BF16)
        ub[...] = u
        eid, _, i1, i2, w1, w2 = _top2(_router_logits(u, wr_ref, br_ref))
        comb[...] = _rows_to_cols(jnp.where(eid == i1, w1, 0.0) + jnp.where(eid == i2, w2, 0.0))
        acc[...] = jnp.zeros(acc.shape, F32)

    u = ub[...]
    col = lax.broadcasted_iota(jnp.int32, comb.shape, 1)
    ce = jnp.sum(jnp.where(col == e, comb[...], 0.0), axis=-1, keepdims=True)
    h = jax.nn.silu(_dot(u, w1_ref[...])) * _dot(u, w3_ref[...]) * ce
    acc[...] += _dot(h.astype(BF16), w2_ref[...])

    @pl.when(e == N_EXP - 1)
    def _():
        _, _, g1p = _split_mod(mod_ref, reps=reps)
        xn_ref[...] = _layer_norm(ALPHA * x_ref[...] + g1p * acc[...], g_ref[...], b_ref[...])


def _moe_dense(x, mod, wr, br, w1, w3, w2, ln_g, ln_b, *, layer, reps):
    n = x.shape[0]
    sub = 2 * layer + 1
    expert = lambda i, e: (layer, e, 0, 0)
    return pl.pallas_call(
        functools.partial(_moe_body, reps=reps),
        grid=(1, N_EXP),
        in_specs=[
            pl.BlockSpec((n, D), lambda i, e: (0, 0)),
            _layer_spec((n // reps, 3 * D), sub),
            _layer_spec((RROWS, D), layer), _layer_spec((RROWS, RLANES), layer),
            pl.BlockSpec((None, None, D, F), expert),
            pl.BlockSpec((None, None, D, F), expert),
            pl.BlockSpec((None, None, F, D), expert),
            _layer_spec((1, D), sub), _layer_spec((1, D), sub),
        ],
        out_specs=[pl.BlockSpec((n, D), lambda i, e: (0, 0)),
                   pl.BlockSpec((None, D, F), lambda i, e: (e, 0, 0)),
                   pl.BlockSpec((None, D, F), lambda i, e: (e, 0, 0)),
                   pl.BlockSpec((None, F, D), lambda i, e: (e, 0, 0))],
        out_shape=[jax.ShapeDtypeStruct((n, D), F32),
                   jax.ShapeDtypeStruct((N_EXP, D, F), BF16),
                   jax.ShapeDtypeStruct((N_EXP, D, F), BF16),
                   jax.ShapeDtypeStruct((N_EXP, F, D), BF16)],
        scratch_shapes=[
            pltpu.VMEM((n, D), BF16),
            pltpu.VMEM((n, RLANES), F32),
            pltpu.VMEM((n, D), F32),
        ],
        compiler_params=_cp(("arbitrary", "arbitrary")),
        name="moe",
    )(x, mod, wr, br, w1, w3, w2, ln_g, ln_b)


def _kv_body(x_ref, w_ref, wvt_ref, k_ref, v_ref, k16_ref, vt16_ref):
    _kv_rows(x_ref[...].astype(BF16), w_ref, wvt_ref, k_ref, v_ref, k16_ref, vt16_ref)


def _kv_rows(xb, w_ref, wvt_ref, k_ref, v_ref, k16_ref, vt16_ref):
    kv = _dot(xb, w_ref[...])
    k_ref[...] = kv[:, :KVW]
    v_ref[...] = kv[:, KVW:]
    k16_ref[...] = kv[:, :KVW].astype(BF16)
    vt16_ref[...] = lax.dot_general(wvt_ref[...], xb, _NT, preferred_element_type=F32).astype(BF16)


def _kv(x, w, wvt, *, tm):
    n = x.shape[0]
    row = lambda i: (i, 0)
    return pl.pallas_call(
        _kv_body,
        grid=(n // tm,),
        in_specs=[pl.BlockSpec((tm, D), row), _const_spec((D, 2 * KVW)), _const_spec((KVW, D))],
        out_specs=[pl.BlockSpec((tm, KVW), row), pl.BlockSpec((tm, KVW), row),
                   pl.BlockSpec((tm, KVW), row), pl.BlockSpec((KVW, tm), lambda i: (0, i))],
        out_shape=[jax.ShapeDtypeStruct((n, KVW), F32), jax.ShapeDtypeStruct((n, KVW), F32),
                   jax.ShapeDtypeStruct((n, KVW), BF16), jax.ShapeDtypeStruct((KVW, n), BF16)],
        compiler_params=_cp(("arbitrary",)),
        name="kv_proj",
    )(x, w, wvt)


def _sink_attend(qh, kh, vh, bias, sink):
    s = lax.dot_general(qh, kh, (((1,), (1,)), ((), ())), preferred_element_type=F32) + bias
    m = jnp.maximum(jnp.max(s, axis=-1, keepdims=True), sink)
    p = jnp.exp(s - m)
    den = jnp.sum(p, axis=-1, keepdims=True) + jnp.exp(sink - m)
    return _dot(p.astype(BF16), vh) / den


def _attn_prompt_body(x_ref, mod_ref, wqt_ref, wo_ref, kc_ref, kp_ref, vtc_ref, vtp_ref,
                      bias_ref, sink_ref, g_ref, b_ref, mod1_ref, wr_ref, br_ref,
                      xn_ref, urow_ref, meta_ref, cnt_ref, carry, *, tq):
    t = pl.program_id(1)
    x = x_ref[...]
    shift, sc1p, g1p = _split_mod(mod_ref, row=pl.program_id(0))
    u = (x * sc1p + shift).astype(BF16)
    qt = lax.dot_general(wqt_ref[...], u, _NT, preferred_element_type=F32)
    qt = (qt * (HD ** -0.5)).astype(BF16)
    kc = kc_ref[...]
    vtc = vtc_ref[...]
    krow = lax.broadcasted_iota(jnp.int32, (2 * WIN, GRP * WIN), 0)
    first = jnp.where((krow < WIN) & (t == 0), NEG, 0.0).astype(F32)
    nwin = tq // WIN
    kks, vvts = [], []
    for j in range(nwin):
        if j == 0:
            kks.append(jnp.concatenate([kp_ref[...], kc[0:WIN]], axis=0))
            vvts.append(jnp.concatenate([vtp_ref[...], vtc[:, 0:WIN]], axis=1))
        else:
            kks.append(kc[(j - 1) * WIN:(j + 1) * WIN])
            vvts.append(vtc[:, (j - 1) * WIN:(j + 1) * WIN])

    def scores(j, hk):
        qht = jnp.concatenate(
            [qt[(hk * GRP + g) * HD:(hk * GRP + g + 1) * HD, j * WIN:(j + 1) * WIN] for g in range(GRP)],
            axis=1)
        st = _dot(kks[j][:, hk * HD:(hk + 1) * HD], qht) + bias_ref[hk]
        return st + first if j == 0 else st

    items = [(j, hk) for j in range(nwin) for hk in range(HKV)]
    heads = [[] for _ in range(nwin)]
    st_next = scores(*items[0])
    for n, (j, hk) in enumerate(items):
        st = st_next
        if n + 1 < len(items):
            st_next = scores(*items[n + 1])
        sink = sink_ref[hk]
        m = jnp.maximum(jnp.max(st, axis=0, keepdims=True), sink)
        pt = jnp.exp(st - m)
        den = jnp.sum(pt, axis=0, keepdims=True) + jnp.exp(sink - m)
        ot = _dot(vvts[j][hk * HD:(hk + 1) * HD, :], pt.astype(BF16)) * (1.0 / den)
        for g in range(GRP):
            heads[j].append(ot[:, g * WIN:(g + 1) * WIN])
    blocks = [jnp.concatenate(h, axis=0) for h in heads]
    oallt = jnp.concatenate(blocks, axis=1) if len(blocks) > 1 else blocks[0]
    mix = lax.dot_general(oallt.astype(BF16), wo_ref[...], _TN, preferred_element_type=F32)
    xn = _layer_norm(ALPHA * x + g1p * mix, g_ref[...], b_ref[...])
    xn_ref[...] = xn
    b = pl.program_id(0)
    _route_rows(xn, b, (b == 0) & (t == 0), mod1_ref, wr_ref, br_ref, urow_ref, meta_ref, cnt_ref, carry)


def _attn_prompt(x, mod, wqt, wo, k16, vt16, bias, sink, ln_g, ln_b, wr, br, *, layer, nb, seq, tq):
    nt = seq // tq
    r_in, r_out, r_shape, r_scratch = _route_specs(layer, nb, nt, tq)
    wpt = tq // WIN
    n = nb * seq
    j = layer - N_A
    sub = 2 * layer

    def prev(b, t):
        return jnp.maximum((b * nt + t) * wpt - 1, 0)

    cur = lambda b, t: (b * nt + t, 0)
    return pl.pallas_call(
        functools.partial(_attn_prompt_body, tq=tq),
        grid=(nb, nt),
        in_specs=[
            pl.BlockSpec((tq, D), cur),
            _layer_spec((nb, 3 * D), sub),
            _layer_spec((HQ * HD, D), j), _layer_spec((HQ * HD, D), j),
            pl.BlockSpec((tq, KVW), cur), pl.BlockSpec((WIN, KVW), lambda b, t: (prev(b, t), 0)),
            pl.BlockSpec((KVW, tq), lambda b, t: (0, b * nt + t)),
            pl.BlockSpec((KVW, WIN), lambda b, t: (0, prev(b, t))),
            _const_spec((HKV, 2 * WIN, GRP * WIN)), _layer_spec((HKV, 1, GRP * WIN), j),
            _layer_spec((1, D), sub), _layer_spec((1, D), sub),
        ] + r_in,
        out_specs=[pl.BlockSpec((tq, D), cur)] + r_out,
        out_shape=[jax.ShapeDtypeStruct((n, D), F32)] + r_shape,
        scratch_shapes=r_scratch,
        compiler_params=_cp(("arbitrary", "arbitrary")),
        name="attn_prompt",
    )(x, mod, wqt, wo, k16, k16, vt16, vt16, bias, sink, ln_g, ln_b, mod, wr, br)


def _attn_sample_body(x_ref, mod_ref, wq_ref, wo_ref, kb_ref, vb_ref, kn_ref, vn_ref,
                      bias_ref, sink_ref, g_ref, b_ref, xn_ref, *, ns, steps):
    nr = ns * steps
    x = x_ref[...].reshape(nr, D)
    shift, sc1p, g1p = _split_mod(mod_ref, reps=steps)
    u = (x * sc1p + shift).astype(BF16)
    q = (_dot(u, wq_ref[...]) * (HD ** -0.5)).astype(BF16)
    nbuf = kb_ref.shape[1]
    k_all = jnp.concatenate([kb_ref[...].reshape(ns * nbuf, KVW), kn_ref[...].reshape(nr, KVW)],
                            axis=0).astype(BF16)
    v_all = jnp.concatenate([vb_ref[...].reshape(ns * nbuf, KVW), vn_ref[...].reshape(nr, KVW)],
                            axis=0).astype(BF16)
    heads = [None] * HQ
    for hk in range(HKV):
        kcols = slice(hk * HD, (hk + 1) * HD)
        qh = jnp.concatenate(
            [q[:, (hk * GRP + g) * HD:(hk * GRP + g + 1) * HD] for g in range(GRP)], axis=0)
        o = _sink_attend(qh, k_all[:, kcols], v_all[:, kcols], bias_ref[hk], sink_ref[hk])
        for g in range(GRP):
            heads[hk * GRP + g] = o[g * nr:(g + 1) * nr]
    o_all = jnp.concatenate(heads, axis=1)
    mix = _dot(o_all.astype(BF16), wo_ref[...])
    xn_ref[...] = _layer_norm(ALPHA * x + g1p * mix, g_ref[...], b_ref[...]).reshape(steps, ns, D)


def _attn_sample(x, mod, wq, wo, kbuf, vbuf, kn, vn, bias, sink, ln_g, ln_b, *, layer, ns):
    steps, nb, _ = x.shape
    nbuf = kbuf.shape[1]
    j = layer - N_A
    sub = 2 * layer
    seqs = lambda i: (0, i, 0)
    return pl.pallas_call(
        functools.partial(_attn_sample_body, ns=ns, steps=steps),
        grid=(nb // ns,),
        in_specs=[
            pl.BlockSpec((steps, ns, D), seqs),
            pl.BlockSpec((None, ns, 3 * D), lambda i: (sub, i, 0)),
            _layer_spec((D, HQ * HD), j), _layer_spec((HQ * HD, D), j),
            pl.BlockSpec((ns, nbuf, KVW), lambda i: (i, 0, 0)),
            pl.BlockSpec((ns, nbuf, KVW), lambda i: (i, 0, 0)),
            pl.BlockSpec((steps, ns, KVW), seqs), pl.BlockSpec((steps, ns, KVW), seqs),
            _const_spec(bias.shape), _layer_spec(sink.shape[1:], j),
            _layer_spec((1, D), sub), _layer_spec((1, D), sub),
        ],
        out_specs=pl.BlockSpec((steps, ns, D), seqs),
        out_shape=jax.ShapeDtypeStruct((steps, nb, D), F32),
        compiler_params=_cp(("arbitrary",)),
        name="attn_sample",
    )(x, mod, wq, wo, kbuf, vbuf, kn, vn, bias, sink, ln_g, ln_b)


def _t5_bucket(d):
    max_exact = N_BUCKETS // 2
    d = np.maximum(d, 0)
    log_ratio = (np.log(np.maximum(d, 1).astype(np.float32) / np.float32(max_exact))
                 / np.float32(math.log(WIN / max_exact)))
    large = max_exact + (log_ratio * (N_BUCKETS - max_exact)).astype(np.int32)
    large = np.minimum(large, N_BUCKETS - 1)
    return np.where(d < max_exact, d, large)


def _bias_table(d, ok, rel_bias):
    sel = np.where(ok, _t5_bucket(d), N_BUCKETS)
    onehot = (sel[..., None] == np.arange(N_BUCKETS + 1)).astype(np.float32)
    tab = jnp.concatenate([rel_bias.astype(F32), jnp.full((1, HQ), NEG, F32)], axis=0)
    return jnp.einsum('abd,dh->hab', jnp.asarray(onehot), tab, precision=lax.Precision.HIGHEST)


def _prompt_bias(rel_bias):
    d = np.arange(WIN)[None, :] + WIN - np.arange(2 * WIN)[:, None]
    b = _bias_table(d, (d >= 0) & (d < WIN), rel_bias).reshape(HKV, GRP, 2 * WIN, WIN)
    return jnp.transpose(b, (0, 2, 1, 3)).reshape(HKV, 2 * WIN, GRP * WIN)


def _sample_bias(rel_bias, ns, steps, nbuf):
    qs = np.tile(np.arange(ns), steps)
    qt = np.repeat(np.arange(steps), ns)
    ks = np.concatenate([np.repeat(np.arange(ns), nbuf), qs])
    kpos = np.concatenate([np.tile(np.arange(nbuf), ns), nbuf + qt])
    d = qt[:, None] + nbuf - kpos[None, :]
    ok = (d >= 0) & (d < WIN) & (qs[:, None] == ks[None, :])
    return _bias_table(d, ok, rel_bias).reshape(HKV, GRP * ns * steps, kpos.shape[0])


LRU_TM = 512
ATTN_TQ = 1024
SAMPLE_NS = 8


def kernel(x_prompt, x_sample, state_rnn_h, state_rnn_conv, cache_win_k, cache_win_v, c_prompt, c_sample, ada_w, ada_b, ln_g, ln_b, lru_w_in, lru_conv_w, lru_conv_b, lru_w_a, lru_b_a, lru_w_x, lru_b_x, lru_lambda, lru_w_out, kv_w, attn_w_q, attn_sinks, attn_w_o, rel_bias, moe_w_group, moe_b_group, moe_w_router, moe_b_router, moe_w1, moe_w3, moe_w2):
    nbp, seq, _ = x_prompt.shape
    nbs, steps, _ = x_sample.shape
    nbuf = cache_win_k.shape[1]
    npt = nbp * seq
    nst = nbs * steps

    mod_p, mod_s = _adaln(c_prompt, c_sample, ada_w, ada_b)

    ln_g3 = ln_g.reshape(2 * DEPTH, 1, D)
    ln_b3 = ln_b.reshape(2 * DEPTH, 1, D)
    lw = (lru_w_in.astype(BF16), lru_conv_w, lru_conv_b[:, None], lru_w_a.astype(BF16),
          lru_w_x.astype(BF16), lru_b_a[:, None], lru_b_x[:, None], lru_lambda[:, None],
          lru_w_out.astype(BF16))
    zrow = lambda k: jnp.zeros((DEPTH, k, D), F32)
    wr = jnp.concatenate([jnp.swapaxes(moe_w_group, 1, 2), zrow(EROW - N_GROUPS),
                          jnp.swapaxes(moe_w_router.reshape(DEPTH, D, N_EXP), 1, 2),
                          zrow(RROWS - EROW - N_EXP)], axis=1).astype(BF16)
    zb = lambda k: jnp.zeros((DEPTH, k), F32)
    br = jnp.concatenate([moe_b_group, zb(EROW - N_GROUPS), moe_b_router.reshape(DEPTH, N_EXP),
                          zb(RROWS - EROW - N_EXP)], axis=1)
    br = jnp.broadcast_to(br[:, :, None], (DEPTH, RROWS, RLANES))
    wq = attn_w_q.astype(BF16)
    wqt = jnp.swapaxes(wq, 1, 2)
    wo = attn_w_o.astype(BF16)
    kvw = kv_w.astype(BF16)
    kvwt = kvw[:, KVW:].T
    bias_p = _prompt_bias(rel_bias)
    bias_s = _sample_bias(rel_bias, SAMPLE_NS, steps, nbuf)
    sink_p = jnp.repeat(attn_sinks.reshape(N_B, HKV, GRP), WIN, axis=2)[:, :, None, :]
    sink_s = jnp.repeat(attn_sinks.reshape(N_B, HKV, GRP), SAMPLE_NS * steps, axis=2)[..., None]
    kbuf = cache_win_k.reshape(nbs, nbuf, KVW)
    vbuf = cache_win_v.reshape(nbs, nbuf, KVW)

    xp = x_prompt.reshape(npt, D)
    xs = jnp.transpose(x_sample, (1, 0, 2)).reshape(nst, D)
    c0 = jnp.transpose(state_rnn_conv, (0, 2, 1, 3)).reshape(N_A, (CONV_W - 1) * nbs, R)
    hp, cp_, hs, cs = [], [], [], []
    kp = vp = kn = vn = kp16 = vpt16 = None
    for l in range(DEPTH):
        if l < N_A:
            xp, h_l, c_l, *route = _lru_prompt(xp, mod_p, lw, ln_g3, ln_b3, wr, br,
                                               layer=l, nb=nbp, seq=seq, tm=LRU_TM)
            hp.append(h_l[:, 0])
            cp_.append(c_l)
            xs, h_l, c_l = _lru_sample(xs, mod_s, state_rnn_h, c0[l], lw, ln_g3, ln_b3,
                                       layer=l, nb=nbs, steps=steps)
            hs.append(h_l)
            cs.append(jnp.transpose(c_l.reshape(CONV_W - 1, nbs, R), (1, 0, 2)))
        else:
            xp, *route = _attn_prompt(xp, mod_p, wqt, wo, kp16, vpt16, bias_p, sink_p, ln_g3, ln_b3, wr, br,
                                      layer=l, nb=nbp, seq=seq, tq=ATTN_TQ)
            xs = _attn_sample(xs.reshape(steps, nbs, D), mod_s, wq, wo, kbuf, vbuf,
                              kn.reshape(steps, nbs, KVW), vn.reshape(steps, nbs, KVW),
                              bias_s, sink_s, ln_g3, ln_b3, layer=l, ns=SAMPLE_NS).reshape(nst, D)
        xs, w1, w3, w2 = _moe_dense(xs, mod_s, wr, br, moe_w1, moe_w3, moe_w2, ln_g3, ln_b3,
                                    layer=l, reps=steps)
        if l == N_A - 1:
            xp, kp, vp, kp16, vpt16 = _moe_sparse(xp, route, mod_p, w1, w3, w2, ln_g3, ln_b3, (kvw, kvwt),
                                                  layer=l, seq=seq)
            kn, vn, _, _ = _kv(xs, kvw, kvwt, tm=nst)
        else:
            xp, = _moe_sparse(xp, route, mod_p, w1, w3, w2, ln_g3, ln_b3, layer=l, seq=seq)

    lp = min(WIN, seq)
    k_p = kp.reshape(nbp, seq, HKV, HD)[:, seq - lp:]
    v_p = vp.reshape(nbp, seq, HKV, HD)[:, seq - lp:]
    kn4 = jnp.transpose(kn.reshape(steps, nbs, HKV, HD), (1, 0, 2, 3))
    vn4 = jnp.transpose(vn.reshape(steps, nbs, HKV, HD), (1, 0, 2, 3))
    k_s = jnp.concatenate([cache_win_k, kn4], axis=1)[:, -nbuf:]
    v_s = jnp.concatenate([cache_win_v, vn4], axis=1)[:, -nbuf:]
    y_s = jnp.transpose(xs.reshape(steps, nbs, D), (1, 0, 2))
    return (xp.reshape(nbp, seq, D), y_s,
            jnp.stack(hp), jnp.stack(cp_), k_p, v_p,
            jnp.stack(hs), jnp.stack(cs), k_s, v_s)
```

```python
import functools
import math

import jax
import jax.numpy as jnp
import numpy as np
from jax import lax
from jax.experimental import pallas as pl
from jax.experimental.pallas import tpu as pltpu

D = 1024
R = 1024
DEPTH = 4
N_A = 2
N_B = DEPTH - N_A
N_BLK = 4
BLK = R // N_BLK
CONV_W = 4
RG_C = 8.0
HQ = 16
HKV = 4
HD = 64
GRP = HQ // HKV
KVW = HKV * HD
WIN = 128
N_BUCKETS = 32
N_GROUPS = 4
E_PER_G = 4
N_EXP = 16
F = 512
ALPHA = (2.0 * DEPTH) ** 0.25
LN_EPS = 1e-5
NEG = -1e30
LANES = 128
RLANES = LANES
ROW_TILE = 8

F32 = jnp.float32
BF16 = jnp.bfloat16

VMEM_LIMIT = 56 * 1024 * 1024


def _cp(sem):
    return pltpu.CompilerParams(dimension_semantics=sem, vmem_limit_bytes=VMEM_LIMIT)


def _dot(a, b):
    return jnp.dot(a, b, preferred_element_type=F32)


def _layer_norm(z, g, b):
    mu = jnp.mean(z, axis=-1, keepdims=True)
    zc = z - mu
    var = jnp.mean(zc * zc, axis=-1, keepdims=True)
    return zc * lax.rsqrt(var + LN_EPS) * g + b


def _split_mod(mod_ref, row=None, reps=1):
    m = mod_ref[...] if row is None else mod_ref[pl.ds(row, 1), :]
    if reps > 1:
        m = jnp.concatenate([m] * reps, axis=0)
    return m[:, 0:D], m[:, D:2 * D], m[:, 2 * D:3 * D]


def _layer_spec(shape, layer):
    nd = len(shape)
    return pl.BlockSpec((None,) + tuple(shape), lambda *_: (layer,) + (0,) * nd)


def _adaln_body(cp_ref, cs_ref, w_ref, b_ref, op_ref, os_ref):
    one = jnp.where(pl.program_id(1) > 0, 1.0, 0.0).astype(F32)
    w = w_ref[...].astype(BF16)
    op_ref[...] = _dot(cp_ref[...].astype(BF16), w) + b_ref[...] + one
    os_ref[...] = _dot(cs_ref[...].astype(BF16), w) + b_ref[...] + one


def _adaln(c_p, c_s, ada_w, ada_b):
    np_, ns_ = c_p.shape[0], c_s.shape[0]
    w = ada_w.reshape(2 * DEPTH, D, 3 * D)
    b = ada_b.reshape(2 * DEPTH, 1, 3 * D)
    return pl.pallas_call(
        _adaln_body,
        grid=(2 * DEPTH, 3),
        in_specs=[
            pl.BlockSpec((np_, D), lambda l, j: (0, 0)),
            pl.BlockSpec((ns_, D), lambda l, j: (0, 0)),
            pl.BlockSpec((None, D, D), lambda l, j: (l, 0, j)),
            pl.BlockSpec((None, 1, D), lambda l, j: (l, 0, j)),
        ],
        out_specs=[pl.BlockSpec((None, np_, D), lambda l, j: (l, 0, j)),
                   pl.BlockSpec((None, ns_, D), lambda l, j: (l, 0, j))],
        out_shape=[jax.ShapeDtypeStruct((2 * DEPTH, np_, 3 * D), F32),
                   jax.ShapeDtypeStruct((2 * DEPTH, ns_, 3 * D), F32)],
        compiler_params=_cp(("arbitrary", "arbitrary")),
        name="adaln",
    )(c_p, c_s, w, b)


def _lru_gates(xc, n, wa_ref, wx_ref, ba_ref, bx_ref, lam_ref):
    cols = slice(n * BLK, (n + 1) * BLK)
    xb = xc.astype(BF16)
    r = jax.nn.sigmoid(_dot(xb, wa_ref[n]) + ba_ref[:, cols])
    i = jax.nn.sigmoid(_dot(xb, wx_ref[n]) + bx_ref[:, cols])
    z = -lam_ref[:, cols]
    softplus = jnp.maximum(z, 0.0) + jnp.log1p(jnp.exp(-jnp.abs(z)))
    log_a = (-RG_C) * r * softplus
    a = jnp.exp(log_a)
    v = 1.0 - a * a
    b = jnp.where(v > 0.0, v * lax.rsqrt(v), 0.0) * i * xc
    return a, b


def _lru_prompt_body(x_ref, mod_ref, win_ref, cw_ref, cb_ref, wa_ref, wx_ref, ba_ref, bx_ref,
                     lam_ref, wout_ref, g_ref, b_ref, mod1_ref, wr_ref, br_ref,
                     xn_ref, hl_ref, cl_ref, urow_ref, meta_ref, cnt_ref,
                     xbuf, abuf, hbuf, hc, carry, *, tm):
    t = pl.program_id(1)

    @pl.when(t == 0)
    def _():
        xbuf[0:ROW_TILE, :] = jnp.zeros((ROW_TILE, R), F32)
        hc[...] = jnp.zeros((1, R), F32)

    x = x_ref[...]
    shift, sc1p, g1p = _split_mod(mod_ref, row=pl.program_id(0))
    u = x * sc1p + shift
    yx = _dot(u.astype(BF16), win_ref[...])
    y = jax.nn.gelu(yx[:, :R])
    xbuf[ROW_TILE:ROW_TILE + tm, :] = yx[:, R:]

    ngrp = tm // ROW_TILE
    rowmod = lax.broadcasted_iota(jnp.int32, (ngrp, ROW_TILE, BLK), 1)
    for n in range(N_BLK):
        cols = slice(n * BLK, (n + 1) * BLK)
        xc = cb_ref[:, cols] + cw_ref[3:4, cols] * xbuf[ROW_TILE:ROW_TILE + tm, cols]
        for k in range(1, CONV_W):
            xc = xc + cw_ref[3 - k:4 - k, cols] * xbuf[ROW_TILE - k:ROW_TILE - k + tm, cols]
        a, b = _lru_gates(xc, n, wa_ref, wx_ref, ba_ref, bx_ref, lam_ref)
        a = a.reshape(ngrp, ROW_TILE, BLK)
        b = b.reshape(ngrp, ROW_TILE, BLK)
        s = 1
        while s < ROW_TILE:
            a_sh = pltpu.roll(a, s, 1)
            b_sh = pltpu.roll(b, s, 1)
            m = rowmod >= s
            b = jnp.where(m, a * b_sh, 0.0) + b
            a = jnp.where(m, a * a_sh, a)
            s *= 2
        abuf[:, cols] = a.reshape(tm, BLK)
        hbuf[:, cols] = b.reshape(tm, BLK)

    def group(g, h):
        r0 = pl.multiple_of(g * ROW_TILE, ROW_TILE)
        hg = abuf[pl.ds(r0, ROW_TILE), :] * h + hbuf[pl.ds(r0, ROW_TILE), :]
        hbuf[pl.ds(r0, ROW_TILE), :] = hg
        return hg[ROW_TILE - 1:ROW_TILE, :]

    h_end = lax.fori_loop(0, tm // ROW_TILE, group, hc[...])
    hc[...] = h_end
    hl_ref[...] = h_end
    tail = xbuf[tm + ROW_TILE - (CONV_W - 1):tm + ROW_TILE, :]
    cl_ref[...] = tail
    xbuf[ROW_TILE - (CONV_W - 1):ROW_TILE, :] = tail

    mix = _dot((y * hbuf[...]).astype(BF16), wout_ref[...])
    xn = _layer_norm(ALPHA * x + g1p * mix, g_ref[...], b_ref[...])
    xn_ref[...] = xn
    b = pl.program_id(0)
    _route_rows(xn, b, (b == 0) & (t == 0), mod1_ref, wr_ref, br_ref, urow_ref, meta_ref, cnt_ref, carry)


def _const_spec(shape):
    nd = len(shape)
    return pl.BlockSpec(shape, lambda *_: (0,) * nd)


def _lru_weight_specs(layer):
    sub = 2 * layer
    return [
        _layer_spec((D, 2 * R), layer), _layer_spec((CONV_W, R), layer), _layer_spec((1, R), layer),
        _layer_spec((N_BLK, BLK, BLK), layer), _layer_spec((N_BLK, BLK, BLK), layer),
        _layer_spec((1, R), layer), _layer_spec((1, R), layer), _layer_spec((1, R), layer),
        _layer_spec((R, D), layer), _layer_spec((1, D), sub), _layer_spec((1, D), sub),
    ]


def _lru_prompt(x, mod, lw, ln_g, ln_b, wr, br, *, layer, nb, seq, tm):
    nt = seq // tm
    n = nb * seq
    r_in, r_out, r_shape, r_scratch = _route_specs(layer, nb, nt, tm)
    return pl.pallas_call(
        functools.partial(_lru_prompt_body, tm=tm),
        grid=(nb, nt),
        in_specs=[
            pl.BlockSpec((tm, D), lambda b, t: (b * nt + t, 0)),
            _layer_spec((nb, 3 * D), 2 * layer),
        ] + _lru_weight_specs(layer) + r_in,
        out_specs=[
            pl.BlockSpec((tm, D), lambda b, t: (b * nt + t, 0)),
            pl.BlockSpec((None, 1, R), lambda b, t: (b, 0, 0)),
            pl.BlockSpec((None, CONV_W - 1, R), lambda b, t: (b, 0, 0)),
        ] + r_out,
        out_shape=[
            jax.ShapeDtypeStruct((n, D), F32),
            jax.ShapeDtypeStruct((nb, 1, R), F32),
            jax.ShapeDtypeStruct((nb, CONV_W - 1, R), F32),
        ] + r_shape,
        scratch_shapes=[
            pltpu.VMEM((tm + ROW_TILE, R), F32),
            pltpu.VMEM((tm, R), F32),
            pltpu.VMEM((tm, R), F32),
            pltpu.VMEM((1, R), F32),
        ] + r_scratch,
        compiler_params=_cp(("arbitrary", "arbitrary")),
        name="lru_prompt",
    )(x, mod, *lw, ln_g, ln_b, mod, wr, br)


def _lru_sample_body(x_ref, mod_ref, h0_ref, c0_ref, win_ref, cw_ref, cb_ref, wa_ref, wx_ref,
                     ba_ref, bx_ref, lam_ref, wout_ref, g_ref, b_ref,
                     xn_ref, hl_ref, cl_ref, hbuf, *, nb, steps):
    x = x_ref[...]
    shift, sc1p, g1p = _split_mod(mod_ref, reps=steps)
    u = x * sc1p + shift
    yx = _dot(u.astype(BF16), win_ref[...])
    y = jax.nn.gelu(yx[:, :R])
    xext = jnp.concatenate([c0_ref[...], yx[:, R:]], axis=0)
    cl_ref[...] = xext[steps * nb:, :]
    for n in range(N_BLK):
        cols = slice(n * BLK, (n + 1) * BLK)
        xc = cb_ref[:, cols] + cw_ref[0:1, cols] * xext[0:steps * nb, cols]
        for k in range(1, CONV_W):
            xc = xc + cw_ref[k:k + 1, cols] * xext[k * nb:(k + steps) * nb, cols]
        a, b = _lru_gates(xc, n, wa_ref, wx_ref, ba_ref, bx_ref, lam_ref)
        h = h0_ref[:, cols]
        for t in range(steps):
            h = a[t * nb:(t + 1) * nb, :] * h + b[t * nb:(t + 1) * nb, :]
            hbuf[t * nb:(t + 1) * nb, cols] = h
        hl_ref[:, cols] = h
    mix = _dot((y * hbuf[...]).astype(BF16), wout_ref[...])
    xn_ref[...] = _layer_norm(ALPHA * x + g1p * mix, g_ref[...], b_ref[...])


def _lru_sample(x, mod, h0, c0, lw, ln_g, ln_b, *, layer, nb, steps):
    n = nb * steps
    return pl.pallas_call(
        functools.partial(_lru_sample_body, nb=nb, steps=steps),
        grid=(1,),
        in_specs=[
            _const_spec((n, D)), _layer_spec((nb, 3 * D), 2 * layer), _layer_spec((nb, R), layer),
            _const_spec(((CONV_W - 1) * nb, R)),
        ] + _lru_weight_specs(layer),
        out_specs=[_const_spec((n, D)), _const_spec((nb, R)), _const_spec(((CONV_W - 1) * nb, R))],
        out_shape=[
            jax.ShapeDtypeStruct((n, D), F32),
            jax.ShapeDtypeStruct((nb, R), F32),
            jax.ShapeDtypeStruct(((CONV_W - 1) * nb, R), F32),
        ],
        scratch_shapes=[pltpu.VMEM((n, R), F32)],
        compiler_params=_cp(("arbitrary",)),
        name="lru_sample",
    )(x, mod, h0, c0, *lw, ln_g, ln_b)


RROWS = 32
EROW = 8
_NT = (((1,), (1,)), ((), ()))
_TN = (((0,), (0,)), ((), ()))


def _top2(lt):
    nt = lt.shape[1]
    gl = lt[0:EROW, :]
    gid = lax.broadcasted_iota(jnp.int32, (EROW, nt), 0).astype(F32)
    is_g = gid < N_GROUPS
    gl = jnp.where(is_g, gl, NEG)
    gmax = jnp.max(gl, axis=0, keepdims=True)
    gidx = jnp.min(jnp.where(gl == gmax, gid, float(EROW)), axis=0, keepdims=True)
    gsum = jnp.sum(jnp.where(is_g, jnp.exp(gl - gmax), 0.0), axis=0, keepdims=True)
    g_p = 1.0 / gsum
    el = lt[EROW:EROW + N_EXP, :]
    eid = lax.broadcasted_iota(jnp.int32, (N_EXP, nt), 0).astype(F32)
    first = gidx * E_PER_G
    in_grp = (eid >= first) & (eid < first + E_PER_G)
    el = jnp.where(in_grp, el, NEG)
    big = float(N_EXP)
    t1 = jnp.max(el, axis=0, keepdims=True)
    i1 = jnp.min(jnp.where(el == t1, eid, big), axis=0, keepdims=True)
    el2 = jnp.where(eid == i1, NEG, el)
    t2 = jnp.max(el2, axis=0, keepdims=True)
    i2 = jnp.min(jnp.where((el2 == t2) & in_grp & (eid != i1), eid, big), axis=0, keepdims=True)
    e21 = jnp.exp(t2 - t1)
    w1 = g_p / (1.0 + e21)
    w2 = w1 * e21
    return eid, gidx, i1, i2, w1, w2


def _router_logits(u16, wrt_ref, brt_ref):
    return lax.dot_general(wrt_ref[...], u16, _NT, preferred_element_type=F32) + brt_ref[:, 0:1]


def _rows_to_cols(rows):
    r, nt = rows.shape
    padded = jnp.concatenate([rows, jnp.zeros((RLANES - r, nt), F32)], axis=0)
    return jnp.transpose(padded)


PAIRS = E_PER_G * (E_PER_G - 1) // 2
NCLS = N_GROUPS * PAIRS
HALF = D // 2
ROW_SHAPE = (D // LANES, LANES)
HI_MASK = -65536
FFN_TM = 512
FFN_CHUNK = 128
DISPATCH_TD = 2048
COMBINE_TC = 512
COMBINE_CHUNK = 64
COMBINE_SLOTS = 3


def _route_rows(x, row, first, mod_ref, wr_ref, br_ref, urow_ref, meta_ref, cnt_ref, carry):
    @pl.when(first)
    def _():
        carry[...] = jnp.zeros(carry.shape, F32)

    tm = x.shape[0]
    shift, sc1p, _ = _split_mod(mod_ref, row=row)
    u = x * sc1p + shift
    _, gidx, i1, i2, w1, w2 = _top2(_router_logits(u.astype(BF16), wr_ref, br_ref))
    first = gidx * E_PER_G
    lo = jnp.minimum(i1, i2) - first
    hi = jnp.maximum(i1, i2) - first
    w_lo = jnp.where(i1 < i2, w1, w2)
    w_hi = jnp.where(i1 < i2, w2, w1)
    cls = gidx * PAIRS + lo * (2 * E_PER_G - 1 - lo) * 0.5 + (hi - lo - 1.0)
    cid = lax.broadcasted_iota(jnp.int32, (RROWS, tm), 0).astype(F32)
    onehot = (cid == cls).astype(F32)
    r = lax.broadcasted_iota(jnp.int32, (tm, tm), 0)
    c = lax.broadcasted_iota(jnp.int32, (tm, tm), 1)
    earlier = _dot(onehot.astype(BF16), (r < c).astype(BF16)) + carry[:, 0:1]
    rank = jnp.sum(onehot * earlier, axis=0, keepdims=True)
    carry[...] += jnp.sum(onehot, axis=1, keepdims=True)
    cnt_ref[...] = carry[...]
    meta = jnp.concatenate([w_lo, w_hi, cls, rank, jnp.zeros((ROW_TILE - 4, tm), F32)], axis=0)
    meta_ref[...] = meta
    ub = u.astype(BF16)
    lo = lax.shift_right_logical(pltpu.bitcast(ub[:, 0:HALF].astype(F32), jnp.int32), 16)
    hi = pltpu.bitcast(ub[:, HALF:D].astype(F32), jnp.int32) & HI_MASK
    ext = pltpu.bitcast(_rows_to_cols(meta), jnp.int32)
    row = jnp.concatenate([hi | lo, ext, jnp.zeros((tm, D - HALF - RLANES), jnp.int32)], axis=1)
    urow_ref[...] = row.reshape((tm,) + ROW_SHAPE)


def _route_specs(layer, nb, nt, tm):
    tile = lambda b, t: b * nt + t
    n = nb * nt * tm
    in_specs = [_layer_spec((nb, 3 * D), 2 * layer + 1),
                _layer_spec((RROWS, D), layer), _layer_spec((RROWS, RLANES), layer)]
    out_specs = [pl.BlockSpec((tm,) + ROW_SHAPE, lambda b, t: (tile(b, t), 0, 0)),
                 pl.BlockSpec((None, ROW_TILE, tm), lambda b, t: (tile(b, t), 0, 0)),
                 _const_spec((RROWS, RLANES))]
    out_shape = [jax.ShapeDtypeStruct((n,) + ROW_SHAPE, jnp.int32),
                 jax.ShapeDtypeStruct((nb * nt, ROW_TILE, tm), F32),
                 jax.ShapeDtypeStruct((RROWS, RLANES), F32)]
    return in_specs, out_specs, out_shape, [pltpu.VMEM((RROWS, RLANES), F32)]


def _row_copy(src, src_row, dst, dst_row, sem):
    return pltpu.make_async_copy(src.at[pl.ds(src_row, 1)], dst.at[pl.ds(dst_row, 1)], sem)


def _dispatch_body(pos_ref, zflag_ref, urow_ref, xs_hbm, zbuf, sem, zsem):
    base = pl.program_id(0) * DISPATCH_TD

    @pl.when(pl.program_id(0) == 0)
    def _():
        zbuf[...] = jnp.zeros(zbuf.shape, jnp.int32)

        def zero_copy(t):
            return pltpu.make_async_copy(zbuf, xs_hbm.at[pl.ds(t * FFN_TM, FFN_TM)], zsem)

        def start(t, carry):
            @pl.when(zflag_ref[t] != 0)
            def _():
                zero_copy(t).start()
            return carry

        def wait(t, carry):
            @pl.when(zflag_ref[t] != 0)
            def _():
                zero_copy(t).wait()
            return carry

        lax.fori_loop(0, zflag_ref.shape[0], start, 0)
        lax.fori_loop(0, zflag_ref.shape[0], wait, 0)

    def issue(g, carry):
        r0 = pl.multiple_of(g * ROW_TILE, ROW_TILE)
        for k in range(ROW_TILE):
            _row_copy(urow_ref.at[pl.ds(r0, ROW_TILE)], k, xs_hbm, pos_ref[base + r0 + k],
                      sem).start(priority=k % 2)
        return carry

    lax.fori_loop(0, DISPATCH_TD // ROW_TILE, issue, 0)
    pltpu.make_async_copy(urow_ref, xs_hbm.at[pl.ds(0, DISPATCH_TD)], sem).wait()


def _dispatch(pos, zflag, urow):
    n = urow.shape[0]
    nt = zflag.shape[0]
    return pl.pallas_call(
        _dispatch_body,
        grid_spec=pltpu.PrefetchScalarGridSpec(
            num_scalar_prefetch=2, grid=(n // DISPATCH_TD,),
            in_specs=[pl.BlockSpec((DISPATCH_TD,) + ROW_SHAPE, lambda i, pos, zflag: (i, 0, 0))],
            out_specs=pl.BlockSpec(memory_space=pl.ANY),
            scratch_shapes=[pltpu.VMEM((FFN_TM,) + ROW_SHAPE, jnp.int32), pltpu.SemaphoreType.DMA(()),
                            pltpu.SemaphoreType.DMA(())]),
        out_shape=jax.ShapeDtypeStruct((nt * FFN_TM,) + ROW_SHAPE, jnp.int32),
        compiler_params=_cp(("arbitrary",)),
        name="moe_dispatch",
    )(pos, zflag, urow)


def _ffn_sorted_body(ta_ref, tb_ref, nv_ref, xs_ref, w1a, w3a, w2a, w1b, w3b, w2b, ys_ref):
    del ta_ref, tb_ref
    nvalid = nv_ref[pl.program_id(0)]

    for c in range(FFN_TM // FFN_CHUNK):
        rows = pl.ds(c * FFN_CHUNK, FFN_CHUNK)
        used = nvalid > c * FFN_CHUNK

        @pl.when(used)
        def _(rows=rows):
            words = xs_ref[rows].reshape(FFN_CHUNK, D)
            pair = words[:, 0:HALF]
            x = jnp.concatenate([pltpu.bitcast(lax.shift_left(pair, 16), F32).astype(BF16),
                                 pltpu.bitcast(pair & HI_MASK, F32).astype(BF16)], axis=1)
            ext = pltpu.bitcast(words[:, HALF:HALF + RLANES], F32)
            ha = jax.nn.silu(_dot(x, w1a[...])) * _dot(x, w3a[...]) * ext[:, 0:1]
            hb = jax.nn.silu(_dot(x, w1b[...])) * _dot(x, w3b[...]) * ext[:, 1:2]
            y = _dot(ha.astype(BF16), w2a[...]) + _dot(hb.astype(BF16), w2b[...])
            ys_ref[rows] = y.reshape(FFN_CHUNK, D // LANES, LANES)

        @pl.when(jnp.logical_not(used))
        def _(rows=rows):
            ys_ref[rows] = jnp.zeros((FFN_CHUNK, D // LANES, LANES), F32)


def _ffn_sorted(ta, tb, nv, xs, w1, w3, w2):
    nt = ta.shape[0]
    ea = lambda t, ta, tb, nv: (ta[t], 0, 0)
    eb = lambda t, ta, tb, nv: (tb[t], 0, 0)
    up, down = (None, D, F), (None, F, D)
    return pl.pallas_call(
        _ffn_sorted_body,
        grid_spec=pltpu.PrefetchScalarGridSpec(
            num_scalar_prefetch=3, grid=(nt,),
            in_specs=[
                pl.BlockSpec((FFN_TM,) + ROW_SHAPE, lambda t, ta, tb, nv: (t, 0, 0)),
                pl.BlockSpec(up, ea), pl.BlockSpec(up, ea), pl.BlockSpec(down, ea),
                pl.BlockSpec(up, eb), pl.BlockSpec(up, eb), pl.BlockSpec(down, eb),
            ],
            out_specs=pl.BlockSpec((FFN_TM, D // LANES, LANES), lambda t, ta, tb, nv: (t, 0, 0))),
        out_shape=jax.ShapeDtypeStruct((nt * FFN_TM, D // LANES, LANES), F32),
        compiler_params=_cp(("arbitrary",)),
        name="moe_ffn",
    )(ta, tb, nv, xs, w1, w3, w2, w1, w3, w2)


def _combine_body(pos_ref, x_ref, mod_ref, ys_hbm, g_ref, b_ref, *rest, tiles_per_seq, with_kv):
    if with_kv:
        kvw_ref, kvwt_ref, xn_ref, *kv_out, buf, sem = rest
    else:
        xn_ref, buf, sem = rest
    i = pl.program_id(0)
    nsteps = pl.num_programs(0)
    tc = x_ref.shape[0]

    def issue_group(step, slot, r0):
        group = buf.at[slot, pl.ds(r0, ROW_TILE)]
        for k in range(ROW_TILE):
            _row_copy(ys_hbm, pos_ref[step * tc + r0 + k], group, k, sem.at[slot]).start(priority=k % 2)

    def wait_tile(slot):
        pltpu.make_async_copy(ys_hbm.at[pl.ds(0, tc)], buf.at[slot], sem.at[slot]).wait()

    ahead = COMBINE_SLOTS - 1

    @pl.when(i == 0)
    def _():
        for t in range(ahead):
            def prime(g, carry, t=t):
                issue_group(t, t, pl.multiple_of(g * ROW_TILE, ROW_TILE))
                return carry
            lax.fori_loop(0, tc // ROW_TILE, prime, 0)

    slot = i % COMBINE_SLOTS
    wait_tile(slot)
    _, _, g1p = _split_mod(mod_ref, row=i // tiles_per_seq)
    gain, bias = g_ref[...], b_ref[...]
    nxt = jnp.minimum(i + ahead, nsteps - 1)
    nslot = (i + ahead) % COMBINE_SLOTS

    for c in range(tc // COMBINE_CHUNK):
        for g in range(COMBINE_CHUNK // ROW_TILE):
            issue_group(nxt, nslot, c * COMBINE_CHUNK + g * ROW_TILE)
        rows = pl.ds(c * COMBINE_CHUNK, COMBINE_CHUNK)
        z = ALPHA * x_ref[rows, :] + g1p * buf[slot, rows].reshape(COMBINE_CHUNK, D)
        xn_ref[rows, :] = _layer_norm(z, gain, bias)

    if with_kv:
        _kv_rows(xn_ref[...].astype(BF16), kvw_ref, kvwt_ref, *kv_out)

    @pl.when(i == nsteps - 1)
    def _():
        for k in range(1, COMBINE_SLOTS):
            wait_tile((i + k) % COMBINE_SLOTS)


def _combine(pos, x, mod, ys, ln_g, ln_b, kv_weights=None, *, layer, seq):
    n = x.shape[0]
    tc = COMBINE_TC
    sub = 2 * layer + 1
    row = lambda i, pos: (i, 0)
    in_specs = [
        pl.BlockSpec((tc, D), row),
        _layer_spec((n // seq, 3 * D), sub),
        pl.BlockSpec(memory_space=pl.ANY),
        _layer_spec((1, D), sub), _layer_spec((1, D), sub),
    ]
    out_specs = [pl.BlockSpec((tc, D), row)]
    out_shape = [jax.ShapeDtypeStruct((n, D), F32)]
    args = [pos, x, mod, ys, ln_g, ln_b]
    if kv_weights is not None:
        in_specs += [pl.BlockSpec((D, 2 * KVW), lambda i, pos: (0, 0)),
                     pl.BlockSpec((KVW, D), lambda i, pos: (0, 0))]
        out_specs += [pl.BlockSpec((tc, KVW), row), pl.BlockSpec((tc, KVW), row),
                      pl.BlockSpec((tc, KVW), row), pl.BlockSpec((KVW, tc), lambda i, pos: (0, i))]
        out_shape += [jax.ShapeDtypeStruct((n, KVW), F32), jax.ShapeDtypeStruct((n, KVW), F32),
                      jax.ShapeDtypeStruct((n, KVW), BF16), jax.ShapeDtypeStruct((KVW, n), BF16)]
        args += list(kv_weights)
    return pl.pallas_call(
        functools.partial(_combine_body, tiles_per_seq=seq // tc, with_kv=kv_weights is not None),
        grid_spec=pltpu.PrefetchScalarGridSpec(
            num_scalar_prefetch=1, grid=(n // tc,),
            in_specs=in_specs, out_specs=out_specs,
            scratch_shapes=[pltpu.VMEM((COMBINE_SLOTS, tc, D // LANES, LANES), F32),
                            pltpu.SemaphoreType.DMA((COMBINE_SLOTS,))]),
        out_shape=out_shape,
        compiler_params=_cp(("arbitrary",)),
        name="moe_combine",
    )(*args)


def _moe_sparse(x, route, mod, w1, w3, w2, ln_g, ln_b, kv_weights=None, *, layer, seq):
    n = x.shape[0]
    urow, meta, cnt = route
    cls = meta[:, 2, :].reshape(n).astype(jnp.int32)
    rank = meta[:, 3, :].reshape(n).astype(jnp.int32)
    count = cnt[:NCLS, 0].astype(jnp.int32)
    ntile = (count + FFN_TM - 1) // FFN_TM
    tile_end = jnp.cumsum(ntile)
    row_start = (tile_end - ntile) * FFN_TM
    pos = rank + jnp.sum(jax.nn.one_hot(cls, NCLS, dtype=jnp.int32) * row_start[None, :], axis=1)
    nt = n // FFN_TM + NCLS
    total = tile_end[NCLS - 1]
    tr = jnp.minimum(jnp.arange(nt, dtype=jnp.int32), total - 1)
    tcls = jnp.sum((tr[:, None] >= tile_end[None, :]).astype(jnp.int32), axis=1)
    grp, pair = tcls // PAIRS, tcls % PAIRS
    lo = (pair >= 3).astype(jnp.int32) + (pair >= 5).astype(jnp.int32)
    hi = pair + 1 - lo * (2 * E_PER_G - 3 - lo) // 2
    ta = grp * E_PER_G + lo
    tb = grp * E_PER_G + hi
    t_all = jnp.arange(nt, dtype=jnp.int32)
    partial_last = ((t_all[:, None] == tile_end[None, :] - 1) & (count % FFN_TM != 0)[None, :]).any(axis=1)
    zflag = ((t_all >= total) | partial_last).astype(jnp.int32)
    xs = _dispatch(pos, zflag, urow)
    of_cls = (tcls[:, None] == jnp.arange(NCLS, dtype=jnp.int32)[None, :]).astype(jnp.int32)
    into_cls = (t_all - jnp.sum(of_cls * (tile_end - ntile)[None, :], axis=1)) * FFN_TM
    nv = jnp.clip(jnp.sum(of_cls * count[None, :], axis=1) - into_cls, 0, FFN_TM)
    nv = jnp.where(t_all < total, nv, 0).astype(jnp.int32)
    ys = _ffn_sorted(ta, tb, nv, xs, w1, w3, w2)
    return _combine(pos, x, mod, ys, ln_g, ln_b, kv_weights, layer=layer, seq=seq)


def _moe_body(x_ref, mod_ref, wr_ref, br_ref, w1f_ref, w3f_ref, w2f_ref, g_ref, b_ref,
              xn_ref, w1_ref, w3_ref, w2_ref, ub, comb, acc, *, reps):
    e = pl.program_id(1)
    w1_ref[...] = w1f_ref[...].astype(BF16)
    w3_ref[...] = w3f_ref[...].astype(BF16)
    w2_ref[...] = w2f_ref[...].astype(BF16)

    @pl.when(e == 0)
    def _():
        shift, sc1p, _ = _split_mod(mod_ref, reps=reps)
        u = (x_ref[...] * sc1p + shift).astype(BF16)
        ub[...] = u
        eid, _, i1, i2, w1, w2 = _top2(_router_logits(u, wr_ref, br_ref))
        comb[...] = _rows_to_cols(jnp.where(eid == i1, w1, 0.0) + jnp.where(eid == i2, w2, 0.0))
        acc[...] = jnp.zeros(acc.shape, F32)

    u = ub[...]
    col = lax.broadcasted_iota(jnp.int32, comb.shape, 1)
    ce = jnp.sum(jnp.where(col == e, comb[...], 0.0), axis=-1, keepdims=True)
    h = jax.nn.silu(_dot(u, w1_ref[...])) * _dot(u, w3_ref[...]) * ce
    acc[...] += _dot(h.astype(BF16), w2_ref[...])

    @pl.when(e == N_EXP - 1)
    def _():
        _, _, g1p = _split_mod(mod_ref, reps=reps)
        xn_ref[...] = _layer_norm(ALPHA * x_ref[...] + g1p * acc[...], g_ref[...], b_ref[...])


def _moe_dense(x, mod, wr, br, w1, w3, w2, ln_g, ln_b, *, layer, reps):
    n = x.shape[0]
    sub = 2 * layer + 1
    expert = lambda i, e: (layer, e, 0, 0)
    return pl.pallas_call(
        functools.partial(_moe_body, reps=reps),
        grid=(1, N_EXP),
        in_specs=[
            pl.BlockSpec((n, D), lambda i, e: (0, 0)),
            _layer_spec((n // reps, 3 * D), sub),
            _layer_spec((RROWS, D), layer), _layer_spec((RROWS, RLANES), layer),
            pl.BlockSpec((None, None, D, F), expert),
            pl.BlockSpec((None, None, D, F), expert),
            pl.BlockSpec((None, None, F, D), expert),
            _layer_spec((1, D), sub), _layer_spec((1, D), sub),
        ],
        out_specs=[pl.BlockSpec((n, D), lambda i, e: (0, 0)),
                   pl.BlockSpec((None, D, F), lambda i, e: (e, 0, 0)),
                   pl.BlockSpec((None, D, F), lambda i, e: (e, 0, 0)),
                   pl.BlockSpec((None, F, D), lambda i, e: (e, 0, 0))],
        out_shape=[jax.ShapeDtypeStruct((n, D), F32),
                   jax.ShapeDtypeStruct((N_EXP, D, F), BF16),
                   jax.ShapeDtypeStruct((N_EXP, D, F), BF16),
                   jax.ShapeDtypeStruct((N_EXP, F, D), BF16)],
        scratch_shapes=[
            pltpu.VMEM((n, D), BF16),
            pltpu.VMEM((n, RLANES), F32),
            pltpu.VMEM((n, D), F32),
        ],
        compiler_params=_cp(("arbitrary", "arbitrary")),
        name="moe",
    )(x, mod, wr, br, w1, w3, w2, ln_g, ln_b)


def _kv_body(x_ref, w_ref, wvt_ref, k_ref, v_ref, k16_ref, vt16_ref):
    _kv_rows(x_ref[...].astype(BF16), w_ref, wvt_ref, k_ref, v_ref, k16_ref, vt16_ref)


def _kv_rows(xb, w_ref, wvt_ref, k_ref, v_ref, k16_ref, vt16_ref):
    kv = _dot(xb, w_ref[...])
    k_ref[...] = kv[:, :KVW]
    v_ref[...] = kv[:, KVW:]
    k16_ref[...] = kv[:, :KVW].astype(BF16)
    vt16_ref[...] = lax.dot_general(wvt_ref[...], xb, _NT, preferred_element_type=F32).astype(BF16)


def _kv(x, w, wvt, *, tm):
    n = x.shape[0]
    row = lambda i: (i, 0)
    return pl.pallas_call(
        _kv_body,
        grid=(n // tm,),
        in_specs=[pl.BlockSpec((tm, D), row), _const_spec((D, 2 * KVW)), _const_spec((KVW, D))],
        out_specs=[pl.BlockSpec((tm, KVW), row), pl.BlockSpec((tm, KVW), row),
                   pl.BlockSpec((tm, KVW), row), pl.BlockSpec((KVW, tm), lambda i: (0, i))],
        out_shape=[jax.ShapeDtypeStruct((n, KVW), F32), jax.ShapeDtypeStruct((n, KVW), F32),
                   jax.ShapeDtypeStruct((n, KVW), BF16), jax.ShapeDtypeStruct((KVW, n), BF16)],
        compiler_params=_cp(("arbitrary",)),
        name="kv_proj",
    )(x, w, wvt)


def _sink_attend(qh, kh, vh, bias, sink):
    s = lax.dot_general(qh, kh, (((1,), (1,)), ((), ())), preferred_element_type=F32) + bias
    m = jnp.maximum(jnp.max(s, axis=-1, keepdims=True), sink)
    p = jnp.exp(s - m)
    den = jnp.sum(p, axis=-1, keepdims=True) + jnp.exp(sink - m)
    return _dot(p.astype(BF16), vh) / den


def _attn_prompt_body(x_ref, mod_ref, wqt_ref, wo_ref, kc_ref, kp_ref, vtc_ref, vtp_ref,
                      bias_ref, sink_ref, g_ref, b_ref, mod1_ref, wr_ref, br_ref,
                      xn_ref, urow_ref, meta_ref, cnt_ref, carry, *, tq):
    t = pl.program_id(1)
    x = x_ref[...]
    shift, sc1p, g1p = _split_mod(mod_ref, row=pl.program_id(0))
    u = (x * sc1p + shift).astype(BF16)
    qt = lax.dot_general(wqt_ref[...], u, _NT, preferred_element_type=F32)
    qt = (qt * (HD ** -0.5)).astype(BF16)
    kc = kc_ref[...]
    vtc = vtc_ref[...]
    krow = lax.broadcasted_iota(jnp.int32, (2 * WIN, GRP * WIN), 0)
    first = jnp.where((krow < WIN) & (t == 0), NEG, 0.0).astype(F32)
    nwin = tq // WIN
    kks, vvts = [], []
    for j in range(nwin):
        if j == 0:
            kks.append(jnp.concatenate([kp_ref[...], kc[0:WIN]], axis=0))
            vvts.append(jnp.concatenate([vtp_ref[...], vtc[:, 0:WIN]], axis=1))
        else:
            kks.append(kc[(j - 1) * WIN:(j + 1) * WIN])
            vvts.append(vtc[:, (j - 1) * WIN:(j + 1) * WIN])

    def scores(j, hk):
        qht = jnp.concatenate(
            [qt[(hk * GRP + g) * HD:(hk * GRP + g + 1) * HD, j * WIN:(j + 1) * WIN] for g in range(GRP)],
            axis=1)
        st = _dot(kks[j][:, hk * HD:(hk + 1) * HD], qht) + bias_ref[hk]
        return st + first if j == 0 else st

    items = [(j, hk) for j in range(nwin) for hk in range(HKV)]
    heads = [[] for _ in range(nwin)]
    st_next = scores(*items[0])
    for n, (j, hk) in enumerate(items):
        st = st_next
        if n + 1 < len(items):
            st_next = scores(*items[n + 1])
        sink = sink_ref[hk]
        m = jnp.maximum(jnp.max(st, axis=0, keepdims=True), sink)
        pt = jnp.exp(st - m)
        den = jnp.sum(pt, axis=0, keepdims=True) + jnp.exp(sink - m)
        ot = _dot(vvts[j][hk * HD:(hk + 1) * HD, :], pt.astype(BF16)) * (1.0 / den)
        for g in range(GRP):
            heads[j].append(ot[:, g * WIN:(g + 1) * WIN])
    blocks = [jnp.concatenate(h, axis=0) for h in heads]
    oallt = jnp.concatenate(blocks, axis=1) if len(blocks) > 1 else blocks[0]
    mix = lax.dot_general(oallt.astype(BF16), wo_ref[...], _TN, preferred_element_type=F32)
    xn = _layer_norm(ALPHA * x + g1p * mix, g_ref[...], b_ref[...])
    xn_ref[...] = xn
    b = pl.program_id(0)
    _route_rows(xn, b, (b == 0) & (t == 0), mod1_ref, wr_ref, br_ref, urow_ref, meta_ref, cnt_ref, carry)


def _attn_prompt(x, mod, wqt, wo, k16, vt16, bias, sink, ln_g, ln_b, wr, br, *, layer, nb, seq, tq):
    nt = seq // tq
    r_in, r_out, r_shape, r_scratch = _route_specs(layer, nb, nt, tq)
    wpt = tq // WIN
    n = nb * seq
    j = layer - N_A
    sub = 2 * layer

    def prev(b, t):
        return jnp.maximum((b * nt + t) * wpt - 1, 0)

    cur = lambda b, t: (b * nt + t, 0)
    return pl.pallas_call(
        functools.partial(_attn_prompt_body, tq=tq),
        grid=(nb, nt),
        in_specs=[
            pl.BlockSpec((tq, D), cur),
            _layer_spec((nb, 3 * D), sub),
            _layer_spec((HQ * HD, D), j), _layer_spec((HQ * HD, D), j),
            pl.BlockSpec((tq, KVW), cur), pl.BlockSpec((WIN, KVW), lambda b, t: (prev(b, t), 0)),
            pl.BlockSpec((KVW, tq), lambda b, t: (0, b * nt + t)),
            pl.BlockSpec((KVW, WIN), lambda b, t: (0, prev(b, t))),
            _const_spec((HKV, 2 * WIN, GRP * WIN)), _layer_spec((HKV, 1, GRP * WIN), j),
            _layer_spec((1, D), sub), _layer_spec((1, D), sub),
        ] + r_in,
        out_specs=[pl.BlockSpec((tq, D), cur)] + r_out,
        out_shape=[jax.ShapeDtypeStruct((n, D), F32)] + r_shape,
        scratch_shapes=r_scratch,
        compiler_params=_cp(("arbitrary", "arbitrary")),
        name="attn_prompt",
    )(x, mod, wqt, wo, k16, k16, vt16, vt16, bias, sink, ln_g, ln_b, mod, wr, br)


def _attn_sample_body(x_ref, mod_ref, wq_ref, wo_ref, kb_ref, vb_ref, kn_ref, vn_ref,
                      bias_ref, sink_ref, g_ref, b_ref, xn_ref, *, ns, steps):
    nr = ns * steps
    x = x_ref[...].reshape(nr, D)
    shift, sc1p, g1p = _split_mod(mod_ref, reps=steps)
    u = (x * sc1p + shift).astype(BF16)
    q = (_dot(u, wq_ref[...]) * (HD ** -0.5)).astype(BF16)
    nbuf = kb_ref.shape[1]
    k_all = jnp.concatenate([kb_ref[...].reshape(ns * nbuf, KVW), kn_ref[...].reshape(nr, KVW)],
                            axis=0).astype(BF16)
    v_all = jnp.concatenate([vb_ref[...].reshape(ns * nbuf, KVW), vn_ref[...].reshape(nr, KVW)],
                            axis=0).astype(BF16)
    heads = [None] * HQ
    for hk in range(HKV):
        kcols = slice(hk * HD, (hk + 1) * HD)
        qh = jnp.concatenate(
            [q[:, (hk * GRP + g) * HD:(hk * GRP + g + 1) * HD] for g in range(GRP)], axis=0)
        o = _sink_attend(qh, k_all[:, kcols], v_all[:, kcols], bias_ref[hk], sink_ref[hk])
        for g in range(GRP):
            heads[hk * GRP + g] = o[g * nr:(g + 1) * nr]
    o_all = jnp.concatenate(heads, axis=1)
    mix = _dot(o_all.astype(BF16), wo_ref[...])
    xn_ref[...] = _layer_norm(ALPHA * x + g1p * mix, g_ref[...], b_ref[...]).reshape(steps, ns, D)


def _attn_sample(x, mod, wq, wo, kbuf, vbuf, kn, vn, bias, sink, ln_g, ln_b, *, layer, ns):
    steps, nb, _ = x.shape
    nbuf = kbuf.shape[1]
    j = layer - N_A
    sub = 2 * layer
    seqs = lambda i: (0, i, 0)
    return pl.pallas_call(
        functools.partial(_attn_sample_body, ns=ns, steps=steps),
        grid=(nb // ns,),
        in_specs=[
            pl.BlockSpec((steps, ns, D), seqs),
            pl.BlockSpec((None, ns, 3 * D), lambda i: (sub, i, 0)),
            _layer_spec((D, HQ * HD), j), _layer_spec((HQ * HD, D), j),
            pl.BlockSpec((ns, nbuf, KVW), lambda i: (i, 0, 0)),
            pl.BlockSpec((ns, nbuf, KVW), lambda i: (i, 0, 0)),
            pl.BlockSpec((steps, ns, KVW), seqs), pl.BlockSpec((steps, ns, KVW), seqs),
            _const_spec(bias.shape), _layer_spec(sink.shape[1:], j),
            _layer_spec((1, D), sub), _layer_spec((1, D), sub),
        ],
        out_specs=pl.BlockSpec((steps, ns, D), seqs),
        out_shape=jax.ShapeDtypeStruct((steps, nb, D), F32),
        compiler_params=_cp(("arbitrary",)),
        name="attn_sample",
    )(x, mod, wq, wo, kbuf, vbuf, kn, vn, bias, sink, ln_g, ln_b)


def _t5_bucket(d):
    max_exact = N_BUCKETS // 2
    d = np.maximum(d, 0)
    log_ratio = (np.log(np.maximum(d, 1).astype(np.float32) / np.float32(max_exact))
                 / np.float32(math.log(WIN / max_exact)))
    large = max_exact + (log_ratio * (N_BUCKETS - max_exact)).astype(np.int32)
    large = np.minimum(large, N_BUCKETS - 1)
    return np.where(d < max_exact, d, large)


def _bias_table(d, ok, rel_bias):
    sel = np.where(ok, _t5_bucket(d), N_BUCKETS)
    onehot = (sel[..., None] == np.arange(N_BUCKETS + 1)).astype(np.float32)
    tab = jnp.concatenate([rel_bias.astype(F32), jnp.full((1, HQ), NEG, F32)], axis=0)
    return jnp.einsum('abd,dh->hab', jnp.asarray(onehot), tab, precision=lax.Precision.HIGHEST)


def _prompt_bias(rel_bias):
    d = np.arange(WIN)[None, :] + WIN - np.arange(2 * WIN)[:, None]
    b = _bias_table(d, (d >= 0) & (d < WIN), rel_bias).reshape(HKV, GRP, 2 * WIN, WIN)
    return jnp.transpose(b, (0, 2, 1, 3)).reshape(HKV, 2 * WIN, GRP * WIN)


def _sample_bias(rel_bias, ns, steps, nbuf):
    qs = np.tile(np.arange(ns), steps)
    qt = np.repeat(np.arange(steps), ns)
    ks = np.concatenate([np.repeat(np.arange(ns), nbuf), qs])
    kpos = np.concatenate([np.tile(np.arange(nbuf), ns), nbuf + qt])
    d = qt[:, None] + nbuf - kpos[None, :]
    ok = (d >= 0) & (d < WIN) & (qs[:, None] == ks[None, :])
    return _bias_table(d, ok, rel_bias).reshape(HKV, GRP * ns * steps, kpos.shape[0])


LRU_TM = 512
ATTN_TQ = 1024
SAMPLE_NS = 8


def kernel(x_prompt, x_sample, state_rnn_h, state_rnn_conv, cache_win_k, cache_win_v, c_prompt, c_sample, ada_w, ada_b, ln_g, ln_b, lru_w_in, lru_conv_w, lru_conv_b, lru_w_a, lru_b_a, lru_w_x, lru_b_x, lru_lambda, lru_w_out, kv_w, attn_w_q, attn_sinks, attn_w_o, rel_bias, moe_w_group, moe_b_group, moe_w_router, moe_b_router, moe_w1, moe_w3, moe_w2):
    nbp, seq, _ = x_prompt.shape
    nbs, steps, _ = x_sample.shape
    nbuf = cache_win_k.shape[1]
    npt = nbp * seq
    nst = nbs * steps

    mod_p, mod_s = _adaln(c_prompt, c_sample, ada_w, ada_b)

    ln_g3 = ln_g.reshape(2 * DEPTH, 1, D)
    ln_b3 = ln_b.reshape(2 * DEPTH, 1, D)
    lw = (lru_w_in.astype(BF16), lru_conv_w, lru_conv_b[:, None], lru_w_a.astype(BF16),
          lru_w_x.astype(BF16), lru_b_a[:, None], lru_b_x[:, None], lru_lambda[:, None],
          lru_w_out.astype(BF16))
    zrow = lambda k: jnp.zeros((DEPTH, k, D), F32)
    wr = jnp.concatenate([jnp.swapaxes(moe_w_group, 1, 2), zrow(EROW - N_GROUPS),
                          jnp.swapaxes(moe_w_router.reshape(DEPTH, D, N_EXP), 1, 2),
                          zrow(RROWS - EROW - N_EXP)], axis=1).astype(BF16)
    zb = lambda k: jnp.zeros((DEPTH, k), F32)
    br = jnp.concatenate([moe_b_group, zb(EROW - N_GROUPS), moe_b_router.reshape(DEPTH, N_EXP),
                          zb(RROWS - EROW - N_EXP)], axis=1)
    br = jnp.broadcast_to(br[:, :, None], (DEPTH, RROWS, RLANES))
    wq = attn_w_q.astype(BF16)
    wqt = jnp.swapaxes(wq, 1, 2)
    wo = attn_w_o.astype(BF16)
    kvw = kv_w.astype(BF16)
    kvwt = kvw[:, KVW:].T
    bias_p = _prompt_bias(rel_bias)
    bias_s = _sample_bias(rel_bias, SAMPLE_NS, steps, nbuf)
    sink_p = jnp.repeat(attn_sinks.reshape(N_B, HKV, GRP), WIN, axis=2)[:, :, None, :]
    sink_s = jnp.repeat(attn_sinks.reshape(N_B, HKV, GRP), SAMPLE_NS * steps, axis=2)[..., None]
    kbuf = cache_win_k.reshape(nbs, nbuf, KVW)
    vbuf = cache_win_v.reshape(nbs, nbuf, KVW)

    xp = x_prompt.reshape(npt, D)
    xs = jnp.transpose(x_sample, (1, 0, 2)).reshape(nst, D)
    c0 = jnp.transpose(state_rnn_conv, (0, 2, 1, 3)).reshape(N_A, (CONV_W - 1) * nbs, R)
    hp, cp_, hs, cs = [], [], [], []
    kp = vp = kn = vn = kp16 = vpt16 = None
    for l in range(DEPTH):
        if l < N_A:
            xp, h_l, c_l, *route = _lru_prompt(xp, mod_p, lw, ln_g3, ln_b3, wr, br,
                                               layer=l, nb=nbp, seq=seq, tm=LRU_TM)
            hp.append(h_l[:, 0])
            cp_.append(c_l)
            xs, h_l, c_l = _lru_sample(xs, mod_s, state_rnn_h, c0[l], lw, ln_g3, ln_b3,
                                       layer=l, nb=nbs, steps=steps)
            hs.append(h_l)
            cs.append(jnp.transpose(c_l.reshape(CONV_W - 1, nbs, R), (1, 0, 2)))
        else:
            xp, *route = _attn_prompt(xp, mod_p, wqt, wo, kp16, vpt16, bias_p, sink_p, ln_g3, ln_b3, wr, br,
                                      layer=l, nb=nbp, seq=seq, tq=ATTN_TQ)
            xs = _attn_sample(xs.reshape(steps, nbs, D), mod_s, wq, wo, kbuf, vbuf,
                              kn.reshape(steps, nbs, KVW), vn.reshape(steps, nbs, KVW),
                              bias_s, sink_s, ln_g3, ln_b3, layer=l, ns=SAMPLE_NS).reshape(nst, D)
        xs, w1, w3, w2 = _moe_dense(xs, mod_s, wr, br, moe_w1, moe_w3, moe_w2, ln_g3, ln_b3,
                                    layer=l, reps=steps)
        if l == N_A - 1:
            xp, kp, vp, kp16, vpt16 = _moe_sparse(xp, route, mod_p, w1, w3, w2, ln_g3, ln_b3, (kvw, kvwt),
                                                  layer=l, seq=seq)
            kn, vn, _, _ = _kv(xs, kvw, kvwt, tm=nst)
        else:
            xp, = _moe_sparse(xp, route, mod_p, w1, w3, w2, ln_g3, ln_b3, layer=l, seq=seq)

    lp = min(WIN, seq)
    k_p = kp.reshape(nbp, seq, HKV, HD)[:, seq - lp:]
    v_p = vp.reshape(nbp, seq, HKV, HD)[:, seq - lp:]
    kn4 = jnp.transpose(kn.reshape(steps, nbs, HKV, HD), (1, 0, 2, 3))
    vn4 = jnp.transpose(vn.reshape(steps, nbs, HKV, HD), (1, 0, 2, 3))
    k_s = jnp.concatenate([cache_win_k, kn4], axis=1)[:, -nbuf:]
    v_s = jnp.concatenate([cache_win_v, vn4], axis=1)[:, -nbuf:]
    y_s = jnp.transpose(xs.reshape(steps, nbs, D), (1, 0, 2))
    return (xp.reshape(nbp, seq, D), y_s,
            jnp.stack(hp), jnp.stack(cp_), k_p, v_p,
            jnp.stack(hs), jnp.stack(cs), k_s, v_s)
```

```python
import functools
import math

import jax
import jax.numpy as jnp
import numpy as np
from jax import lax
from jax.experimental import pallas as pl
from jax.experimental.pallas import tpu as pltpu

D = 1024
R = 1024
DEPTH = 4
N_A = 2
N_B = DEPTH - N_A
N_BLK = 4
BLK = R // N_BLK
CONV_W = 4
RG_C = 8.0
HQ = 16
HKV = 4
HD = 64
GRP = HQ // HKV
KVW = HKV * HD
WIN = 128
N_BUCKETS = 32
N_GROUPS = 4
E_PER_G = 4
N_EXP = 16
F = 512
ALPHA = (2.0 * DEPTH) ** 0.25
LN_EPS = 1e-5
NEG = -1e30
LANES = 128
RLANES = LANES
ROW_TILE = 8

F32 = jnp.float32
BF16 = jnp.bfloat16

VMEM_LIMIT = 56 * 1024 * 1024


def _cp(sem):
    return pltpu.CompilerParams(dimension_semantics=sem, vmem_limit_bytes=VMEM_LIMIT)


def _dot(a, b):
    return jnp.dot(a, b, preferred_element_type=F32)


def _layer_norm(z, g, b):
    mu = jnp.mean(z, axis=-1, keepdims=True)
    zc = z - mu
    var = jnp.mean(zc * zc, axis=-1, keepdims=True)
    return zc * lax.rsqrt(var + LN_EPS) * g + b


def _split_mod(mod_ref, row=None, reps=1):
    m = mod_ref[...] if row is None else mod_ref[pl.ds(row, 1), :]
    if reps > 1:
        m = jnp.concatenate([m] * reps, axis=0)
    return m[:, 0:D], m[:, D:2 * D], m[:, 2 * D:3 * D]


def _layer_spec(shape, layer):
    nd = len(shape)
    return pl.BlockSpec((None,) + tuple(shape), lambda *_: (layer,) + (0,) * nd)


def _adaln_body(cp_ref, cs_ref, w_ref, b_ref, op_ref, os_ref):
    one = jnp.where(pl.program_id(1) > 0, 1.0, 0.0).astype(F32)
    w = w_ref[...].astype(BF16)
    op_ref[...] = _dot(cp_ref[...].astype(BF16), w) + b_ref[...] + one
    os_ref[...] = _dot(cs_ref[...].astype(BF16), w) + b_ref[...] + one


def _adaln(c_p, c_s, ada_w, ada_b):
    np_, ns_ = c_p.shape[0], c_s.shape[0]
    w = ada_w.reshape(2 * DEPTH, D, 3 * D)
    b = ada_b.reshape(2 * DEPTH, 1, 3 * D)
    return pl.pallas_call(
        _adaln_body,
        grid=(2 * DEPTH, 3),
        in_specs=[
            pl.BlockSpec((np_, D), lambda l, j: (0, 0)),
            pl.BlockSpec((ns_, D), lambda l, j: (0, 0)),
            pl.BlockSpec((None, D, D), lambda l, j: (l, 0, j)),
            pl.BlockSpec((None, 1, D), lambda l, j: (l, 0, j)),
        ],
        out_specs=[pl.BlockSpec((None, np_, D), lambda l, j: (l, 0, j)),
                   pl.BlockSpec((None, ns_, D), lambda l, j: (l, 0, j))],
        out_shape=[jax.ShapeDtypeStruct((2 * DEPTH, np_, 3 * D), F32),
                   jax.ShapeDtypeStruct((2 * DEPTH, ns_, 3 * D), F32)],
        compiler_params=_cp(("arbitrary", "arbitrary")),
        name="adaln",
    )(c_p, c_s, w, b)


def _lru_gates(xc, n, wa_ref, wx_ref, ba_ref, bx_ref, lam_ref):
    cols = slice(n * BLK, (n + 1) * BLK)
    xb = xc.astype(BF16)
    r = jax.nn.sigmoid(_dot(xb, wa_ref[n]) + ba_ref[:, cols])
    i = jax.nn.sigmoid(_dot(xb, wx_ref[n]) + bx_ref[:, cols])
    z = -lam_ref[:, cols]
    softplus = jnp.maximum(z, 0.0) + jnp.log1p(jnp.exp(-jnp.abs(z)))
    log_a = (-RG_C) * r * softplus
    a = jnp.exp(log_a)
    v = 1.0 - a * a
    b = jnp.where(v > 0.0, v * lax.rsqrt(v), 0.0) * i * xc
    return a, b


def _lru_prompt_body(x_ref, mod_ref, win_ref, cw_ref, cb_ref, wa_ref, wx_ref, ba_ref, bx_ref,
                     lam_ref, wout_ref, g_ref, b_ref, mod1_ref, wr_ref, br_ref,
                     xn_ref, hl_ref, cl_ref, urow_ref, meta_ref, cnt_ref,
                     xbuf, abuf, hbuf, hc, carry, *, tm):
    t = pl.program_id(1)

    @pl.when(t == 0)
    def _():
        xbuf[0:ROW_TILE, :] = jnp.zeros((ROW_TILE, R), F32)
        hc[...] = jnp.zeros((1, R), F32)

    x = x_ref[...]
    shift, sc1p, g1p = _split_mod(mod_ref, row=pl.program_id(0))
    u = x * sc1p + shift
    yx = _dot(u.astype(BF16), win_ref[...])
    y = jax.nn.gelu(yx[:, :R])
    xbuf[ROW_TILE:ROW_TILE + tm, :] = yx[:, R:]

    ngrp = tm // ROW_TILE
    rowmod = lax.broadcasted_iota(jnp.int32, (ngrp, ROW_TILE, BLK), 1)
    for n in range(N_BLK):
        cols = slice(n * BLK, (n + 1) * BLK)
        xc = cb_ref[:, cols] + cw_ref[3:4, cols] * xbuf[ROW_TILE:ROW_TILE + tm, cols]
        for k in range(1, CONV_W):
            xc = xc + cw_ref[3 - k:4 - k, cols] * xbuf[ROW_TILE - k:ROW_TILE - k + tm, cols]
        a, b = _lru_gates(xc, n, wa_ref, wx_ref, ba_ref, bx_ref, lam_ref)
        a = a.reshape(ngrp, ROW_TILE, BLK)
        b = b.reshape(ngrp, ROW_TILE, BLK)
        s = 1
        while s < ROW_TILE:
            a_sh = pltpu.roll(a, s, 1)
            b_sh = pltpu.roll(b, s, 1)
            m = rowmod >= s
            b = jnp.where(m, a * b_sh, 0.0) + b
            a = jnp.where(m, a * a_sh, a)
            s *= 2
        abuf[:, cols] = a.reshape(tm, BLK)
        hbuf[:, cols] = b.reshape(tm, BLK)

    def group(g, h):
        r0 = pl.multiple_of(g * ROW_TILE, ROW_TILE)
        hg = abuf[pl.ds(r0, ROW_TILE), :] * h + hbuf[pl.ds(r0, ROW_TILE), :]
        hbuf[pl.ds(r0, ROW_TILE), :] = hg
        return hg[ROW_TILE - 1:ROW_TILE, :]

    h_end = lax.fori_loop(0, tm // ROW_TILE, group, hc[...])
    hc[...] = h_end
    hl_ref[...] = h_end
    tail = xbuf[tm + ROW_TILE - (CONV_W - 1):tm + ROW_TILE, :]
    cl_ref[...] = tail
    xbuf[ROW_TILE - (CONV_W - 1):ROW_TILE, :] = tail

    mix = _dot((y * hbuf[...]).astype(BF16), wout_ref[...])
    xn = _layer_norm(ALPHA * x + g1p * mix, g_ref[...], b_ref[...])
    xn_ref[...] = xn
    b = pl.program_id(0)
    _route_rows(xn, b, (b == 0) & (t == 0), mod1_ref, wr_ref, br_ref, urow_ref, meta_ref, cnt_ref, carry)


def _const_spec(shape):
    nd = len(shape)
    return pl.BlockSpec(shape, lambda *_: (0,) * nd)


def _lru_weight_specs(layer):
    sub = 2 * layer
    return [
        _layer_spec((D, 2 * R), layer), _layer_spec((CONV_W, R), layer), _layer_spec((1, R), layer),
        _layer_spec((N_BLK, BLK, BLK), layer), _layer_spec((N_BLK, BLK, BLK), layer),
        _layer_spec((1, R), layer), _layer_spec((1, R), layer), _layer_spec((1, R), layer),
        _layer_spec((R, D), layer), _layer_spec((1, D), sub), _layer_spec((1, D), sub),
    ]


def _lru_prompt(x, mod, lw, ln_g, ln_b, wr, br, *, layer, nb, seq, tm):
    nt = seq // tm
    n = nb * seq
    r_in, r_out, r_shape, r_scratch = _route_specs(layer, nb, nt, tm)
    return pl.pallas_call(
        functools.partial(_lru_prompt_body, tm=tm),
        grid=(nb, nt),
        in_specs=[
            pl.BlockSpec((tm, D), lambda b, t: (b * nt + t, 0)),
            _layer_spec((nb, 3 * D), 2 * layer),
        ] + _lru_weight_specs(layer) + r_in,
        out_specs=[
            pl.BlockSpec((tm, D), lambda b, t: (b * nt + t, 0)),
            pl.BlockSpec((None, 1, R), lambda b, t: (b, 0, 0)),
            pl.BlockSpec((None, CONV_W - 1, R), lambda b, t: (b, 0, 0)),
        ] + r_out,
        out_shape=[
            jax.ShapeDtypeStruct((n, D), F32),
            jax.ShapeDtypeStruct((nb, 1, R), F32),
            jax.ShapeDtypeStruct((nb, CONV_W - 1, R), F32),
        ] + r_shape,
        scratch_shapes=[
            pltpu.VMEM((tm + ROW_TILE, R), F32),
            pltpu.VMEM((tm, R), F32),
            pltpu.VMEM((tm, R), F32),
            pltpu.VMEM((1, R), F32),
        ] + r_scratch,
        compiler_params=_cp(("arbitrary", "arbitrary")),
        name="lru_prompt",
    )(x, mod, *lw, ln_g, ln_b, mod, wr, br)


def _lru_sample_body(x_ref, mod_ref, h0_ref, c0_ref, win_ref, cw_ref, cb_ref, wa_ref, wx_ref,
                     ba_ref, bx_ref, lam_ref, wout_ref, g_ref, b_ref,
                     xn_ref, hl_ref, cl_ref, hbuf, *, nb, steps):
    x = x_ref[...]
    shift, sc1p, g1p = _split_mod(mod_ref, reps=steps)
    u = x * sc1p + shift
    yx = _dot(u.astype(BF16), win_ref[...])
    y = jax.nn.gelu(yx[:, :R])
    xext = jnp.concatenate([c0_ref[...], yx[:, R:]], axis=0)
    cl_ref[...] = xext[steps * nb:, :]
    for n in range(N_BLK):
        cols = slice(n * BLK, (n + 1) * BLK)
        xc = cb_ref[:, cols] + cw_ref[0:1, cols] * xext[0:steps * nb, cols]
        for k in range(1, CONV_W):
            xc = xc + cw_ref[k:k + 1, cols] * xext[k * nb:(k + steps) * nb, cols]
        a, b = _lru_gates(xc, n, wa_ref, wx_ref, ba_ref, bx_ref, lam_ref)
        h = h0_ref[:, cols]
        for t in range(steps):
            h = a[t * nb:(t + 1) * nb, :] * h + b[t * nb:(t + 1) * nb, :]
            hbuf[t * nb:(t + 1) * nb, cols] = h
        hl_ref[:, cols] = h
    mix = _dot((y * hbuf[...]).astype(BF16), wout_ref[...])
    xn_ref[...] = _layer_norm(ALPHA * x + g1p * mix, g_ref[...], b_ref[...])


def _lru_sample(x, mod, h0, c0, lw, ln_g, ln_b, *, layer, nb, steps):
    n = nb * steps
    return pl.pallas_call(
        functools.partial(_lru_sample_body, nb=nb, steps=steps),
        grid=(1,),
        in_specs=[
            _const_spec((n, D)), _layer_spec((nb, 3 * D), 2 * layer), _layer_spec((nb, R), layer),
            _const_spec(((CONV_W - 1) * nb, R)),
        ] + _lru_weight_specs(layer),
        out_specs=[_const_spec((n, D)), _const_spec((nb, R)), _const_spec(((CONV_W - 1) * nb, R))],
        out_shape=[
            jax.ShapeDtypeStruct((n, D), F32),
            jax.ShapeDtypeStruct((nb, R), F32),
            jax.ShapeDtypeStruct(((CONV_W - 1) * nb, R), F32),
        ],
        scratch_shapes=[pltpu.VMEM((n, R), F32)],
        compiler_params=_cp(("arbitrary",)),
        name="lru_sample",
    )(x, mod, h0, c0, *lw, ln_g, ln_b)


RROWS = 32
EROW = 8
_NT = (((1,), (1,)), ((), ()))
_TN = (((0,), (0,)), ((), ()))


def _top2(lt):
    nt = lt.shape[1]
    gl = lt[0:EROW, :]
    gid = lax.broadcasted_iota(jnp.int32, (EROW, nt), 0).astype(F32)
    is_g = gid < N_GROUPS
    gl = jnp.where(is_g, gl, NEG)
    gmax = jnp.max(gl, axis=0, keepdims=True)
    gidx = jnp.min(jnp.where(gl == gmax, gid, float(EROW)), axis=0, keepdims=True)
    gsum = jnp.sum(jnp.where(is_g, jnp.exp(gl - gmax), 0.0), axis=0, keepdims=True)
    g_p = 1.0 / gsum
    el = lt[EROW:EROW + N_EXP, :]
    eid = lax.broadcasted_iota(jnp.int32, (N_EXP, nt), 0).astype(F32)
    first = gidx * E_PER_G
    in_grp = (eid >= first) & (eid < first + E_PER_G)
    el = jnp.where(in_grp, el, NEG)
    big = float(N_EXP)
    t1 = jnp.max(el, axis=0, keepdims=True)
    i1 = jnp.min(jnp.where(el == t1, eid, big), axis=0, keepdims=True)
    el2 = jnp.where(eid == i1, NEG, el)
    t2 = jnp.max(el2, axis=0, keepdims=True)
    i2 = jnp.min(jnp.where((el2 == t2) & in_grp & (eid != i1), eid, big), axis=0, keepdims=True)
    e21 = jnp.exp(t2 - t1)
    w1 = g_p / (1.0 + e21)
    w2 = w1 * e21
    return eid, gidx, i1, i2, w1, w2


def _router_logits(u16, wrt_ref, brt_ref):
    return lax.dot_general(wrt_ref[...], u16, _NT, preferred_element_type=F32) + brt_ref[:, 0:1]


def _rows_to_cols(rows):
    r, nt = rows.shape
    padded = jnp.concatenate([rows, jnp.zeros((RLANES - r, nt), F32)], axis=0)
    return jnp.transpose(padded)


PAIRS = E_PER_G * (E_PER_G - 1) // 2
NCLS = N_GROUPS * PAIRS
HALF = D // 2
ROW_SHAPE = (D // LANES, LANES)
HI_MASK = -65536
FFN_TM = 256
DISPATCH_TD = 2048
COMBINE_TC = 1024
COMBINE_CHUNK = 64
COMBINE_SLOTS = 3


def _route_rows(x, row, first, mod_ref, wr_ref, br_ref, urow_ref, meta_ref, cnt_ref, carry):
    @pl.when(first)
    def _():
        carry[...] = jnp.zeros(carry.shape, F32)

    tm = x.shape[0]
    shift, sc1p, _ = _split_mod(mod_ref, row=row)
    u = x * sc1p + shift
    _, gidx, i1, i2, w1, w2 = _top2(_router_logits(u.astype(BF16), wr_ref, br_ref))
    first = gidx * E_PER_G
    lo = jnp.minimum(i1, i2) - first
    hi = jnp.maximum(i1, i2) - first
    w_lo = jnp.where(i1 < i2, w1, w2)
    w_hi = jnp.where(i1 < i2, w2, w1)
    cls = gidx * PAIRS + lo * (2 * E_PER_G - 1 - lo) * 0.5 + (hi - lo - 1.0)
    cid = lax.broadcasted_iota(jnp.int32, (RROWS, tm), 0).astype(F32)
    onehot = (cid == cls).astype(F32)
    r = lax.broadcasted_iota(jnp.int32, (tm, tm), 0)
    c = lax.broadcasted_iota(jnp.int32, (tm, tm), 1)
    earlier = _dot(onehot.astype(BF16), (r < c).astype(BF16)) + carry[:, 0:1]
    rank = jnp.sum(onehot * earlier, axis=0, keepdims=True)
    carry[...] += jnp.sum(onehot, axis=1, keepdims=True)
    cnt_ref[...] = carry[...]
    meta = jnp.concatenate([w_lo, w_hi, cls, rank, jnp.zeros((ROW_TILE - 4, tm), F32)], axis=0)
    meta_ref[...] = meta
    ub = u.astype(BF16)
    lo = lax.shift_right_logical(pltpu.bitcast(ub[:, 0:HALF].astype(F32), jnp.int32), 16)
    hi = pltpu.bitcast(ub[:, HALF:D].astype(F32), jnp.int32) & HI_MASK
    ext = pltpu.bitcast(_rows_to_cols(meta), jnp.int32)
    row = jnp.concatenate([hi | lo, ext, jnp.zeros((tm, D - HALF - RLANES), jnp.int32)], axis=1)
    urow_ref[...] = row.reshape((tm,) + ROW_SHAPE)


def _route_specs(layer, nb, nt, tm):
    tile = lambda b, t: b * nt + t
    n = nb * nt * tm
    in_specs = [_layer_spec((nb, 3 * D), 2 * layer + 1),
                _layer_spec((RROWS, D), layer), _layer_spec((RROWS, RLANES), layer)]
    out_specs = [pl.BlockSpec((tm,) + ROW_SHAPE, lambda b, t: (tile(b, t), 0, 0)),
                 pl.BlockSpec((None, ROW_TILE, tm), lambda b, t: (tile(b, t), 0, 0)),
                 _const_spec((RROWS, RLANES))]
    out_shape = [jax.ShapeDtypeStruct((n,) + ROW_SHAPE, jnp.int32),
                 jax.ShapeDtypeStruct((nb * nt, ROW_TILE, tm), F32),
                 jax.ShapeDtypeStruct((RROWS, RLANES), F32)]
    return in_specs, out_specs, out_shape, [pltpu.VMEM((RROWS, RLANES), F32)]


def _row_copy(src, src_row, dst, dst_row, sem):
    return pltpu.make_async_copy(src.at[pl.ds(src_row, 1)], dst.at[pl.ds(dst_row, 1)], sem)


def _dispatch_body(pos_ref, zflag_ref, urow_ref, xs_hbm, zbuf, sem, zsem):
    base = pl.program_id(0) * DISPATCH_TD

    @pl.when(pl.program_id(0) == 0)
    def _():
        zbuf[...] = jnp.zeros(zbuf.shape, jnp.int32)

        def zero_copy(t):
            return pltpu.make_async_copy(zbuf, xs_hbm.at[pl.ds(t * FFN_TM, FFN_TM)], zsem)

        def start(t, carry):
            @pl.when(zflag_ref[t] != 0)
            def _():
                zero_copy(t).start()
            return carry

        def wait(t, carry):
            @pl.when(zflag_ref[t] != 0)
            def _():
                zero_copy(t).wait()
            return carry

        lax.fori_loop(0, zflag_ref.shape[0], start, 0)
        lax.fori_loop(0, zflag_ref.shape[0], wait, 0)

    def issue(g, carry):
        r0 = pl.multiple_of(g * ROW_TILE, ROW_TILE)
        for k in range(ROW_TILE):
            _row_copy(urow_ref.at[pl.ds(r0, ROW_TILE)], k, xs_hbm, pos_ref[base + r0 + k],
                      sem).start(priority=k % 2)
        return carry

    lax.fori_loop(0, DISPATCH_TD // ROW_TILE, issue, 0)
    pltpu.make_async_copy(urow_ref, xs_hbm.at[pl.ds(0, DISPATCH_TD)], sem).wait()


def _dispatch(pos, zflag, urow):
    n = urow.shape[0]
    nt = zflag.shape[0]
    return pl.pallas_call(
        _dispatch_body,
        grid_spec=pltpu.PrefetchScalarGridSpec(
            num_scalar_prefetch=2, grid=(n // DISPATCH_TD,),
            in_specs=[pl.BlockSpec((DISPATCH_TD,) + ROW_SHAPE, lambda i, pos, zflag: (i, 0, 0))],
            out_specs=pl.BlockSpec(memory_space=pl.ANY),
            scratch_shapes=[pltpu.VMEM((FFN_TM,) + ROW_SHAPE, jnp.int32), pltpu.SemaphoreType.DMA(()),
                            pltpu.SemaphoreType.DMA(())]),
        out_shape=jax.ShapeDtypeStruct((nt * FFN_TM,) + ROW_SHAPE, jnp.int32),
        compiler_params=_cp(("arbitrary",)),
        name="moe_dispatch",
    )(pos, zflag, urow)


def _ffn_sorted_body(ta_ref, tb_ref, na_ref, xs_ref, w1a, w3a, w2a, w1b, w3b, w2b, ys_ref):
    del ta_ref, tb_ref
    active = pl.program_id(0) < na_ref[0]

    @pl.when(active)
    def _():
        words = xs_ref[...].reshape(FFN_TM, D)
        pair = words[:, 0:HALF]
        x = jnp.concatenate([pltpu.bitcast(lax.shift_left(pair, 16), F32).astype(BF16),
                             pltpu.bitcast(pair & HI_MASK, F32).astype(BF16)], axis=1)
        ext = pltpu.bitcast(words[:, HALF:HALF + RLANES], F32)
        ha = jax.nn.silu(_dot(x, w1a[...])) * _dot(x, w3a[...]) * ext[:, 0:1]
        hb = jax.nn.silu(_dot(x, w1b[...])) * _dot(x, w3b[...]) * ext[:, 1:2]
        y = _dot(ha.astype(BF16), w2a[...]) + _dot(hb.astype(BF16), w2b[...])
        ys_ref[...] = y.reshape(FFN_TM, D // LANES, LANES)

    @pl.when(jnp.logical_not(active))
    def _():
        ys_ref[...] = jnp.zeros(ys_ref.shape, F32)


def _ffn_sorted(ta, tb, na, xs, w1, w3, w2):
    nt = ta.shape[0]
    ea = lambda t, ta, tb, na: (ta[t], 0, 0)
    eb = lambda t, ta, tb, na: (tb[t], 0, 0)
    rows = lambda t, ta, tb, na: (t, 0)
    up, down = (None, D, F), (None, F, D)
    return pl.pallas_call(
        _ffn_sorted_body,
        grid_spec=pltpu.PrefetchScalarGridSpec(
            num_scalar_prefetch=3, grid=(nt,),
            in_specs=[
                pl.BlockSpec((FFN_TM,) + ROW_SHAPE, lambda t, ta, tb, na: (t, 0, 0)),
                pl.BlockSpec(up, ea), pl.BlockSpec(up, ea), pl.BlockSpec(down, ea),
                pl.BlockSpec(up, eb), pl.BlockSpec(up, eb), pl.BlockSpec(down, eb),
            ],
            out_specs=pl.BlockSpec((FFN_TM, D // LANES, LANES), lambda t, ta, tb, na: (t, 0, 0))),
        out_shape=jax.ShapeDtypeStruct((nt * FFN_TM, D // LANES, LANES), F32),
        compiler_params=_cp(("arbitrary",)),
        name="moe_ffn",
    )(ta, tb, na, xs, w1, w3, w2, w1, w3, w2)


def _combine_body(pos_ref, x_ref, mod_ref, ys_hbm, g_ref, b_ref, *rest, tiles_per_seq, with_kv):
    if with_kv:
        kvw_ref, kvwt_ref, xn_ref, *kv_out, buf, sem = rest
    else:
        xn_ref, buf, sem = rest
    i = pl.program_id(0)
    nsteps = pl.num_programs(0)
    tc = x_ref.shape[0]

    def issue_group(step, slot, r0):
        group = buf.at[slot, pl.ds(r0, ROW_TILE)]
        for k in range(ROW_TILE):
            _row_copy(ys_hbm, pos_ref[step * tc + r0 + k], group, k, sem.at[slot]).start(priority=k % 2)

    def wait_tile(slot):
        pltpu.make_async_copy(ys_hbm.at[pl.ds(0, tc)], buf.at[slot], sem.at[slot]).wait()

    ahead = COMBINE_SLOTS - 1

    @pl.when(i == 0)
    def _():
        for t in range(ahead):
            def prime(g, carry, t=t):
                issue_group(t, t, pl.multiple_of(g * ROW_TILE, ROW_TILE))
                return carry
            lax.fori_loop(0, tc // ROW_TILE, prime, 0)

    slot = i % COMBINE_SLOTS
    wait_tile(slot)
    _, _, g1p = _split_mod(mod_ref, row=i // tiles_per_seq)
    gain, bias = g_ref[...], b_ref[...]
    nxt = jnp.minimum(i + ahead, nsteps - 1)
    nslot = (i + ahead) % COMBINE_SLOTS

    for c in range(tc // COMBINE_CHUNK):
        for g in range(COMBINE_CHUNK // ROW_TILE):
            issue_group(nxt, nslot, c * COMBINE_CHUNK + g * ROW_TILE)
        rows = pl.ds(c * COMBINE_CHUNK, COMBINE_CHUNK)
        z = ALPHA * x_ref[rows, :] + g1p * buf[slot, rows].reshape(COMBINE_CHUNK, D)
        xn_ref[rows, :] = _layer_norm(z, gain, bias)

    if with_kv:
        _kv_rows(xn_ref[...].astype(BF16), kvw_ref, kvwt_ref, *kv_out)

    @pl.when(i == nsteps - 1)
    def _():
        for k in range(1, COMBINE_SLOTS):
            wait_tile((i + k) % COMBINE_SLOTS)


def _combine(pos, x, mod, ys, ln_g, ln_b, kv_weights=None, *, layer, seq):
    n = x.shape[0]
    tc = COMBINE_TC
    sub = 2 * layer + 1
    row = lambda i, pos: (i, 0)
    in_specs = [
        pl.BlockSpec((tc, D), row),
        _layer_spec((n // seq, 3 * D), sub),
        pl.BlockSpec(memory_space=pl.ANY),
        _layer_spec((1, D), sub), _layer_spec((1, D), sub),
    ]
    out_specs = [pl.BlockSpec((tc, D), row)]
    out_shape = [jax.ShapeDtypeStruct((n, D), F32)]
    args = [pos, x, mod, ys, ln_g, ln_b]
    if kv_weights is not None:
        in_specs += [pl.BlockSpec((D, 2 * KVW), lambda i, pos: (0, 0)),
                     pl.BlockSpec((KVW, D), lambda i, pos: (0, 0))]
        out_specs += [pl.BlockSpec((tc, KVW), row), pl.BlockSpec((tc, KVW), row),
                      pl.BlockSpec((tc, KVW), row), pl.BlockSpec((KVW, tc), lambda i, pos: (0, i))]
        out_shape += [jax.ShapeDtypeStruct((n, KVW), F32), jax.ShapeDtypeStruct((n, KVW), F32),
                      jax.ShapeDtypeStruct((n, KVW), BF16), jax.ShapeDtypeStruct((KVW, n), BF16)]
        args += list(kv_weights)
    return pl.pallas_call(
        functools.partial(_combine_body, tiles_per_seq=seq // tc, with_kv=kv_weights is not None),
        grid_spec=pltpu.PrefetchScalarGridSpec(
            num_scalar_prefetch=1, grid=(n // tc,),
            in_specs=in_specs, out_specs=out_specs,
            scratch_shapes=[pltpu.VMEM((COMBINE_SLOTS, tc, D // LANES, LANES), F32),
                            pltpu.SemaphoreType.DMA((COMBINE_SLOTS,))]),
        out_shape=out_shape,
        compiler_params=_cp(("arbitrary",)),
        name="moe_combine",
    )(*args)


def _moe_sparse(x, route, mod, w1, w3, w2, ln_g, ln_b, kv_weights=None, *, layer, seq):
    n = x.shape[0]
    urow, meta, cnt = route
    cls = meta[:, 2, :].reshape(n).astype(jnp.int32)
    rank = meta[:, 3, :].reshape(n).astype(jnp.int32)
    count = cnt[:NCLS, 0].astype(jnp.int32)
    ntile = (count + FFN_TM - 1) // FFN_TM
    tile_end = jnp.cumsum(ntile)
    row_start = (tile_end - ntile) * FFN_TM
    pos = rank + jnp.sum(jax.nn.one_hot(cls, NCLS, dtype=jnp.int32) * row_start[None, :], axis=1)
    nt = n // FFN_TM + NCLS
    total = tile_end[NCLS - 1]
    tr = jnp.minimum(jnp.arange(nt, dtype=jnp.int32), total - 1)
    tcls = jnp.sum((tr[:, None] >= tile_end[None, :]).astype(jnp.int32), axis=1)
    grp, pair = tcls // PAIRS, tcls % PAIRS
    lo = (pair >= 3).astype(jnp.int32) + (pair >= 5).astype(jnp.int32)
    hi = pair + 1 - lo * (2 * E_PER_G - 3 - lo) // 2
    ta = grp * E_PER_G + lo
    tb = grp * E_PER_G + hi
    t_all = jnp.arange(nt, dtype=jnp.int32)
    partial_last = ((t_all[:, None] == tile_end[None, :] - 1) & (count % FFN_TM != 0)[None, :]).any(axis=1)
    zflag = ((t_all >= total) | partial_last).astype(jnp.int32)
    xs = _dispatch(pos, zflag, urow)
    ys = _ffn_sorted(ta, tb, total[None], xs, w1, w3, w2)
    return _combine(pos, x, mod, ys, ln_g, ln_b, kv_weights, layer=layer, seq=seq)


def _moe_body(x_ref, mod_ref, wr_ref, br_ref, w1f_ref, w3f_ref, w2f_ref, g_ref, b_ref,
              xn_ref, w1_ref, w3_ref, w2_ref, ub, comb, acc, *, reps):
    e = pl.program_id(1)
    w1_ref[...] = w1f_ref[...].astype(BF16)
    w3_ref[...] = w3f_ref[...].astype(BF16)
    w2_ref[...] = w2f_ref[...].astype(BF16)

    @pl.when(e == 0)
    def _():
        shift, sc1p, _ = _split_mod(mod_ref, reps=reps)
        u = (x_ref[...] * sc1p + shift).astype(BF16)
        ub[...] = u
        eid, _, i1, i2, w1, w2 = _top2(_router_logits(u, wr_ref, br_ref))
        comb[...] = _rows_to_cols(jnp.where(eid == i1, w1, 0.0) + jnp.where(eid == i2, w2, 0.0))
        acc[...] = jnp.zeros(acc.shape, F32)

    u = ub[...]
    col = lax.broadcasted_iota(jnp.int32, comb.shape, 1)
    ce = jnp.sum(jnp.where(col == e, comb[...], 0.0), axis=-1, keepdims=True)
    h = jax.nn.silu(_dot(u, w1_ref[...])) * _dot(u, w3_ref[...]) * ce
    acc[...] += _dot(h.astype(BF16), w2_ref[...])

    @pl.when(e == N_EXP - 1)
    def _():
        _, _, g1p = _split_mod(mod_ref, reps=reps)
        xn_ref[...] = _layer_norm(ALPHA * x_ref[...] + g1p * acc[...], g_ref[...], b_ref[...])


def _moe_dense(x, mod, wr, br, w1, w3, w2, ln_g, ln_b, *, layer, reps):
    n = x.shape[0]
    sub = 2 * layer + 1
    expert = lambda i, e: (layer, e, 0, 0)
    return pl.pallas_call(
        functools.partial(_moe_body, reps=reps),
        grid=(1, N_EXP),
        in_specs=[
            pl.BlockSpec((n, D), lambda i, e: (0, 0)),
            _layer_spec((n // reps, 3 * D), sub),
            _layer_spec((RROWS, D), layer), _layer_spec((RROWS, RLANES), layer),
            pl.BlockSpec((None, None, D, F), expert),
            pl.BlockSpec((None, None, D, F), expert),
            pl.BlockSpec((None, None, F, D), expert),
            _layer_spec((1, D), sub), _layer_spec((1, D), sub),
        ],
        out_specs=[pl.BlockSpec((n, D), lambda i, e: (0, 0)),
                   pl.BlockSpec((None, D, F), lambda i, e: (e, 0, 0)),
                   pl.BlockSpec((None, D, F), lambda i, e: (e, 0, 0)),
                   pl.BlockSpec((None, F, D), lambda i, e: (e, 0, 0))],
        out_shape=[jax.ShapeDtypeStruct((n, D), F32),
                   jax.ShapeDtypeStruct((N_EXP, D, F), BF16),
                   jax.ShapeDtypeStruct((N_EXP, D, F), BF16),
                   jax.ShapeDtypeStruct((N_EXP, F, D), BF16)],
        scratch_shapes=[
            pltpu.VMEM((n, D), BF16),
            pltpu.VMEM((n, RLANES), F32),
            pltpu.VMEM((n, D), F32),
        ],
        compiler_params=_cp(("arbitrary", "arbitrary")),
        name="moe",
    )(x, mod, wr, br, w1, w3, w2, ln_g, ln_b)


def _kv_body(x_ref, w_ref, wvt_ref, k_ref, v_ref, k16_ref, vt16_ref):
    _kv_rows(x_ref[...].astype(BF16), w_ref, wvt_ref, k_ref, v_ref, k16_ref, vt16_ref)


def _kv_rows(xb, w_ref, wvt_ref, k_ref, v_ref, k16_ref, vt16_ref):
    kv = _dot(xb, w_ref[...])
    k_ref[...] = kv[:, :KVW]
    v_ref[...] = kv[:, KVW:]
    k16_ref[...] = kv[:, :KVW].astype(BF16)
    vt16_ref[...] = lax.dot_general(wvt_ref[...], xb, _NT, preferred_element_type=F32).astype(BF16)


def _kv(x, w, wvt, *, tm):
    n = x.shape[0]
    row = lambda i: (i, 0)
    return pl.pallas_call(
        _kv_body,
        grid=(n // tm,),
        in_specs=[pl.BlockSpec((tm, D), row), _const_spec((D, 2 * KVW)), _const_spec((KVW, D))],
        out_specs=[pl.BlockSpec((tm, KVW), row), pl.BlockSpec((tm, KVW), row),
                   pl.BlockSpec((tm, KVW), row), pl.BlockSpec((KVW, tm), lambda i: (0, i))],
        out_shape=[jax.ShapeDtypeStruct((n, KVW), F32), jax.ShapeDtypeStruct((n, KVW), F32),
                   jax.ShapeDtypeStruct((n, KVW), BF16), jax.ShapeDtypeStruct((KVW, n), BF16)],
        compiler_params=_cp(("arbitrary",)),
        name="kv_proj",
    )(x, w, wvt)


def _sink_attend(qh, kh, vh, bias, sink):
    s = lax.dot_general(qh, kh, (((1,), (1,)), ((), ())), preferred_element_type=F32) + bias
    m = jnp.maximum(jnp.max(s, axis=-1, keepdims=True), sink)
    p = jnp.exp(s - m)
    den = jnp.sum(p, axis=-1, keepdims=True) + jnp.exp(sink - m)
    return _dot(p.astype(BF16), vh) / den


def _attn_prompt_body(x_ref, mod_ref, wqt_ref, wo_ref, kc_ref, kp_ref, vtc_ref, vtp_ref,
                      bias_ref, sink_ref, g_ref, b_ref, mod1_ref, wr_ref, br_ref,
                      xn_ref, urow_ref, meta_ref, cnt_ref, carry, *, tq):
    t = pl.program_id(1)
    x = x_ref[...]
    shift, sc1p, g1p = _split_mod(mod_ref, row=pl.program_id(0))
    u = (x * sc1p + shift).astype(BF16)
    qt = lax.dot_general(wqt_ref[...], u, _NT, preferred_element_type=F32)
    qt = (qt * (HD ** -0.5)).astype(BF16)
    kc = kc_ref[...]
    vtc = vtc_ref[...]
    krow = lax.broadcasted_iota(jnp.int32, (2 * WIN, GRP * WIN), 0)
    first = jnp.where((krow < WIN) & (t == 0), NEG, 0.0).astype(F32)
    nwin = tq // WIN
    kks, vvts = [], []
    for j in range(nwin):
        if j == 0:
            kks.append(jnp.concatenate([kp_ref[...], kc[0:WIN]], axis=0))
            vvts.append(jnp.concatenate([vtp_ref[...], vtc[:, 0:WIN]], axis=1))
        else:
            kks.append(kc[(j - 1) * WIN:(j + 1) * WIN])
            vvts.append(vtc[:, (j - 1) * WIN:(j + 1) * WIN])

    def scores(j, hk):
        qht = jnp.concatenate(
            [qt[(hk * GRP + g) * HD:(hk * GRP + g + 1) * HD, j * WIN:(j + 1) * WIN] for g in range(GRP)],
            axis=1)
        st = _dot(kks[j][:, hk * HD:(hk + 1) * HD], qht) + bias_ref[hk]
        return st + first if j == 0 else st

    items = [(j, hk) for j in range(nwin) for hk in range(HKV)]
    heads = [[] for _ in range(nwin)]
    st_next = scores(*items[0])
    for n, (j, hk) in enumerate(items):
        st = st_next
        if n + 1 < len(items):
            st_next = scores(*items[n + 1])
        sink = sink_ref[hk]
        m = jnp.maximum(jnp.max(st, axis=0, keepdims=True), sink)
        pt = jnp.exp(st - m)
        den = jnp.sum(pt, axis=0, keepdims=True) + jnp.exp(sink - m)
        ot = _dot(vvts[j][hk * HD:(hk + 1) * HD, :], pt.astype(BF16)) * (1.0 / den)
        for g in range(GRP):
            heads[j].append(ot[:, g * WIN:(g + 1) * WIN])
    blocks = [jnp.concatenate(h, axis=0) for h in heads]
    oallt = jnp.concatenate(blocks, axis=1) if len(blocks) > 1 else blocks[0]
    mix = lax.dot_general(oallt.astype(BF16), wo_ref[...], _TN, preferred_element_type=F32)
    xn = _layer_norm(ALPHA * x + g1p * mix, g_ref[...], b_ref[...])
    xn_ref[...] = xn
    b = pl.program_id(0)
    _route_rows(xn, b, (b == 0) & (t == 0), mod1_ref, wr_ref, br_ref, urow_ref, meta_ref, cnt_ref, carry)


def _attn_prompt(x, mod, wqt, wo, k16, vt16, bias, sink, ln_g, ln_b, wr, br, *, layer, nb, seq, tq):
    nt = seq // tq
    r_in, r_out, r_shape, r_scratch = _route_specs(layer, nb, nt, tq)
    wpt = tq // WIN
    n = nb * seq
    j = layer - N_A
    sub = 2 * layer

    def prev(b, t):
        return jnp.maximum((b * nt + t) * wpt - 1, 0)

    cur = lambda b, t: (b * nt + t, 0)
    return pl.pallas_call(
        functools.partial(_attn_prompt_body, tq=tq),
        grid=(nb, nt),
        in_specs=[
            pl.BlockSpec((tq, D), cur),
            _layer_spec((nb, 3 * D), sub),
            _layer_spec((HQ * HD, D), j), _layer_spec((HQ * HD, D), j),
            pl.BlockSpec((tq, KVW), cur), pl.BlockSpec((WIN, KVW), lambda b, t: (prev(b, t), 0)),
            pl.BlockSpec((KVW, tq), lambda b, t: (0, b * nt + t)),
            pl.BlockSpec((KVW, WIN), lambda b, t: (0, prev(b, t))),
            _const_spec((HKV, 2 * WIN, GRP * WIN)), _layer_spec((HKV, 1, GRP * WIN), j),
            _layer_spec((1, D), sub), _layer_spec((1, D), sub),
        ] + r_in,
        out_specs=[pl.BlockSpec((tq, D), cur)] + r_out,
        out_shape=[jax.ShapeDtypeStruct((n, D), F32)] + r_shape,
        scratch_shapes=r_scratch,
        compiler_params=_cp(("arbitrary", "arbitrary")),
        name="attn_prompt",
    )(x, mod, wqt, wo, k16, k16, vt16, vt16, bias, sink, ln_g, ln_b, mod, wr, br)


def _attn_sample_body(x_ref, mod_ref, wq_ref, wo_ref, kb_ref, vb_ref, kn_ref, vn_ref,
                      bias_ref, sink_ref, g_ref, b_ref, xn_ref, *, ns, steps):
    nr = ns * steps
    x = x_ref[...].reshape(nr, D)
    shift, sc1p, g1p = _split_mod(mod_ref, reps=steps)
    u = (x * sc1p + shift).astype(BF16)
    q = (_dot(u, wq_ref[...]) * (HD ** -0.5)).astype(BF16)
    nbuf = kb_ref.shape[1]
    k_all = jnp.concatenate([kb_ref[...].reshape(ns * nbuf, KVW), kn_ref[...].reshape(nr, KVW)],
                            axis=0).astype(BF16)
    v_all = jnp.concatenate([vb_ref[...].reshape(ns * nbuf, KVW), vn_ref[...].reshape(nr, KVW)],
                            axis=0).astype(BF16)
    heads = [None] * HQ
    for hk in range(HKV):
        kcols = slice(hk * HD, (hk + 1) * HD)
        qh = jnp.concatenate(
            [q[:, (hk * GRP + g) * HD:(hk * GRP + g + 1) * HD] for g in range(GRP)], axis=0)
        o = _sink_attend(qh, k_all[:, kcols], v_all[:, kcols], bias_ref[hk], sink_ref[hk])
        for g in range(GRP):
            heads[hk * GRP + g] = o[g * nr:(g + 1) * nr]
    o_all = jnp.concatenate(heads, axis=1)
    mix = _dot(o_all.astype(BF16), wo_ref[...])
    xn_ref[...] = _layer_norm(ALPHA * x + g1p * mix, g_ref[...], b_ref[...]).reshape(steps, ns, D)


def _attn_sample(x, mod, wq, wo, kbuf, vbuf, kn, vn, bias, sink, ln_g, ln_b, *, layer, ns):
    steps, nb, _ = x.shape
    nbuf = kbuf.shape[1]
    j = layer - N_A
    sub = 2 * layer
    seqs = lambda i: (0, i, 0)
    return pl.pallas_call(
        functools.partial(_attn_sample_body, ns=ns, steps=steps),
        grid=(nb // ns,),
        in_specs=[
            pl.BlockSpec((steps, ns, D), seqs),
            pl.BlockSpec((None, ns, 3 * D), lambda i: (sub, i, 0)),
            _layer_spec((D, HQ * HD), j), _layer_spec((HQ * HD, D), j),
            pl.BlockSpec((ns, nbuf, KVW), lambda i: (i, 0, 0)),
            pl.BlockSpec((ns, nbuf, KVW), lambda i: (i, 0, 0)),
            pl.BlockSpec((steps, ns, KVW), seqs), pl.BlockSpec((steps, ns, KVW), seqs),
            _const_spec(bias.shape), _layer_spec(sink.shape[1:], j),
            _layer_spec((1, D), sub), _layer_spec((1, D), sub),
        ],
        out_specs=pl.BlockSpec((steps, ns, D), seqs),
        out_shape=jax.ShapeDtypeStruct((steps, nb, D), F32),
        compiler_params=_cp(("arbitrary",)),
        name="attn_sample",
    )(x, mod, wq, wo, kbuf, vbuf, kn, vn, bias, sink, ln_g, ln_b)


def _t5_bucket(d):
    max_exact = N_BUCKETS // 2
    d = np.maximum(d, 0)
    log_ratio = (np.log(np.maximum(d, 1).astype(np.float32) / np.float32(max_exact))
                 / np.float32(math.log(WIN / max_exact)))
    large = max_exact + (log_ratio * (N_BUCKETS - max_exact)).astype(np.int32)
    large = np.minimum(large, N_BUCKETS - 1)
    return np.where(d < max_exact, d, large)


def _bias_table(d, ok, rel_bias):
    sel = np.where(ok, _t5_bucket(d), N_BUCKETS)
    onehot = (sel[..., None] == np.arange(N_BUCKETS + 1)).astype(np.float32)
    tab = jnp.concatenate([rel_bias.astype(F32), jnp.full((1, HQ), NEG, F32)], axis=0)
    return jnp.einsum('abd,dh->hab', jnp.asarray(onehot), tab, precision=lax.Precision.HIGHEST)


def _prompt_bias(rel_bias):
    d = np.arange(WIN)[None, :] + WIN - np.arange(2 * WIN)[:, None]
    b = _bias_table(d, (d >= 0) & (d < WIN), rel_bias).reshape(HKV, GRP, 2 * WIN, WIN)
    return jnp.transpose(b, (0, 2, 1, 3)).reshape(HKV, 2 * WIN, GRP * WIN)


def _sample_bias(rel_bias, ns, steps, nbuf):
    qs = np.tile(np.arange(ns), steps)
    qt = np.repeat(np.arange(steps), ns)
    ks = np.concatenate([np.repeat(np.arange(ns), nbuf), qs])
    kpos = np.concatenate([np.tile(np.arange(nbuf), ns), nbuf + qt])
    d = qt[:, None] + nbuf - kpos[None, :]
    ok = (d >= 0) & (d < WIN) & (qs[:, None] == ks[None, :])
    return _bias_table(d, ok, rel_bias).reshape(HKV, GRP * ns * steps, kpos.shape[0])


LRU_TM = 512
ATTN_TQ = 1024
SAMPLE_NS = 8


def kernel(x_prompt, x_sample, state_rnn_h, state_rnn_conv, cache_win_k, cache_win_v, c_prompt, c_sample, ada_w, ada_b, ln_g, ln_b, lru_w_in, lru_conv_w, lru_conv_b, lru_w_a, lru_b_a, lru_w_x, lru_b_x, lru_lambda, lru_w_out, kv_w, attn_w_q, attn_sinks, attn_w_o, rel_bias, moe_w_group, moe_b_group, moe_w_router, moe_b_router, moe_w1, moe_w3, moe_w2):
    nbp, seq, _ = x_prompt.shape
    nbs, steps, _ = x_sample.shape
    nbuf = cache_win_k.shape[1]
    npt = nbp * seq
    nst = nbs * steps

    mod_p, mod_s = _adaln(c_prompt, c_sample, ada_w, ada_b)

    ln_g3 = ln_g.reshape(2 * DEPTH, 1, D)
    ln_b3 = ln_b.reshape(2 * DEPTH, 1, D)
    lw = (lru_w_in.astype(BF16), lru_conv_w, lru_conv_b[:, None], lru_w_a.astype(BF16),
          lru_w_x.astype(BF16), lru_b_a[:, None], lru_b_x[:, None], lru_lambda[:, None],
          lru_w_out.astype(BF16))
    zrow = lambda k: jnp.zeros((DEPTH, k, D), F32)
    wr = jnp.concatenate([jnp.swapaxes(moe_w_group, 1, 2), zrow(EROW - N_GROUPS),
                          jnp.swapaxes(moe_w_router.reshape(DEPTH, D, N_EXP), 1, 2),
                          zrow(RROWS - EROW - N_EXP)], axis=1).astype(BF16)
    zb = lambda k: jnp.zeros((DEPTH, k), F32)
    br = jnp.concatenate([moe_b_group, zb(EROW - N_GROUPS), moe_b_router.reshape(DEPTH, N_EXP),
                          zb(RROWS - EROW - N_EXP)], axis=1)
    br = jnp.broadcast_to(br[:, :, None], (DEPTH, RROWS, RLANES))
    wq = attn_w_q.astype(BF16)
    wqt = jnp.swapaxes(wq, 1, 2)
    wo = attn_w_o.astype(BF16)
    kvw = kv_w.astype(BF16)
    kvwt = kvw[:, KVW:].T
    bias_p = _prompt_bias(rel_bias)
    bias_s = _sample_bias(rel_bias, SAMPLE_NS, steps, nbuf)
    sink_p = jnp.repeat(attn_sinks.reshape(N_B, HKV, GRP), WIN, axis=2)[:, :, None, :]
    sink_s = jnp.repeat(attn_sinks.reshape(N_B, HKV, GRP), SAMPLE_NS * steps, axis=2)[..., None]
    kbuf = cache_win_k.reshape(nbs, nbuf, KVW)
    vbuf = cache_win_v.reshape(nbs, nbuf, KVW)

    xp = x_prompt.reshape(npt, D)
    xs = jnp.transpose(x_sample, (1, 0, 2)).reshape(nst, D)
    c0 = jnp.transpose(state_rnn_conv, (0, 2, 1, 3)).reshape(N_A, (CONV_W - 1) * nbs, R)
    hp, cp_, hs, cs = [], [], [], []
    kp = vp = kn = vn = kp16 = vpt16 = None
    for l in range(DEPTH):
        if l < N_A:
            xp, h_l, c_l, *route = _lru_prompt(xp, mod_p, lw, ln_g3, ln_b3, wr, br,
                                               layer=l, nb=nbp, seq=seq, tm=LRU_TM)
            hp.append(h_l[:, 0])
            cp_.append(c_l)
            xs, h_l, c_l = _lru_sample(xs, mod_s, state_rnn_h, c0[l], lw, ln_g3, ln_b3,
                                       layer=l, nb=nbs, steps=steps)
            hs.append(h_l)
            cs.append(jnp.transpose(c_l.reshape(CONV_W - 1, nbs, R), (1, 0, 2)))
        else:
            xp, *route = _attn_prompt(xp, mod_p, wqt, wo, kp16, vpt16, bias_p, sink_p, ln_g3, ln_b3, wr, br,
                                      layer=l, nb=nbp, seq=seq, tq=ATTN_TQ)
            xs = _attn_sample(xs.reshape(steps, nbs, D), mod_s, wq, wo, kbuf, vbuf,
                              kn.reshape(steps, nbs, KVW), vn.reshape(steps, nbs, KVW),
                              bias_s, sink_s, ln_g3, ln_b3, layer=l, ns=SAMPLE_NS).reshape(nst, D)
        xs, w1, w3, w2 = _moe_dense(xs, mod_s, wr, br, moe_w1, moe_w3, moe_w2, ln_g3, ln_b3,
                                    layer=l, reps=steps)
        if l == N_A - 1:
            xp, kp, vp, kp16, vpt16 = _moe_sparse(xp, route, mod_p, w1, w3, w2, ln_g3, ln_b3, (kvw, kvwt),
                                                  layer=l, seq=seq)
            kn, vn, _, _ = _kv(xs, kvw, kvwt, tm=nst)
        else:
            xp, = _moe_sparse(xp, route, mod_p, w1, w3, w2, ln_g3, ln_b3, layer=l, seq=seq)

    lp = min(WIN, seq)
    k_p = kp.reshape(nbp, seq, HKV, HD)[:, seq - lp:]
    v_p = vp.reshape(nbp, seq, HKV, HD)[:, seq - lp:]
    kn3 = jnp.transpose(kn.reshape(steps, nbs, KVW), (1, 0, 2))
    vn3 = jnp.transpose(vn.reshape(steps, nbs, KVW), (1, 0, 2))
    k_s = jnp.concatenate([kbuf, kn3], axis=1)[:, -nbuf:].reshape(nbs, nbuf, HKV, HD)
    v_s = jnp.concatenate([vbuf, vn3], axis=1)[:, -nbuf:].reshape(nbs, nbuf, HKV, HD)
    y_s = jnp.transpose(xs.reshape(steps, nbs, D), (1, 0, 2))
    return (xp.reshape(nbp, seq, D), y_s,
            jnp.stack(hp), jnp.stack(cp_), k_p, v_p,
            jnp.stack(hs), jnp.stack(cs), k_s, v_s)
```

```python
import functools
import math

import jax
import jax.numpy as jnp
import numpy as np
from jax import lax
from jax.experimental import pallas as pl
from jax.experimental.pallas import tpu as pltpu

D = 1024
R = 1024
DEPTH = 4
N_A = 2
N_B = DEPTH - N_A
N_BLK = 4
BLK = R // N_BLK
CONV_W = 4
RG_C = 8.0
HQ = 16
HKV = 4
HD = 64
GRP = HQ // HKV
KVW = HKV * HD
WIN = 128
N_BUCKETS = 32
N_GROUPS = 4
E_PER_G = 4
N_EXP = 16
F = 512
ALPHA = (2.0 * DEPTH) ** 0.25
LN_EPS = 1e-5
NEG = -1e30
LANES = 128
RLANES = LANES
ROW_TILE = 8

F32 = jnp.float32
BF16 = jnp.bfloat16

VMEM_LIMIT = 56 * 1024 * 1024


def _cp(sem):
    return pltpu.CompilerParams(dimension_semantics=sem, vmem_limit_bytes=VMEM_LIMIT)


def _dot(a, b):
    return jnp.dot(a, b, preferred_element_type=F32)


def _layer_norm(z, g, b):
    mu = jnp.mean(z, axis=-1, keepdims=True)
    zc = z - mu
    var = jnp.mean(zc * zc, axis=-1, keepdims=True)
    return zc * lax.rsqrt(var + LN_EPS) * g + b


def _split_mod(mod_ref, row=None, reps=1):
    m = mod_ref[...] if row is None else mod_ref[pl.ds(row, 1), :]
    if reps > 1:
        m = jnp.concatenate([m] * reps, axis=0)
    return m[:, 0:D], m[:, D:2 * D], m[:, 2 * D:3 * D]


def _layer_spec(shape, layer):
    nd = len(shape)
    return pl.BlockSpec((None,) + tuple(shape), lambda *_: (layer,) + (0,) * nd)


def _adaln_body(cp_ref, cs_ref, w_ref, b_ref, op_ref, os_ref):
    one = jnp.where(pl.program_id(1) > 0, 1.0, 0.0).astype(F32)
    w = w_ref[...].astype(BF16)
    op_ref[...] = _dot(cp_ref[...].astype(BF16), w) + b_ref[...] + one
    os_ref[...] = _dot(cs_ref[...].astype(BF16), w) + b_ref[...] + one


def _adaln(c_p, c_s, ada_w, ada_b):
    np_, ns_ = c_p.shape[0], c_s.shape[0]
    w = ada_w.reshape(2 * DEPTH, D, 3 * D)
    b = ada_b.reshape(2 * DEPTH, 1, 3 * D)
    return pl.pallas_call(
        _adaln_body,
        grid=(2 * DEPTH, 3),
        in_specs=[
            pl.BlockSpec((np_, D), lambda l, j: (0, 0)),
            pl.BlockSpec((ns_, D), lambda l, j: (0, 0)),
            pl.BlockSpec((None, D, D), lambda l, j: (l, 0, j)),
            pl.BlockSpec((None, 1, D), lambda l, j: (l, 0, j)),
        ],
        out_specs=[pl.BlockSpec((None, np_, D), lambda l, j: (l, 0, j)),
                   pl.BlockSpec((None, ns_, D), lambda l, j: (l, 0, j))],
        out_shape=[jax.ShapeDtypeStruct((2 * DEPTH, np_, 3 * D), F32),
                   jax.ShapeDtypeStruct((2 * DEPTH, ns_, 3 * D), F32)],
        compiler_params=_cp(("arbitrary", "arbitrary")),
        name="adaln",
    )(c_p, c_s, w, b)


def _lru_gates(xc, n, wa_ref, wx_ref, ba_ref, bx_ref, lam_ref):
    cols = slice(n * BLK, (n + 1) * BLK)
    xb = xc.astype(BF16)
    r = jax.nn.sigmoid(_dot(xb, wa_ref[n]) + ba_ref[:, cols])
    i = jax.nn.sigmoid(_dot(xb, wx_ref[n]) + bx_ref[:, cols])
    z = -lam_ref[:, cols]
    softplus = jnp.maximum(z, 0.0) + jnp.log1p(jnp.exp(-jnp.abs(z)))
    log_a = (-RG_C) * r * softplus
    a = jnp.exp(log_a)
    v = 1.0 - a * a
    b = jnp.where(v > 0.0, v * lax.rsqrt(v), 0.0) * i * xc
    return a, b


def _lru_prompt_body(x_ref, mod_ref, win_ref, cw_ref, cb_ref, wa_ref, wx_ref, ba_ref, bx_ref,
                     lam_ref, wout_ref, g_ref, b_ref, mod1_ref, wr_ref, br_ref,
                     xn_ref, hl_ref, cl_ref, urow_ref, meta_ref, cnt_ref,
                     xbuf, abuf, hbuf, hc, carry, *, tm):
    t = pl.program_id(1)

    @pl.when(t == 0)
    def _():
        xbuf[0:ROW_TILE, :] = jnp.zeros((ROW_TILE, R), F32)
        hc[...] = jnp.zeros((1, R), F32)

    x = x_ref[...]
    shift, sc1p, g1p = _split_mod(mod_ref, row=pl.program_id(0))
    u = x * sc1p + shift
    yx = _dot(u.astype(BF16), win_ref[...])
    y = jax.nn.gelu(yx[:, :R])
    xbuf[ROW_TILE:ROW_TILE + tm, :] = yx[:, R:]

    ngrp = tm // ROW_TILE
    rowmod = lax.broadcasted_iota(jnp.int32, (ngrp, ROW_TILE, BLK), 1)
    for n in range(N_BLK):
        cols = slice(n * BLK, (n + 1) * BLK)
        xc = cb_ref[:, cols] + cw_ref[3:4, cols] * xbuf[ROW_TILE:ROW_TILE + tm, cols]
        for k in range(1, CONV_W):
            xc = xc + cw_ref[3 - k:4 - k, cols] * xbuf[ROW_TILE - k:ROW_TILE - k + tm, cols]
        a, b = _lru_gates(xc, n, wa_ref, wx_ref, ba_ref, bx_ref, lam_ref)
        a = a.reshape(ngrp, ROW_TILE, BLK)
        b = b.reshape(ngrp, ROW_TILE, BLK)
        s = 1
        while s < ROW_TILE:
            a_sh = pltpu.roll(a, s, 1)
            b_sh = pltpu.roll(b, s, 1)
            m = rowmod >= s
            b = jnp.where(m, a * b_sh, 0.0) + b
            a = jnp.where(m, a * a_sh, a)
            s *= 2
        abuf[:, cols] = a.reshape(tm, BLK)
        hbuf[:, cols] = b.reshape(tm, BLK)

    def group(g, h):
        r0 = pl.multiple_of(g * ROW_TILE, ROW_TILE)
        hg = abuf[pl.ds(r0, ROW_TILE), :] * h + hbuf[pl.ds(r0, ROW_TILE), :]
        hbuf[pl.ds(r0, ROW_TILE), :] = hg
        return hg[ROW_TILE - 1:ROW_TILE, :]

    h_end = lax.fori_loop(0, tm // ROW_TILE, group, hc[...])
    hc[...] = h_end
    hl_ref[...] = h_end
    tail = xbuf[tm + ROW_TILE - (CONV_W - 1):tm + ROW_TILE, :]
    cl_ref[...] = tail
    xbuf[ROW_TILE - (CONV_W - 1):ROW_TILE, :] = tail

    mix = _dot((y * hbuf[...]).astype(BF16), wout_ref[...])
    xn = _layer_norm(ALPHA * x + g1p * mix, g_ref[...], b_ref[...])
    xn_ref[...] = xn
    b = pl.program_id(0)
    _route_rows(xn, b, (b == 0) & (t == 0), mod1_ref, wr_ref, br_ref, urow_ref, meta_ref, cnt_ref, carry)


def _const_spec(shape):
    nd = len(shape)
    return pl.BlockSpec(shape, lambda *_: (0,) * nd)


def _lru_weight_specs(layer):
    sub = 2 * layer
    return [
        _layer_spec((D, 2 * R), layer), _layer_spec((CONV_W, R), layer), _layer_spec((1, R), layer),
        _layer_spec((N_BLK, BLK, BLK), layer), _layer_spec((N_BLK, BLK, BLK), layer),
        _layer_spec((1, R), layer), _layer_spec((1, R), layer), _layer_spec((1, R), layer),
        _layer_spec((R, D), layer), _layer_spec((1, D), sub), _layer_spec((1, D), sub),
    ]


def _lru_prompt(x, mod, lw, ln_g, ln_b, wr, br, *, layer, nb, seq, tm):
    nt = seq // tm
    n = nb * seq
    r_in, r_out, r_shape, r_scratch = _route_specs(layer, nb, nt, tm)
    return pl.pallas_call(
        functools.partial(_lru_prompt_body, tm=tm),
        grid=(nb, nt),
        in_specs=[
            pl.BlockSpec((tm, D), lambda b, t: (b * nt + t, 0)),
            _layer_spec((nb, 3 * D), 2 * layer),
        ] + _lru_weight_specs(layer) + r_in,
        out_specs=[
            pl.BlockSpec((tm, D), lambda b, t: (b * nt + t, 0)),
            pl.BlockSpec((None, 1, R), lambda b, t: (b, 0, 0)),
            pl.BlockSpec((None, CONV_W - 1, R), lambda b, t: (b, 0, 0)),
        ] + r_out,
        out_shape=[
            jax.ShapeDtypeStruct((n, D), F32),
            jax.ShapeDtypeStruct((nb, 1, R), F32),
            jax.ShapeDtypeStruct((nb, CONV_W - 1, R), F32),
        ] + r_shape,
        scratch_shapes=[
            pltpu.VMEM((tm + ROW_TILE, R), F32),
            pltpu.VMEM((tm, R), F32),
            pltpu.VMEM((tm, R), F32),
            pltpu.VMEM((1, R), F32),
        ] + r_scratch,
        compiler_params=_cp(("arbitrary", "arbitrary")),
        name="lru_prompt",
    )(x, mod, *lw, ln_g, ln_b, mod, wr, br)


def _lru_sample_body(x_ref, mod_ref, h0_ref, c0_ref, win_ref, cw_ref, cb_ref, wa_ref, wx_ref,
                     ba_ref, bx_ref, lam_ref, wout_ref, g_ref, b_ref,
                     xn_ref, hl_ref, cl_ref, hbuf, *, nb, steps):
    x = x_ref[...]
    shift, sc1p, g1p = _split_mod(mod_ref, reps=steps)
    u = x * sc1p + shift
    yx = _dot(u.astype(BF16), win_ref[...])
    y = jax.nn.gelu(yx[:, :R])
    xext = jnp.concatenate([c0_ref[...], yx[:, R:]], axis=0)
    cl_ref[...] = xext[steps * nb:, :]
    for n in range(N_BLK):
        cols = slice(n * BLK, (n + 1) * BLK)
        xc = cb_ref[:, cols] + cw_ref[0:1, cols] * xext[0:steps * nb, cols]
        for k in range(1, CONV_W):
            xc = xc + cw_ref[k:k + 1, cols] * xext[k * nb:(k + steps) * nb, cols]
        a, b = _lru_gates(xc, n, wa_ref, wx_ref, ba_ref, bx_ref, lam_ref)
        h = h0_ref[:, cols]
        for t in range(steps):
            h = a[t * nb:(t + 1) * nb, :] * h + b[t * nb:(t + 1) * nb, :]
            hbuf[t * nb:(t + 1) * nb, cols] = h
        hl_ref[:, cols] = h
    mix = _dot((y * hbuf[...]).astype(BF16), wout_ref[...])
    xn_ref[...] = _layer_norm(ALPHA * x + g1p * mix, g_ref[...], b_ref[...])


def _lru_sample(x, mod, h0, c0, lw, ln_g, ln_b, *, layer, nb, steps):
    n = nb * steps
    return pl.pallas_call(
        functools.partial(_lru_sample_body, nb=nb, steps=steps),
        grid=(1,),
        in_specs=[
            _const_spec((n, D)), _layer_spec((nb, 3 * D), 2 * layer), _layer_spec((nb, R), layer),
            _const_spec(((CONV_W - 1) * nb, R)),
        ] + _lru_weight_specs(layer),
        out_specs=[_const_spec((n, D)), _const_spec((nb, R)), _const_spec(((CONV_W - 1) * nb, R))],
        out_shape=[
            jax.ShapeDtypeStruct((n, D), F32),
            jax.ShapeDtypeStruct((nb, R), F32),
            jax.ShapeDtypeStruct(((CONV_W - 1) * nb, R), F32),
        ],
        scratch_shapes=[pltpu.VMEM((n, R), F32)],
        compiler_params=_cp(("arbitrary",)),
        name="lru_sample",
    )(x, mod, h0, c0, *lw, ln_g, ln_b)


RROWS = 32
EROW = 8
_NT = (((1,), (1,)), ((), ()))
_TN = (((0,), (0,)), ((), ()))


def _top2(lt):
    nt = lt.shape[1]
    gl = lt[0:EROW, :]
    gid = lax.broadcasted_iota(jnp.int32, (EROW, nt), 0).astype(F32)
    is_g = gid < N_GROUPS
    gl = jnp.where(is_g, gl, NEG)
    gmax = jnp.max(gl, axis=0, keepdims=True)
    gidx = jnp.min(jnp.where(gl == gmax, gid, float(EROW)), axis=0, keepdims=True)
    gsum = jnp.sum(jnp.where(is_g, jnp.exp(gl - gmax), 0.0), axis=0, keepdims=True)
    g_p = 1.0 / gsum
    el = lt[EROW:EROW + N_EXP, :]
    eid = lax.broadcasted_iota(jnp.int32, (N_EXP, nt), 0).astype(F32)
    first = gidx * E_PER_G
    in_grp = (eid >= first) & (eid < first + E_PER_G)
    el = jnp.where(in_grp, el, NEG)
    big = float(N_EXP)
    t1 = jnp.max(el, axis=0, keepdims=True)
    i1 = jnp.min(jnp.where(el == t1, eid, big), axis=0, keepdims=True)
    el2 = jnp.where(eid == i1, NEG, el)
    t2 = jnp.max(el2, axis=0, keepdims=True)
    i2 = jnp.min(jnp.where((el2 == t2) & in_grp & (eid != i1), eid, big), axis=0, keepdims=True)
    e21 = jnp.exp(t2 - t1)
    w1 = g_p / (1.0 + e21)
    w2 = w1 * e21
    return eid, gidx, i1, i2, w1, w2


def _router_logits(u16, wrt_ref, brt_ref):
    return lax.dot_general(wrt_ref[...], u16, _NT, preferred_element_type=F32) + brt_ref[:, 0:1]


def _rows_to_cols(rows):
    r, nt = rows.shape
    padded = jnp.concatenate([rows, jnp.zeros((RLANES - r, nt), F32)], axis=0)
    return jnp.transpose(padded)


PAIRS = E_PER_G * (E_PER_G - 1) // 2
NCLS = N_GROUPS * PAIRS
HALF = D // 2
ROW_SHAPE = (D // LANES, LANES)
HI_MASK = -65536
FFN_TM = 256
DISPATCH_TD = 2048
COMBINE_TC = 512
COMBINE_CHUNK = 64
COMBINE_SLOTS = 3


def _route_rows(x, row, first, mod_ref, wr_ref, br_ref, urow_ref, meta_ref, cnt_ref, carry):
    @pl.when(first)
    def _():
        carry[...] = jnp.zeros(carry.shape, F32)

    tm = x.shape[0]
    shift, sc1p, _ = _split_mod(mod_ref, row=row)
    u = x * sc1p + shift
    _, gidx, i1, i2, w1, w2 = _top2(_router_logits(u.astype(BF16), wr_ref, br_ref))
    first = gidx * E_PER_G
    lo = jnp.minimum(i1, i2) - first
    hi = jnp.maximum(i1, i2) - first
    w_lo = jnp.where(i1 < i2, w1, w2)
    w_hi = jnp.where(i1 < i2, w2, w1)
    cls = gidx * PAIRS + lo * (2 * E_PER_G - 1 - lo) * 0.5 + (hi - lo - 1.0)
    cid = lax.broadcasted_iota(jnp.int32, (RROWS, tm), 0).astype(F32)
    onehot = (cid == cls).astype(F32)
    r = lax.broadcasted_iota(jnp.int32, (tm, tm), 0)
    c = lax.broadcasted_iota(jnp.int32, (tm, tm), 1)
    earlier = _dot(onehot.astype(BF16), (r < c).astype(BF16)) + carry[:, 0:1]
    rank = jnp.sum(onehot * earlier, axis=0, keepdims=True)
    carry[...] += jnp.sum(onehot, axis=1, keepdims=True)
    cnt_ref[...] = carry[...]
    meta = jnp.concatenate([w_lo, w_hi, cls, rank, jnp.zeros((ROW_TILE - 4, tm), F32)], axis=0)
    meta_ref[...] = meta
    ub = u.astype(BF16)
    lo = lax.shift_right_logical(pltpu.bitcast(ub[:, 0:HALF].astype(F32), jnp.int32), 16)
    hi = pltpu.bitcast(ub[:, HALF:D].astype(F32), jnp.int32) & HI_MASK
    ext = pltpu.bitcast(_rows_to_cols(meta), jnp.int32)
    row = jnp.concatenate([hi | lo, ext, jnp.zeros((tm, D - HALF - RLANES), jnp.int32)], axis=1)
    urow_ref[...] = row.reshape((tm,) + ROW_SHAPE)


def _route_specs(layer, nb, nt, tm):
    tile = lambda b, t: b * nt + t
    n = nb * nt * tm
    in_specs = [_layer_spec((nb, 3 * D), 2 * layer + 1),
                _layer_spec((RROWS, D), layer), _layer_spec((RROWS, RLANES), layer)]
    out_specs = [pl.BlockSpec((tm,) + ROW_SHAPE, lambda b, t: (tile(b, t), 0, 0)),
                 pl.BlockSpec((None, ROW_TILE, tm), lambda b, t: (tile(b, t), 0, 0)),
                 _const_spec((RROWS, RLANES))]
    out_shape = [jax.ShapeDtypeStruct((n,) + ROW_SHAPE, jnp.int32),
                 jax.ShapeDtypeStruct((nb * nt, ROW_TILE, tm), F32),
                 jax.ShapeDtypeStruct((RROWS, RLANES), F32)]
    return in_specs, out_specs, out_shape, [pltpu.VMEM((RROWS, RLANES), F32)]


def _row_copy(src, src_row, dst, dst_row, sem):
    return pltpu.make_async_copy(src.at[pl.ds(src_row, 1)], dst.at[pl.ds(dst_row, 1)], sem)


def _dispatch_body(pos_ref, zflag_ref, urow_ref, xs_hbm, zbuf, sem, zsem):
    base = pl.program_id(0) * DISPATCH_TD

    @pl.when(pl.program_id(0) == 0)
    def _():
        zbuf[...] = jnp.zeros(zbuf.shape, jnp.int32)

        def zero_copy(t):
            return pltpu.make_async_copy(zbuf, xs_hbm.at[pl.ds(t * FFN_TM, FFN_TM)], zsem)

        def start(t, carry):
            @pl.when(zflag_ref[t] != 0)
            def _():
                zero_copy(t).start()
            return carry

        def wait(t, carry):
            @pl.when(zflag_ref[t] != 0)
            def _():
                zero_copy(t).wait()
            return carry

        lax.fori_loop(0, zflag_ref.shape[0], start, 0)
        lax.fori_loop(0, zflag_ref.shape[0], wait, 0)

    def issue(g, carry):
        r0 = pl.multiple_of(g * ROW_TILE, ROW_TILE)
        for k in range(ROW_TILE):
            _row_copy(urow_ref.at[pl.ds(r0, ROW_TILE)], k, xs_hbm, pos_ref[base + r0 + k],
                      sem).start(priority=k % 2)
        return carry

    lax.fori_loop(0, DISPATCH_TD // ROW_TILE, issue, 0)
    pltpu.make_async_copy(urow_ref, xs_hbm.at[pl.ds(0, DISPATCH_TD)], sem).wait()


def _dispatch(pos, zflag, urow):
    n = urow.shape[0]
    nt = zflag.shape[0]
    return pl.pallas_call(
        _dispatch_body,
        grid_spec=pltpu.PrefetchScalarGridSpec(
            num_scalar_prefetch=2, grid=(n // DISPATCH_TD,),
            in_specs=[pl.BlockSpec((DISPATCH_TD,) + ROW_SHAPE, lambda i, pos, zflag: (i, 0, 0))],
            out_specs=pl.BlockSpec(memory_space=pl.ANY),
            scratch_shapes=[pltpu.VMEM((FFN_TM,) + ROW_SHAPE, jnp.int32), pltpu.SemaphoreType.DMA(()),
                            pltpu.SemaphoreType.DMA(())]),
        out_shape=jax.ShapeDtypeStruct((nt * FFN_TM,) + ROW_SHAPE, jnp.int32),
        compiler_params=_cp(("arbitrary",)),
        name="moe_dispatch",
    )(pos, zflag, urow)


def _ffn_sorted_body(ta_ref, tb_ref, na_ref, xs_ref, w1a, w3a, w2a, w1b, w3b, w2b, ys_ref):
    del ta_ref, tb_ref
    active = pl.program_id(0) < na_ref[0]

    @pl.when(active)
    def _():
        words = xs_ref[...].reshape(FFN_TM, D)
        pair = words[:, 0:HALF]
        x = jnp.concatenate([pltpu.bitcast(lax.shift_left(pair, 16), F32).astype(BF16),
                             pltpu.bitcast(pair & HI_MASK, F32).astype(BF16)], axis=1)
        ext = pltpu.bitcast(words[:, HALF:HALF + RLANES], F32)
        ha = jax.nn.silu(_dot(x, w1a[...])) * _dot(x, w3a[...]) * ext[:, 0:1]
        hb = jax.nn.silu(_dot(x, w1b[...])) * _dot(x, w3b[...]) * ext[:, 1:2]
        y = _dot(ha.astype(BF16), w2a[...]) + _dot(hb.astype(BF16), w2b[...])
        ys_ref[...] = y.reshape(FFN_TM, D // LANES, LANES)

    @pl.when(jnp.logical_not(active))
    def _():
        ys_ref[...] = jnp.zeros(ys_ref.shape, F32)


def _ffn_sorted(ta, tb, na, xs, w1, w3, w2):
    nt = ta.shape[0]
    ea = lambda t, ta, tb, na: (ta[t], 0, 0)
    eb = lambda t, ta, tb, na: (tb[t], 0, 0)
    rows = lambda t, ta, tb, na: (t, 0)
    up, down = (None, D, F), (None, F, D)
    return pl.pallas_call(
        _ffn_sorted_body,
        grid_spec=pltpu.PrefetchScalarGridSpec(
            num_scalar_prefetch=3, grid=(nt,),
            in_specs=[
                pl.BlockSpec((FFN_TM,) + ROW_SHAPE, lambda t, ta, tb, na: (t, 0, 0)),
                pl.BlockSpec(up, ea), pl.BlockSpec(up, ea), pl.BlockSpec(down, ea),
                pl.BlockSpec(up, eb), pl.BlockSpec(up, eb), pl.BlockSpec(down, eb),
            ],
            out_specs=pl.BlockSpec((FFN_TM, D // LANES, LANES), lambda t, ta, tb, na: (t, 0, 0))),
        out_shape=jax.ShapeDtypeStruct((nt * FFN_TM, D // LANES, LANES), F32),
        compiler_params=_cp(("arbitrary",)),
        name="moe_ffn",
    )(ta, tb, na, xs, w1, w3, w2, w1, w3, w2)


def _combine_body(pos_ref, x_ref, mod_ref, ys_hbm, g_ref, b_ref, *rest, tiles_per_seq, with_kv):
    if with_kv:
        kvw_ref, kvwt_ref, xn_ref, *kv_out, buf, sem = rest
    else:
        xn_ref, buf, sem = rest
    i = pl.program_id(0)
    nsteps = pl.num_programs(0)
    tc = x_ref.shape[0]

    def issue_group(step, slot, r0):
        group = buf.at[slot, pl.ds(r0, ROW_TILE)]
        for k in range(ROW_TILE):
            _row_copy(ys_hbm, pos_ref[step * tc + r0 + k], group, k, sem.at[slot]).start(priority=1)

    def wait_tile(slot):
        pltpu.make_async_copy(ys_hbm.at[pl.ds(0, tc)], buf.at[slot], sem.at[slot]).wait()

    ahead = COMBINE_SLOTS - 1

    @pl.when(i == 0)
    def _():
        for t in range(ahead):
            def prime(g, carry, t=t):
                issue_group(t, t, pl.multiple_of(g * ROW_TILE, ROW_TILE))
                return carry
            lax.fori_loop(0, tc // ROW_TILE, prime, 0)

    slot = i % COMBINE_SLOTS
    wait_tile(slot)
    _, _, g1p = _split_mod(mod_ref, row=i // tiles_per_seq)
    gain, bias = g_ref[...], b_ref[...]
    nxt = jnp.minimum(i + ahead, nsteps - 1)
    nslot = (i + ahead) % COMBINE_SLOTS

    for c in range(tc // COMBINE_CHUNK):
        for g in range(COMBINE_CHUNK // ROW_TILE):
            issue_group(nxt, nslot, c * COMBINE_CHUNK + g * ROW_TILE)
        rows = pl.ds(c * COMBINE_CHUNK, COMBINE_CHUNK)
        z = ALPHA * x_ref[rows, :] + g1p * buf[slot, rows].reshape(COMBINE_CHUNK, D)
        xn_ref[rows, :] = _layer_norm(z, gain, bias)

    if with_kv:
        _kv_rows(xn_ref[...].astype(BF16), kvw_ref, kvwt_ref, *kv_out)

    @pl.when(i == nsteps - 1)
    def _():
        for k in range(1, COMBINE_SLOTS):
            wait_tile((i + k) % COMBINE_SLOTS)


def _combine(pos, x, mod, ys, ln_g, ln_b, kv_weights=None, *, layer, seq):
    n = x.shape[0]
    tc = COMBINE_TC
    sub = 2 * layer + 1
    row = lambda i, pos: (i, 0)
    in_specs = [
        pl.BlockSpec((tc, D), row),
        _layer_spec((n // seq, 3 * D), sub),
        pl.BlockSpec(memory_space=pl.ANY),
        _layer_spec((1, D), sub), _layer_spec((1, D), sub),
    ]
    out_specs = [pl.BlockSpec((tc, D), row)]
    out_shape = [jax.ShapeDtypeStruct((n, D), F32)]
    args = [pos, x, mod, ys, ln_g, ln_b]
    if kv_weights is not None:
        in_specs += [pl.BlockSpec((D, 2 * KVW), lambda i, pos: (0, 0)),
                     pl.BlockSpec((KVW, D), lambda i, pos: (0, 0))]
        out_specs += [pl.BlockSpec((tc, KVW), row), pl.BlockSpec((tc, KVW), row),
                      pl.BlockSpec((tc, KVW), row), pl.BlockSpec((KVW, tc), lambda i, pos: (0, i))]
        out_shape += [jax.ShapeDtypeStruct((n, KVW), F32), jax.ShapeDtypeStruct((n, KVW), F32),
                      jax.ShapeDtypeStruct((n, KVW), BF16), jax.ShapeDtypeStruct((KVW, n), BF16)]
        args += list(kv_weights)
    return pl.pallas_call(
        functools.partial(_combine_body, tiles_per_seq=seq // tc, with_kv=kv_weights is not None),
        grid_spec=pltpu.PrefetchScalarGridSpec(
            num_scalar_prefetch=1, grid=(n // tc,),
            in_specs=in_specs, out_specs=out_specs,
            scratch_shapes=[pltpu.VMEM((COMBINE_SLOTS, tc, D // LANES, LANES), F32),
                            pltpu.SemaphoreType.DMA((COMBINE_SLOTS,))]),
        out_shape=out_shape,
        compiler_params=_cp(("arbitrary",)),
        name="moe_combine",
    )(*args)


def _moe_sparse(x, route, mod, w1, w3, w2, ln_g, ln_b, kv_weights=None, *, layer, seq):
    n = x.shape[0]
    urow, meta, cnt = route
    cls = meta[:, 2, :].reshape(n).astype(jnp.int32)
    rank = meta[:, 3, :].reshape(n).astype(jnp.int32)
    count = cnt[:NCLS, 0].astype(jnp.int32)
    ntile = (count + FFN_TM - 1) // FFN_TM
    tile_end = jnp.cumsum(ntile)
    row_start = (tile_end - ntile) * FFN_TM
    pos = rank + jnp.sum(jax.nn.one_hot(cls, NCLS, dtype=jnp.int32) * row_start[None, :], axis=1)
    nt = n // FFN_TM + NCLS
    total = tile_end[NCLS - 1]
    tr = jnp.minimum(jnp.arange(nt, dtype=jnp.int32), total - 1)
    tcls = jnp.sum((tr[:, None] >= tile_end[None, :]).astype(jnp.int32), axis=1)
    grp, pair = tcls // PAIRS, tcls % PAIRS
    lo = (pair >= 3).astype(jnp.int32) + (pair >= 5).astype(jnp.int32)
    hi = pair + 1 - lo * (2 * E_PER_G - 3 - lo) // 2
    ta = grp * E_PER_G + lo
    tb = grp * E_PER_G + hi
    t_all = jnp.arange(nt, dtype=jnp.int32)
    partial_last = ((t_all[:, None] == tile_end[None, :] - 1) & (count % FFN_TM != 0)[None, :]).any(axis=1)
    zflag = ((t_all >= total) | partial_last).astype(jnp.int32)
    xs = _dispatch(pos, zflag, urow)
    ys = _ffn_sorted(ta, tb, total[None], xs, w1, w3, w2)
    return _combine(pos, x, mod, ys, ln_g, ln_b, kv_weights, layer=layer, seq=seq)


def _moe_body(x_ref, mod_ref, wr_ref, br_ref, w1f_ref, w3f_ref, w2f_ref, g_ref, b_ref,
              xn_ref, w1_ref, w3_ref, w2_ref, ub, comb, acc, *, reps):
    e = pl.program_id(1)
    w1_ref[...] = w1f_ref[...].astype(BF16)
    w3_ref[...] = w3f_ref[...].astype(BF16)
    w2_ref[...] = w2f_ref[...].astype(BF16)

    @pl.when(e == 0)
    def _():
        shift, sc1p, _ = _split_mod(mod_ref, reps=reps)
        u = (x_ref[...] * sc1p + shift).astype(BF16)
        ub[...] = u
        eid, _, i1, i2, w1, w2 = _top2(_router_logits(u, wr_ref, br_ref))
        comb[...] = _rows_to_cols(jnp.where(eid == i1, w1, 0.0) + jnp.where(eid == i2, w2, 0.0))
        acc[...] = jnp.zeros(acc.shape, F32)

    u = ub[...]
    col = lax.broadcasted_iota(jnp.int32, comb.shape, 1)
    ce = jnp.sum(jnp.where(col == e, comb[...], 0.0), axis=-1, keepdims=True)
    h = jax.nn.silu(_dot(u, w1_ref[...])) * _dot(u, w3_ref[...]) * ce
    acc[...] += _dot(h.astype(BF16), w2_ref[...])

    @pl.when(e == N_EXP - 1)
    def _():
        _, _, g1p = _split_mod(mod_ref, reps=reps)
        xn_ref[...] = _layer_norm(ALPHA * x_ref[...] + g1p * acc[...], g_ref[...], b_ref[...])


def _moe_dense(x, mod, wr, br, w1, w3, w2, ln_g, ln_b, *, layer, reps):
    n = x.shape[0]
    sub = 2 * layer + 1
    expert = lambda i, e: (layer, e, 0, 0)
    return pl.pallas_call(
        functools.partial(_moe_body, reps=reps),
        grid=(1, N_EXP),
        in_specs=[
            pl.BlockSpec((n, D), lambda i, e: (0, 0)),
            _layer_spec((n // reps, 3 * D), sub),
            _layer_spec((RROWS, D), layer), _layer_spec((RROWS, RLANES), layer),
            pl.BlockSpec((None, None, D, F), expert),
            pl.BlockSpec((None, None, D, F), expert),
            pl.BlockSpec((None, None, F, D), expert),
            _layer_spec((1, D), sub), _layer_spec((1, D), sub),
        ],
        out_specs=[pl.BlockSpec((n, D), lambda i, e: (0, 0)),
                   pl.BlockSpec((None, D, F), lambda i, e: (e, 0, 0)),
                   pl.BlockSpec((None, D, F), lambda i, e: (e, 0, 0)),
                   pl.BlockSpec((None, F, D), lambda i, e: (e, 0, 0))],
        out_shape=[jax.ShapeDtypeStruct((n, D), F32),
                   jax.ShapeDtypeStruct((N_EXP, D, F), BF16),
                   jax.ShapeDtypeStruct((N_EXP, D, F), BF16),
                   jax.ShapeDtypeStruct((N_EXP, F, D), BF16)],
        scratch_shapes=[
            pltpu.VMEM((n, D), BF16),
            pltpu.VMEM((n, RLANES), F32),
            pltpu.VMEM((n, D), F32),
        ],
        compiler_params=_cp(("arbitrary", "arbitrary")),
        name="moe",
    )(x, mod, wr, br, w1, w3, w2, ln_g, ln_b)


def _kv_body(x_ref, w_ref, wvt_ref, k_ref, v_ref, k16_ref, vt16_ref):
    _kv_rows(x_ref[...].astype(BF16), w_ref, wvt_ref, k_ref, v_ref, k16_ref, vt16_ref)


def _kv_rows(xb, w_ref, wvt_ref, k_ref, v_ref, k16_ref, vt16_ref):
    kv = _dot(xb, w_ref[...])
    k_ref[...] = kv[:, :KVW]
    v_ref[...] = kv[:, KVW:]
    k16_ref[...] = kv[:, :KVW].astype(BF16)
    vt16_ref[...] = lax.dot_general(wvt_ref[...], xb, _NT, preferred_element_type=F32).astype(BF16)


def _kv(x, w, wvt, *, tm):
    n = x.shape[0]
    row = lambda i: (i, 0)
    return pl.pallas_call(
        _kv_body,
        grid=(n // tm,),
        in_specs=[pl.BlockSpec((tm, D), row), _const_spec((D, 2 * KVW)), _const_spec((KVW, D))],
        out_specs=[pl.BlockSpec((tm, KVW), row), pl.BlockSpec((tm, KVW), row),
                   pl.BlockSpec((tm, KVW), row), pl.BlockSpec((KVW, tm), lambda i: (0, i))],
        out_shape=[jax.ShapeDtypeStruct((n, KVW), F32), jax.ShapeDtypeStruct((n, KVW), F32),
                   jax.ShapeDtypeStruct((n, KVW), BF16), jax.ShapeDtypeStruct((KVW, n), BF16)],
        compiler_params=_cp(("arbitrary",)),
        name="kv_proj",
    )(x, w, wvt)


def _sink_attend(qh, kh, vh, bias, sink):
    s = lax.dot_general(qh, kh, (((1,), (1,)), ((), ())), preferred_element_type=F32) + bias
    m = jnp.maximum(jnp.max(s, axis=-1, keepdims=True), sink)
    p = jnp.exp(s - m)
    den = jnp.sum(p, axis=-1, keepdims=True) + jnp.exp(sink - m)
    return _dot(p.astype(BF16), vh) / den


def _attn_prompt_body(x_ref, mod_ref, wqt_ref, wo_ref, kc_ref, kp_ref, vtc_ref, vtp_ref,
                      bias_ref, sink_ref, g_ref, b_ref, mod1_ref, wr_ref, br_ref,
                      xn_ref, urow_ref, meta_ref, cnt_ref, carry, *, tq):
    t = pl.program_id(1)
    x = x_ref[...]
    shift, sc1p, g1p = _split_mod(mod_ref, row=pl.program_id(0))
    u = (x * sc1p + shift).astype(BF16)
    qt = lax.dot_general(wqt_ref[...], u, _NT, preferred_element_type=F32)
    qt = (qt * (HD ** -0.5)).astype(BF16)
    kc = kc_ref[...]
    vtc = vtc_ref[...]
    krow = lax.broadcasted_iota(jnp.int32, (2 * WIN, GRP * WIN), 0)
    first = jnp.where((krow < WIN) & (t == 0), NEG, 0.0).astype(F32)
    nwin = tq // WIN
    kks, vvts = [], []
    for j in range(nwin):
        if j == 0:
            kks.append(jnp.concatenate([kp_ref[...], kc[0:WIN]], axis=0))
            vvts.append(jnp.concatenate([vtp_ref[...], vtc[:, 0:WIN]], axis=1))
        else:
            kks.append(kc[(j - 1) * WIN:(j + 1) * WIN])
            vvts.append(vtc[:, (j - 1) * WIN:(j + 1) * WIN])

    def scores(j, hk):
        qht = jnp.concatenate(
            [qt[(hk * GRP + g) * HD:(hk * GRP + g + 1) * HD, j * WIN:(j + 1) * WIN] for g in range(GRP)],
            axis=1)
        st = _dot(kks[j][:, hk * HD:(hk + 1) * HD], qht) + bias_ref[hk]
        return st + first if j == 0 else st

    items = [(j, hk) for j in range(nwin) for hk in range(HKV)]
    heads = [[] for _ in range(nwin)]
    st_next = scores(*items[0])
    for n, (j, hk) in enumerate(items):
        st = st_next
        if n + 1 < len(items):
            st_next = scores(*items[n + 1])
        sink = sink_ref[hk]
        m = jnp.maximum(jnp.max(st, axis=0, keepdims=True), sink)
        pt = jnp.exp(st - m)
        den = jnp.sum(pt, axis=0, keepdims=True) + jnp.exp(sink - m)
        ot = _dot(vvts[j][hk * HD:(hk + 1) * HD, :], pt.astype(BF16)) * (1.0 / den)
        for g in range(GRP):
            heads[j].append(ot[:, g * WIN:(g + 1) * WIN])
    blocks = [jnp.concatenate(h, axis=0) for h in heads]
    oallt = jnp.concatenate(blocks, axis=1) if len(blocks) > 1 else blocks[0]
    mix = lax.dot_general(oallt.astype(BF16), wo_ref[...], _TN, preferred_element_type=F32)
    xn = _layer_norm(ALPHA * x + g1p * mix, g_ref[...], b_ref[...])
    xn_ref[...] = xn
    b = pl.program_id(0)
    _route_rows(xn, b, (b == 0) & (t == 0), mod1_ref, wr_ref, br_ref, urow_ref, meta_ref, cnt_ref, carry)


def _attn_prompt(x, mod, wqt, wo, k16, vt16, bias, sink, ln_g, ln_b, wr, br, *, layer, nb, seq, tq):
    nt = seq // tq
    r_in, r_out, r_shape, r_scratch = _route_specs(layer, nb, nt, tq)
    wpt = tq // WIN
    n = nb * seq
    j = layer - N_A
    sub = 2 * layer

    def prev(b, t):
        return jnp.maximum((b * nt + t) * wpt - 1, 0)

    cur = lambda b, t: (b * nt + t, 0)
    return pl.pallas_call(
        functools.partial(_attn_prompt_body, tq=tq),
        grid=(nb, nt),
        in_specs=[
            pl.BlockSpec((tq, D), cur),
            _layer_spec((nb, 3 * D), sub),
            _layer_spec((HQ * HD, D), j), _layer_spec((HQ * HD, D), j),
            pl.BlockSpec((tq, KVW), cur), pl.BlockSpec((WIN, KVW), lambda b, t: (prev(b, t), 0)),
            pl.BlockSpec((KVW, tq), lambda b, t: (0, b * nt + t)),
            pl.BlockSpec((KVW, WIN), lambda b, t: (0, prev(b, t))),
            _const_spec((HKV, 2 * WIN, GRP * WIN)), _layer_spec((HKV, 1, GRP * WIN), j),
            _layer_spec((1, D), sub), _layer_spec((1, D), sub),
        ] + r_in,
        out_specs=[pl.BlockSpec((tq, D), cur)] + r_out,
        out_shape=[jax.ShapeDtypeStruct((n, D), F32)] + r_shape,
        scratch_shapes=r_scratch,
        compiler_params=_cp(("arbitrary", "arbitrary")),
        name="attn_prompt",
    )(x, mod, wqt, wo, k16, k16, vt16, vt16, bias, sink, ln_g, ln_b, mod, wr, br)


def _attn_sample_body(x_ref, mod_ref, wq_ref, wo_ref, kb_ref, vb_ref, kn_ref, vn_ref,
                      bias_ref, sink_ref, g_ref, b_ref, xn_ref, *, ns, steps):
    nr = ns * steps
    x = x_ref[...].reshape(nr, D)
    shift, sc1p, g1p = _split_mod(mod_ref, reps=steps)
    u = (x * sc1p + shift).astype(BF16)
    q = (_dot(u, wq_ref[...]) * (HD ** -0.5)).astype(BF16)
    nbuf = kb_ref.shape[1]
    k_all = jnp.concatenate([kb_ref[...].reshape(ns * nbuf, KVW), kn_ref[...].reshape(nr, KVW)],
                            axis=0).astype(BF16)
    v_all = jnp.concatenate([vb_ref[...].reshape(ns * nbuf, KVW), vn_ref[...].reshape(nr, KVW)],
                            axis=0).astype(BF16)
    heads = [None] * HQ
    for hk in range(HKV):
        kcols = slice(hk * HD, (hk + 1) * HD)
        qh = jnp.concatenate(
            [q[:, (hk * GRP + g) * HD:(hk * GRP + g + 1) * HD] for g in range(GRP)], axis=0)
        o = _sink_attend(qh, k_all[:, kcols], v_all[:, kcols], bias_ref[hk], sink_ref[hk])
        for g in range(GRP):
            heads[hk * GRP + g] = o[g * nr:(g + 1) * nr]
    o_all = jnp.concatenate(heads, axis=1)
    mix = _dot(o_all.astype(BF16), wo_ref[...])
    xn_ref[...] = _layer_norm(ALPHA * x + g1p * mix, g_ref[...], b_ref[...]).reshape(steps, ns, D)


def _attn_sample(x, mod, wq, wo, kbuf, vbuf, kn, vn, bias, sink, ln_g, ln_b, *, layer, ns):
    steps, nb, _ = x.shape
    nbuf = kbuf.shape[1]
    j = layer - N_A
    sub = 2 * layer
    seqs = lambda i: (0, i, 0)
    return pl.pallas_call(
        functools.partial(_attn_sample_body, ns=ns, steps=steps),
        grid=(nb // ns,),
        in_specs=[
            pl.BlockSpec((steps, ns, D), seqs),
            pl.BlockSpec((None, ns, 3 * D), lambda i: (sub, i, 0)),
            _layer_spec((D, HQ * HD), j), _layer_spec((HQ * HD, D), j),
            pl.BlockSpec((ns, nbuf, KVW), lambda i: (i, 0, 0)),
            pl.BlockSpec((ns, nbuf, KVW), lambda i: (i, 0, 0)),
            pl.BlockSpec((steps, ns, KVW), seqs), pl.BlockSpec((steps, ns, KVW), seqs),
            _const_spec(bias.shape), _layer_spec(sink.shape[1:], j),
            _layer_spec((1, D), sub), _layer_spec((1, D), sub),
        ],
        out_specs=pl.BlockSpec((steps, ns, D), seqs),
        out_shape=jax.ShapeDtypeStruct((steps, nb, D), F32),
        compiler_params=_cp(("arbitrary",)),
        name="attn_sample",
    )(x, mod, wq, wo, kbuf, vbuf, kn, vn, bias, sink, ln_g, ln_b)


def _t5_bucket(d):
    max_exact = N_BUCKETS // 2
    d = np.maximum(d, 0)
    log_ratio = (np.log(np.maximum(d, 1).astype(np.float32) / np.float32(max_exact))
                 / np.float32(math.log(WIN / max_exact)))
    large = max_exact + (log_ratio * (N_BUCKETS - max_exact)).astype(np.int32)
    large = np.minimum(large, N_BUCKETS - 1)
    return np.where(d < max_exact, d, large)


def _bias_table(d, ok, rel_bias):
    sel = np.where(ok, _t5_bucket(d), N_BUCKETS)
    onehot = (sel[..., None] == np.arange(N_BUCKETS + 1)).astype(np.float32)
    tab = jnp.concatenate([rel_bias.astype(F32), jnp.full((1, HQ), NEG, F32)], axis=0)
    return jnp.einsum('abd,dh->hab', jnp.asarray(onehot), tab, precision=lax.Precision.HIGHEST)


def _prompt_bias(rel_bias):
    d = np.arange(WIN)[None, :] + WIN - np.arange(2 * WIN)[:, None]
    b = _bias_table(d, (d >= 0) & (d < WIN), rel_bias).reshape(HKV, GRP, 2 * WIN, WIN)
    return jnp.transpose(b, (0, 2, 1, 3)).reshape(HKV, 2 * WIN, GRP * WIN)


def _sample_bias(rel_bias, ns, steps, nbuf):
    qs = np.tile(np.arange(ns), steps)
    qt = np.repeat(np.arange(steps), ns)
    ks = np.concatenate([np.repeat(np.arange(ns), nbuf), qs])
    kpos = np.concatenate([np.tile(np.arange(nbuf), ns), nbuf + qt])
    d = qt[:, None] + nbuf - kpos[None, :]
    ok = (d >= 0) & (d < WIN) & (qs[:, None] == ks[None, :])
    return _bias_table(d, ok, rel_bias).reshape(HKV, GRP * ns * steps, kpos.shape[0])


LRU_TM = 512
ATTN_TQ = 1024
SAMPLE_NS = 8


def kernel(x_prompt, x_sample, state_rnn_h, state_rnn_conv, cache_win_k, cache_win_v, c_prompt, c_sample, ada_w, ada_b, ln_g, ln_b, lru_w_in, lru_conv_w, lru_conv_b, lru_w_a, lru_b_a, lru_w_x, lru_b_x, lru_lambda, lru_w_out, kv_w, attn_w_q, attn_sinks, attn_w_o, rel_bias, moe_w_group, moe_b_group, moe_w_router, moe_b_router, moe_w1, moe_w3, moe_w2):
    nbp, seq, _ = x_prompt.shape
    nbs, steps, _ = x_sample.shape
    nbuf = cache_win_k.shape[1]
    npt = nbp * seq
    nst = nbs * steps

    mod_p, mod_s = _adaln(c_prompt, c_sample, ada_w, ada_b)

    ln_g3 = ln_g.reshape(2 * DEPTH, 1, D)
    ln_b3 = ln_b.reshape(2 * DEPTH, 1, D)
    lw = (lru_w_in.astype(BF16), lru_conv_w, lru_conv_b[:, None], lru_w_a.astype(BF16),
          lru_w_x.astype(BF16), lru_b_a[:, None], lru_b_x[:, None], lru_lambda[:, None],
          lru_w_out.astype(BF16))
    zrow = lambda k: jnp.zeros((DEPTH, k, D), F32)
    wr = jnp.concatenate([jnp.swapaxes(moe_w_group, 1, 2), zrow(EROW - N_GROUPS),
                          jnp.swapaxes(moe_w_router.reshape(DEPTH, D, N_EXP), 1, 2),
                          zrow(RROWS - EROW - N_EXP)], axis=1).astype(BF16)
    zb = lambda k: jnp.zeros((DEPTH, k), F32)
    br = jnp.concatenate([moe_b_group, zb(EROW - N_GROUPS), moe_b_router.reshape(DEPTH, N_EXP),
                          zb(RROWS - EROW - N_EXP)], axis=1)
    br = jnp.broadcast_to(br[:, :, None], (DEPTH, RROWS, RLANES))
    wq = attn_w_q.astype(BF16)
    wqt = jnp.swapaxes(wq, 1, 2)
    wo = attn_w_o.astype(BF16)
    kvw = kv_w.astype(BF16)
    kvwt = kvw[:, KVW:].T
    bias_p = _prompt_bias(rel_bias)
    bias_s = _sample_bias(rel_bias, SAMPLE_NS, steps, nbuf)
    sink_p = jnp.repeat(attn_sinks.reshape(N_B, HKV, GRP), WIN, axis=2)[:, :, None, :]
    sink_s = jnp.repeat(attn_sinks.reshape(N_B, HKV, GRP), SAMPLE_NS * steps, axis=2)[..., None]
    kbuf = cache_win_k.reshape(nbs, nbuf, KVW)
    vbuf = cache_win_v.reshape(nbs, nbuf, KVW)

    xp = x_prompt.reshape(npt, D)
    xs = jnp.transpose(x_sample, (1, 0, 2)).reshape(nst, D)
    c0 = jnp.transpose(state_rnn_conv, (0, 2, 1, 3)).reshape(N_A, (CONV_W - 1) * nbs, R)
    hp, cp_, hs, cs = [], [], [], []
    kp = vp = kn = vn = kp16 = vpt16 = None
    for l in range(DEPTH):
        if l < N_A:
            xp, h_l, c_l, *route = _lru_prompt(xp, mod_p, lw, ln_g3, ln_b3, wr, br,
                                               layer=l, nb=nbp, seq=seq, tm=LRU_TM)
            hp.append(h_l[:, 0])
            cp_.append(c_l)
            xs, h_l, c_l = _lru_sample(xs, mod_s, state_rnn_h, c0[l], lw, ln_g3, ln_b3,
                                       layer=l, nb=nbs, steps=steps)
            hs.append(h_l)
            cs.append(jnp.transpose(c_l.reshape(CONV_W - 1, nbs, R), (1, 0, 2)))
        else:
            xp, *route = _attn_prompt(xp, mod_p, wqt, wo, kp16, vpt16, bias_p, sink_p, ln_g3, ln_b3, wr, br,
                                      layer=l, nb=nbp, seq=seq, tq=ATTN_TQ)
            xs = _attn_sample(xs.reshape(steps, nbs, D), mod_s, wq, wo, kbuf, vbuf,
                              kn.reshape(steps, nbs, KVW), vn.reshape(steps, nbs, KVW),
                              bias_s, sink_s, ln_g3, ln_b3, layer=l, ns=SAMPLE_NS).reshape(nst, D)
        xs, w1, w3, w2 = _moe_dense(xs, mod_s, wr, br, moe_w1, moe_w3, moe_w2, ln_g3, ln_b3,
                                    layer=l, reps=steps)
        if l == N_A - 1:
            xp, kp, vp, kp16, vpt16 = _moe_sparse(xp, route, mod_p, w1, w3, w2, ln_g3, ln_b3, (kvw, kvwt),
                                                  layer=l, seq=seq)
            kn, vn, _, _ = _kv(xs, kvw, kvwt, tm=nst)
        else:
            xp, = _moe_sparse(xp, route, mod_p, w1, w3, w2, ln_g3, ln_b3, layer=l, seq=seq)

    lp = min(WIN, seq)
    k_p = kp.reshape(nbp, seq, HKV, HD)[:, seq - lp:]
    v_p = vp.reshape(nbp, seq, HKV, HD)[:, seq - lp:]
    kn4 = jnp.transpose(kn.reshape(steps, nbs, HKV, HD), (1, 0, 2, 3))
    vn4 = jnp.transpose(vn.reshape(steps, nbs, HKV, HD), (1, 0, 2, 3))
    k_s = jnp.concatenate([cache_win_k, kn4], axis=1)[:, -nbuf:]
    v_s = jnp.concatenate([cache_win_v, vn4], axis=1)[:, -nbuf:]
    y_s = jnp.transpose(xs.reshape(steps, nbs, D), (1, 0, 2))
    return (xp.reshape(nbp, seq, D), y_s,
            jnp.stack(hp), jnp.stack(cp_), k_p, v_p,
            jnp.stack(hs), jnp.stack(cs), k_s, v_s)
```

```python
import functools
import math

import jax
import jax.numpy as jnp
import numpy as np
from jax import lax
from jax.experimental import pallas as pl
from jax.experimental.pallas import tpu as pltpu

D = 1024
R = 1024
DEPTH = 4
N_A = 2
N_B = DEPTH - N_A
N_BLK = 4
BLK = R // N_BLK
CONV_W = 4
RG_C = 8.0
HQ = 16
HKV = 4
HD = 64
GRP = HQ // HKV
KVW = HKV * HD
WIN = 128
N_BUCKETS = 32
N_GROUPS = 4
E_PER_G = 4
N_EXP = 16
F = 512
ALPHA = (2.0 * DEPTH) ** 0.25
LN_EPS = 1e-5
NEG = -1e30
LANES = 128
RLANES = LANES
ROW_TILE = 8

F32 = jnp.float32
BF16 = jnp.bfloat16

VMEM_LIMIT = 56 * 1024 * 1024


def _cp(sem):
    return pltpu.CompilerParams(dimension_semantics=sem, vmem_limit_bytes=VMEM_LIMIT)


def _dot(a, b):
    return jnp.dot(a, b, preferred_element_type=F32)


def _layer_norm(z, g, b):
    mu = jnp.mean(z, axis=-1, keepdims=True)
    zc = z - mu
    var = jnp.mean(zc * zc, axis=-1, keepdims=True)
    return zc * lax.rsqrt(var + LN_EPS) * g + b


def _split_mod(mod_ref, row=None, reps=1):
    m = mod_ref[...] if row is None else mod_ref[pl.ds(row, 1), :]
    if reps > 1:
        m = jnp.concatenate([m] * reps, axis=0)
    return m[:, 0:D], m[:, D:2 * D], m[:, 2 * D:3 * D]


def _layer_spec(shape, layer):
    nd = len(shape)
    return pl.BlockSpec((None,) + tuple(shape), lambda *_: (layer,) + (0,) * nd)


def _adaln_body(cp_ref, cs_ref, w_ref, b_ref, op_ref, os_ref):
    one = jnp.where(pl.program_id(1) > 0, 1.0, 0.0).astype(F32)
    w = w_ref[...].astype(BF16)
    op_ref[...] = _dot(cp_ref[...].astype(BF16), w) + b_ref[...] + one
    os_ref[...] = _dot(cs_ref[...].astype(BF16), w) + b_ref[...] + one


def _adaln(c_p, c_s, ada_w, ada_b):
    np_, ns_ = c_p.shape[0], c_s.shape[0]
    w = ada_w.reshape(2 * DEPTH, D, 3 * D)
    b = ada_b.reshape(2 * DEPTH, 1, 3 * D)
    return pl.pallas_call(
        _adaln_body,
        grid=(2 * DEPTH, 3),
        in_specs=[
            pl.BlockSpec((np_, D), lambda l, j: (0, 0)),
            pl.BlockSpec((ns_, D), lambda l, j: (0, 0)),
            pl.BlockSpec((None, D, D), lambda l, j: (l, 0, j)),
            pl.BlockSpec((None, 1, D), lambda l, j: (l, 0, j)),
        ],
        out_specs=[pl.BlockSpec((None, np_, D), lambda l, j: (l, 0, j)),
                   pl.BlockSpec((None, ns_, D), lambda l, j: (l, 0, j))],
        out_shape=[jax.ShapeDtypeStruct((2 * DEPTH, np_, 3 * D), F32),
                   jax.ShapeDtypeStruct((2 * DEPTH, ns_, 3 * D), F32)],
        compiler_params=_cp(("arbitrary", "arbitrary")),
        name="adaln",
    )(c_p, c_s, w, b)


def _lru_gates(xc, n, wa_ref, wx_ref, ba_ref, bx_ref, lam_ref):
    cols = slice(n * BLK, (n + 1) * BLK)
    xb = xc.astype(BF16)
    r = jax.nn.sigmoid(_dot(xb, wa_ref[n]) + ba_ref[:, cols])
    i = jax.nn.sigmoid(_dot(xb, wx_ref[n]) + bx_ref[:, cols])
    z = -lam_ref[:, cols]
    softplus = jnp.maximum(z, 0.0) + jnp.log1p(jnp.exp(-jnp.abs(z)))
    log_a = (-RG_C) * r * softplus
    a = jnp.exp(log_a)
    v = 1.0 - a * a
    b = jnp.where(v > 0.0, v * lax.rsqrt(v), 0.0) * i * xc
    return a, b


def _lru_prompt_body(x_ref, mod_ref, win_ref, cw_ref, cb_ref, wa_ref, wx_ref, ba_ref, bx_ref,
                     lam_ref, wout_ref, g_ref, b_ref, mod1_ref, wr_ref, br_ref,
                     xn_ref, hl_ref, cl_ref, urow_ref, meta_ref, cnt_ref,
                     xbuf, abuf, hbuf, hc, carry, *, tm):
    t = pl.program_id(1)

    @pl.when(t == 0)
    def _():
        xbuf[0:ROW_TILE, :] = jnp.zeros((ROW_TILE, R), F32)
        hc[...] = jnp.zeros((1, R), F32)

    x = x_ref[...]
    shift, sc1p, g1p = _split_mod(mod_ref, row=pl.program_id(0))
    u = x * sc1p + shift
    yx = _dot(u.astype(BF16), win_ref[...])
    y = jax.nn.gelu(yx[:, :R])
    xbuf[ROW_TILE:ROW_TILE + tm, :] = yx[:, R:]

    ngrp = tm // ROW_TILE
    rowmod = lax.broadcasted_iota(jnp.int32, (ngrp, ROW_TILE, BLK), 1)
    for n in range(N_BLK):
        cols = slice(n * BLK, (n + 1) * BLK)
        xc = cb_ref[:, cols] + cw_ref[3:4, cols] * xbuf[ROW_TILE:ROW_TILE + tm, cols]
        for k in range(1, CONV_W):
            xc = xc + cw_ref[3 - k:4 - k, cols] * xbuf[ROW_TILE - k:ROW_TILE - k + tm, cols]
        a, b = _lru_gates(xc, n, wa_ref, wx_ref, ba_ref, bx_ref, lam_ref)
        a = a.reshape(ngrp, ROW_TILE, BLK)
        b = b.reshape(ngrp, ROW_TILE, BLK)
        s = 1
        while s < ROW_TILE:
            a_sh = pltpu.roll(a, s, 1)
            b_sh = pltpu.roll(b, s, 1)
            m = rowmod >= s
            b = jnp.where(m, a * b_sh, 0.0) + b
            a = jnp.where(m, a * a_sh, a)
            s *= 2
        abuf[:, cols] = a.reshape(tm, BLK)
        hbuf[:, cols] = b.reshape(tm, BLK)

    def group(g, h):
        r0 = pl.multiple_of(g * ROW_TILE, ROW_TILE)
        hg = abuf[pl.ds(r0, ROW_TILE), :] * h + hbuf[pl.ds(r0, ROW_TILE), :]
        hbuf[pl.ds(r0, ROW_TILE), :] = hg
        return hg[ROW_TILE - 1:ROW_TILE, :]

    h_end = lax.fori_loop(0, tm // ROW_TILE, group, hc[...])
    hc[...] = h_end
    hl_ref[...] = h_end
    tail = xbuf[tm + ROW_TILE - (CONV_W - 1):tm + ROW_TILE, :]
    cl_ref[...] = tail
    xbuf[ROW_TILE - (CONV_W - 1):ROW_TILE, :] = tail

    mix = _dot((y * hbuf[...]).astype(BF16), wout_ref[...])
    xn = _layer_norm(ALPHA * x + g1p * mix, g_ref[...], b_ref[...])
    xn_ref[...] = xn
    b = pl.program_id(0)
    _route_rows(xn, b, (b == 0) & (t == 0), mod1_ref, wr_ref, br_ref, urow_ref, meta_ref, cnt_ref, carry)


def _const_spec(shape):
    nd = len(shape)
    return pl.BlockSpec(shape, lambda *_: (0,) * nd)


def _lru_weight_specs(layer):
    sub = 2 * layer
    return [
        _layer_spec((D, 2 * R), layer), _layer_spec((CONV_W, R), layer), _layer_spec((1, R), layer),
        _layer_spec((N_BLK, BLK, BLK), layer), _layer_spec((N_BLK, BLK, BLK), layer),
        _layer_spec((1, R), layer), _layer_spec((1, R), layer), _layer_spec((1, R), layer),
        _layer_spec((R, D), layer), _layer_spec((1, D), sub), _layer_spec((1, D), sub),
    ]


def _lru_prompt(x, mod, lw, ln_g, ln_b, wr, br, *, layer, nb, seq, tm):
    nt = seq // tm
    n = nb * seq
    r_in, r_out, r_shape, r_scratch = _route_specs(layer, nb, nt, tm)
    return pl.pallas_call(
        functools.partial(_lru_prompt_body, tm=tm),
        grid=(nb, nt),
        in_specs=[
            pl.BlockSpec((tm, D), lambda b, t: (b * nt + t, 0)),
            _layer_spec((nb, 3 * D), 2 * layer),
        ] + _lru_weight_specs(layer) + r_in,
        out_specs=[
            pl.BlockSpec((tm, D), lambda b, t: (b * nt + t, 0)),
            pl.BlockSpec((None, 1, R), lambda b, t: (b, 0, 0)),
            pl.BlockSpec((None, CONV_W - 1, R), lambda b, t: (b, 0, 0)),
        ] + r_out,
        out_shape=[
            jax.ShapeDtypeStruct((n, D), F32),
            jax.ShapeDtypeStruct((nb, 1, R), F32),
            jax.ShapeDtypeStruct((nb, CONV_W - 1, R), F32),
        ] + r_shape,
        scratch_shapes=[
            pltpu.VMEM((tm + ROW_TILE, R), F32),
            pltpu.VMEM((tm, R), F32),
            pltpu.VMEM((tm, R), F32),
            pltpu.VMEM((1, R), F32),
        ] + r_scratch,
        compiler_params=_cp(("arbitrary", "arbitrary")),
        name="lru_prompt",
    )(x, mod, *lw, ln_g, ln_b, mod, wr, br)


def _lru_sample_body(x_ref, mod_ref, h0_ref, c0_ref, win_ref, cw_ref, cb_ref, wa_ref, wx_ref,
                     ba_ref, bx_ref, lam_ref, wout_ref, g_ref, b_ref,
                     xn_ref, hl_ref, cl_ref, hbuf, *, nb, steps):
    x = x_ref[...]
    shift, sc1p, g1p = _split_mod(mod_ref, reps=steps)
    u = x * sc1p + shift
    yx = _dot(u.astype(BF16), win_ref[...])
    y = jax.nn.gelu(yx[:, :R])
    xext = jnp.concatenate([c0_ref[...], yx[:, R:]], axis=0)
    cl_ref[...] = xext[steps * nb:, :]
    for n in range(N_BLK):
        cols = slice(n * BLK, (n + 1) * BLK)
        xc = cb_ref[:, cols] + cw_ref[0:1, cols] * xext[0:steps * nb, cols]
        for k in range(1, CONV_W):
            xc = xc + cw_ref[k:k + 1, cols] * xext[k * nb:(k + steps) * nb, cols]
        a, b = _lru_gates(xc, n, wa_ref, wx_ref, ba_ref, bx_ref, lam_ref)
        h = h0_ref[:, cols]
        for t in range(steps):
            h = a[t * nb:(t + 1) * nb, :] * h + b[t * nb:(t + 1) * nb, :]
            hbuf[t * nb:(t + 1) * nb, cols] = h
        hl_ref[:, cols] = h
    mix = _dot((y * hbuf[...]).astype(BF16), wout_ref[...])
    xn_ref[...] = _layer_norm(ALPHA * x + g1p * mix, g_ref[...], b_ref[...])


def _lru_sample(x, mod, h0, c0, lw, ln_g, ln_b, *, layer, nb, steps):
    n = nb * steps
    return pl.pallas_call(
        functools.partial(_lru_sample_body, nb=nb, steps=steps),
        grid=(1,),
        in_specs=[
            _const_spec((n, D)), _layer_spec((nb, 3 * D), 2 * layer), _layer_spec((nb, R), layer),
            _const_spec(((CONV_W - 1) * nb, R)),
        ] + _lru_weight_specs(layer),
        out_specs=[_const_spec((n, D)), _const_spec((nb, R)), _const_spec(((CONV_W - 1) * nb, R))],
        out_shape=[
            jax.ShapeDtypeStruct((n, D), F32),
            jax.ShapeDtypeStruct((nb, R), F32),
            jax.ShapeDtypeStruct(((CONV_W - 1) * nb, R), F32),
        ],
        scratch_shapes=[pltpu.VMEM((n, R), F32)],
        compiler_params=_cp(("arbitrary",)),
        name="lru_sample",
    )(x, mod, h0, c0, *lw, ln_g, ln_b)


RROWS = 32
EROW = 8
_NT = (((1,), (1,)), ((), ()))
_TN = (((0,), (0,)), ((), ()))


def _top2(lt):
    nt = lt.shape[1]
    gl = lt[0:EROW, :]
    gid = lax.broadcasted_iota(jnp.int32, (EROW, nt), 0).astype(F32)
    is_g = gid < N_GROUPS
    gl = jnp.where(is_g, gl, NEG)
    gmax = jnp.max(gl, axis=0, keepdims=True)
    gidx = jnp.min(jnp.where(gl == gmax, gid, float(EROW)), axis=0, keepdims=True)
    gsum = jnp.sum(jnp.where(is_g, jnp.exp(gl - gmax), 0.0), axis=0, keepdims=True)
    g_p = 1.0 / gsum
    el = lt[EROW:EROW + N_EXP, :]
    eid = lax.broadcasted_iota(jnp.int32, (N_EXP, nt), 0).astype(F32)
    first = gidx * E_PER_G
    in_grp = (eid >= first) & (eid < first + E_PER_G)
    el = jnp.where(in_grp, el, NEG)
    big = float(N_EXP)
    t1 = jnp.max(el, axis=0, keepdims=True)
    i1 = jnp.min(jnp.where(el == t1, eid, big), axis=0, keepdims=True)
    el2 = jnp.where(eid == i1, NEG, el)
    t2 = jnp.max(el2, axis=0, keepdims=True)
    i2 = jnp.min(jnp.where((el2 == t2) & in_grp & (eid != i1), eid, big), axis=0, keepdims=True)
    e21 = jnp.exp(t2 - t1)
    w1 = g_p / (1.0 + e21)
    w2 = w1 * e21
    return eid, gidx, i1, i2, w1, w2


def _router_logits(u16, wrt_ref, brt_ref):
    return lax.dot_general(wrt_ref[...], u16, _NT, preferred_element_type=F32) + brt_ref[:, 0:1]


def _rows_to_cols(rows):
    r, nt = rows.shape
    padded = jnp.concatenate([rows, jnp.zeros((RLANES - r, nt), F32)], axis=0)
    return jnp.transpose(padded)


PAIRS = E_PER_G * (E_PER_G - 1) // 2
NCLS = N_GROUPS * PAIRS
HALF = D // 2
ROW_SHAPE = (D // LANES, LANES)
HI_MASK = -65536
FFN_TM = 256
DISPATCH_TD = 2048
COMBINE_TC = 512
COMBINE_CHUNK = 64
COMBINE_SLOTS = 4


def _route_rows(x, row, first, mod_ref, wr_ref, br_ref, urow_ref, meta_ref, cnt_ref, carry):
    @pl.when(first)
    def _():
        carry[...] = jnp.zeros(carry.shape, F32)

    tm = x.shape[0]
    shift, sc1p, _ = _split_mod(mod_ref, row=row)
    u = x * sc1p + shift
    _, gidx, i1, i2, w1, w2 = _top2(_router_logits(u.astype(BF16), wr_ref, br_ref))
    first = gidx * E_PER_G
    lo = jnp.minimum(i1, i2) - first
    hi = jnp.maximum(i1, i2) - first
    w_lo = jnp.where(i1 < i2, w1, w2)
    w_hi = jnp.where(i1 < i2, w2, w1)
    cls = gidx * PAIRS + lo * (2 * E_PER_G - 1 - lo) * 0.5 + (hi - lo - 1.0)
    cid = lax.broadcasted_iota(jnp.int32, (RROWS, tm), 0).astype(F32)
    onehot = (cid == cls).astype(F32)
    r = lax.broadcasted_iota(jnp.int32, (tm, tm), 0)
    c = lax.broadcasted_iota(jnp.int32, (tm, tm), 1)
    earlier = _dot(onehot.astype(BF16), (r < c).astype(BF16)) + carry[:, 0:1]
    rank = jnp.sum(onehot * earlier, axis=0, keepdims=True)
    carry[...] += jnp.sum(onehot, axis=1, keepdims=True)
    cnt_ref[...] = carry[...]
    meta = jnp.concatenate([w_lo, w_hi, cls, rank, jnp.zeros((ROW_TILE - 4, tm), F32)], axis=0)
    meta_ref[...] = meta
    ub = u.astype(BF16)
    lo = lax.shift_right_logical(pltpu.bitcast(ub[:, 0:HALF].astype(F32), jnp.int32), 16)
    hi = pltpu.bitcast(ub[:, HALF:D].astype(F32), jnp.int32) & HI_MASK
    ext = pltpu.bitcast(_rows_to_cols(meta), jnp.int32)
    row = jnp.concatenate([hi | lo, ext, jnp.zeros((tm, D - HALF - RLANES), jnp.int32)], axis=1)
    urow_ref[...] = row.reshape((tm,) + ROW_SHAPE)


def _route_specs(layer, nb, nt, tm):
    tile = lambda b, t: b * nt + t
    n = nb * nt * tm
    in_specs = [_layer_spec((nb, 3 * D), 2 * layer + 1),
                _layer_spec((RROWS, D), layer), _layer_spec((RROWS, RLANES), layer)]
    out_specs = [pl.BlockSpec((tm,) + ROW_SHAPE, lambda b, t: (tile(b, t), 0, 0)),
                 pl.BlockSpec((None, ROW_TILE, tm), lambda b, t: (tile(b, t), 0, 0)),
                 _const_spec((RROWS, RLANES))]
    out_shape = [jax.ShapeDtypeStruct((n,) + ROW_SHAPE, jnp.int32),
                 jax.ShapeDtypeStruct((nb * nt, ROW_TILE, tm), F32),
                 jax.ShapeDtypeStruct((RROWS, RLANES), F32)]
    return in_specs, out_specs, out_shape, [pltpu.VMEM((RROWS, RLANES), F32)]


def _row_copy(src, src_row, dst, dst_row, sem):
    return pltpu.make_async_copy(src.at[pl.ds(src_row, 1)], dst.at[pl.ds(dst_row, 1)], sem)


def _dispatch_body(pos_ref, zflag_ref, urow_ref, xs_hbm, zbuf, sem, zsem):
    base = pl.program_id(0) * DISPATCH_TD

    @pl.when(pl.program_id(0) == 0)
    def _():
        zbuf[...] = jnp.zeros(zbuf.shape, jnp.int32)

        def zero_copy(t):
            return pltpu.make_async_copy(zbuf, xs_hbm.at[pl.ds(t * FFN_TM, FFN_TM)], zsem)

        def start(t, carry):
            @pl.when(zflag_ref[t] != 0)
            def _():
                zero_copy(t).start()
            return carry

        def wait(t, carry):
            @pl.when(zflag_ref[t] != 0)
            def _():
                zero_copy(t).wait()
            return carry

        lax.fori_loop(0, zflag_ref.shape[0], start, 0)
        lax.fori_loop(0, zflag_ref.shape[0], wait, 0)

    def issue(g, carry):
        r0 = pl.multiple_of(g * ROW_TILE, ROW_TILE)
        for k in range(ROW_TILE):
            _row_copy(urow_ref.at[pl.ds(r0, ROW_TILE)], k, xs_hbm, pos_ref[base + r0 + k],
                      sem).start(priority=k % 2)
        return carry

    lax.fori_loop(0, DISPATCH_TD // ROW_TILE, issue, 0)
    pltpu.make_async_copy(urow_ref, xs_hbm.at[pl.ds(0, DISPATCH_TD)], sem).wait()


def _dispatch(pos, zflag, urow):
    n = urow.shape[0]
    nt = zflag.shape[0]
    return pl.pallas_call(
        _dispatch_body,
        grid_spec=pltpu.PrefetchScalarGridSpec(
            num_scalar_prefetch=2, grid=(n // DISPATCH_TD,),
            in_specs=[pl.BlockSpec((DISPATCH_TD,) + ROW_SHAPE, lambda i, pos, zflag: (i, 0, 0))],
            out_specs=pl.BlockSpec(memory_space=pl.ANY),
            scratch_shapes=[pltpu.VMEM((FFN_TM,) + ROW_SHAPE, jnp.int32), pltpu.SemaphoreType.DMA(()),
                            pltpu.SemaphoreType.DMA(())]),
        out_shape=jax.ShapeDtypeStruct((nt * FFN_TM,) + ROW_SHAPE, jnp.int32),
        compiler_params=_cp(("arbitrary",)),
        name="moe_dispatch",
    )(pos, zflag, urow)


def _ffn_sorted_body(ta_ref, tb_ref, na_ref, xs_ref, w1a, w3a, w2a, w1b, w3b, w2b, ys_ref):
    del ta_ref, tb_ref
    active = pl.program_id(0) < na_ref[0]

    @pl.when(active)
    def _():
        words = xs_ref[...].reshape(FFN_TM, D)
        pair = words[:, 0:HALF]
        x = jnp.concatenate([pltpu.bitcast(lax.shift_left(pair, 16), F32).astype(BF16),
                             pltpu.bitcast(pair & HI_MASK, F32).astype(BF16)], axis=1)
        ext = pltpu.bitcast(words[:, HALF:HALF + RLANES], F32)
        ha = jax.nn.silu(_dot(x, w1a[...])) * _dot(x, w3a[...]) * ext[:, 0:1]
        hb = jax.nn.silu(_dot(x, w1b[...])) * _dot(x, w3b[...]) * ext[:, 1:2]
        y = _dot(ha.astype(BF16), w2a[...]) + _dot(hb.astype(BF16), w2b[...])
        ys_ref[...] = y.reshape(FFN_TM, D // LANES, LANES)

    @pl.when(jnp.logical_not(active))
    def _():
        ys_ref[...] = jnp.zeros(ys_ref.shape, F32)


def _ffn_sorted(ta, tb, na, xs, w1, w3, w2):
    nt = ta.shape[0]
    ea = lambda t, ta, tb, na: (ta[t], 0, 0)
    eb = lambda t, ta, tb, na: (tb[t], 0, 0)
    rows = lambda t, ta, tb, na: (t, 0)
    up, down = (None, D, F), (None, F, D)
    return pl.pallas_call(
        _ffn_sorted_body,
        grid_spec=pltpu.PrefetchScalarGridSpec(
            num_scalar_prefetch=3, grid=(nt,),
            in_specs=[
                pl.BlockSpec((FFN_TM,) + ROW_SHAPE, lambda t, ta, tb, na: (t, 0, 0)),
                pl.BlockSpec(up, ea), pl.BlockSpec(up, ea), pl.BlockSpec(down, ea),
                pl.BlockSpec(up, eb), pl.BlockSpec(up, eb), pl.BlockSpec(down, eb),
            ],
            out_specs=pl.BlockSpec((FFN_TM, D // LANES, LANES), lambda t, ta, tb, na: (t, 0, 0))),
        out_shape=jax.ShapeDtypeStruct((nt * FFN_TM, D // LANES, LANES), F32),
        compiler_params=_cp(("arbitrary",)),
        name="moe_ffn",
    )(ta, tb, na, xs, w1, w3, w2, w1, w3, w2)


def _combine_body(pos_ref, x_ref, mod_ref, ys_hbm, g_ref, b_ref, *rest, tiles_per_seq, with_kv):
    if with_kv:
        kvw_ref, kvwt_ref, xn_ref, *kv_out, buf, sem = rest
    else:
        xn_ref, buf, sem = rest
    i = pl.program_id(0)
    nsteps = pl.num_programs(0)
    tc = x_ref.shape[0]

    def issue_group(step, slot, r0):
        group = buf.at[slot, pl.ds(r0, ROW_TILE)]
        for k in range(ROW_TILE):
            _row_copy(ys_hbm, pos_ref[step * tc + r0 + k], group, k, sem.at[slot]).start(priority=k % 2)

    def wait_tile(slot):
        pltpu.make_async_copy(ys_hbm.at[pl.ds(0, tc)], buf.at[slot], sem.at[slot]).wait()

    ahead = COMBINE_SLOTS - 1

    @pl.when(i == 0)
    def _():
        for t in range(ahead):
            def prime(g, carry, t=t):
                issue_group(t, t, pl.multiple_of(g * ROW_TILE, ROW_TILE))
                return carry
            lax.fori_loop(0, tc // ROW_TILE, prime, 0)

    slot = i % COMBINE_SLOTS
    wait_tile(slot)
    _, _, g1p = _split_mod(mod_ref, row=i // tiles_per_seq)
    gain, bias = g_ref[...], b_ref[...]
    nxt = jnp.minimum(i + ahead, nsteps - 1)
    nslot = (i + ahead) % COMBINE_SLOTS

    for c in range(tc // COMBINE_CHUNK):
        for g in range(COMBINE_CHUNK // ROW_TILE):
            issue_group(nxt, nslot, c * COMBINE_CHUNK + g * ROW_TILE)
        rows = pl.ds(c * COMBINE_CHUNK, COMBINE_CHUNK)
        z = ALPHA * x_ref[rows, :] + g1p * buf[slot, rows].reshape(COMBINE_CHUNK, D)
        xn_ref[rows, :] = _layer_norm(z, gain, bias)

    if with_kv:
        _kv_rows(xn_ref[...].astype(BF16), kvw_ref, kvwt_ref, *kv_out)

    @pl.when(i == nsteps - 1)
    def _():
        for k in range(1, COMBINE_SLOTS):
            wait_tile((i + k) % COMBINE_SLOTS)


def _combine(pos, x, mod, ys, ln_g, ln_b, kv_weights=None, *, layer, seq):
    n = x.shape[0]
    tc = COMBINE_TC
    sub = 2 * layer + 1
    row = lambda i, pos: (i, 0)
    in_specs = [
        pl.BlockSpec((tc, D), row),
        _layer_spec((n // seq, 3 * D), sub),
        pl.BlockSpec(memory_space=pl.ANY),
        _layer_spec((1, D), sub), _layer_spec((1, D), sub),
    ]
    out_specs = [pl.BlockSpec((tc, D), row)]
    out_shape = [jax.ShapeDtypeStruct((n, D), F32)]
    args = [pos, x, mod, ys, ln_g, ln_b]
    if kv_weights is not None:
        in_specs += [pl.BlockSpec((D, 2 * KVW), lambda i, pos: (0, 0)),
                     pl.BlockSpec((KVW, D), lambda i, pos: (0, 0))]
        out_specs += [pl.BlockSpec((tc, KVW), row), pl.BlockSpec((tc, KVW), row),
                      pl.BlockSpec((tc, KVW), row), pl.BlockSpec((KVW, tc), lambda i, pos: (0, i))]
        out_shape += [jax.ShapeDtypeStruct((n, KVW), F32), jax.ShapeDtypeStruct((n, KVW), F32),
                      jax.ShapeDtypeStruct((n, KVW), BF16), jax.ShapeDtypeStruct((KVW, n), BF16)]
        args += list(kv_weights)
    return pl.pallas_call(
        functools.partial(_combine_body, tiles_per_seq=seq // tc, with_kv=kv_weights is not None),
        grid_spec=pltpu.PrefetchScalarGridSpec(
            num_scalar_prefetch=1, grid=(n // tc,),
            in_specs=in_specs, out_specs=out_specs,
            scratch_shapes=[pltpu.VMEM((COMBINE_SLOTS, tc, D // LANES, LANES), F32),
                            pltpu.SemaphoreType.DMA((COMBINE_SLOTS,))]),
        out_shape=out_shape,
        compiler_params=_cp(("arbitrary",)),
        name="moe_combine",
    )(*args)


def _moe_sparse(x, route, mod, w1, w3, w2, ln_g, ln_b, kv_weights=None, *, layer, seq):
    n = x.shape[0]
    urow, meta, cnt = route
    cls = meta[:, 2, :].reshape(n).astype(jnp.int32)
    rank = meta[:, 3, :].reshape(n).astype(jnp.int32)
    count = cnt[:NCLS, 0].astype(jnp.int32)
    ntile = (count + FFN_TM - 1) // FFN_TM
    tile_end = jnp.cumsum(ntile)
    row_start = (tile_end - ntile) * FFN_TM
    pos = rank + jnp.sum(jax.nn.one_hot(cls, NCLS, dtype=jnp.int32) * row_start[None, :], axis=1)
    nt = n // FFN_TM + NCLS
    total = tile_end[NCLS - 1]
    tr = jnp.minimum(jnp.arange(nt, dtype=jnp.int32), total - 1)
    tcls = jnp.sum((tr[:, None] >= tile_end[None, :]).astype(jnp.int32), axis=1)
    grp, pair = tcls // PAIRS, tcls % PAIRS
    lo = (pair >= 3).astype(jnp.int32) + (pair >= 5).astype(jnp.int32)
    hi = pair + 1 - lo * (2 * E_PER_G - 3 - lo) // 2
    ta = grp * E_PER_G + lo
    tb = grp * E_PER_G + hi
    t_all = jnp.arange(nt, dtype=jnp.int32)
    partial_last = ((t_all[:, None] == tile_end[None, :] - 1) & (count % FFN_TM != 0)[None, :]).any(axis=1)
    zflag = ((t_all >= total) | partial_last).astype(jnp.int32)
    xs = _dispatch(pos, zflag, urow)
    ys = _ffn_sorted(ta, tb, total[None], xs, w1, w3, w2)
    return _combine(pos, x, mod, ys, ln_g, ln_b, kv_weights, layer=layer, seq=seq)


def _moe_body(x_ref, mod_ref, wr_ref, br_ref, w1f_ref, w3f_ref, w2f_ref, g_ref, b_ref,
              xn_ref, w1_ref, w3_ref, w2_ref, ub, comb, acc, *, reps):
    e = pl.program_id(1)
    w1_ref[...] = w1f_ref[...].astype(BF16)
    w3_ref[...] = w3f_ref[...].astype(BF16)
    w2_ref[...] = w2f_ref[...].astype(BF16)

    @pl.when(e == 0)
    def _():
        shift, sc1p, _ = _split_mod(mod_ref, reps=reps)
        u = (x_ref[...] * sc1p + shift).astype(BF16)
        ub[...] = u
        eid, _, i1, i2, w1, w2 = _top2(_router_logits(u, wr_ref, br_ref))
        comb[...] = _rows_to_cols(jnp.where(eid == i1, w1, 0.0) + jnp.where(eid == i2, w2, 0.0))
        acc[...] = jnp.zeros(acc.shape, F32)

    u = ub[...]
    col = lax.broadcasted_iota(jnp.int32, comb.shape, 1)
    ce = jnp.sum(jnp.where(col == e, comb[...], 0.0), axis=-1, keepdims=True)
    h = jax.nn.silu(_dot(u, w1_ref[...])) * _dot(u, w3_ref[...]) * ce
    acc[...] += _dot(h.astype(BF16), w2_ref[...])

    @pl.when(e == N_EXP - 1)
    def _():
        _, _, g1p = _split_mod(mod_ref, reps=reps)
        xn_ref[...] = _layer_norm(ALPHA * x_ref[...] + g1p * acc[...], g_ref[...], b_ref[...])


def _moe_dense(x, mod, wr, br, w1, w3, w2, ln_g, ln_b, *, layer, reps):
    n = x.shape[0]
    sub = 2 * layer + 1
    expert = lambda i, e: (layer, e, 0, 0)
    return pl.pallas_call(
        functools.partial(_moe_body, reps=reps),
        grid=(1, N_EXP),
        in_specs=[
            pl.BlockSpec((n, D), lambda i, e: (0, 0)),
            _layer_spec((n // reps, 3 * D), sub),
            _layer_spec((RROWS, D), layer), _layer_spec((RROWS, RLANES), layer),
            pl.BlockSpec((None, None, D, F), expert),
            pl.BlockSpec((None, None, D, F), expert),
            pl.BlockSpec((None, None, F, D), expert),
            _layer_spec((1, D), sub), _layer_spec((1, D), sub),
        ],
        out_specs=[pl.BlockSpec((n, D), lambda i, e: (0, 0)),
                   pl.BlockSpec((None, D, F), lambda i, e: (e, 0, 0)),
                   pl.BlockSpec((None, D, F), lambda i, e: (e, 0, 0)),
                   pl.BlockSpec((None, F, D), lambda i, e: (e, 0, 0))],
        out_shape=[jax.ShapeDtypeStruct((n, D), F32),
                   jax.ShapeDtypeStruct((N_EXP, D, F), BF16),
                   jax.ShapeDtypeStruct((N_EXP, D, F), BF16),
                   jax.ShapeDtypeStruct((N_EXP, F, D), BF16)],
        scratch_shapes=[
            pltpu.VMEM((n, D), BF16),
            pltpu.VMEM((n, RLANES), F32),
            pltpu.VMEM((n, D), F32),
        ],
        compiler_params=_cp(("arbitrary", "arbitrary")),
        name="moe",
    )(x, mod, wr, br, w1, w3, w2, ln_g, ln_b)


def _kv_body(x_ref, w_ref, wvt_ref, k_ref, v_ref, k16_ref, vt16_ref):
    _kv_rows(x_ref[...].astype(BF16), w_ref, wvt_ref, k_ref, v_ref, k16_ref, vt16_ref)


def _kv_rows(xb, w_ref, wvt_ref, k_ref, v_ref, k16_ref, vt16_ref):
    kv = _dot(xb, w_ref[...])
    k_ref[...] = kv[:, :KVW]
    v_ref[...] = kv[:, KVW:]
    k16_ref[...] = kv[:, :KVW].astype(BF16)
    vt16_ref[...] = lax.dot_general(wvt_ref[...], xb, _NT, preferred_element_type=F32).astype(BF16)


def _kv(x, w, wvt, *, tm):
    n = x.shape[0]
    row = lambda i: (i, 0)
    return pl.pallas_call(
        _kv_body,
        grid=(n // tm,),
        in_specs=[pl.BlockSpec((tm, D), row), _const_spec((D, 2 * KVW)), _const_spec((KVW, D))],
        out_specs=[pl.BlockSpec((tm, KVW), row), pl.BlockSpec((tm, KVW), row),
                   pl.BlockSpec((tm, KVW), row), pl.BlockSpec((KVW, tm), lambda i: (0, i))],
        out_shape=[jax.ShapeDtypeStruct((n, KVW), F32), jax.ShapeDtypeStruct((n, KVW), F32),
                   jax.ShapeDtypeStruct((n, KVW), BF16), jax.ShapeDtypeStruct((KVW, n), BF16)],
        compiler_params=_cp(("arbitrary",)),
        name="kv_proj",
    )(x, w, wvt)


def _sink_attend(qh, kh, vh, bias, sink):
    s = lax.dot_general(qh, kh, (((1,), (1,)), ((), ())), preferred_element_type=F32) + bias
    m = jnp.maximum(jnp.max(s, axis=-1, keepdims=True), sink)
    p = jnp.exp(s - m)
    den = jnp.sum(p, axis=-1, keepdims=True) + jnp.exp(sink - m)
    return _dot(p.astype(BF16), vh) / den


def _attn_prompt_body(x_ref, mod_ref, wqt_ref, wo_ref, kc_ref, kp_ref, vtc_ref, vtp_ref,
                      bias_ref, sink_ref, g_ref, b_ref, mod1_ref, wr_ref, br_ref,
                      xn_ref, urow_ref, meta_ref, cnt_ref, carry, *, tq):
    t = pl.program_id(1)
    x = x_ref[...]
    shift, sc1p, g1p = _split_mod(mod_ref, row=pl.program_id(0))
    u = (x * sc1p + shift).astype(BF16)
    qt = lax.dot_general(wqt_ref[...], u, _NT, preferred_element_type=F32)
    qt = (qt * (HD ** -0.5)).astype(BF16)
    kc = kc_ref[...]
    vtc = vtc_ref[...]
    krow = lax.broadcasted_iota(jnp.int32, (2 * WIN, GRP * WIN), 0)
    first = jnp.where((krow < WIN) & (t == 0), NEG, 0.0).astype(F32)
    nwin = tq // WIN
    kks, vvts = [], []
    for j in range(nwin):
        if j == 0:
            kks.append(jnp.concatenate([kp_ref[...], kc[0:WIN]], axis=0))
            vvts.append(jnp.concatenate([vtp_ref[...], vtc[:, 0:WIN]], axis=1))
        else:
            kks.append(kc[(j - 1) * WIN:(j + 1) * WIN])
            vvts.append(vtc[:, (j - 1) * WIN:(j + 1) * WIN])

    def scores(j, hk):
        qht = jnp.concatenate(
            [qt[(hk * GRP + g) * HD:(hk * GRP + g + 1) * HD, j * WIN:(j + 1) * WIN] for g in range(GRP)],
            axis=1)
        st = _dot(kks[j][:, hk * HD:(hk + 1) * HD], qht) + bias_ref[hk]
        return st + first if j == 0 else st

    items = [(j, hk) for j in range(nwin) for hk in range(HKV)]
    heads = [[] for _ in range(nwin)]
    st_next = scores(*items[0])
    for n, (j, hk) in enumerate(items):
        st = st_next
        if n + 1 < len(items):
            st_next = scores(*items[n + 1])
        sink = sink_ref[hk]
        m = jnp.maximum(jnp.max(st, axis=0, keepdims=True), sink)
        pt = jnp.exp(st - m)
        den = jnp.sum(pt, axis=0, keepdims=True) + jnp.exp(sink - m)
        ot = _dot(vvts[j][hk * HD:(hk + 1) * HD, :], pt.astype(BF16)) * (1.0 / den)
        for g in range(GRP):
            heads[j].append(ot[:, g * WIN:(g + 1) * WIN])
    blocks = [jnp.concatenate(h, axis=0) for h in heads]
    oallt = jnp.concatenate(blocks, axis=1) if len(blocks) > 1 else blocks[0]
    mix = lax.dot_general(oallt.astype(BF16), wo_ref[...], _TN, preferred_element_type=F32)
    xn = _layer_norm(ALPHA * x + g1p * mix, g_ref[...], b_ref[...])
    xn_ref[...] = xn
    b = pl.program_id(0)
    _route_rows(xn, b, (b == 0) & (t == 0), mod1_ref, wr_ref, br_ref, urow_ref, meta_ref, cnt_ref, carry)


def _attn_prompt(x, mod, wqt, wo, k16, vt16, bias, sink, ln_g, ln_b, wr, br, *, layer, nb, seq, tq):
    nt = seq // tq
    r_in, r_out, r_shape, r_scratch = _route_specs(layer, nb, nt, tq)
    wpt = tq // WIN
    n = nb * seq
    j = layer - N_A
    sub = 2 * layer

    def prev(b, t):
        return jnp.maximum((b * nt + t) * wpt - 1, 0)

    cur = lambda b, t: (b * nt + t, 0)
    return pl.pallas_call(
        functools.partial(_attn_prompt_body, tq=tq),
        grid=(nb, nt),
        in_specs=[
            pl.BlockSpec((tq, D), cur),
            _layer_spec((nb, 3 * D), sub),
            _layer_spec((HQ * HD, D), j), _layer_spec((HQ * HD, D), j),
            pl.BlockSpec((tq, KVW), cur), pl.BlockSpec((WIN, KVW), lambda b, t: (prev(b, t), 0)),
            pl.BlockSpec((KVW, tq), lambda b, t: (0, b * nt + t)),
            pl.BlockSpec((KVW, WIN), lambda b, t: (0, prev(b, t))),
            _const_spec((HKV, 2 * WIN, GRP * WIN)), _layer_spec((HKV, 1, GRP * WIN), j),
            _layer_spec((1, D), sub), _layer_spec((1, D), sub),
        ] + r_in,
        out_specs=[pl.BlockSpec((tq, D), cur)] + r_out,
        out_shape=[jax.ShapeDtypeStruct((n, D), F32)] + r_shape,
        scratch_shapes=r_scratch,
        compiler_params=_cp(("arbitrary", "arbitrary")),
        name="attn_prompt",
    )(x, mod, wqt, wo, k16, k16, vt16, vt16, bias, sink, ln_g, ln_b, mod, wr, br)


def _attn_sample_body(x_ref, mod_ref, wq_ref, wo_ref, kb_ref, vb_ref, kn_ref, vn_ref,
                      bias_ref, sink_ref, g_ref, b_ref, xn_ref, *, ns, steps):
    nr = ns * steps
    x = x_ref[...].reshape(nr, D)
    shift, sc1p, g1p = _split_mod(mod_ref, reps=steps)
    u = (x * sc1p + shift).astype(BF16)
    q = (_dot(u, wq_ref[...]) * (HD ** -0.5)).astype(BF16)
    nbuf = kb_ref.shape[1]
    k_all = jnp.concatenate([kb_ref[...].reshape(ns * nbuf, KVW), kn_ref[...].reshape(nr, KVW)],
                            axis=0).astype(BF16)
    v_all = jnp.concatenate([vb_ref[...].reshape(ns * nbuf, KVW), vn_ref[...].reshape(nr, KVW)],
                            axis=0).astype(BF16)
    heads = [None] * HQ
    for hk in range(HKV):
        kcols = slice(hk * HD, (hk + 1) * HD)
        qh = jnp.concatenate(
            [q[:, (hk * GRP + g) * HD:(hk * GRP + g + 1) * HD] for g in range(GRP)], axis=0)
        o = _sink_attend(qh, k_all[:, kcols], v_all[:, kcols], bias_ref[hk], sink_ref[hk])
        for g in range(GRP):
            heads[hk * GRP + g] = o[g * nr:(g + 1) * nr]
    o_all = jnp.concatenate(heads, axis=1)
    mix = _dot(o_all.astype(BF16), wo_ref[...])
    xn_ref[...] = _layer_norm(ALPHA * x + g1p * mix, g_ref[...], b_ref[...]).reshape(steps, ns, D)


def _attn_sample(x, mod, wq, wo, kbuf, vbuf, kn, vn, bias, sink, ln_g, ln_b, *, layer, ns):
    steps, nb, _ = x.shape
    nbuf = kbuf.shape[1]
    j = layer - N_A
    sub = 2 * layer
    seqs = lambda i: (0, i, 0)
    return pl.pallas_call(
        functools.partial(_attn_sample_body, ns=ns, steps=steps),
        grid=(nb // ns,),
        in_specs=[
            pl.BlockSpec((steps, ns, D), seqs),
            pl.BlockSpec((None, ns, 3 * D), lambda i: (sub, i, 0)),
            _layer_spec((D, HQ * HD), j), _layer_spec((HQ * HD, D), j),
            pl.BlockSpec((ns, nbuf, KVW), lambda i: (i, 0, 0)),
            pl.BlockSpec((ns, nbuf, KVW), lambda i: (i, 0, 0)),
            pl.BlockSpec((steps, ns, KVW), seqs), pl.BlockSpec((steps, ns, KVW), seqs),
            _const_spec(bias.shape), _layer_spec(sink.shape[1:], j),
            _layer_spec((1, D), sub), _layer_spec((1, D), sub),
        ],
        out_specs=pl.BlockSpec((steps, ns, D), seqs),
        out_shape=jax.ShapeDtypeStruct((steps, nb, D), F32),
        compiler_params=_cp(("arbitrary",)),
        name="attn_sample",
    )(x, mod, wq, wo, kbuf, vbuf, kn, vn, bias, sink, ln_g, ln_b)


def _t5_bucket(d):
    max_exact = N_BUCKETS // 2
    d = np.maximum(d, 0)
    log_ratio = (np.log(np.maximum(d, 1).astype(np.float32) / np.float32(max_exact))
                 / np.float32(math.log(WIN / max_exact)))
    large = max_exact + (log_ratio * (N_BUCKETS - max_exact)).astype(np.int32)
    large = np.minimum(large, N_BUCKETS - 1)
    return np.where(d < max_exact, d, large)


def _bias_table(d, ok, rel_bias):
    sel = np.where(ok, _t5_bucket(d), N_BUCKETS)
    onehot = (sel[..., None] == np.arange(N_BUCKETS + 1)).astype(np.float32)
    tab = jnp.concatenate([rel_bias.astype(F32), jnp.full((1, HQ), NEG, F32)], axis=0)
    return jnp.einsum('abd,dh->hab', jnp.asarray(onehot), tab, precision=lax.Precision.HIGHEST)


def _prompt_bias(rel_bias):
    d = np.arange(WIN)[None, :] + WIN - np.arange(2 * WIN)[:, None]
    b = _bias_table(d, (d >= 0) & (d < WIN), rel_bias).reshape(HKV, GRP, 2 * WIN, WIN)
    return jnp.transpose(b, (0, 2, 1, 3)).reshape(HKV, 2 * WIN, GRP * WIN)


def _sample_bias(rel_bias, ns, steps, nbuf):
    qs = np.tile(np.arange(ns), steps)
    qt = np.repeat(np.arange(steps), ns)
    ks = np.concatenate([np.repeat(np.arange(ns), nbuf), qs])
    kpos = np.concatenate([np.tile(np.arange(nbuf), ns), nbuf + qt])
    d = qt[:, None] + nbuf - kpos[None, :]
    ok = (d >= 0) & (d < WIN) & (qs[:, None] == ks[None, :])
    return _bias_table(d, ok, rel_bias).reshape(HKV, GRP * ns * steps, kpos.shape[0])


LRU_TM = 512
ATTN_TQ = 1024
SAMPLE_NS = 8


def kernel(x_prompt, x_sample, state_rnn_h, state_rnn_conv, cache_win_k, cache_win_v, c_prompt, c_sample, ada_w, ada_b, ln_g, ln_b, lru_w_in, lru_conv_w, lru_conv_b, lru_w_a, lru_b_a, lru_w_x, lru_b_x, lru_lambda, lru_w_out, kv_w, attn_w_q, attn_sinks, attn_w_o, rel_bias, moe_w_group, moe_b_group, moe_w_router, moe_b_router, moe_w1, moe_w3, moe_w2):
    nbp, seq, _ = x_prompt.shape
    nbs, steps, _ = x_sample.shape
    nbuf = cache_win_k.shape[1]
    npt = nbp * seq
    nst = nbs * steps

    mod_p, mod_s = _adaln(c_prompt, c_sample, ada_w, ada_b)

    ln_g3 = ln_g.reshape(2 * DEPTH, 1, D)
    ln_b3 = ln_b.reshape(2 * DEPTH, 1, D)
    lw = (lru_w_in.astype(BF16), lru_conv_w, lru_conv_b[:, None], lru_w_a.astype(BF16),
          lru_w_x.astype(BF16), lru_b_a[:, None], lru_b_x[:, None], lru_lambda[:, None],
          lru_w_out.astype(BF16))
    zrow = lambda k: jnp.zeros((DEPTH, k, D), F32)
    wr = jnp.concatenate([jnp.swapaxes(moe_w_group, 1, 2), zrow(EROW - N_GROUPS),
                          jnp.swapaxes(moe_w_router.reshape(DEPTH, D, N_EXP), 1, 2),
                          zrow(RROWS - EROW - N_EXP)], axis=1).astype(BF16)
    zb = lambda k: jnp.zeros((DEPTH, k), F32)
    br = jnp.concatenate([moe_b_group, zb(EROW - N_GROUPS), moe_b_router.reshape(DEPTH, N_EXP),
                          zb(RROWS - EROW - N_EXP)], axis=1)
    br = jnp.broadcast_to(br[:, :, None], (DEPTH, RROWS, RLANES))
    wq = attn_w_q.astype(BF16)
    wqt = jnp.swapaxes(wq, 1, 2)
    wo = attn_w_o.astype(BF16)
    kvw = kv_w.astype(BF16)
    kvwt = kvw[:, KVW:].T
    bias_p = _prompt_bias(rel_bias)
    bias_s = _sample_bias(rel_bias, SAMPLE_NS, steps, nbuf)
    sink_p = jnp.repeat(attn_sinks.reshape(N_B, HKV, GRP), WIN, axis=2)[:, :, None, :]
    sink_s = jnp.repeat(attn_sinks.reshape(N_B, HKV, GRP), SAMPLE_NS * steps, axis=2)[..., None]
    kbuf = cache_win_k.reshape(nbs, nbuf, KVW)
    vbuf = cache_win_v.reshape(nbs, nbuf, KVW)

    xp = x_prompt.reshape(npt, D)
    xs = jnp.transpose(x_sample, (1, 0, 2)).reshape(nst, D)
    c0 = jnp.transpose(state_rnn_conv, (0, 2, 1, 3)).reshape(N_A, (CONV_W - 1) * nbs, R)
    hp, cp_, hs, cs = [], [], [], []
    kp = vp = kn = vn = kp16 = vpt16 = None
    for l in range(DEPTH):
        if l < N_A:
            xp, h_l, c_l, *route = _lru_prompt(xp, mod_p, lw, ln_g3, ln_b3, wr, br,
                                               layer=l, nb=nbp, seq=seq, tm=LRU_TM)
            hp.append(h_l[:, 0])
            cp_.append(c_l)
            xs, h_l, c_l = _lru_sample(xs, mod_s, state_rnn_h, c0[l], lw, ln_g3, ln_b3,
                                       layer=l, nb=nbs, steps=steps)
            hs.append(h_l)
            cs.append(jnp.transpose(c_l.reshape(CONV_W - 1, nbs, R), (1, 0, 2)))
        else:
            xp, *route = _attn_prompt(xp, mod_p, wqt, wo, kp16, vpt16, bias_p, sink_p, ln_g3, ln_b3, wr, br,
                                      layer=l, nb=nbp, seq=seq, tq=ATTN_TQ)
            xs = _attn_sample(xs.reshape(steps, nbs, D), mod_s, wq, wo, kbuf, vbuf,
                              kn.reshape(steps, nbs, KVW), vn.reshape(steps, nbs, KVW),
                              bias_s, sink_s, ln_g3, ln_b3, layer=l, ns=SAMPLE_NS).reshape(nst, D)
        xs, w1, w3, w2 = _moe_dense(xs, mod_s, wr, br, moe_w1, moe_w3, moe_w2, ln_g3, ln_b3,
                                    layer=l, reps=steps)
        if l == N_A - 1:
            xp, kp, vp, kp16, vpt16 = _moe_sparse(xp, route, mod_p, w1, w3, w2, ln_g3, ln_b3, (kvw, kvwt),
                                                  layer=l, seq=seq)
            kn, vn, _, _ = _kv(xs, kvw, kvwt, tm=nst)
        else:
            xp, = _moe_sparse(xp, route, mod_p, w1, w3, w2, ln_g3, ln_b3, layer=l, seq=seq)

    lp = min(WIN, seq)
    k_p = kp.reshape(nbp, seq, HKV, HD)[:, seq - lp:]
    v_p = vp.reshape(nbp, seq, HKV, HD)[:, seq - lp:]
    kn4 = jnp.transpose(kn.reshape(steps, nbs, HKV, HD), (1, 0, 2, 3))
    vn4 = jnp.transpose(vn.reshape(steps, nbs, HKV, HD), (1, 0, 2, 3))
    k_s = jnp.concatenate([cache_win_k, kn4], axis=1)[:, -nbuf:]
    v_s = jnp.concatenate([cache_win_v, vn4], axis=1)[:, -nbuf:]
    y_s = jnp.transpose(xs.reshape(steps, nbs, D), (1, 0, 2))
    return (xp.reshape(nbp, seq, D), y_s,
            jnp.stack(hp), jnp.stack(cp_), k_p, v_p,
            jnp.stack(hs), jnp.stack(cs), k_s, v_s)
```

```python
import functools
import math

import jax
import jax.numpy as jnp
import numpy as np
from jax import lax
from jax.experimental import pallas as pl
from jax.experimental.pallas import tpu as pltpu

D = 1024
R = 1024
DEPTH = 4
N_A = 2
N_B = DEPTH - N_A
N_BLK = 4
BLK = R // N_BLK
CONV_W = 4
RG_C = 8.0
HQ = 16
HKV = 4
HD = 64
GRP = HQ // HKV
KVW = HKV * HD
WIN = 128
N_BUCKETS = 32
N_GROUPS = 4
E_PER_G = 4
N_EXP = 16
F = 512
ALPHA = (2.0 * DEPTH) ** 0.25
LN_EPS = 1e-5
NEG = -1e30
LOG2E = math.log2(math.e)
LANES = 128
RLANES = LANES
ROW_TILE = 8

F32 = jnp.float32
BF16 = jnp.bfloat16

VMEM_LIMIT = 56 * 1024 * 1024


def _cp(sem):
    return pltpu.CompilerParams(dimension_semantics=sem, vmem_limit_bytes=VMEM_LIMIT)


def _dot(a, b):
    return jnp.dot(a, b, preferred_element_type=F32)


def _layer_norm(z, g, b):
    mu = jnp.mean(z, axis=-1, keepdims=True)
    zc = z - mu
    var = jnp.mean(zc * zc, axis=-1, keepdims=True)
    return zc * lax.rsqrt(var + LN_EPS) * g + b


def _split_mod(mod_ref, row=None, reps=1):
    m = mod_ref[...] if row is None else mod_ref[pl.ds(row, 1), :]
    if reps > 1:
        m = jnp.concatenate([m] * reps, axis=0)
    return m[:, 0:D], m[:, D:2 * D], m[:, 2 * D:3 * D]


def _layer_spec(shape, layer):
    nd = len(shape)
    return pl.BlockSpec((None,) + tuple(shape), lambda *_: (layer,) + (0,) * nd)


def _adaln_body(cp_ref, cs_ref, w_ref, b_ref, op_ref, os_ref):
    one = jnp.where(pl.program_id(1) > 0, 1.0, 0.0).astype(F32)
    w = w_ref[...].astype(BF16)
    op_ref[...] = _dot(cp_ref[...].astype(BF16), w) + b_ref[...] + one
    os_ref[...] = _dot(cs_ref[...].astype(BF16), w) + b_ref[...] + one


def _adaln(c_p, c_s, ada_w, ada_b):
    np_, ns_ = c_p.shape[0], c_s.shape[0]
    w = ada_w.reshape(2 * DEPTH, D, 3 * D)
    b = ada_b.reshape(2 * DEPTH, 1, 3 * D)
    return pl.pallas_call(
        _adaln_body,
        grid=(2 * DEPTH, 3),
        in_specs=[
            pl.BlockSpec((np_, D), lambda l, j: (0, 0)),
            pl.BlockSpec((ns_, D), lambda l, j: (0, 0)),
            pl.BlockSpec((None, D, D), lambda l, j: (l, 0, j)),
            pl.BlockSpec((None, 1, D), lambda l, j: (l, 0, j)),
        ],
        out_specs=[pl.BlockSpec((None, np_, D), lambda l, j: (l, 0, j)),
                   pl.BlockSpec((None, ns_, D), lambda l, j: (l, 0, j))],
        out_shape=[jax.ShapeDtypeStruct((2 * DEPTH, np_, 3 * D), F32),
                   jax.ShapeDtypeStruct((2 * DEPTH, ns_, 3 * D), F32)],
        compiler_params=_cp(("arbitrary", "arbitrary")),
        name="adaln",
    )(c_p, c_s, w, b)


def _lru_gates(xc, n, wa_ref, wx_ref, ba_ref, bx_ref, lam_ref):
    cols = slice(n * BLK, (n + 1) * BLK)
    xb = xc.astype(BF16)
    r = jax.nn.sigmoid(_dot(xb, wa_ref[n]) + ba_ref[:, cols])
    i = jax.nn.sigmoid(_dot(xb, wx_ref[n]) + bx_ref[:, cols])
    z = -lam_ref[:, cols]
    softplus = jnp.maximum(z, 0.0) + jnp.log1p(jnp.exp(-jnp.abs(z)))
    log_a = (-RG_C) * r * softplus
    a = jnp.exp(log_a)
    v = 1.0 - a * a
    b = jnp.where(v > 0.0, v * lax.rsqrt(v), 0.0) * i * xc
    return a, b


def _lru_prompt_body(x_ref, mod_ref, win_ref, cw_ref, cb_ref, wa_ref, wx_ref, ba_ref, bx_ref,
                     lam_ref, wout_ref, g_ref, b_ref, mod1_ref, wr_ref, br_ref,
                     xn_ref, hl_ref, cl_ref, urow_ref, meta_ref, cnt_ref,
                     xbuf, abuf, hbuf, hc, carry, *, tm):
    t = pl.program_id(1)

    @pl.when(t == 0)
    def _():
        xbuf[0:ROW_TILE, :] = jnp.zeros((ROW_TILE, R), F32)
        hc[...] = jnp.zeros((1, R), F32)

    x = x_ref[...]
    shift, sc1p, g1p = _split_mod(mod_ref, row=pl.program_id(0))
    u = x * sc1p + shift
    yx = _dot(u.astype(BF16), win_ref[...])
    y = jax.nn.gelu(yx[:, :R])
    xbuf[ROW_TILE:ROW_TILE + tm, :] = yx[:, R:]

    ngrp = tm // ROW_TILE
    rowmod = lax.broadcasted_iota(jnp.int32, (ngrp, ROW_TILE, BLK), 1)
    for n in range(N_BLK):
        cols = slice(n * BLK, (n + 1) * BLK)
        xc = cb_ref[:, cols] + cw_ref[3:4, cols] * xbuf[ROW_TILE:ROW_TILE + tm, cols]
        for k in range(1, CONV_W):
            xc = xc + cw_ref[3 - k:4 - k, cols] * xbuf[ROW_TILE - k:ROW_TILE - k + tm, cols]
        a, b = _lru_gates(xc, n, wa_ref, wx_ref, ba_ref, bx_ref, lam_ref)
        a = a.reshape(ngrp, ROW_TILE, BLK)
        b = b.reshape(ngrp, ROW_TILE, BLK)
        s = 1
        while s < ROW_TILE:
            a_sh = pltpu.roll(a, s, 1)
            b_sh = pltpu.roll(b, s, 1)
            m = rowmod >= s
            b = jnp.where(m, a * b_sh, 0.0) + b
            a = jnp.where(m, a * a_sh, a)
            s *= 2
        abuf[:, cols] = a.reshape(tm, BLK)
        hbuf[:, cols] = b.reshape(tm, BLK)

    def group(g, h):
        r0 = pl.multiple_of(g * ROW_TILE, ROW_TILE)
        hg = abuf[pl.ds(r0, ROW_TILE), :] * h + hbuf[pl.ds(r0, ROW_TILE), :]
        hbuf[pl.ds(r0, ROW_TILE), :] = hg
        return hg[ROW_TILE - 1:ROW_TILE, :]

    h_end = lax.fori_loop(0, tm // ROW_TILE, group, hc[...])
    hc[...] = h_end
    hl_ref[...] = h_end
    tail = xbuf[tm + ROW_TILE - (CONV_W - 1):tm + ROW_TILE, :]
    cl_ref[...] = tail
    xbuf[ROW_TILE - (CONV_W - 1):ROW_TILE, :] = tail

    mix = _dot((y * hbuf[...]).astype(BF16), wout_ref[...])
    xn = _layer_norm(ALPHA * x + g1p * mix, g_ref[...], b_ref[...])
    xn_ref[...] = xn
    b = pl.program_id(0)
    _route_rows(xn, b, (b == 0) & (t == 0), mod1_ref, wr_ref, br_ref, urow_ref, meta_ref, cnt_ref, carry)


def _const_spec(shape):
    nd = len(shape)
    return pl.BlockSpec(shape, lambda *_: (0,) * nd)


def _lru_weight_specs(layer):
    sub = 2 * layer
    return [
        _layer_spec((D, 2 * R), layer), _layer_spec((CONV_W, R), layer), _layer_spec((1, R), layer),
        _layer_spec((N_BLK, BLK, BLK), layer), _layer_spec((N_BLK, BLK, BLK), layer),
        _layer_spec((1, R), layer), _layer_spec((1, R), layer), _layer_spec((1, R), layer),
        _layer_spec((R, D), layer), _layer_spec((1, D), sub), _layer_spec((1, D), sub),
    ]


def _lru_prompt(x, mod, lw, ln_g, ln_b, wr, br, *, layer, nb, seq, tm):
    nt = seq // tm
    n = nb * seq
    r_in, r_out, r_shape, r_scratch = _route_specs(layer, nb, nt, tm)
    return pl.pallas_call(
        functools.partial(_lru_prompt_body, tm=tm),
        grid=(nb, nt),
        in_specs=[
            pl.BlockSpec((tm, D), lambda b, t: (b * nt + t, 0)),
            _layer_spec((nb, 3 * D), 2 * layer),
        ] + _lru_weight_specs(layer) + r_in,
        out_specs=[
            pl.BlockSpec((tm, D), lambda b, t: (b * nt + t, 0)),
            pl.BlockSpec((None, 1, R), lambda b, t: (b, 0, 0)),
            pl.BlockSpec((None, CONV_W - 1, R), lambda b, t: (b, 0, 0)),
        ] + r_out,
        out_shape=[
            jax.ShapeDtypeStruct((n, D), F32),
            jax.ShapeDtypeStruct((nb, 1, R), F32),
            jax.ShapeDtypeStruct((nb, CONV_W - 1, R), F32),
        ] + r_shape,
        scratch_shapes=[
            pltpu.VMEM((tm + ROW_TILE, R), F32),
            pltpu.VMEM((tm, R), F32),
            pltpu.VMEM((tm, R), F32),
            pltpu.VMEM((1, R), F32),
        ] + r_scratch,
        compiler_params=_cp(("arbitrary", "arbitrary")),
        name="lru_prompt",
    )(x, mod, *lw, ln_g, ln_b, mod, wr, br)


def _lru_sample_body(x_ref, mod_ref, h0_ref, c0_ref, win_ref, cw_ref, cb_ref, wa_ref, wx_ref,
                     ba_ref, bx_ref, lam_ref, wout_ref, g_ref, b_ref,
                     xn_ref, hl_ref, cl_ref, hbuf, *, nb, steps):
    x = x_ref[...]
    shift, sc1p, g1p = _split_mod(mod_ref, reps=steps)
    u = x * sc1p + shift
    yx = _dot(u.astype(BF16), win_ref[...])
    y = jax.nn.gelu(yx[:, :R])
    xext = jnp.concatenate([c0_ref[...], yx[:, R:]], axis=0)
    cl_ref[...] = xext[steps * nb:, :]
    for n in range(N_BLK):
        cols = slice(n * BLK, (n + 1) * BLK)
        xc = cb_ref[:, cols] + cw_ref[0:1, cols] * xext[0:steps * nb, cols]
        for k in range(1, CONV_W):
            xc = xc + cw_ref[k:k + 1, cols] * xext[k * nb:(k + steps) * nb, cols]
        a, b = _lru_gates(xc, n, wa_ref, wx_ref, ba_ref, bx_ref, lam_ref)
        h = h0_ref[:, cols]
        for t in range(steps):
            h = a[t * nb:(t + 1) * nb, :] * h + b[t * nb:(t + 1) * nb, :]
            hbuf[t * nb:(t + 1) * nb, cols] = h
        hl_ref[:, cols] = h
    mix = _dot((y * hbuf[...]).astype(BF16), wout_ref[...])
    xn_ref[...] = _layer_norm(ALPHA * x + g1p * mix, g_ref[...], b_ref[...])


def _lru_sample(x, mod, h0, c0, lw, ln_g, ln_b, *, layer, nb, steps):
    n = nb * steps
    return pl.pallas_call(
        functools.partial(_lru_sample_body, nb=nb, steps=steps),
        grid=(1,),
        in_specs=[
            _const_spec((n, D)), _layer_spec((nb, 3 * D), 2 * layer), _layer_spec((nb, R), layer),
            _const_spec(((CONV_W - 1) * nb, R)),
        ] + _lru_weight_specs(layer),
        out_specs=[_const_spec((n, D)), _const_spec((nb, R)), _const_spec(((CONV_W - 1) * nb, R))],
        out_shape=[
            jax.ShapeDtypeStruct((n, D), F32),
            jax.ShapeDtypeStruct((nb, R), F32),
            jax.ShapeDtypeStruct(((CONV_W - 1) * nb, R), F32),
        ],
        scratch_shapes=[pltpu.VMEM((n, R), F32)],
        compiler_params=_cp(("arbitrary",)),
        name="lru_sample",
    )(x, mod, h0, c0, *lw, ln_g, ln_b)


RROWS = 32
EROW = 8
_NT = (((1,), (1,)), ((), ()))
_TN = (((0,), (0,)), ((), ()))


def _top2(lt):
    nt = lt.shape[1]
    gl = lt[0:EROW, :]
    gid = lax.broadcasted_iota(jnp.int32, (EROW, nt), 0).astype(F32)
    is_g = gid < N_GROUPS
    gl = jnp.where(is_g, gl, NEG)
    gmax = jnp.max(gl, axis=0, keepdims=True)
    gidx = jnp.min(jnp.where(gl == gmax, gid, float(EROW)), axis=0, keepdims=True)
    gsum = jnp.sum(jnp.where(is_g, jnp.exp(gl - gmax), 0.0), axis=0, keepdims=True)
    g_p = 1.0 / gsum
    el = lt[EROW:EROW + N_EXP, :]
    eid = lax.broadcasted_iota(jnp.int32, (N_EXP, nt), 0).astype(F32)
    first = gidx * E_PER_G
    in_grp = (eid >= first) & (eid < first + E_PER_G)
    el = jnp.where(in_grp, el, NEG)
    big = float(N_EXP)
    t1 = jnp.max(el, axis=0, keepdims=True)
    i1 = jnp.min(jnp.where(el == t1, eid, big), axis=0, keepdims=True)
    el2 = jnp.where(eid == i1, NEG, el)
    t2 = jnp.max(el2, axis=0, keepdims=True)
    i2 = jnp.min(jnp.where((el2 == t2) & in_grp & (eid != i1), eid, big), axis=0, keepdims=True)
    e21 = jnp.exp(t2 - t1)
    w1 = g_p / (1.0 + e21)
    w2 = w1 * e21
    return eid, gidx, i1, i2, w1, w2


def _router_logits(u16, wrt_ref, brt_ref):
    return lax.dot_general(wrt_ref[...], u16, _NT, preferred_element_type=F32) + brt_ref[:, 0:1]


def _rows_to_cols(rows):
    r, nt = rows.shape
    padded = jnp.concatenate([rows, jnp.zeros((RLANES - r, nt), F32)], axis=0)
    return jnp.transpose(padded)


PAIRS = E_PER_G * (E_PER_G - 1) // 2
NCLS = N_GROUPS * PAIRS
HALF = D // 2
ROW_SHAPE = (D // LANES, LANES)
HI_MASK = -65536
FFN_TM = 256
DISPATCH_TD = 2048
COMBINE_TC = 512
COMBINE_CHUNK = 64
COMBINE_SLOTS = 3


def _route_rows(x, row, first, mod_ref, wr_ref, br_ref, urow_ref, meta_ref, cnt_ref, carry):
    @pl.when(first)
    def _():
        carry[...] = jnp.zeros(carry.shape, F32)

    tm = x.shape[0]
    shift, sc1p, _ = _split_mod(mod_ref, row=row)
    u = x * sc1p + shift
    _, gidx, i1, i2, w1, w2 = _top2(_router_logits(u.astype(BF16), wr_ref, br_ref))
    first = gidx * E_PER_G
    lo = jnp.minimum(i1, i2) - first
    hi = jnp.maximum(i1, i2) - first
    w_lo = jnp.where(i1 < i2, w1, w2)
    w_hi = jnp.where(i1 < i2, w2, w1)
    cls = gidx * PAIRS + lo * (2 * E_PER_G - 1 - lo) * 0.5 + (hi - lo - 1.0)
    cid = lax.broadcasted_iota(jnp.int32, (RROWS, tm), 0).astype(F32)
    onehot = (cid == cls).astype(F32)
    r = lax.broadcasted_iota(jnp.int32, (tm, tm), 0)
    c = lax.broadcasted_iota(jnp.int32, (tm, tm), 1)
    earlier = _dot(onehot.astype(BF16), (r < c).astype(BF16)) + carry[:, 0:1]
    rank = jnp.sum(onehot * earlier, axis=0, keepdims=True)
    carry[...] += jnp.sum(onehot, axis=1, keepdims=True)
    cnt_ref[...] = carry[...]
    meta = jnp.concatenate([w_lo, w_hi, cls, rank, jnp.zeros((ROW_TILE - 4, tm), F32)], axis=0)
    meta_ref[...] = meta
    ub = u.astype(BF16)
    lo = lax.shift_right_logical(pltpu.bitcast(ub[:, 0:HALF].astype(F32), jnp.int32), 16)
    hi = pltpu.bitcast(ub[:, HALF:D].astype(F32), jnp.int32) & HI_MASK
    ext = pltpu.bitcast(_rows_to_cols(meta), jnp.int32)
    row = jnp.concatenate([hi | lo, ext, jnp.zeros((tm, D - HALF - RLANES), jnp.int32)], axis=1)
    urow_ref[...] = row.reshape((tm,) + ROW_SHAPE)


def _route_specs(layer, nb, nt, tm):
    tile = lambda b, t: b * nt + t
    n = nb * nt * tm
    in_specs = [_layer_spec((nb, 3 * D), 2 * layer + 1),
                _layer_spec((RROWS, D), layer), _layer_spec((RROWS, RLANES), layer)]
    out_specs = [pl.BlockSpec((tm,) + ROW_SHAPE, lambda b, t: (tile(b, t), 0, 0)),
                 pl.BlockSpec((None, ROW_TILE, tm), lambda b, t: (tile(b, t), 0, 0)),
                 _const_spec((RROWS, RLANES))]
    out_shape = [jax.ShapeDtypeStruct((n,) + ROW_SHAPE, jnp.int32),
                 jax.ShapeDtypeStruct((nb * nt, ROW_TILE, tm), F32),
                 jax.ShapeDtypeStruct((RROWS, RLANES), F32)]
    return in_specs, out_specs, out_shape, [pltpu.VMEM((RROWS, RLANES), F32)]


def _row_copy(src, src_row, dst, dst_row, sem):
    return pltpu.make_async_copy(src.at[pl.ds(src_row, 1)], dst.at[pl.ds(dst_row, 1)], sem)


def _dispatch_body(pos_ref, zflag_ref, urow_ref, xs_hbm, zbuf, sem, zsem):
    base = pl.program_id(0) * DISPATCH_TD

    @pl.when(pl.program_id(0) == 0)
    def _():
        zbuf[...] = jnp.zeros(zbuf.shape, jnp.int32)

        def zero_copy(t):
            return pltpu.make_async_copy(zbuf, xs_hbm.at[pl.ds(t * FFN_TM, FFN_TM)], zsem)

        def start(t, carry):
            @pl.when(zflag_ref[t] != 0)
            def _():
                zero_copy(t).start()
            return carry

        def wait(t, carry):
            @pl.when(zflag_ref[t] != 0)
            def _():
                zero_copy(t).wait()
            return carry

        lax.fori_loop(0, zflag_ref.shape[0], start, 0)
        lax.fori_loop(0, zflag_ref.shape[0], wait, 0)

    def issue(g, carry):
        r0 = pl.multiple_of(g * ROW_TILE, ROW_TILE)
        for k in range(ROW_TILE):
            _row_copy(urow_ref.at[pl.ds(r0, ROW_TILE)], k, xs_hbm, pos_ref[base + r0 + k],
                      sem).start(priority=k % 2)
        return carry

    lax.fori_loop(0, DISPATCH_TD // ROW_TILE, issue, 0)
    pltpu.make_async_copy(urow_ref, xs_hbm.at[pl.ds(0, DISPATCH_TD)], sem).wait()


def _dispatch(pos, zflag, urow):
    n = urow.shape[0]
    nt = zflag.shape[0]
    return pl.pallas_call(
        _dispatch_body,
        grid_spec=pltpu.PrefetchScalarGridSpec(
            num_scalar_prefetch=2, grid=(n // DISPATCH_TD,),
            in_specs=[pl.BlockSpec((DISPATCH_TD,) + ROW_SHAPE, lambda i, pos, zflag: (i, 0, 0))],
            out_specs=pl.BlockSpec(memory_space=pl.ANY),
            scratch_shapes=[pltpu.VMEM((FFN_TM,) + ROW_SHAPE, jnp.int32), pltpu.SemaphoreType.DMA(()),
                            pltpu.SemaphoreType.DMA(())]),
        out_shape=jax.ShapeDtypeStruct((nt * FFN_TM,) + ROW_SHAPE, jnp.int32),
        compiler_params=_cp(("arbitrary",)),
        name="moe_dispatch",
    )(pos, zflag, urow)


def _ffn_sorted_body(ta_ref, tb_ref, na_ref, xs_ref, w1a, w3a, w2a, w1b, w3b, w2b, ys_ref):
    del ta_ref, tb_ref
    active = pl.program_id(0) < na_ref[0]

    @pl.when(active)
    def _():
        words = xs_ref[...].reshape(FFN_TM, D)
        pair = words[:, 0:HALF]
        x = jnp.concatenate([pltpu.bitcast(lax.shift_left(pair, 16), F32).astype(BF16),
                             pltpu.bitcast(pair & HI_MASK, F32).astype(BF16)], axis=1)
        ext = pltpu.bitcast(words[:, HALF:HALF + RLANES], F32)
        ha = jax.nn.silu(_dot(x, w1a[...])) * _dot(x, w3a[...]) * ext[:, 0:1]
        hb = jax.nn.silu(_dot(x, w1b[...])) * _dot(x, w3b[...]) * ext[:, 1:2]
        y = _dot(ha.astype(BF16), w2a[...]) + _dot(hb.astype(BF16), w2b[...])
        ys_ref[...] = y.reshape(FFN_TM, D // LANES, LANES)

    @pl.when(jnp.logical_not(active))
    def _():
        ys_ref[...] = jnp.zeros(ys_ref.shape, F32)


def _ffn_sorted(ta, tb, na, xs, w1, w3, w2):
    nt = ta.shape[0]
    ea = lambda t, ta, tb, na: (ta[t], 0, 0)
    eb = lambda t, ta, tb, na: (tb[t], 0, 0)
    rows = lambda t, ta, tb, na: (t, 0)
    up, down = (None, D, F), (None, F, D)
    return pl.pallas_call(
        _ffn_sorted_body,
        grid_spec=pltpu.PrefetchScalarGridSpec(
            num_scalar_prefetch=3, grid=(nt,),
            in_specs=[
                pl.BlockSpec((FFN_TM,) + ROW_SHAPE, lambda t, ta, tb, na: (t, 0, 0)),
                pl.BlockSpec(up, ea), pl.BlockSpec(up, ea), pl.BlockSpec(down, ea),
                pl.BlockSpec(up, eb), pl.BlockSpec(up, eb), pl.BlockSpec(down, eb),
            ],
            out_specs=pl.BlockSpec((FFN_TM, D // LANES, LANES), lambda t, ta, tb, na: (t, 0, 0))),
        out_shape=jax.ShapeDtypeStruct((nt * FFN_TM, D // LANES, LANES), F32),
        compiler_params=_cp(("arbitrary",)),
        name="moe_ffn",
    )(ta, tb, na, xs, w1, w3, w2, w1, w3, w2)


def _combine_body(pos_ref, x_ref, mod_ref, ys_hbm, g_ref, b_ref, *rest, tiles_per_seq, with_kv):
    if with_kv:
        kvw_ref, kvwt_ref, xn_ref, *kv_out, buf, sem = rest
    else:
        xn_ref, buf, sem = rest
    i = pl.program_id(0)
    nsteps = pl.num_programs(0)
    tc = x_ref.shape[0]

    def issue_group(step, slot, r0):
        group = buf.at[slot, pl.ds(r0, ROW_TILE)]
        for k in range(ROW_TILE):
            _row_copy(ys_hbm, pos_ref[step * tc + r0 + k], group, k, sem.at[slot]).start(priority=k % 2)

    def wait_tile(slot):
        pltpu.make_async_copy(ys_hbm.at[pl.ds(0, tc)], buf.at[slot], sem.at[slot]).wait()

    ahead = COMBINE_SLOTS - 1

    @pl.when(i == 0)
    def _():
        for t in range(ahead):
            def prime(g, carry, t=t):
                issue_group(t, t, pl.multiple_of(g * ROW_TILE, ROW_TILE))
                return carry
            lax.fori_loop(0, tc // ROW_TILE, prime, 0)

    slot = i % COMBINE_SLOTS
    wait_tile(slot)
    _, _, g1p = _split_mod(mod_ref, row=i // tiles_per_seq)
    gain, bias = g_ref[...], b_ref[...]
    nxt = jnp.minimum(i + ahead, nsteps - 1)
    nslot = (i + ahead) % COMBINE_SLOTS

    for c in range(tc // COMBINE_CHUNK):
        for g in range(COMBINE_CHUNK // ROW_TILE):
            issue_group(nxt, nslot, c * COMBINE_CHUNK + g * ROW_TILE)
        rows = pl.ds(c * COMBINE_CHUNK, COMBINE_CHUNK)
        z = ALPHA * x_ref[rows, :] + g1p * buf[slot, rows].reshape(COMBINE_CHUNK, D)
        xn_ref[rows, :] = _layer_norm(z, gain, bias)

    if with_kv:
        _kv_rows(xn_ref[...].astype(BF16), kvw_ref, kvwt_ref, *kv_out)

    @pl.when(i == nsteps - 1)
    def _():
        for k in range(1, COMBINE_SLOTS):
            wait_tile((i + k) % COMBINE_SLOTS)


def _combine(pos, x, mod, ys, ln_g, ln_b, kv_weights=None, *, layer, seq):
    n = x.shape[0]
    tc = COMBINE_TC
    sub = 2 * layer + 1
    row = lambda i, pos: (i, 0)
    in_specs = [
        pl.BlockSpec((tc, D), row),
        _layer_spec((n // seq, 3 * D), sub),
        pl.BlockSpec(memory_space=pl.ANY),
        _layer_spec((1, D), sub), _layer_spec((1, D), sub),
    ]
    out_specs = [pl.BlockSpec((tc, D), row)]
    out_shape = [jax.ShapeDtypeStruct((n, D), F32)]
    args = [pos, x, mod, ys, ln_g, ln_b]
    if kv_weights is not None:
        in_specs += [pl.BlockSpec((D, 2 * KVW), lambda i, pos: (0, 0)),
                     pl.BlockSpec((KVW, D), lambda i, pos: (0, 0))]
        out_specs += [pl.BlockSpec((tc, KVW), row), pl.BlockSpec((tc, KVW), row),
                      pl.BlockSpec((tc, KVW), row), pl.BlockSpec((KVW, tc), lambda i, pos: (0, i))]
        out_shape += [jax.ShapeDtypeStruct((n, KVW), F32), jax.ShapeDtypeStruct((n, KVW), F32),
                      jax.ShapeDtypeStruct((n, KVW), BF16), jax.ShapeDtypeStruct((KVW, n), BF16)]
        args += list(kv_weights)
    return pl.pallas_call(
        functools.partial(_combine_body, tiles_per_seq=seq // tc, with_kv=kv_weights is not None),
        grid_spec=pltpu.PrefetchScalarGridSpec(
            num_scalar_prefetch=1, grid=(n // tc,),
            in_specs=in_specs, out_specs=out_specs,
            scratch_shapes=[pltpu.VMEM((COMBINE_SLOTS, tc, D // LANES, LANES), F32),
                            pltpu.SemaphoreType.DMA((COMBINE_SLOTS,))]),
        out_shape=out_shape,
        compiler_params=_cp(("arbitrary",)),
        name="moe_combine",
    )(*args)


def _moe_sparse(x, route, mod, w1, w3, w2, ln_g, ln_b, kv_weights=None, *, layer, seq):
    n = x.shape[0]
    urow, meta, cnt = route
    cls = meta[:, 2, :].reshape(n).astype(jnp.int32)
    rank = meta[:, 3, :].reshape(n).astype(jnp.int32)
    count = cnt[:NCLS, 0].astype(jnp.int32)
    ntile = (count + FFN_TM - 1) // FFN_TM
    tile_end = jnp.cumsum(ntile)
    row_start = (tile_end - ntile) * FFN_TM
    pos = rank + jnp.sum(jax.nn.one_hot(cls, NCLS, dtype=jnp.int32) * row_start[None, :], axis=1)
    nt = n // FFN_TM + NCLS
    total = tile_end[NCLS - 1]
    tr = jnp.minimum(jnp.arange(nt, dtype=jnp.int32), total - 1)
    tcls = jnp.sum((tr[:, None] >= tile_end[None, :]).astype(jnp.int32), axis=1)
    grp, pair = tcls // PAIRS, tcls % PAIRS
    lo = (pair >= 3).astype(jnp.int32) + (pair >= 5).astype(jnp.int32)
    hi = pair + 1 - lo * (2 * E_PER_G - 3 - lo) // 2
    ta = grp * E_PER_G + lo
    tb = grp * E_PER_G + hi
    t_all = jnp.arange(nt, dtype=jnp.int32)
    partial_last = ((t_all[:, None] == tile_end[None, :] - 1) & (count % FFN_TM != 0)[None, :]).any(axis=1)
    zflag = ((t_all >= total) | partial_last).astype(jnp.int32)
    xs = _dispatch(pos, zflag, urow)
    ys = _ffn_sorted(ta, tb, total[None], xs, w1, w3, w2)
    return _combine(pos, x, mod, ys, ln_g, ln_b, kv_weights, layer=layer, seq=seq)


def _moe_body(x_ref, mod_ref, wr_ref, br_ref, w1f_ref, w3f_ref, w2f_ref, g_ref, b_ref,
              xn_ref, w1_ref, w3_ref, w2_ref, ub, comb, acc, *, reps):
    e = pl.program_id(1)
    w1_ref[...] = w1f_ref[...].astype(BF16)
    w3_ref[...] = w3f_ref[...].astype(BF16)
    w2_ref[...] = w2f_ref[...].astype(BF16)

    @pl.when(e == 0)
    def _():
        shift, sc1p, _ = _split_mod(mod_ref, reps=reps)
        u = (x_ref[...] * sc1p + shift).astype(BF16)
        ub[...] = u
        eid, _, i1, i2, w1, w2 = _top2(_router_logits(u, wr_ref, br_ref))
        comb[...] = _rows_to_cols(jnp.where(eid == i1, w1, 0.0) + jnp.where(eid == i2, w2, 0.0))
        acc[...] = jnp.zeros(acc.shape, F32)

    u = ub[...]
    col = lax.broadcasted_iota(jnp.int32, comb.shape, 1)
    ce = jnp.sum(jnp.where(col == e, comb[...], 0.0), axis=-1, keepdims=True)
    h = jax.nn.silu(_dot(u, w1_ref[...])) * _dot(u, w3_ref[...]) * ce
    acc[...] += _dot(h.astype(BF16), w2_ref[...])

    @pl.when(e == N_EXP - 1)
    def _():
        _, _, g1p = _split_mod(mod_ref, reps=reps)
        xn_ref[...] = _layer_norm(ALPHA * x_ref[...] + g1p * acc[...], g_ref[...], b_ref[...])


def _moe_dense(x, mod, wr, br, w1, w3, w2, ln_g, ln_b, *, layer, reps):
    n = x.shape[0]
    sub = 2 * layer + 1
    expert = lambda i, e: (layer, e, 0, 0)
    return pl.pallas_call(
        functools.partial(_moe_body, reps=reps),
        grid=(1, N_EXP),
        in_specs=[
            pl.BlockSpec((n, D), lambda i, e: (0, 0)),
            _layer_spec((n // reps, 3 * D), sub),
            _layer_spec((RROWS, D), layer), _layer_spec((RROWS, RLANES), layer),
            pl.BlockSpec((None, None, D, F), expert),
            pl.BlockSpec((None, None, D, F), expert),
            pl.BlockSpec((None, None, F, D), expert),
            _layer_spec((1, D), sub), _layer_spec((1, D), sub),
        ],
        out_specs=[pl.BlockSpec((n, D), lambda i, e: (0, 0)),
                   pl.BlockSpec((None, D, F), lambda i, e: (e, 0, 0)),
                   pl.BlockSpec((None, D, F), lambda i, e: (e, 0, 0)),
                   pl.BlockSpec((None, F, D), lambda i, e: (e, 0, 0))],
        out_shape=[jax.ShapeDtypeStruct((n, D), F32),
                   jax.ShapeDtypeStruct((N_EXP, D, F), BF16),
                   jax.ShapeDtypeStruct((N_EXP, D, F), BF16),
                   jax.ShapeDtypeStruct((N_EXP, F, D), BF16)],
        scratch_shapes=[
            pltpu.VMEM((n, D), BF16),
            pltpu.VMEM((n, RLANES), F32),
            pltpu.VMEM((n, D), F32),
        ],
        compiler_params=_cp(("arbitrary", "arbitrary")),
        name="moe",
    )(x, mod, wr, br, w1, w3, w2, ln_g, ln_b)


def _kv_body(x_ref, w_ref, wvt_ref, k_ref, v_ref, k16_ref, vt16_ref):
    _kv_rows(x_ref[...].astype(BF16), w_ref, wvt_ref, k_ref, v_ref, k16_ref, vt16_ref)


def _kv_rows(xb, w_ref, wvt_ref, k_ref, v_ref, k16_ref, vt16_ref):
    kv = _dot(xb, w_ref[...])
    k_ref[...] = kv[:, :KVW]
    v_ref[...] = kv[:, KVW:]
    k16_ref[...] = kv[:, :KVW].astype(BF16)
    vt16_ref[...] = lax.dot_general(wvt_ref[...], xb, _NT, preferred_element_type=F32).astype(BF16)


def _kv(x, w, wvt, *, tm):
    n = x.shape[0]
    row = lambda i: (i, 0)
    return pl.pallas_call(
        _kv_body,
        grid=(n // tm,),
        in_specs=[pl.BlockSpec((tm, D), row), _const_spec((D, 2 * KVW)), _const_spec((KVW, D))],
        out_specs=[pl.BlockSpec((tm, KVW), row), pl.BlockSpec((tm, KVW), row),
                   pl.BlockSpec((tm, KVW), row), pl.BlockSpec((KVW, tm), lambda i: (0, i))],
        out_shape=[jax.ShapeDtypeStruct((n, KVW), F32), jax.ShapeDtypeStruct((n, KVW), F32),
                   jax.ShapeDtypeStruct((n, KVW), BF16), jax.ShapeDtypeStruct((KVW, n), BF16)],
        compiler_params=_cp(("arbitrary",)),
        name="kv_proj",
    )(x, w, wvt)


def _sink_attend(qh, kh, vh, bias, sink):
    s = lax.dot_general(qh, kh, (((1,), (1,)), ((), ())), preferred_element_type=F32) + bias
    m = jnp.maximum(jnp.max(s, axis=-1, keepdims=True), sink)
    p = jnp.exp(s - m)
    den = jnp.sum(p, axis=-1, keepdims=True) + jnp.exp(sink - m)
    return _dot(p.astype(BF16), vh) / den


def _attn_prompt_body(x_ref, mod_ref, wqt_ref, wo_ref, kc_ref, kp_ref, vtc_ref, vtp_ref,
                      bias_ref, sink_ref, g_ref, b_ref, mod1_ref, wr_ref, br_ref,
                      xn_ref, urow_ref, meta_ref, cnt_ref, carry, *, tq):
    t = pl.program_id(1)
    x = x_ref[...]
    shift, sc1p, g1p = _split_mod(mod_ref, row=pl.program_id(0))
    u = (x * sc1p + shift).astype(BF16)
    qt = lax.dot_general(wqt_ref[...], u, _NT, preferred_element_type=F32)
    qt = (qt * (HD ** -0.5 * LOG2E)).astype(BF16)
    kc = kc_ref[...]
    vtc = vtc_ref[...]
    krow = lax.broadcasted_iota(jnp.int32, (2 * WIN, GRP * WIN), 0)
    first = jnp.where((krow < WIN) & (t == 0), NEG, 0.0).astype(F32)
    nwin = tq // WIN
    kks, vvts = [], []
    for j in range(nwin):
        if j == 0:
            kks.append(jnp.concatenate([kp_ref[...], kc[0:WIN]], axis=0))
            vvts.append(jnp.concatenate([vtp_ref[...], vtc[:, 0:WIN]], axis=1))
        else:
            kks.append(kc[(j - 1) * WIN:(j + 1) * WIN])
            vvts.append(vtc[:, (j - 1) * WIN:(j + 1) * WIN])

    def scores(j, hk):
        qht = jnp.concatenate(
            [qt[(hk * GRP + g) * HD:(hk * GRP + g + 1) * HD, j * WIN:(j + 1) * WIN] for g in range(GRP)],
            axis=1)
        st = _dot(kks[j][:, hk * HD:(hk + 1) * HD], qht) + bias_ref[hk]
        return st + first if j == 0 else st

    items = [(j, hk) for j in range(nwin) for hk in range(HKV)]
    heads = [[] for _ in range(nwin)]
    st_next = scores(*items[0])
    for n, (j, hk) in enumerate(items):
        st = st_next
        if n + 1 < len(items):
            st_next = scores(*items[n + 1])
        sink = sink_ref[hk]
        m = jnp.maximum(jnp.max(st, axis=0, keepdims=True), sink)
        pt = jnp.exp2(st - m)
        den = jnp.sum(pt, axis=0, keepdims=True) + jnp.exp2(sink - m)
        ot = _dot(vvts[j][hk * HD:(hk + 1) * HD, :], pt.astype(BF16)) * (1.0 / den)
        for g in range(GRP):
            heads[j].append(ot[:, g * WIN:(g + 1) * WIN])
    blocks = [jnp.concatenate(h, axis=0) for h in heads]
    oallt = jnp.concatenate(blocks, axis=1) if len(blocks) > 1 else blocks[0]
    mix = lax.dot_general(oallt.astype(BF16), wo_ref[...], _TN, preferred_element_type=F32)
    xn = _layer_norm(ALPHA * x + g1p * mix, g_ref[...], b_ref[...])
    xn_ref[...] = xn
    b = pl.program_id(0)
    _route_rows(xn, b, (b == 0) & (t == 0), mod1_ref, wr_ref, br_ref, urow_ref, meta_ref, cnt_ref, carry)


def _attn_prompt(x, mod, wqt, wo, k16, vt16, bias, sink, ln_g, ln_b, wr, br, *, layer, nb, seq, tq):
    nt = seq // tq
    r_in, r_out, r_shape, r_scratch = _route_specs(layer, nb, nt, tq)
    wpt = tq // WIN
    n = nb * seq
    j = layer - N_A
    sub = 2 * layer

    def prev(b, t):
        return jnp.maximum((b * nt + t) * wpt - 1, 0)

    cur = lambda b, t: (b * nt + t, 0)
    return pl.pallas_call(
        functools.partial(_attn_prompt_body, tq=tq),
        grid=(nb, nt),
        in_specs=[
            pl.BlockSpec((tq, D), cur),
            _layer_spec((nb, 3 * D), sub),
            _layer_spec((HQ * HD, D), j), _layer_spec((HQ * HD, D), j),
            pl.BlockSpec((tq, KVW), cur), pl.BlockSpec((WIN, KVW), lambda b, t: (prev(b, t), 0)),
            pl.BlockSpec((KVW, tq), lambda b, t: (0, b * nt + t)),
            pl.BlockSpec((KVW, WIN), lambda b, t: (0, prev(b, t))),
            _const_spec((HKV, 2 * WIN, GRP * WIN)), _layer_spec((HKV, 1, GRP * WIN), j),
            _layer_spec((1, D), sub), _layer_spec((1, D), sub),
        ] + r_in,
        out_specs=[pl.BlockSpec((tq, D), cur)] + r_out,
        out_shape=[jax.ShapeDtypeStruct((n, D), F32)] + r_shape,
        scratch_shapes=r_scratch,
        compiler_params=_cp(("arbitrary", "arbitrary")),
        name="attn_prompt",
    )(x, mod, wqt, wo, k16, k16, vt16, vt16, bias, sink, ln_g, ln_b, mod, wr, br)


def _attn_sample_body(x_ref, mod_ref, wq_ref, wo_ref, kb_ref, vb_ref, kn_ref, vn_ref,
                      bias_ref, sink_ref, g_ref, b_ref, xn_ref, *, ns, steps):
    nr = ns * steps
    x = x_ref[...].reshape(nr, D)
    shift, sc1p, g1p = _split_mod(mod_ref, reps=steps)
    u = (x * sc1p + shift).astype(BF16)
    q = (_dot(u, wq_ref[...]) * (HD ** -0.5)).astype(BF16)
    nbuf = kb_ref.shape[1]
    k_all = jnp.concatenate([kb_ref[...].reshape(ns * nbuf, KVW), kn_ref[...].reshape(nr, KVW)],
                            axis=0).astype(BF16)
    v_all = jnp.concatenate([vb_ref[...].reshape(ns * nbuf, KVW), vn_ref[...].reshape(nr, KVW)],
                            axis=0).astype(BF16)
    heads = [None] * HQ
    for hk in range(HKV):
        kcols = slice(hk * HD, (hk + 1) * HD)
        qh = jnp.concatenate(
            [q[:, (hk * GRP + g) * HD:(hk * GRP + g + 1) * HD] for g in range(GRP)], axis=0)
        o = _sink_attend(qh, k_all[:, kcols], v_all[:, kcols], bias_ref[hk], sink_ref[hk])
        for g in range(GRP):
            heads[hk * GRP + g] = o[g * nr:(g + 1) * nr]
    o_all = jnp.concatenate(heads, axis=1)
    mix = _dot(o_all.astype(BF16), wo_ref[...])
    xn_ref[...] = _layer_norm(ALPHA * x + g1p * mix, g_ref[...], b_ref[...]).reshape(steps, ns, D)


def _attn_sample(x, mod, wq, wo, kbuf, vbuf, kn, vn, bias, sink, ln_g, ln_b, *, layer, ns):
    steps, nb, _ = x.shape
    nbuf = kbuf.shape[1]
    j = layer - N_A
    sub = 2 * layer
    seqs = lambda i: (0, i, 0)
    return pl.pallas_call(
        functools.partial(_attn_sample_body, ns=ns, steps=steps),
        grid=(nb // ns,),
        in_specs=[
            pl.BlockSpec((steps, ns, D), seqs),
            pl.BlockSpec((None, ns, 3 * D), lambda i: (sub, i, 0)),
            _layer_spec((D, HQ * HD), j), _layer_spec((HQ * HD, D), j),
            pl.BlockSpec((ns, nbuf, KVW), lambda i: (i, 0, 0)),
            pl.BlockSpec((ns, nbuf, KVW), lambda i: (i, 0, 0)),
            pl.BlockSpec((steps, ns, KVW), seqs), pl.BlockSpec((steps, ns, KVW), seqs),
            _const_spec(bias.shape), _layer_spec(sink.shape[1:], j),
            _layer_spec((1, D), sub), _layer_spec((1, D), sub),
        ],
        out_specs=pl.BlockSpec((steps, ns, D), seqs),
        out_shape=jax.ShapeDtypeStruct((steps, nb, D), F32),
        compiler_params=_cp(("arbitrary",)),
        name="attn_sample",
    )(x, mod, wq, wo, kbuf, vbuf, kn, vn, bias, sink, ln_g, ln_b)


def _t5_bucket(d):
    max_exact = N_BUCKETS // 2
    d = np.maximum(d, 0)
    log_ratio = (np.log(np.maximum(d, 1).astype(np.float32) / np.float32(max_exact))
                 / np.float32(math.log(WIN / max_exact)))
    large = max_exact + (log_ratio * (N_BUCKETS - max_exact)).astype(np.int32)
    large = np.minimum(large, N_BUCKETS - 1)
    return np.where(d < max_exact, d, large)


def _bias_table(d, ok, rel_bias):
    sel = np.where(ok, _t5_bucket(d), N_BUCKETS)
    onehot = (sel[..., None] == np.arange(N_BUCKETS + 1)).astype(np.float32)
    tab = jnp.concatenate([rel_bias.astype(F32), jnp.full((1, HQ), NEG, F32)], axis=0)
    return jnp.einsum('abd,dh->hab', jnp.asarray(onehot), tab, precision=lax.Precision.HIGHEST)


def _prompt_bias(rel_bias):
    d = np.arange(WIN)[None, :] + WIN - np.arange(2 * WIN)[:, None]
    b = _bias_table(d, (d >= 0) & (d < WIN), rel_bias).reshape(HKV, GRP, 2 * WIN, WIN)
    return jnp.transpose(b, (0, 2, 1, 3)).reshape(HKV, 2 * WIN, GRP * WIN)


def _sample_bias(rel_bias, ns, steps, nbuf):
    qs = np.tile(np.arange(ns), steps)
    qt = np.repeat(np.arange(steps), ns)
    ks = np.concatenate([np.repeat(np.arange(ns), nbuf), qs])
    kpos = np.concatenate([np.tile(np.arange(nbuf), ns), nbuf + qt])
    d = qt[:, None] + nbuf - kpos[None, :]
    ok = (d >= 0) & (d < WIN) & (qs[:, None] == ks[None, :])
    return _bias_table(d, ok, rel_bias).reshape(HKV, GRP * ns * steps, kpos.shape[0])


LRU_TM = 512
ATTN_TQ = 1024
SAMPLE_NS = 8


def kernel(x_prompt, x_sample, state_rnn_h, state_rnn_conv, cache_win_k, cache_win_v, c_prompt, c_sample, ada_w, ada_b, ln_g, ln_b, lru_w_in, lru_conv_w, lru_conv_b, lru_w_a, lru_b_a, lru_w_x, lru_b_x, lru_lambda, lru_w_out, kv_w, attn_w_q, attn_sinks, attn_w_o, rel_bias, moe_w_group, moe_b_group, moe_w_router, moe_b_router, moe_w1, moe_w3, moe_w2):
    nbp, seq, _ = x_prompt.shape
    nbs, steps, _ = x_sample.shape
    nbuf = cache_win_k.shape[1]
    npt = nbp * seq
    nst = nbs * steps

    mod_p, mod_s = _adaln(c_prompt, c_sample, ada_w, ada_b)

    ln_g3 = ln_g.reshape(2 * DEPTH, 1, D)
    ln_b3 = ln_b.reshape(2 * DEPTH, 1, D)
    lw = (lru_w_in.astype(BF16), lru_conv_w, lru_conv_b[:, None], lru_w_a.astype(BF16),
          lru_w_x.astype(BF16), lru_b_a[:, None], lru_b_x[:, None], lru_lambda[:, None],
          lru_w_out.astype(BF16))
    zrow = lambda k: jnp.zeros((DEPTH, k, D), F32)
    wr = jnp.concatenate([jnp.swapaxes(moe_w_group, 1, 2), zrow(EROW - N_GROUPS),
                          jnp.swapaxes(moe_w_router.reshape(DEPTH, D, N_EXP), 1, 2),
                          zrow(RROWS - EROW - N_EXP)], axis=1).astype(BF16)
    zb = lambda k: jnp.zeros((DEPTH, k), F32)
    br = jnp.concatenate([moe_b_group, zb(EROW - N_GROUPS), moe_b_router.reshape(DEPTH, N_EXP),
                          zb(RROWS - EROW - N_EXP)], axis=1)
    br = jnp.broadcast_to(br[:, :, None], (DEPTH, RROWS, RLANES))
    wq = attn_w_q.astype(BF16)
    wqt = jnp.swapaxes(wq, 1, 2)
    wo = attn_w_o.astype(BF16)
    kvw = kv_w.astype(BF16)
    kvwt = kvw[:, KVW:].T
    bias_p = _prompt_bias(rel_bias) * LOG2E
    bias_s = _sample_bias(rel_bias, SAMPLE_NS, steps, nbuf)
    sink_p = jnp.repeat(attn_sinks.reshape(N_B, HKV, GRP), WIN, axis=2)[:, :, None, :] * LOG2E
    sink_s = jnp.repeat(attn_sinks.reshape(N_B, HKV, GRP), SAMPLE_NS * steps, axis=2)[..., None]
    kbuf = cache_win_k.reshape(nbs, nbuf, KVW)
    vbuf = cache_win_v.reshape(nbs, nbuf, KVW)

    xp = x_prompt.reshape(npt, D)
    xs = jnp.transpose(x_sample, (1, 0, 2)).reshape(nst, D)
    c0 = jnp.transpose(state_rnn_conv, (0, 2, 1, 3)).reshape(N_A, (CONV_W - 1) * nbs, R)
    hp, cp_, hs, cs = [], [], [], []
    kp = vp = kn = vn = kp16 = vpt16 = None
    for l in range(DEPTH):
        if l < N_A:
            xp, h_l, c_l, *route = _lru_prompt(xp, mod_p, lw, ln_g3, ln_b3, wr, br,
                                               layer=l, nb=nbp, seq=seq, tm=LRU_TM)
            hp.append(h_l[:, 0])
            cp_.append(c_l)
            xs, h_l, c_l = _lru_sample(xs, mod_s, state_rnn_h, c0[l], lw, ln_g3, ln_b3,
                                       layer=l, nb=nbs, steps=steps)
            hs.append(h_l)
            cs.append(jnp.transpose(c_l.reshape(CONV_W - 1, nbs, R), (1, 0, 2)))
        else:
            xp, *route = _attn_prompt(xp, mod_p, wqt, wo, kp16, vpt16, bias_p, sink_p, ln_g3, ln_b3, wr, br,
                                      layer=l, nb=nbp, seq=seq, tq=ATTN_TQ)
            xs = _attn_sample(xs.reshape(steps, nbs, D), mod_s, wq, wo, kbuf, vbuf,
                              kn.reshape(steps, nbs, KVW), vn.reshape(steps, nbs, KVW),
                              bias_s, sink_s, ln_g3, ln_b3, layer=l, ns=SAMPLE_NS).reshape(nst, D)
        xs, w1, w3, w2 = _moe_dense(xs, mod_s, wr, br, moe_w1, moe_w3, moe_w2, ln_g3, ln_b3,
                                    layer=l, reps=steps)
        if l == N_A - 1:
            xp, kp, vp, kp16, vpt16 = _moe_sparse(xp, route, mod_p, w1, w3, w2, ln_g3, ln_b3, (kvw, kvwt),
                                                  layer=l, seq=seq)
            kn, vn, _, _ = _kv(xs, kvw, kvwt, tm=nst)
        else:
            xp, = _moe_sparse(xp, route, mod_p, w1, w3, w2, ln_g3, ln_b3, layer=l, seq=seq)

    lp = min(WIN, seq)
    k_p = kp.reshape(nbp, seq, HKV, HD)[:, seq - lp:]
    v_p = vp.reshape(nbp, seq, HKV, HD)[:, seq - lp:]
    kn4 = jnp.transpose(kn.reshape(steps, nbs, HKV, HD), (1, 0, 2, 3))
    vn4 = jnp.transpose(vn.reshape(steps, nbs, HKV, HD), (1, 0, 2, 3))
    k_s = jnp.concatenate([cache_win_k, kn4], axis=1)[:, -nbuf:]
    v_s = jnp.concatenate([cache_win_v, vn4], axis=1)[:, -nbuf:]
    y_s = jnp.transpose(xs.reshape(steps, nbs, D), (1, 0, 2))
    return (xp.reshape(nbp, seq, D), y_s,
            jnp.stack(hp), jnp.stack(cp_), k_p, v_p,
            jnp.stack(hs), jnp.stack(cs), k_s, v_s)
```

```python
import functools
import math

import jax
import jax.numpy as jnp
import numpy as np
from jax import lax
from jax.experimental import pallas as pl
from jax.experimental.pallas import tpu as pltpu

D = 1024
R = 1024
DEPTH = 4
N_A = 2
N_B = DEPTH - N_A
N_BLK = 4
BLK = R // N_BLK
CONV_W = 4
RG_C = 8.0
HQ = 16
HKV = 4
HD = 64
GRP = HQ // HKV
KVW = HKV * HD
WIN = 128
N_BUCKETS = 32
N_GROUPS = 4
E_PER_G = 4
N_EXP = 16
F = 512
ALPHA = (2.0 * DEPTH) ** 0.25
LN_EPS = 1e-5
NEG = -1e30
LOG2E = math.log2(math.e)
LANES = 128
RLANES = LANES
ROW_TILE = 8

F32 = jnp.float32
BF16 = jnp.bfloat16

VMEM_LIMIT = 56 * 1024 * 1024


def _cp(sem):
    return pltpu.CompilerParams(dimension_semantics=sem, vmem_limit_bytes=VMEM_LIMIT)


def _dot(a, b):
    return jnp.dot(a, b, preferred_element_type=F32)


def _layer_norm(z, g, b):
    mu = jnp.mean(z, axis=-1, keepdims=True)
    zc = z - mu
    var = jnp.mean(zc * zc, axis=-1, keepdims=True)
    return zc * lax.rsqrt(var + LN_EPS) * g + b


def _split_mod(mod_ref, row=None, reps=1):
    m = mod_ref[...] if row is None else mod_ref[pl.ds(row, 1), :]
    if reps > 1:
        m = jnp.concatenate([m] * reps, axis=0)
    return m[:, 0:D], m[:, D:2 * D], m[:, 2 * D:3 * D]


def _layer_spec(shape, layer):
    nd = len(shape)
    return pl.BlockSpec((None,) + tuple(shape), lambda *_: (layer,) + (0,) * nd)


def _adaln_body(cp_ref, cs_ref, w_ref, b_ref, op_ref, os_ref):
    one = jnp.where(pl.program_id(1) > 0, 1.0, 0.0).astype(F32)
    w = w_ref[...].astype(BF16)
    op_ref[...] = _dot(cp_ref[...].astype(BF16), w) + b_ref[...] + one
    os_ref[...] = _dot(cs_ref[...].astype(BF16), w) + b_ref[...] + one


def _adaln(c_p, c_s, ada_w, ada_b):
    np_, ns_ = c_p.shape[0], c_s.shape[0]
    w = ada_w.reshape(2 * DEPTH, D, 3 * D)
    b = ada_b.reshape(2 * DEPTH, 1, 3 * D)
    return pl.pallas_call(
        _adaln_body,
        grid=(2 * DEPTH, 3),
        in_specs=[
            pl.BlockSpec((np_, D), lambda l, j: (0, 0)),
            pl.BlockSpec((ns_, D), lambda l, j: (0, 0)),
            pl.BlockSpec((None, D, D), lambda l, j: (l, 0, j)),
            pl.BlockSpec((None, 1, D), lambda l, j: (l, 0, j)),
        ],
        out_specs=[pl.BlockSpec((None, np_, D), lambda l, j: (l, 0, j)),
                   pl.BlockSpec((None, ns_, D), lambda l, j: (l, 0, j))],
        out_shape=[jax.ShapeDtypeStruct((2 * DEPTH, np_, 3 * D), F32),
                   jax.ShapeDtypeStruct((2 * DEPTH, ns_, 3 * D), F32)],
        compiler_params=_cp(("arbitrary", "arbitrary")),
        name="adaln",
    )(c_p, c_s, w, b)


def _lru_gates(xc, n, wa_ref, wx_ref, ba_ref, bx_ref, lam_ref):
    cols = slice(n * BLK, (n + 1) * BLK)
    xb = xc.astype(BF16)
    r = jax.nn.sigmoid(_dot(xb, wa_ref[n]) + ba_ref[:, cols])
    i = jax.nn.sigmoid(_dot(xb, wx_ref[n]) + bx_ref[:, cols])
    z = -lam_ref[:, cols]
    softplus = jnp.maximum(z, 0.0) + jnp.log1p(jnp.exp(-jnp.abs(z)))
    log_a = (-RG_C) * r * softplus
    a = jnp.exp(log_a)
    v = 1.0 - a * a
    b = jnp.where(v > 0.0, v * lax.rsqrt(v), 0.0) * i * xc
    return a, b


def _lru_prompt_body(x_ref, mod_ref, win_ref, cw_ref, cb_ref, wa_ref, wx_ref, ba_ref, bx_ref,
                     lam_ref, wout_ref, g_ref, b_ref, mod1_ref, wr_ref, br_ref,
                     xn_ref, hl_ref, cl_ref, urow_ref, meta_ref, cnt_ref,
                     xbuf, abuf, hbuf, hc, carry, *, tm):
    t = pl.program_id(1)

    @pl.when(t == 0)
    def _():
        xbuf[0:ROW_TILE, :] = jnp.zeros((ROW_TILE, R), F32)
        hc[...] = jnp.zeros((1, R), F32)

    x = x_ref[...]
    shift, sc1p, g1p = _split_mod(mod_ref, row=pl.program_id(0))
    u = x * sc1p + shift
    yx = _dot(u.astype(BF16), win_ref[...])
    y = jax.nn.gelu(yx[:, :R])
    xbuf[ROW_TILE:ROW_TILE + tm, :] = yx[:, R:]

    ngrp = tm // ROW_TILE
    rowmod = lax.broadcasted_iota(jnp.int32, (ngrp, ROW_TILE, BLK), 1)
    for n in range(N_BLK):
        cols = slice(n * BLK, (n + 1) * BLK)
        xc = cb_ref[:, cols] + cw_ref[3:4, cols] * xbuf[ROW_TILE:ROW_TILE + tm, cols]
        for k in range(1, CONV_W):
            xc = xc + cw_ref[3 - k:4 - k, cols] * xbuf[ROW_TILE - k:ROW_TILE - k + tm, cols]
        a, b = _lru_gates(xc, n, wa_ref, wx_ref, ba_ref, bx_ref, lam_ref)
        a = a.reshape(ngrp, ROW_TILE, BLK)
        b = b.reshape(ngrp, ROW_TILE, BLK)
        s = 1
        while s < ROW_TILE:
            a_sh = pltpu.roll(a, s, 1)
            b_sh = pltpu.roll(b, s, 1)
            m = rowmod >= s
            b = jnp.where(m, a * b_sh, 0.0) + b
            a = jnp.where(m, a * a_sh, a)
            s *= 2
        abuf[:, cols] = a.reshape(tm, BLK)
        hbuf[:, cols] = b.reshape(tm, BLK)

    def group(g, h):
        r0 = pl.multiple_of(g * ROW_TILE, ROW_TILE)
        hg = abuf[pl.ds(r0, ROW_TILE), :] * h + hbuf[pl.ds(r0, ROW_TILE), :]
        hbuf[pl.ds(r0, ROW_TILE), :] = hg
        return hg[ROW_TILE - 1:ROW_TILE, :]

    h_end = lax.fori_loop(0, tm // ROW_TILE, group, hc[...])
    hc[...] = h_end
    hl_ref[...] = h_end
    tail = xbuf[tm + ROW_TILE - (CONV_W - 1):tm + ROW_TILE, :]
    cl_ref[...] = tail
    xbuf[ROW_TILE - (CONV_W - 1):ROW_TILE, :] = tail

    mix = _dot((y * hbuf[...]).astype(BF16), wout_ref[...])
    xn = _layer_norm(ALPHA * x + g1p * mix, g_ref[...], b_ref[...])
    xn_ref[...] = xn
    b = pl.program_id(0)
    _route_rows(xn, b, (b == 0) & (t == 0), mod1_ref, wr_ref, br_ref, urow_ref, meta_ref, cnt_ref, carry)


def _const_spec(shape):
    nd = len(shape)
    return pl.BlockSpec(shape, lambda *_: (0,) * nd)


def _lru_weight_specs(layer):
    sub = 2 * layer
    return [
        _layer_spec((D, 2 * R), layer), _layer_spec((CONV_W, R), layer), _layer_spec((1, R), layer),
        _layer_spec((N_BLK, BLK, BLK), layer), _layer_spec((N_BLK, BLK, BLK), layer),
        _layer_spec((1, R), layer), _layer_spec((1, R), layer), _layer_spec((1, R), layer),
        _layer_spec((R, D), layer), _layer_spec((1, D), sub), _layer_spec((1, D), sub),
    ]


def _lru_prompt(x, mod, lw, ln_g, ln_b, wr, br, *, layer, nb, seq, tm):
    nt = seq // tm
    n = nb * seq
    r_in, r_out, r_shape, r_scratch = _route_specs(layer, nb, nt, tm)
    return pl.pallas_call(
        functools.partial(_lru_prompt_body, tm=tm),
        grid=(nb, nt),
        in_specs=[
            pl.BlockSpec((tm, D), lambda b, t: (b * nt + t, 0)),
            _layer_spec((nb, 3 * D), 2 * layer),
        ] + _lru_weight_specs(layer) + r_in,
        out_specs=[
            pl.BlockSpec((tm, D), lambda b, t: (b * nt + t, 0)),
            pl.BlockSpec((None, 1, R), lambda b, t: (b, 0, 0)),
            pl.BlockSpec((None, CONV_W - 1, R), lambda b, t: (b, 0, 0)),
        ] + r_out,
        out_shape=[
            jax.ShapeDtypeStruct((n, D), F32),
            jax.ShapeDtypeStruct((nb, 1, R), F32),
            jax.ShapeDtypeStruct((nb, CONV_W - 1, R), F32),
        ] + r_shape,
        scratch_shapes=[
            pltpu.VMEM((tm + ROW_TILE, R), F32),
            pltpu.VMEM((tm, R), F32),
            pltpu.VMEM((tm, R), F32),
            pltpu.VMEM((1, R), F32),
        ] + r_scratch,
        compiler_params=_cp(("arbitrary", "arbitrary")),
        name="lru_prompt",
    )(x, mod, *lw, ln_g, ln_b, mod, wr, br)


def _lru_sample_body(x_ref, mod_ref, h0_ref, c0_ref, win_ref, cw_ref, cb_ref, wa_ref, wx_ref,
                     ba_ref, bx_ref, lam_ref, wout_ref, g_ref, b_ref,
                     xn_ref, hl_ref, cl_ref, hbuf, *, nb, steps):
    x = x_ref[...]
    shift, sc1p, g1p = _split_mod(mod_ref, reps=steps)
    u = x * sc1p + shift
    yx = _dot(u.astype(BF16), win_ref[...])
    y = jax.nn.gelu(yx[:, :R])
    xext = jnp.concatenate([c0_ref[...], yx[:, R:]], axis=0)
    cl_ref[...] = xext[steps * nb:, :]
    for n in range(N_BLK):
        cols = slice(n * BLK, (n + 1) * BLK)
        xc = cb_ref[:, cols] + cw_ref[0:1, cols] * xext[0:steps * nb, cols]
        for k in range(1, CONV_W):
            xc = xc + cw_ref[k:k + 1, cols] * xext[k * nb:(k + steps) * nb, cols]
        a, b = _lru_gates(xc, n, wa_ref, wx_ref, ba_ref, bx_ref, lam_ref)
        h = h0_ref[:, cols]
        for t in range(steps):
            h = a[t * nb:(t + 1) * nb, :] * h + b[t * nb:(t + 1) * nb, :]
            hbuf[t * nb:(t + 1) * nb, cols] = h
        hl_ref[:, cols] = h
    mix = _dot((y * hbuf[...]).astype(BF16), wout_ref[...])
    xn_ref[...] = _layer_norm(ALPHA * x + g1p * mix, g_ref[...], b_ref[...])


def _lru_sample(x, mod, h0, c0, lw, ln_g, ln_b, *, layer, nb, steps):
    n = nb * steps
    return pl.pallas_call(
        functools.partial(_lru_sample_body, nb=nb, steps=steps),
        grid=(1,),
        in_specs=[
            _const_spec((n, D)), _layer_spec((nb, 3 * D), 2 * layer), _layer_spec((nb, R), layer),
            _const_spec(((CONV_W - 1) * nb, R)),
        ] + _lru_weight_specs(layer),
        out_specs=[_const_spec((n, D)), _const_spec((nb, R)), _const_spec(((CONV_W - 1) * nb, R))],
        out_shape=[
            jax.ShapeDtypeStruct((n, D), F32),
            jax.ShapeDtypeStruct((nb, R), F32),
            jax.ShapeDtypeStruct(((CONV_W - 1) * nb, R), F32),
        ],
        scratch_shapes=[pltpu.VMEM((n, R), F32)],
        compiler_params=_cp(("arbitrary",)),
        name="lru_sample",
    )(x, mod, h0, c0, *lw, ln_g, ln_b)


RROWS = 32
EROW = 8
_NT = (((1,), (1,)), ((), ()))
_TN = (((0,), (0,)), ((), ()))


def _top2(lt):
    nt = lt.shape[1]
    gl = lt[0:EROW, :]
    gid = lax.broadcasted_iota(jnp.int32, (EROW, nt), 0).astype(F32)
    is_g = gid < N_GROUPS
    gl = jnp.where(is_g, gl, NEG)
    gmax = jnp.max(gl, axis=0, keepdims=True)
    gidx = jnp.min(jnp.where(gl == gmax, gid, float(EROW)), axis=0, keepdims=True)
    gsum = jnp.sum(jnp.where(is_g, jnp.exp(gl - gmax), 0.0), axis=0, keepdims=True)
    g_p = 1.0 / gsum
    el = lt[EROW:EROW + N_EXP, :]
    eid = lax.broadcasted_iota(jnp.int32, (N_EXP, nt), 0).astype(F32)
    first = gidx * E_PER_G
    in_grp = (eid >= first) & (eid < first + E_PER_G)
    el = jnp.where(in_grp, el, NEG)
    big = float(N_EXP)
    t1 = jnp.max(el, axis=0, keepdims=True)
    i1 = jnp.min(jnp.where(el == t1, eid, big), axis=0, keepdims=True)
    el2 = jnp.where(eid == i1, NEG, el)
    t2 = jnp.max(el2, axis=0, keepdims=True)
    i2 = jnp.min(jnp.where((el2 == t2) & in_grp & (eid != i1), eid, big), axis=0, keepdims=True)
    e21 = jnp.exp(t2 - t1)
    w1 = g_p / (1.0 + e21)
    w2 = w1 * e21
    return eid, gidx, i1, i2, w1, w2


def _router_logits(u16, wrt_ref, brt_ref):
    return lax.dot_general(wrt_ref[...], u16, _NT, preferred_element_type=F32) + brt_ref[:, 0:1]


def _rows_to_cols(rows):
    r, nt = rows.shape
    padded = jnp.concatenate([rows, jnp.zeros((RLANES - r, nt), F32)], axis=0)
    return jnp.transpose(padded)


PAIRS = E_PER_G * (E_PER_G - 1) // 2
NCLS = N_GROUPS * PAIRS
HALF = D // 2
ROW_SHAPE = (D // LANES, LANES)
HI_MASK = -65536
FFN_TM = 256
DISPATCH_TD = 2048
COMBINE_TC = 512
COMBINE_CHUNK = 64
COMBINE_SLOTS = 3


def _route_rows(x, row, first, mod_ref, wr_ref, br_ref, urow_ref, meta_ref, cnt_ref, carry):
    @pl.when(first)
    def _():
        carry[...] = jnp.zeros(carry.shape, F32)

    tm = x.shape[0]
    shift, sc1p, _ = _split_mod(mod_ref, row=row)
    u = x * sc1p + shift
    _, gidx, i1, i2, w1, w2 = _top2(_router_logits(u.astype(BF16), wr_ref, br_ref))
    first = gidx * E_PER_G
    lo = jnp.minimum(i1, i2) - first
    hi = jnp.maximum(i1, i2) - first
    w_lo = jnp.where(i1 < i2, w1, w2)
    w_hi = jnp.where(i1 < i2, w2, w1)
    cls = gidx * PAIRS + lo * (2 * E_PER_G - 1 - lo) * 0.5 + (hi - lo - 1.0)
    cid = lax.broadcasted_iota(jnp.int32, (RROWS, tm), 0).astype(F32)
    onehot = (cid == cls).astype(F32)
    r = lax.broadcasted_iota(jnp.int32, (tm, tm), 0)
    c = lax.broadcasted_iota(jnp.int32, (tm, tm), 1)
    earlier = _dot(onehot.astype(BF16), (r < c).astype(BF16)) + carry[:, 0:1]
    rank = jnp.sum(onehot * earlier, axis=0, keepdims=True)
    carry[...] += jnp.sum(onehot, axis=1, keepdims=True)
    cnt_ref[...] = carry[...]
    meta = jnp.concatenate([w_lo, w_hi, cls, rank, jnp.zeros((ROW_TILE - 4, tm), F32)], axis=0)
    meta_ref[...] = meta
    ub = u.astype(BF16)
    lo = lax.shift_right_logical(pltpu.bitcast(ub[:, 0:HALF].astype(F32), jnp.int32), 16)
    hi = pltpu.bitcast(ub[:, HALF:D].astype(F32), jnp.int32) & HI_MASK
    ext = pltpu.bitcast(_rows_to_cols(meta), jnp.int32)
    row = jnp.concatenate([hi | lo, ext, jnp.zeros((tm, D - HALF - RLANES), jnp.int32)], axis=1)
    urow_ref[...] = row.reshape((tm,) + ROW_SHAPE)


def _route_specs(layer, nb, nt, tm):
    tile = lambda b, t: b * nt + t
    n = nb * nt * tm
    in_specs = [_layer_spec((nb, 3 * D), 2 * layer + 1),
                _layer_spec((RROWS, D), layer), _layer_spec((RROWS, RLANES), layer)]
    out_specs = [pl.BlockSpec((tm,) + ROW_SHAPE, lambda b, t: (tile(b, t), 0, 0)),
                 pl.BlockSpec((None, ROW_TILE, tm), lambda b, t: (tile(b, t), 0, 0)),
                 _const_spec((RROWS, RLANES))]
    out_shape = [jax.ShapeDtypeStruct((n,) + ROW_SHAPE, jnp.int32),
                 jax.ShapeDtypeStruct((nb * nt, ROW_TILE, tm), F32),
                 jax.ShapeDtypeStruct((RROWS, RLANES), F32)]
    return in_specs, out_specs, out_shape, [pltpu.VMEM((RROWS, RLANES), F32)]


def _row_copy(src, src_row, dst, dst_row, sem):
    return pltpu.make_async_copy(src.at[pl.ds(src_row, 1)], dst.at[pl.ds(dst_row, 1)], sem)


def _dispatch_body(pos_ref, zflag_ref, urow_ref, xs_hbm, zbuf, sem, zsem):
    base = pl.program_id(0) * DISPATCH_TD

    @pl.when(pl.program_id(0) == 0)
    def _():
        zbuf[...] = jnp.zeros(zbuf.shape, jnp.int32)

        def zero_copy(t):
            return pltpu.make_async_copy(zbuf, xs_hbm.at[pl.ds(t * FFN_TM, FFN_TM)], zsem)

        def start(t, carry):
            @pl.when(zflag_ref[t] != 0)
            def _():
                zero_copy(t).start()
            return carry

        def wait(t, carry):
            @pl.when(zflag_ref[t] != 0)
            def _():
                zero_copy(t).wait()
            return carry

        lax.fori_loop(0, zflag_ref.shape[0], start, 0)
        lax.fori_loop(0, zflag_ref.shape[0], wait, 0)

    def issue(g, carry):
        r0 = pl.multiple_of(g * ROW_TILE, ROW_TILE)
        for k in range(ROW_TILE):
            _row_copy(urow_ref.at[pl.ds(r0, ROW_TILE)], k, xs_hbm, pos_ref[base + r0 + k],
                      sem).start(priority=k % 2)
        return carry

    lax.fori_loop(0, DISPATCH_TD // ROW_TILE, issue, 0)
    pltpu.make_async_copy(urow_ref, xs_hbm.at[pl.ds(0, DISPATCH_TD)], sem).wait()


def _dispatch(pos, zflag, urow):
    n = urow.shape[0]
    nt = zflag.shape[0]
    return pl.pallas_call(
        _dispatch_body,
        grid_spec=pltpu.PrefetchScalarGridSpec(
            num_scalar_prefetch=2, grid=(n // DISPATCH_TD,),
            in_specs=[pl.BlockSpec((DISPATCH_TD,) + ROW_SHAPE, lambda i, pos, zflag: (i, 0, 0))],
            out_specs=pl.BlockSpec(memory_space=pl.ANY),
            scratch_shapes=[pltpu.VMEM((FFN_TM,) + ROW_SHAPE, jnp.int32), pltpu.SemaphoreType.DMA(()),
                            pltpu.SemaphoreType.DMA(())]),
        out_shape=jax.ShapeDtypeStruct((nt * FFN_TM,) + ROW_SHAPE, jnp.int32),
        compiler_params=_cp(("arbitrary",)),
        name="moe_dispatch",
    )(pos, zflag, urow)


def _ffn_sorted_body(ta_ref, tb_ref, na_ref, xs_ref, w1a, w3a, w2a, w1b, w3b, w2b, ys_ref):
    del ta_ref, tb_ref
    active = pl.program_id(0) < na_ref[0]

    @pl.when(active)
    def _():
        words = xs_ref[...].reshape(FFN_TM, D)
        pair = words[:, 0:HALF]
        x = jnp.concatenate([pltpu.bitcast(lax.shift_left(pair, 16), F32).astype(BF16),
                             pltpu.bitcast(pair & HI_MASK, F32).astype(BF16)], axis=1)
        ext = pltpu.bitcast(words[:, HALF:HALF + RLANES], F32)
        ha = jax.nn.silu(_dot(x, w1a[...])) * _dot(x, w3a[...]) * ext[:, 0:1]
        hb = jax.nn.silu(_dot(x, w1b[...])) * _dot(x, w3b[...]) * ext[:, 1:2]
        y = _dot(ha.astype(BF16), w2a[...]) + _dot(hb.astype(BF16), w2b[...])
        ys_ref[...] = y.reshape(FFN_TM, D // LANES, LANES)

    @pl.when(jnp.logical_not(active))
    def _():
        ys_ref[...] = jnp.zeros(ys_ref.shape, F32)


def _ffn_sorted(ta, tb, na, xs, w1, w3, w2):
    nt = ta.shape[0]
    ea = lambda t, ta, tb, na: (ta[t], 0, 0)
    eb = lambda t, ta, tb, na: (tb[t], 0, 0)
    rows = lambda t, ta, tb, na: (t, 0)
    up, down = (None, D, F), (None, F, D)
    return pl.pallas_call(
        _ffn_sorted_body,
        grid_spec=pltpu.PrefetchScalarGridSpec(
            num_scalar_prefetch=3, grid=(nt,),
            in_specs=[
                pl.BlockSpec((FFN_TM,) + ROW_SHAPE, lambda t, ta, tb, na: (t, 0, 0)),
                pl.BlockSpec(up, ea), pl.BlockSpec(up, ea), pl.BlockSpec(down, ea),
                pl.BlockSpec(up, eb), pl.BlockSpec(up, eb), pl.BlockSpec(down, eb),
            ],
            out_specs=pl.BlockSpec((FFN_TM, D // LANES, LANES), lambda t, ta, tb, na: (t, 0, 0))),
        out_shape=jax.ShapeDtypeStruct((nt * FFN_TM, D // LANES, LANES), F32),
        compiler_params=_cp(("arbitrary",)),
        name="moe_ffn",
    )(ta, tb, na, xs, w1, w3, w2, w1, w3, w2)


def _combine_body(pos_ref, x_ref, mod_ref, ys_hbm, g_ref, b_ref, *rest, tiles_per_seq, with_kv):
    if with_kv:
        kvw_ref, kvwt_ref, xn_ref, *kv_out, buf, sem = rest
    else:
        xn_ref, buf, sem = rest
    i = pl.program_id(0)
    nsteps = pl.num_programs(0)
    tc = x_ref.shape[0]

    def issue_group(step, slot, r0):
        group = buf.at[slot, pl.ds(r0, ROW_TILE)]
        for k in range(ROW_TILE):
            _row_copy(ys_hbm, pos_ref[step * tc + r0 + k], group, k, sem.at[slot]).start(priority=k % 2)

    def wait_tile(slot):
        pltpu.make_async_copy(ys_hbm.at[pl.ds(0, tc)], buf.at[slot], sem.at[slot]).wait()

    ahead = COMBINE_SLOTS - 1

    @pl.when(i == 0)
    def _():
        for t in range(ahead):
            def prime(g, carry, t=t):
                issue_group(t, t, pl.multiple_of(g * ROW_TILE, ROW_TILE))
                return carry
            lax.fori_loop(0, tc // ROW_TILE, prime, 0)

    slot = i % COMBINE_SLOTS
    wait_tile(slot)
    _, _, g1p = _split_mod(mod_ref, row=i // tiles_per_seq)
    gain, bias = g_ref[...], b_ref[...]
    nxt = jnp.minimum(i + ahead, nsteps - 1)
    nslot = (i + ahead) % COMBINE_SLOTS

    for c in range(tc // COMBINE_CHUNK):
        for g in range(COMBINE_CHUNK // ROW_TILE):
            issue_group(nxt, nslot, c * COMBINE_CHUNK + g * ROW_TILE)
        rows = pl.ds(c * COMBINE_CHUNK, COMBINE_CHUNK)
        z = ALPHA * x_ref[rows, :] + g1p * buf[slot, rows].reshape(COMBINE_CHUNK, D)
        xn_ref[rows, :] = _layer_norm(z, gain, bias)

    if with_kv:
        _kv_rows(xn_ref[...].astype(BF16), kvw_ref, kvwt_ref, *kv_out)

    @pl.when(i == nsteps - 1)
    def _():
        for k in range(1, COMBINE_SLOTS):
            wait_tile((i + k) % COMBINE_SLOTS)


def _combine(pos, x, mod, ys, ln_g, ln_b, kv_weights=None, *, layer, seq):
    n = x.shape[0]
    tc = COMBINE_TC
    sub = 2 * layer + 1
    row = lambda i, pos: (i, 0)
    in_specs = [
        pl.BlockSpec((tc, D), row),
        _layer_spec((n // seq, 3 * D), sub),
        pl.BlockSpec(memory_space=pl.ANY),
        _layer_spec((1, D), sub), _layer_spec((1, D), sub),
    ]
    out_specs = [pl.BlockSpec((tc, D), row)]
    out_shape = [jax.ShapeDtypeStruct((n, D), F32)]
    args = [pos, x, mod, ys, ln_g, ln_b]
    if kv_weights is not None:
        in_specs += [pl.BlockSpec((D, 2 * KVW), lambda i, pos: (0, 0)),
                     pl.BlockSpec((KVW, D), lambda i, pos: (0, 0))]
        out_specs += [pl.BlockSpec((tc, KVW), row), pl.BlockSpec((tc, KVW), row),
                      pl.BlockSpec((tc, KVW), row), pl.BlockSpec((KVW, tc), lambda i, pos: (0, i))]
        out_shape += [jax.ShapeDtypeStruct((n, KVW), F32), jax.ShapeDtypeStruct((n, KVW), F32),
                      jax.ShapeDtypeStruct((n, KVW), BF16), jax.ShapeDtypeStruct((KVW, n), BF16)]
        args += list(kv_weights)
    return pl.pallas_call(
        functools.partial(_combine_body, tiles_per_seq=seq // tc, with_kv=kv_weights is not None),
        grid_spec=pltpu.PrefetchScalarGridSpec(
            num_scalar_prefetch=1, grid=(n // tc,),
            in_specs=in_specs, out_specs=out_specs,
            scratch_shapes=[pltpu.VMEM((COMBINE_SLOTS, tc, D // LANES, LANES), F32),
                            pltpu.SemaphoreType.DMA((COMBINE_SLOTS,))]),
        out_shape=out_shape,
        compiler_params=_cp(("arbitrary",)),
        name="moe_combine",
    )(*args)


def _moe_sparse(x, route, mod, w1, w3, w2, ln_g, ln_b, kv_weights=None, *, layer, seq):
    n = x.shape[0]
    urow, meta, cnt = route
    cls = meta[:, 2, :].reshape(n).astype(jnp.int32)
    rank = meta[:, 3, :].reshape(n).astype(jnp.int32)
    count = cnt[:NCLS, 0].astype(jnp.int32)
    ntile = (count + FFN_TM - 1) // FFN_TM
    tile_end = jnp.cumsum(ntile)
    row_start = (tile_end - ntile) * FFN_TM
    pos = rank + jnp.sum(jax.nn.one_hot(cls, NCLS, dtype=jnp.int32) * row_start[None, :], axis=1)
    nt = n // FFN_TM + NCLS
    total = tile_end[NCLS - 1]
    tr = jnp.minimum(jnp.arange(nt, dtype=jnp.int32), total - 1)
    tcls = jnp.sum((tr[:, None] >= tile_end[None, :]).astype(jnp.int32), axis=1)
    grp, pair = tcls // PAIRS, tcls % PAIRS
    lo = (pair >= 3).astype(jnp.int32) + (pair >= 5).astype(jnp.int32)
    hi = pair + 1 - lo * (2 * E_PER_G - 3 - lo) // 2
    ta = grp * E_PER_G + lo
    tb = grp * E_PER_G + hi
    t_all = jnp.arange(nt, dtype=jnp.int32)
    partial_last = ((t_all[:, None] == tile_end[None, :] - 1) & (count % FFN_TM != 0)[None, :]).any(axis=1)
    zflag = ((t_all >= total) | partial_last).astype(jnp.int32)
    xs = _dispatch(pos, zflag, urow)
    ys = _ffn_sorted(ta, tb, total[None], xs, w1, w3, w2)
    return _combine(pos, x, mod, ys, ln_g, ln_b, kv_weights, layer=layer, seq=seq)


def _moe_body(x_ref, mod_ref, wr_ref, br_ref, w1f_ref, w3f_ref, w2f_ref, g_ref, b_ref,
              xn_ref, w1_ref, w3_ref, w2_ref, ub, comb, acc, *, reps):
    e = pl.program_id(1)
    w1_ref[...] = w1f_ref[...].astype(BF16)
    w3_ref[...] = w3f_ref[...].astype(BF16)
    w2_ref[...] = w2f_ref[...].astype(BF16)

    @pl.when(e == 0)
    def _():
        shift, sc1p, _ = _split_mod(mod_ref, reps=reps)
        u = (x_ref[...] * sc1p + shift).astype(BF16)
        ub[...] = u
        eid, _, i1, i2, w1, w2 = _top2(_router_logits(u, wr_ref, br_ref))
        comb[...] = _rows_to_cols(jnp.where(eid == i1, w1, 0.0) + jnp.where(eid == i2, w2, 0.0))
        acc[...] = jnp.zeros(acc.shape, F32)

    u = ub[...]
    col = lax.broadcasted_iota(jnp.int32, comb.shape, 1)
    ce = jnp.sum(jnp.where(col == e, comb[...], 0.0), axis=-1, keepdims=True)
    h = jax.nn.silu(_dot(u, w1_ref[...])) * _dot(u, w3_ref[...]) * ce
    acc[...] += _dot(h.astype(BF16), w2_ref[...])

    @pl.when(e == N_EXP - 1)
    def _():
        _, _, g1p = _split_mod(mod_ref, reps=reps)
        xn_ref[...] = _layer_norm(ALPHA * x_ref[...] + g1p * acc[...], g_ref[...], b_ref[...])


def _moe_dense(x, mod, wr, br, w1, w3, w2, ln_g, ln_b, *, layer, reps):
    n = x.shape[0]
    sub = 2 * layer + 1
    expert = lambda i, e: (layer, e, 0, 0)
    return pl.pallas_call(
        functools.partial(_moe_body, reps=reps),
        grid=(1, N_EXP),
        in_specs=[
            pl.BlockSpec((n, D), lambda i, e: (0, 0)),
            _layer_spec((n // reps, 3 * D), sub),
            _layer_spec((RROWS, D), layer), _layer_spec((RROWS, RLANES), layer),
            pl.BlockSpec((None, None, D, F), expert),
            pl.BlockSpec((None, None, D, F), expert),
            pl.BlockSpec((None, None, F, D), expert),
            _layer_spec((1, D), sub), _layer_spec((1, D), sub),
        ],
        out_specs=[pl.BlockSpec((n, D), lambda i, e: (0, 0)),
                   pl.BlockSpec((None, D, F), lambda i, e: (e, 0, 0)),
                   pl.BlockSpec((None, D, F), lambda i, e: (e, 0, 0)),
                   pl.BlockSpec((None, F, D), lambda i, e: (e, 0, 0))],
        out_shape=[jax.ShapeDtypeStruct((n, D), F32),
                   jax.ShapeDtypeStruct((N_EXP, D, F), BF16),
                   jax.ShapeDtypeStruct((N_EXP, D, F), BF16),
                   jax.ShapeDtypeStruct((N_EXP, F, D), BF16)],
        scratch_shapes=[
            pltpu.VMEM((n, D), BF16),
            pltpu.VMEM((n, RLANES), F32),
            pltpu.VMEM((n, D), F32),
        ],
        compiler_params=_cp(("arbitrary", "arbitrary")),
        name="moe",
    )(x, mod, wr, br, w1, w3, w2, ln_g, ln_b)


def _kv_body(x_ref, w_ref, wvt_ref, k_ref, v_ref, k16_ref, vt16_ref):
    _kv_rows(x_ref[...].astype(BF16), w_ref, wvt_ref, k_ref, v_ref, k16_ref, vt16_ref)


def _kv_rows(xb, w_ref, wvt_ref, k_ref, v_ref, k16_ref, vt16_ref):
    kv = _dot(xb, w_ref[...])
    k_ref[...] = kv[:, :KVW]
    v_ref[...] = kv[:, KVW:]
    k16_ref[...] = kv[:, :KVW].astype(BF16)
    vt16_ref[...] = lax.dot_general(wvt_ref[...], xb, _NT, preferred_element_type=F32).astype(BF16)


def _kv(x, w, wvt, *, tm):
    n = x.shape[0]
    row = lambda i: (i, 0)
    return pl.pallas_call(
        _kv_body,
        grid=(n // tm,),
        in_specs=[pl.BlockSpec((tm, D), row), _const_spec((D, 2 * KVW)), _const_spec((KVW, D))],
        out_specs=[pl.BlockSpec((tm, KVW), row), pl.BlockSpec((tm, KVW), row),
                   pl.BlockSpec((tm, KVW), row), pl.BlockSpec((KVW, tm), lambda i: (0, i))],
        out_shape=[jax.ShapeDtypeStruct((n, KVW), F32), jax.ShapeDtypeStruct((n, KVW), F32),
                   jax.ShapeDtypeStruct((n, KVW), BF16), jax.ShapeDtypeStruct((KVW, n), BF16)],
        compiler_params=_cp(("arbitrary",)),
        name="kv_proj",
    )(x, w, wvt)


def _sink_attend(qh, kh, vh, bias, sink):
    s = lax.dot_general(qh, kh, (((1,), (1,)), ((), ())), preferred_element_type=F32) + bias
    m = jnp.maximum(jnp.max(s, axis=-1, keepdims=True), sink)
    p = jnp.exp2(s - m)
    den = jnp.sum(p, axis=-1, keepdims=True) + jnp.exp2(sink - m)
    return _dot(p.astype(BF16), vh) / den


def _attn_prompt_body(x_ref, mod_ref, wqt_ref, wo_ref, kc_ref, kp_ref, vtc_ref, vtp_ref,
                      bias_ref, sink_ref, g_ref, b_ref, mod1_ref, wr_ref, br_ref,
                      xn_ref, urow_ref, meta_ref, cnt_ref, carry, *, tq):
    t = pl.program_id(1)
    x = x_ref[...]
    shift, sc1p, g1p = _split_mod(mod_ref, row=pl.program_id(0))
    u = (x * sc1p + shift).astype(BF16)
    qt = lax.dot_general(wqt_ref[...], u, _NT, preferred_element_type=F32)
    qt = (qt * (HD ** -0.5 * LOG2E)).astype(BF16)
    kc = kc_ref[...]
    vtc = vtc_ref[...]
    krow = lax.broadcasted_iota(jnp.int32, (2 * WIN, GRP * WIN), 0)
    first = jnp.where((krow < WIN) & (t == 0), NEG, 0.0).astype(F32)
    nwin = tq // WIN
    kks, vvts = [], []
    for j in range(nwin):
        if j == 0:
            kks.append(jnp.concatenate([kp_ref[...], kc[0:WIN]], axis=0))
            vvts.append(jnp.concatenate([vtp_ref[...], vtc[:, 0:WIN]], axis=1))
        else:
            kks.append(kc[(j - 1) * WIN:(j + 1) * WIN])
            vvts.append(vtc[:, (j - 1) * WIN:(j + 1) * WIN])

    def scores(j, hk):
        qht = jnp.concatenate(
            [qt[(hk * GRP + g) * HD:(hk * GRP + g + 1) * HD, j * WIN:(j + 1) * WIN] for g in range(GRP)],
            axis=1)
        st = _dot(kks[j][:, hk * HD:(hk + 1) * HD], qht) + bias_ref[hk]
        return st + first if j == 0 else st

    items = [(j, hk) for j in range(nwin) for hk in range(HKV)]
    heads = [[] for _ in range(nwin)]
    st_next = scores(*items[0])
    for n, (j, hk) in enumerate(items):
        st = st_next
        if n + 1 < len(items):
            st_next = scores(*items[n + 1])
        sink = sink_ref[hk]
        m = jnp.maximum(jnp.max(st, axis=0, keepdims=True), sink)
        pt = jnp.exp2(st - m)
        den = jnp.sum(pt, axis=0, keepdims=True) + jnp.exp2(sink - m)
        ot = _dot(vvts[j][hk * HD:(hk + 1) * HD, :], pt.astype(BF16)) * (1.0 / den)
        for g in range(GRP):
            heads[j].append(ot[:, g * WIN:(g + 1) * WIN])
    blocks = [jnp.concatenate(h, axis=0) for h in heads]
    oallt = jnp.concatenate(blocks, axis=1) if len(blocks) > 1 else blocks[0]
    mix = lax.dot_general(oallt.astype(BF16), wo_ref[...], _TN, preferred_element_type=F32)
    xn = _layer_norm(ALPHA * x + g1p * mix, g_ref[...], b_ref[...])
    xn_ref[...] = xn
    b = pl.program_id(0)
    _route_rows(xn, b, (b == 0) & (t == 0), mod1_ref, wr_ref, br_ref, urow_ref, meta_ref, cnt_ref, carry)


def _attn_prompt(x, mod, wqt, wo, k16, vt16, bias, sink, ln_g, ln_b, wr, br, *, layer, nb, seq, tq):
    nt = seq // tq
    r_in, r_out, r_shape, r_scratch = _route_specs(layer, nb, nt, tq)
    wpt = tq // WIN
    n = nb * seq
    j = layer - N_A
    sub = 2 * layer

    def prev(b, t):
        return jnp.maximum((b * nt + t) * wpt - 1, 0)

    cur = lambda b, t: (b * nt + t, 0)
    return pl.pallas_call(
        functools.partial(_attn_prompt_body, tq=tq),
        grid=(nb, nt),
        in_specs=[
            pl.BlockSpec((tq, D), cur),
            _layer_spec((nb, 3 * D), sub),
            _layer_spec((HQ * HD, D), j), _layer_spec((HQ * HD, D), j),
            pl.BlockSpec((tq, KVW), cur), pl.BlockSpec((WIN, KVW), lambda b, t: (prev(b, t), 0)),
            pl.BlockSpec((KVW, tq), lambda b, t: (0, b * nt + t)),
            pl.BlockSpec((KVW, WIN), lambda b, t: (0, prev(b, t))),
            _const_spec((HKV, 2 * WIN, GRP * WIN)), _layer_spec((HKV, 1, GRP * WIN), j),
            _layer_spec((1, D), sub), _layer_spec((1, D), sub),
        ] + r_in,
        out_specs=[pl.BlockSpec((tq, D), cur)] + r_out,
        out_shape=[jax.ShapeDtypeStruct((n, D), F32)] + r_shape,
        scratch_shapes=r_scratch,
        compiler_params=_cp(("arbitrary", "arbitrary")),
        name="attn_prompt",
    )(x, mod, wqt, wo, k16, k16, vt16, vt16, bias, sink, ln_g, ln_b, mod, wr, br)


def _attn_sample_body(x_ref, mod_ref, wq_ref, wo_ref, kb_ref, vb_ref, kn_ref, vn_ref,
                      bias_ref, sink_ref, g_ref, b_ref, xn_ref, *, ns, steps):
    nr = ns * steps
    x = x_ref[...].reshape(nr, D)
    shift, sc1p, g1p = _split_mod(mod_ref, reps=steps)
    u = (x * sc1p + shift).astype(BF16)
    q = (_dot(u, wq_ref[...]) * (HD ** -0.5 * LOG2E)).astype(BF16)
    nbuf = kb_ref.shape[1]
    k_all = jnp.concatenate([kb_ref[...].reshape(ns * nbuf, KVW), kn_ref[...].reshape(nr, KVW)],
                            axis=0).astype(BF16)
    v_all = jnp.concatenate([vb_ref[...].reshape(ns * nbuf, KVW), vn_ref[...].reshape(nr, KVW)],
                            axis=0).astype(BF16)
    heads = [None] * HQ
    for hk in range(HKV):
        kcols = slice(hk * HD, (hk + 1) * HD)
        qh = jnp.concatenate(
            [q[:, (hk * GRP + g) * HD:(hk * GRP + g + 1) * HD] for g in range(GRP)], axis=0)
        o = _sink_attend(qh, k_all[:, kcols], v_all[:, kcols], bias_ref[hk], sink_ref[hk])
        for g in range(GRP):
            heads[hk * GRP + g] = o[g * nr:(g + 1) * nr]
    o_all = jnp.concatenate(heads, axis=1)
    mix = _dot(o_all.astype(BF16), wo_ref[...])
    xn_ref[...] = _layer_norm(ALPHA * x + g1p * mix, g_ref[...], b_ref[...]).reshape(steps, ns, D)


def _attn_sample(x, mod, wq, wo, kbuf, vbuf, kn, vn, bias, sink, ln_g, ln_b, *, layer, ns):
    steps, nb, _ = x.shape
    nbuf = kbuf.shape[1]
    j = layer - N_A
    sub = 2 * layer
    seqs = lambda i: (0, i, 0)
    return pl.pallas_call(
        functools.partial(_attn_sample_body, ns=ns, steps=steps),
        grid=(nb // ns,),
        in_specs=[
            pl.BlockSpec((steps, ns, D), seqs),
            pl.BlockSpec((None, ns, 3 * D), lambda i: (sub, i, 0)),
            _layer_spec((D, HQ * HD), j), _layer_spec((HQ * HD, D), j),
            pl.BlockSpec((ns, nbuf, KVW), lambda i: (i, 0, 0)),
            pl.BlockSpec((ns, nbuf, KVW), lambda i: (i, 0, 0)),
            pl.BlockSpec((steps, ns, KVW), seqs), pl.BlockSpec((steps, ns, KVW), seqs),
            _const_spec(bias.shape), _layer_spec(sink.shape[1:], j),
            _layer_spec((1, D), sub), _layer_spec((1, D), sub),
        ],
        out_specs=pl.BlockSpec((steps, ns, D), seqs),
        out_shape=jax.ShapeDtypeStruct((steps, nb, D), F32),
        compiler_params=_cp(("arbitrary",)),
        name="attn_sample",
    )(x, mod, wq, wo, kbuf, vbuf, kn, vn, bias, sink, ln_g, ln_b)


def _t5_bucket(d):
    max_exact = N_BUCKETS // 2
    d = np.maximum(d, 0)
    log_ratio = (np.log(np.maximum(d, 1).astype(np.float32) / np.float32(max_exact))
                 / np.float32(math.log(WIN / max_exact)))
    large = max_exact + (log_ratio * (N_BUCKETS - max_exact)).astype(np.int32)
    large = np.minimum(large, N_BUCKETS - 1)
    return np.where(d < max_exact, d, large)


def _bias_table(d, ok, rel_bias):
    sel = np.where(ok, _t5_bucket(d), N_BUCKETS)
    onehot = (sel[..., None] == np.arange(N_BUCKETS + 1)).astype(np.float32)
    tab = jnp.concatenate([rel_bias.astype(F32), jnp.full((1, HQ), NEG, F32)], axis=0)
    return jnp.einsum('abd,dh->hab', jnp.asarray(onehot), tab, precision=lax.Precision.HIGHEST)


def _prompt_bias(rel_bias):
    d = np.arange(WIN)[None, :] + WIN - np.arange(2 * WIN)[:, None]
    b = _bias_table(d, (d >= 0) & (d < WIN), rel_bias).reshape(HKV, GRP, 2 * WIN, WIN)
    return jnp.transpose(b, (0, 2, 1, 3)).reshape(HKV, 2 * WIN, GRP * WIN)


def _sample_bias(rel_bias, ns, steps, nbuf):
    qs = np.tile(np.arange(ns), steps)
    qt = np.repeat(np.arange(steps), ns)
    ks = np.concatenate([np.repeat(np.arange(ns), nbuf), qs])
    kpos = np.concatenate([np.tile(np.arange(nbuf), ns), nbuf + qt])
    d = qt[:, None] + nbuf - kpos[None, :]
    ok = (d >= 0) & (d < WIN) & (qs[:, None] == ks[None, :])
    return _bias_table(d, ok, rel_bias).reshape(HKV, GRP * ns * steps, kpos.shape[0])


LRU_TM = 512
ATTN_TQ = 1024
SAMPLE_NS = 8


def kernel(x_prompt, x_sample, state_rnn_h, state_rnn_conv, cache_win_k, cache_win_v, c_prompt, c_sample, ada_w, ada_b, ln_g, ln_b, lru_w_in, lru_conv_w, lru_conv_b, lru_w_a, lru_b_a, lru_w_x, lru_b_x, lru_lambda, lru_w_out, kv_w, attn_w_q, attn_sinks, attn_w_o, rel_bias, moe_w_group, moe_b_group, moe_w_router, moe_b_router, moe_w1, moe_w3, moe_w2):
    nbp, seq, _ = x_prompt.shape
    nbs, steps, _ = x_sample.shape
    nbuf = cache_win_k.shape[1]
    npt = nbp * seq
    nst = nbs * steps

    mod_p, mod_s = _adaln(c_prompt, c_sample, ada_w, ada_b)

    ln_g3 = ln_g.reshape(2 * DEPTH, 1, D)
    ln_b3 = ln_b.reshape(2 * DEPTH, 1, D)
    lw = (lru_w_in.astype(BF16), lru_conv_w, lru_conv_b[:, None], lru_w_a.astype(BF16),
          lru_w_x.astype(BF16), lru_b_a[:, None], lru_b_x[:, None], lru_lambda[:, None],
          lru_w_out.astype(BF16))
    zrow = lambda k: jnp.zeros((DEPTH, k, D), F32)
    wr = jnp.concatenate([jnp.swapaxes(moe_w_group, 1, 2), zrow(EROW - N_GROUPS),
                          jnp.swapaxes(moe_w_router.reshape(DEPTH, D, N_EXP), 1, 2),
                          zrow(RROWS - EROW - N_EXP)], axis=1).astype(BF16)
    zb = lambda k: jnp.zeros((DEPTH, k), F32)
    br = jnp.concatenate([moe_b_group, zb(EROW - N_GROUPS), moe_b_router.reshape(DEPTH, N_EXP),
                          zb(RROWS - EROW - N_EXP)], axis=1)
    br = jnp.broadcast_to(br[:, :, None], (DEPTH, RROWS, RLANES))
    wq = attn_w_q.astype(BF16)
    wqt = jnp.swapaxes(wq, 1, 2)
    wo = attn_w_o.astype(BF16)
    kvw = kv_w.astype(BF16)
    kvwt = kvw[:, KVW:].T
    bias_p = _prompt_bias(rel_bias) * LOG2E
    bias_s = _sample_bias(rel_bias, SAMPLE_NS, steps, nbuf) * LOG2E
    sink_p = jnp.repeat(attn_sinks.reshape(N_B, HKV, GRP), WIN, axis=2)[:, :, None, :] * LOG2E
    sink_s = jnp.repeat(attn_sinks.reshape(N_B, HKV, GRP), SAMPLE_NS * steps, axis=2)[..., None] * LOG2E
    kbuf = cache_win_k.reshape(nbs, nbuf, KVW)
    vbuf = cache_win_v.reshape(nbs, nbuf, KVW)

    xp = x_prompt.reshape(npt, D)
    xs = jnp.transpose(x_sample, (1, 0, 2)).reshape(nst, D)
    c0 = jnp.transpose(state_rnn_conv, (0, 2, 1, 3)).reshape(N_A, (CONV_W - 1) * nbs, R)
    hp, cp_, hs, cs = [], [], [], []
    kp = vp = kn = vn = kp16 = vpt16 = None
    for l in range(DEPTH):
        if l < N_A:
            xp, h_l, c_l, *route = _lru_prompt(xp, mod_p, lw, ln_g3, ln_b3, wr, br,
                                               layer=l, nb=nbp, seq=seq, tm=LRU_TM)
            hp.append(h_l[:, 0])
            cp_.append(c_l)
            xs, h_l, c_l = _lru_sample(xs, mod_s, state_rnn_h, c0[l], lw, ln_g3, ln_b3,
                                       layer=l, nb=nbs, steps=steps)
            hs.append(h_l)
            cs.append(jnp.transpose(c_l.reshape(CONV_W - 1, nbs, R), (1, 0, 2)))
        else:
            xp, *route = _attn_prompt(xp, mod_p, wqt, wo, kp16, vpt16, bias_p, sink_p, ln_g3, ln_b3, wr, br,
                                      layer=l, nb=nbp, seq=seq, tq=ATTN_TQ)
            xs = _attn_sample(xs.reshape(steps, nbs, D), mod_s, wq, wo, kbuf, vbuf,
                              kn.reshape(steps, nbs, KVW), vn.reshape(steps, nbs, KVW),
                              bias_s, sink_s, ln_g3, ln_b3, layer=l, ns=SAMPLE_NS).reshape(nst, D)
        xs, w1, w3, w2 = _moe_dense(xs, mod_s, wr, br, moe_w1, moe_w3, moe_w2, ln_g3, ln_b3,
                                    layer=l, reps=steps)
        if l == N_A - 1:
            xp, kp, vp, kp16, vpt16 = _moe_sparse(xp, route, mod_p, w1, w3, w2, ln_g3, ln_b3, (kvw, kvwt),
                                                  layer=l, seq=seq)
            kn, vn, _, _ = _kv(xs, kvw, kvwt, tm=nst)
        else:
            xp, = _moe_sparse(xp, route, mod_p, w1, w3, w2, ln_g3, ln_b3, layer=l, seq=seq)

    lp = min(WIN, seq)
    k_p = kp.reshape(nbp, seq, HKV, HD)[:, seq - lp:]
    v_p = vp.reshape(nbp, seq, HKV, HD)[:, seq - lp:]
    kn4 = jnp.transpose(kn.reshape(steps, nbs, HKV, HD), (1, 0, 2, 3))
    vn4 = jnp.transpose(vn.reshape(steps, nbs, HKV, HD), (1, 0, 2, 3))
    k_s = jnp.concatenate([cache_win_k, kn4], axis=1)[:, -nbuf:]
    v_s = jnp.concatenate([cache_win_v, vn4], axis=1)[:, -nbuf:]
    y_s = jnp.transpose(xs.reshape(steps, nbs, D), (1, 0, 2))
    return (xp.reshape(nbp, seq, D), y_s,
            jnp.stack(hp), jnp.stack(cp_), k_p, v_p,
            jnp.stack(hs), jnp.stack(cs), k_s, v_s)
```
